```python
import math
import jax, jax.numpy as jnp
from jax import lax
import numpy as np

D_MODEL = 1024
BATCH = 8
SEQ = 4096
DEPTH = 4

GRID_W = 64
CTX_LEN = 256
HEAD_DIM = 64
M_HEADS = 4
M_WIDTH = M_HEADS * HEAD_DIM
NA_HEADS = 8
NA_WIDTH = NA_HEADS * HEAD_DIM
LRU_WIDTH = D_MODEL - M_WIDTH - NA_WIDTH
LRU_BLOCKS = 4
LRU_BLOCK = LRU_WIDTH // LRU_BLOCKS
N_GATES = 4 * M_HEADS
M_COLS = 4 * M_WIDTH + N_GATES
NA_COLS = 3 * NA_WIDTH
LRU_COLS = 2 * LRU_WIDTH
IN_COLS = M_COLS + NA_COLS + LRU_COLS
MLP_HIDDEN = 4 * D_MODEL
CHUNK = 64
WIN_R_MAX = 8
WIN_C = 16
RPB_R = 2 * WIN_R_MAX - 1
RPB_C = 2 * WIN_C - 1
CONV_W = 4
CONV_LEFT = 2
LRU_C = 8.0
ROPE_BASE = 10000.0
EPS = 1e-6

kernel_name = 'hybrid_mlstm_natten_rglru_dit_block'


def rmsnorm(x, g):
    xf = x.astype(jnp.float32)
    y = xf * lax.rsqrt(jnp.mean(xf * xf, axis=-1, keepdims=True) + EPS)
    return (y * g.astype(jnp.float32)).astype(x.dtype)


def modulate(h, shift, scale):
    return h * (1 + scale[:, None]) + shift[:, None]


def axial_rope(n):
    t = jnp.arange(n)
    row = (t // GRID_W).astype(jnp.float32)
    col = (t % GRID_W).astype(jnp.float32)
    nf = HEAD_DIM // 4
    inv = ROPE_BASE ** (-jnp.arange(nf, dtype=jnp.float32) / nf)
    ar = row[:, None] * inv
    ac = col[:, None] * inv
    ang = jnp.concatenate([ar, ar, ac, ac], axis=-1)
    return jnp.cos(ang), jnp.sin(ang)


def apply_axial_rope(x, cos, sin):
    half = HEAD_DIM // 2
    qd = half // 2
    def rot(u):
        return jnp.concatenate([-u[..., qd:], u[..., :qd]], axis=-1)
    rotated = jnp.concatenate([rot(x[..., :half]), rot(x[..., half:])], axis=-1)
    return (x * cos + rotated * sin).astype(x.dtype)


def to_heads(t, n_heads):
    b, n, _ = t.shape
    return t.reshape(b, n, n_heads, -1).transpose(0, 2, 1, 3)


def mlstm_chunk_scan(q, k, v, ig, fg, state):
    b_, h_, n_, dh = q.shape
    nc = n_ // CHUNK
    f32 = jnp.float32
    def chunks(a):
        a = a.astype(f32)
        return jnp.moveaxis(a.reshape(b_, h_, nc, CHUNK, *a.shape[3:]), 2, 0)
    xs = (chunks(q), chunks(k), chunks(v), chunks(ig), chunks(jax.nn.log_sigmoid(fg.astype(f32))))
    causal = jnp.tril(jnp.ones((CHUNK, CHUNK), dtype=bool))
    def step(carry, inp):
        C, n, m = carry
        qb, kb, vb, ib, lfb = inp
        cum = jnp.cumsum(lfb, axis=-1)
        logw = jnp.where(causal, cum[..., :, None] - cum[..., None, :] + ib[..., None, :], -jnp.inf)
        inter = cum + m[..., None]
        m_t = jnp.maximum(inter, jnp.max(logw, axis=-1))
        w_inter = jnp.exp(inter - m_t)
        s = jnp.einsum('bhld,bhsd->bhls', qb, kb) * jnp.exp(logw - m_t[..., None])
        num = w_inter[..., None] * jnp.einsum('bhde,bhle->bhld', C, qb) + jnp.einsum('bhls,bhsd->bhld', s, vb)
        den = w_inter * jnp.einsum('bhe,bhle->bhl', n, qb) + jnp.sum(s, axis=-1)
        h = num / jnp.maximum(jnp.abs(den), jnp.exp(-m_t))[..., None]
        m_new = m_t[..., -1]
        w_old = jnp.exp(cum[..., -1] + m - m_new)
        w_src = jnp.exp(cum[..., -1:] - cum + ib - m_new[..., None])
        C_new = w_old[..., None, None] * C + jnp.einsum('bhs,bhsd,bhse->bhde', w_src, vb, kb)
        n_new = w_old[..., None] * n + jnp.einsum('bhs,bhse->bhe', w_src, kb)
        return (C_new, n_new, m_new), h
    state, hs = lax.scan(step, state, xs)
    return jnp.moveaxis(hs, 0, 2).reshape(b_, h_, n_, dh), state


def mlstm_dir(q, k, v, ig, fg, state, reverse):
    if reverse:
        q, k, v, ig, fg = (jnp.flip(t, 2) for t in (q, k, v, ig, fg))
    h, st = mlstm_chunk_scan(q, k, v, ig, fg, state)
    if reverse:
        h = jnp.flip(h, 2)
    return h, st


def mlstm_mixer(u_lat, u_ctx, gate_b, norm_g, want_ctx):
    def prep(u, rope):
        b_, n_, _ = u.shape
        q, k, v, o, g = jnp.split(u, [M_WIDTH, 2 * M_WIDTH, 3 * M_WIDTH, 4 * M_WIDTH], axis=-1)
        q, k, v = to_heads(q, M_HEADS), to_heads(k, M_HEADS) * (HEAD_DIM ** -0.5), to_heads(v, M_HEADS)
        if rope:
            cos, sin = axial_rope(n_)
            q, k = apply_axial_rope(q, cos, sin), apply_axial_rope(k, cos, sin)
        g = (g + gate_b.reshape(-1)).reshape(b_, n_, 4, M_HEADS).transpose(2, 0, 3, 1)
        return q, k, v, o, g
    ql, kl, vl, ol, gl = prep(u_lat, True)
    qc, kc, vc, oc, gc = prep(u_ctx, False)
    b_ = u_lat.shape[0]
    zero = (jnp.zeros((b_, M_HEADS, HEAD_DIM, HEAD_DIM), jnp.float32),
            jnp.zeros((b_, M_HEADS, HEAD_DIM), jnp.float32),
            jnp.zeros((b_, M_HEADS), jnp.float32))
    hc_f, st_f = mlstm_dir(qc, kc, vc, gc[0], gc[1], zero, False)
    hc_b, st_b = mlstm_dir(qc, kc, vc, gc[2], gc[3], zero, True)
    hl_f, _ = mlstm_dir(ql, kl, vl, gl[0], gl[1], st_f, False)
    hl_b, _ = mlstm_dir(ql, kl, vl, gl[2], gl[3], st_b, True)
    g_head = norm_g.reshape(M_HEADS, 1, HEAD_DIM)
    def post(h, o):
        b2, _, n2, _ = h.shape
        hn = rmsnorm(h, g_head).transpose(0, 2, 1, 3).reshape(b2, n2, M_WIDTH)
        return (jax.nn.sigmoid(o.astype(jnp.float32)) * hn).astype(o.dtype)
    y_lat = post(hl_f + hl_b, ol)
    y_ctx = post(hc_f + hc_b, oc) if want_ctx else None
    return y_lat, y_ctx


def na_mixer(u_lat, u_ctx, qn_g, kn_g, rpb, want_ctx):
    b_, n_, _ = u_lat.shape
    rows = n_ // GRID_W
    wr = min(WIN_R_MAX, rows)
    wc = min(WIN_C, GRID_W)
    scale = HEAD_DIM ** -0.5
    def qkv(u):
        q, k, v = jnp.split(u, 3, axis=-1)
        sh = u.shape[:2] + (NA_HEADS, HEAD_DIM)
        return rmsnorm(q.reshape(sh), qn_g), rmsnorm(k.reshape(sh), kn_g), v.reshape(sh)
    q, k, v = qkv(u_lat)
    qc, kc, vc = qkv(u_ctx)
    grid = (b_, rows, GRID_W, NA_HEADS, HEAD_DIM)
    qg, kg, vg = q.reshape(grid), k.reshape(grid), v.reshape(grid)
    cols = np.arange(GRID_W)
    col0 = np.clip(cols - wc // 2, 0, GRID_W - wc)
    col_idx = col0[:, None] + np.arange(wc)[None, :]
    dc = col_idx - cols[:, None] + (WIN_C - 1)
    def row_block(r):
        sr = jnp.clip(r - wr // 2, 0, rows - wr)
        q_r = lax.dynamic_index_in_dim(qg, r, axis=1, keepdims=False)
        k_band = lax.dynamic_slice_in_dim(kg, sr, wr, axis=1)
        v_band = lax.dynamic_slice_in_dim(vg, sr, wr, axis=1)
        k_win = k_band[:, :, col_idx]
        v_win = v_band[:, :, col_idx]
        s_loc = jnp.einsum('bqhd,brqchd->bhqrc', q_r, k_win).astype(jnp.float32) * scale
        dr = sr + jnp.arange(wr) - r + (WIN_R_MAX - 1)
        bias = jnp.transpose(rpb[:, dr][:, :, dc], (0, 2, 1, 3))
        s_loc = (s_loc + bias.astype(jnp.float32)).reshape(b_, NA_HEADS, GRID_W, wr * wc)
        s_ctx = jnp.einsum('bqhd,bkhd->bhqk', q_r, kc).astype(jnp.float32) * scale
        p = jax.nn.softmax(jnp.concatenate([s_loc, s_ctx], axis=-1), axis=-1).astype(v.dtype)
        p_loc = p[..., :wr * wc].reshape(b_, NA_HEADS, GRID_W, wr, wc)
        p_ctx = p[..., wr * wc:]
        o = jnp.einsum('bhqrc,brqchd->bqhd', p_loc, v_win) + jnp.einsum('bhqk,bkhd->bqhd', p_ctx, vc)
        return o.astype(v.dtype)
    out = lax.map(row_block, jnp.arange(rows))
    y_lat = jnp.transpose(out, (1, 0, 2, 3, 4)).reshape(b_, n_, NA_WIDTH)
    y_ctx = None
    if want_ctx:
        s = jnp.einsum('bqhd,bkhd->bhqk', qc, kc).astype(jnp.float32) * scale
        p = jax.nn.softmax(s, axis=-1).astype(vc.dtype)
        y_ctx = jnp.einsum('bhqk,bkhd->bqhd', p, vc).reshape(b_, u_ctx.shape[1], NA_WIDTH)
    return y_lat, y_ctx


def conv_centred(x, w, b):
    n_ = x.shape[1]
    xp = jnp.pad(x, ((0, 0), (CONV_LEFT, CONV_W - 1 - CONV_LEFT), (0, 0)))
    return sum(xp[:, j:j + n_] * w[j] for j in range(CONV_W)) + b


def block_diag(x, w, b):
    xb = x.reshape(*x.shape[:-1], LRU_BLOCKS, LRU_BLOCK)
    return jnp.einsum('bnkc,kcd->bnkd', xb, w.astype(jnp.float32)).reshape(x.shape) + b.astype(jnp.float32)


def lin_combine(left, right):
    a1, b1 = left
    a2, b2 = right
    return a1 * a2, a2 * b1 + b2


def rglru_dir(x, w_a, b_a, w_x, b_x, lam, h0, reverse):
    if reverse:
        x = jnp.flip(x, 1)
    xf = x.astype(jnp.float32)
    r = jax.nn.sigmoid(block_diag(xf, w_a, b_a))
    i = jax.nn.sigmoid(block_diag(xf, w_x, b_x))
    log_a = -LRU_C * r * jax.nn.softplus(-lam.astype(jnp.float32))
    a = jnp.exp(log_a)
    bt = jnp.sqrt(-jnp.expm1(2.0 * log_a)) * (i * xf)
    bt = bt.at[:, 0].add(a[:, 0] * h0)
    _, h = lax.associative_scan(lin_combine, (a, bt), axis=1)
    h_last = h[:, -1]
    if reverse:
        h = jnp.flip(h, 1)
    return h, h_last


def lru_mixer(u_lat, u_ctx, conv_w, conv_b, w_a, b_a, w_x, b_x, lam, want_ctx):
    xr_l, gt_l = jnp.split(u_lat, 2, axis=-1)
    xr_c, gt_c = jnp.split(u_ctx, 2, axis=-1)
    xl = conv_centred(xr_l, conv_w, conv_b)
    xc = conv_centred(xr_c, conv_w, conv_b)
    h0 = jnp.zeros((u_lat.shape[0], LRU_WIDTH), jnp.float32)
    hc_f, st_f = rglru_dir(xc, w_a[0], b_a[0], w_x[0], b_x[0], lam[0], h0, False)
    hc_b, st_b = rglru_dir(xc, w_a[1], b_a[1], w_x[1], b_x[1], lam[1], h0, True)
    hl_f, _ = rglru_dir(xl, w_a[0], b_a[0], w_x[0], b_x[0], lam[0], st_f, False)
    hl_b, _ = rglru_dir(xl, w_a[1], b_a[1], w_x[1], b_x[1], lam[1], st_b, True)
    y_lat = ((hl_f + hl_b) * jax.nn.gelu(gt_l.astype(jnp.float32))).astype(u_lat.dtype)
    y_ctx = ((hc_f + hc_b) * jax.nn.gelu(gt_c.astype(jnp.float32))).astype(u_ctx.dtype) if want_ctx else None
    return y_lat, y_ctx


def sq_relu_mlp(h, w1, w2):
    return jnp.square(jax.nn.relu(h @ w1)) @ w2


def hybrid_layer(x, xc, c_act, cctx_act, w_mod, b_mod, n1, n2, w_in, m_gate_b, m_norm_g,
                 na_qn, na_kn, na_rpb, conv_w, conv_b, l_wa, l_ba, l_wx, l_bx, l_lam,
                 w_out, w1, w2, want_ctx):
    sh1, sc1, g1, sh2, sc2, g2 = jnp.split(c_act @ w_mod + b_mod, 6, axis=-1)
    csh1, csc1, cg1, csh2, csc2, cg2 = jnp.split(cctx_act @ w_mod + b_mod, 6, axis=-1)
    u = modulate(rmsnorm(x, n1), sh1, sc1) @ w_in
    uc = modulate(rmsnorm(xc, n1), csh1, csc1) @ w_in
    sl_m, sl_na = slice(0, M_COLS), slice(M_COLS, M_COLS + NA_COLS)
    sl_l = slice(M_COLS + NA_COLS, IN_COLS)
    ym, ymc = mlstm_mixer(u[..., sl_m], uc[..., sl_m], m_gate_b, m_norm_g, want_ctx)
    yn, ync = na_mixer(u[..., sl_na], uc[..., sl_na], na_qn, na_kn, na_rpb, want_ctx)
    yl, ylc = lru_mixer(u[..., sl_l], uc[..., sl_l], conv_w, conv_b, l_wa, l_ba, l_wx, l_bx, l_lam, want_ctx)
    x = x + g1[:, None] * (jnp.concatenate([ym, yn, yl], axis=-1) @ w_out)
    x = x + g2[:, None] * sq_relu_mlp(modulate(rmsnorm(x, n2), sh2, sc2), w1, w2)
    if want_ctx:
        xc = xc + cg1[:, None] * (jnp.concatenate([ymc, ync, ylc], axis=-1) @ w_out)
        xc = xc + cg2[:, None] * sq_relu_mlp(modulate(rmsnorm(xc, n2), csh2, csc2), w1, w2)
    return x, xc


def setup_inputs(seed: int = 0) -> dict:
    key = jax.random.key(seed)
    ks = jax.random.split(key, 24)
    f32 = jnp.float32
    def nrm(k, shape, fan_in, gain=1.0):
        return jax.random.normal(k, shape, f32) * (gain * fan_in ** -0.5)
    def small(k, shape, s=0.02):
        return jax.random.normal(k, shape, f32) * s
    f_bias_base = jnp.array([0.0, 1.0, 0.0, 1.0], f32)[:, None] * jnp.linspace(3.0, 6.0, M_HEADS, dtype=f32)[None, :]
    a_target = jax.random.uniform(ks[20], (DEPTH, 2, LRU_WIDTH), f32, 0.9, 0.999)
    sig_l = a_target ** (1.0 / LRU_C)
    lam = jnp.log(sig_l) - jnp.log1p(-sig_l)
    return {
        'x': jax.random.normal(ks[0], (BATCH, SEQ, D_MODEL), f32),
        'c': jax.random.normal(ks[1], (BATCH, D_MODEL), f32),
        'ctx': jax.random.normal(ks[2], (BATCH, CTX_LEN, D_MODEL), f32),
        'c_ctx': jax.random.normal(ks[3], (D_MODEL,), f32),
        'w_mod': nrm(ks[4], (DEPTH, D_MODEL, 6 * D_MODEL), D_MODEL, 0.5),
        'b_mod': small(ks[5], (DEPTH, 6 * D_MODEL)),
        'norm1_g': 1.0 + small(ks[6], (DEPTH, D_MODEL)),
        'norm2_g': 1.0 + small(ks[7], (DEPTH, D_MODEL)),
        'w_in': nrm(ks[8], (DEPTH, D_MODEL, IN_COLS), D_MODEL),
        'mlstm_gate_b': f_bias_base[None] + small(ks[9], (DEPTH, 4, M_HEADS), 0.1),
        'mlstm_norm_g': 1.0 + small(ks[10], (DEPTH, M_WIDTH)),
        'na_q_norm_g': 1.0 + small(ks[11], (DEPTH, HEAD_DIM)),
        'na_k_norm_g': 1.0 + small(ks[12], (DEPTH, HEAD_DIM)),
        'na_rpb': small(ks[13], (DEPTH, NA_HEADS, RPB_R, RPB_C), 0.1),
        'lru_conv_w': nrm(ks[14], (DEPTH, CONV_W, LRU_WIDTH), CONV_W),
        'lru_conv_b': small(ks[15], (DEPTH, LRU_WIDTH)),
        'lru_w_a': nrm(ks[16], (DEPTH, 2, LRU_BLOCKS, LRU_BLOCK, LRU_BLOCK), LRU_BLOCK),
        'lru_b_a': small(ks[17], (DEPTH, 2, LRU_WIDTH)),
        'lru_w_x': nrm(ks[18], (DEPTH, 2, LRU_BLOCKS, LRU_BLOCK, LRU_BLOCK), LRU_BLOCK),
        'lru_b_x': small(ks[19], (DEPTH, 2, LRU_WIDTH)),
        'lru_lambda': lam,
        'w_out': nrm(ks[21], (DEPTH, D_MODEL, D_MODEL), D_MODEL),
        'w_mlp1': nrm(ks[22], (DEPTH, D_MODEL, MLP_HIDDEN), D_MODEL),
        'w_mlp2': nrm(ks[23], (DEPTH, MLP_HIDDEN, D_MODEL), MLP_HIDDEN),
    }


def reference(x, c, ctx, c_ctx, w_mod, b_mod, norm1_g, norm2_g, w_in, mlstm_gate_b, mlstm_norm_g,
              na_q_norm_g, na_k_norm_g, na_rpb, lru_conv_w, lru_conv_b, lru_w_a, lru_b_a, lru_w_x,
              lru_b_x, lru_lambda, w_out, w_mlp1, w_mlp2):
    c_act = jax.nn.silu(c)
    cctx_act = jax.nn.silu(c_ctx)[None]
    xc = ctx
    for l in range(DEPTH):
        x, xc = hybrid_layer(x, xc, c_act, cctx_act, w_mod[l], b_mod[l], norm1_g[l], norm2_g[l], w_in[l],
                             mlstm_gate_b[l], mlstm_norm_g[l], na_q_norm_g[l], na_k_norm_g[l], na_rpb[l],
                             lru_conv_w[l], lru_conv_b[l], lru_w_a[l], lru_b_a[l], lru_w_x[l], lru_b_x[l],
                             lru_lambda[l], w_out[l], w_mlp1[l], w_mlp2[l], l < DEPTH - 1)
    return x
```

```python
import functools
import math

import numpy as np
import jax
import jax.numpy as jnp
from jax import lax
from jax.experimental import pallas as pl
from jax.experimental.pallas import tpu as pltpu

F32 = jnp.float32
BF16 = jnp.bfloat16

HEAD_DIM = 64
M_HEADS = 4
M_WIDTH = M_HEADS * HEAD_DIM
NA_HEADS = 8
NA_WIDTH = NA_HEADS * HEAD_DIM
N_GATES = 4 * M_HEADS
GRID_W = 64
CHUNK = 64
WIN_R = 8
WIN_C = 16
RPB_R = 2 * WIN_R - 1
RPB_C = 2 * WIN_C - 1
N_BIAS_CASES = 8
CONV_W = 4
LRU_C = 8.0
ROPE_BASE = 10000.0
EPS = 1e-6
NEG = -1e30

LANES = 128
ROW_TILE = 256
LRU_TILE = 256
LRU_HALO = 8
VMEM_LIMIT = 56 * 1024 * 1024

U_MQ, U_MK, U_MV, U_MO = 0, 256, 512, 768
U_XR, U_GT = 1024, 1280
U_NQ, U_NK, U_NV = 1536, 2048, 2560
U_G = 3072
U_COLS = 3200


def _dot(a, b):
    return jnp.dot(a, b, preferred_element_type=F32)


def _dot_nt(a, b):
    return lax.dot_general(a, b, (((1,), (1,)), ((), ())), preferred_element_type=F32)


def _dot_tn(a, b):
    return lax.dot_general(a, b, (((0,), (0,)), ((), ())), preferred_element_type=F32)


def _split3(x):
    hi = x.astype(BF16)
    r = x - hi.astype(F32)
    mid = r.astype(BF16)
    lo = (r - mid.astype(F32)).astype(BF16)
    return hi, mid, lo


def _dot_f32_rhs01(x, m01):
    hi, mid, lo = _split3(x)
    return _dot(hi, m01) + _dot(mid, m01) + _dot(lo, m01)


def _dot_f32_lhs01(m01, x):
    hi, mid, lo = _split3(x)
    return _dot(m01, hi) + _dot(m01, mid) + _dot(m01, lo)


def _log_sigmoid(x):
    return jnp.minimum(x, 0.0) - jnp.log1p(jnp.exp(-jnp.abs(x)))


def _softplus(x):
    return jnp.maximum(x, 0.0) + jnp.log1p(jnp.exp(-jnp.abs(x)))


def _params(sem):
    return pltpu.CompilerParams(dimension_semantics=sem, vmem_limit_bytes=VMEM_LIMIT)


def _resident(shape, index_map):
    return pl.BlockSpec(shape, index_map, pipeline_mode=pl.Buffered(1))


def _mod_kernel(a_ref, w_ref, b_ref, o_ref):
    a = a_ref[...]
    act = (a * jax.nn.sigmoid(a)).astype(BF16)
    o_ref[0] = _dot(act, w_ref[0].astype(BF16)) + b_ref[0]


def _modulation(cond, w_mod, b_mod):
    depth, d, n6 = w_mod.shape
    rows = cond.shape[0]
    tn = 1536
    return pl.pallas_call(
        _mod_kernel,
        grid=(depth, n6 // tn),
        in_specs=[
            pl.BlockSpec((rows, d), lambda l, j: (0, 0)),
            pl.BlockSpec((1, d, tn), lambda l, j: (l, 0, j)),
            pl.BlockSpec((1, 1, tn), lambda l, j: (l, 0, j)),
        ],
        out_specs=pl.BlockSpec((1, rows, tn), lambda l, j: (l, 0, j)),
        out_shape=jax.ShapeDtypeStruct((depth, rows, n6), F32),
        compiler_params=_params(("arbitrary", "arbitrary")),
    )(cond, w_mod, b_mod.reshape(depth, 1, n6))


def _inproj_kernel(x_ref, mod_ref, g_ref, w_ref, u_ref):
    x = x_ref[0]
    ms = jnp.mean(x * x, axis=-1, keepdims=True)
    y = x * lax.rsqrt(ms + EPS) * g_ref[...]
    mod = mod_ref[0, 0]
    h = (y * (1.0 + mod[1:2, :]) + mod[0:1, :]).astype(BF16)
    u_ref[0] = _dot(h, w_ref[...])


def _inproj(xs, modv, g1, w_in_p, n_ctx):
    b, t, d = xs.shape
    ctx_tiles = n_ctx // ROW_TILE
    return pl.pallas_call(
        _inproj_kernel,
        grid=(b, t // ROW_TILE),
        in_specs=[
            pl.BlockSpec((1, ROW_TILE, d), lambda bi, i: (bi, i, 0)),
            pl.BlockSpec((1, 1, 6, d), lambda bi, i: (bi, jnp.minimum(i // ctx_tiles, 1), 0, 0)),
            pl.BlockSpec((1, d), lambda bi, i: (0, 0)),
            _resident((d, U_COLS), lambda bi, i: (0, 0)),
        ],
        out_specs=pl.BlockSpec((1, ROW_TILE, U_COLS), lambda bi, i: (bi, i, 0)),
        out_shape=jax.ShapeDtypeStruct((b, t, U_COLS), F32),
        compiler_params=_params(("arbitrary", "arbitrary")),
    )(xs, modv, g1.reshape(1, d), w_in_p)


def _mlstm_kernel(u_ref, gc_ref, gr_ref, gbc_ref, gbr_ref, ng_ref, rrow_ref, rcol_ref, o_ref,
                  q_s, k_s, hb_s, c_s, *, n_ctx_chunks, n_chunks):
    L = CHUNK
    W = M_WIDTH
    lane = lax.broadcasted_iota(jnp.int32, (1, W), 1)
    head_masks = [(lane >= h * HEAD_DIM) & (lane < (h + 1) * HEAD_DIM) for h in range(M_HEADS)]

    def head_lanes(cols):
        out = cols[M_HEADS - 1]
        for h in range(M_HEADS - 2, -1, -1):
            out = jnp.where(lane < (h + 1) * HEAD_DIM, cols[h], out)
        return out

    def prep_ctx(j, carry):
        r0 = pl.multiple_of(j * L, L)
        q_s[pl.ds(r0, L), :] = u_ref[0, pl.ds(r0, L), U_MQ:U_MQ + W].astype(BF16)
        k_s[pl.ds(r0, L), :] = (u_ref[0, pl.ds(r0, L), U_MK:U_MK + W] * HEAD_DIM ** -0.5).astype(BF16)
        return carry

    def prep_lat(j, carry):
        r0 = pl.multiple_of((n_ctx_chunks + j) * L, L)
        cos = rrow_ref[0, pl.ds(j, 1), :] + rcol_ref[0]
        sin_a = rrow_ref[1, pl.ds(j, 1), :] + rcol_ref[1]
        sin_b = rrow_ref[2, pl.ds(j, 1), :] + rcol_ref[2]

        def rope(x):
            return x * cos + pltpu.roll(x, W - HEAD_DIM // 4, 1) * sin_a + pltpu.roll(x, HEAD_DIM // 4, 1) * sin_b

        q_s[pl.ds(r0, L), :] = rope(u_ref[0, pl.ds(r0, L), U_MQ:U_MQ + W]).astype(BF16)
        k_s[pl.ds(r0, L), :] = rope(u_ref[0, pl.ds(r0, L), U_MK:U_MK + W] * HEAD_DIM ** -0.5).astype(BF16)
        return carry

    lax.fori_loop(0, n_ctx_chunks, prep_ctx, 0)
    lax.fori_loop(0, n_chunks - n_ctx_chunks, prep_lat, 0)

    c_s[...] = jnp.zeros(c_s.shape, F32)

    ri = lax.broadcasted_iota(jnp.int32, (L, L), 0)
    ci = lax.broadcasted_iota(jnp.int32, (L, L), 1)
    ones_aug = jnp.ones((L, LANES), BF16)
    gb_c = gbc_ref[...]
    gb_r = gbr_ref[...]

    def dir_step(c, m_prev, dirn):
        if dirn == 0:
            keep = ci <= ri
            last = L - 1
        else:
            keep = ci >= ri
            last = 0
        tri = keep.astype(BF16)
        tri_t = (ri <= ci if dirn == 0 else ri >= ci).astype(BF16)
        gi = 2 * dirn * M_HEADS
        gf = gi + M_HEADS
        r0 = pl.multiple_of(c * L, L)
        q = q_s[pl.ds(r0, L), :]
        k = k_s[pl.ds(r0, L), :]
        v = u_ref[0, pl.ds(r0, L), U_MV:U_MV + W].astype(BF16)
        gc = gc_ref[0, pl.ds(r0, L), :] + gb_c
        gr = gr_ref[0, c] + gb_r
        cum_c = _dot_f32_lhs01(tri, _log_sigmoid(gc))
        cum_r = _dot_f32_rhs01(_log_sigmoid(gr), tri_t)

        cum_l = jnp.concatenate([cum_c[:, gf + h:gf + h + 1] for h in range(M_HEADS)], axis=0)
        src_r = gr[gi:gi + M_HEADS, :] - cum_r[gf:gf + M_HEADS, :]
        src = jnp.concatenate([jnp.broadcast_to(src_r[h:h + 1, :], (L, L)) for h in range(M_HEADS)], axis=0)
        keep4 = jnp.concatenate([keep] * M_HEADS, axis=0)
        logw = jnp.where(keep4, cum_l + src, NEG)
        inter = cum_l + m_prev
        m_t = jnp.maximum(inter, jnp.max(logw, axis=1, keepdims=True))
        w_inter = jnp.exp(inter - m_t)
        dmat = jnp.exp(logw - m_t)

        q_st = jnp.concatenate([jnp.where(head_masks[h], q, jnp.zeros_like(q)) for h in range(M_HEADS)], axis=0)
        v_aug = jnp.concatenate([v, ones_aug], axis=1)
        p = (_dot_nt(q_st, k) * dmat).astype(BF16)
        c_old = c_s[dirn]
        r = w_inter * _dot(q_st, c_old.astype(BF16)) + _dot(p, v_aug)
        denom = jnp.maximum(jnp.abs(r[:, W:W + 1]), jnp.exp(-m_t))
        hs = r[:, :W] / denom
        h_out = jnp.where(head_masks[0], hs[0:L], 0.0)
        for h in range(1, M_HEADS):
            h_out = h_out + jnp.where(head_masks[h], hs[h * L:(h + 1) * L], 0.0)

        m_last = [m_t[h * L + last:h * L + last + 1, :] for h in range(M_HEADS)]
        c_last = [cum_c[last:last + 1, gf + h:gf + h + 1] for h in range(M_HEADS)]
        m_new = jnp.concatenate([jnp.broadcast_to(m_last[h], (L, 1)) for h in range(M_HEADS)], axis=0)
        cl4 = jnp.concatenate([jnp.broadcast_to(c_last[h], (L, 1)) for h in range(M_HEADS)], axis=0)
        w_old = jnp.exp(cl4 + m_prev - m_new)
        w_src = [jnp.exp(c_last[h] - cum_c[:, gf + h:gf + h + 1] + gc[:, gi + h:gi + h + 1] - m_last[h])
                 for h in range(M_HEADS)]
        kw = (k.astype(F32) * head_lanes(w_src)).astype(BF16)
        c_s[dirn] = w_old * c_old + _dot_tn(kw, v_aug)
        return h_out, m_new

    def body(i, carry):
        m_f, m_b = carry
        c_f = i
        c_b = jnp.where(i < n_ctx_chunks, n_ctx_chunks - 1 - i, n_chunks - 1 + n_ctx_chunks - i)
        h_f, m_f = dir_step(c_f, m_f, 0)
        o_ref[0, pl.ds(pl.multiple_of(c_f * L, L), L), :] = h_f
        h_b, m_b = dir_step(c_b, m_b, 1)
        hb_s[pl.ds(pl.multiple_of(c_b * L, L), L), :] = h_b
        return m_f, m_b

    m0 = jnp.zeros((M_HEADS * L, 1), F32)
    lax.fori_loop(0, n_chunks, body, (m0, m0))

    hi = lax.broadcasted_iota(jnp.int32, (W, W), 0) // HEAD_DIM
    hj = lax.broadcasted_iota(jnp.int32, (W, W), 1) // HEAD_DIM
    same_head = (hi == hj).astype(BF16)
    ng = ng_ref[...]

    def post(j, carry):
        r0 = pl.multiple_of(j * L, L)
        h = o_ref[0, pl.ds(r0, L), :] + hb_s[pl.ds(r0, L), :]
        ms = _dot_f32_rhs01(h * h, same_head) * (1.0 / HEAD_DIM)
        hn = h * lax.rsqrt(ms + EPS) * ng
        o_ref[0, pl.ds(r0, L), :] = jax.nn.sigmoid(u_ref[0, pl.ds(r0, L), U_MO:U_MO + W]) * hn
        return carry

    lax.fori_loop(0, n_chunks, post, 0)


def _rope_tables(n_rows):
    nf = HEAD_DIM // 4
    inv = ROPE_BASE ** (-jnp.arange(nf, dtype=F32) / nf)
    d = np.arange(M_WIDTH) % HEAD_DIM
    f_idx = d % nf
    is_row = d < HEAD_DIM // 2
    first_quarter = (d % (HEAD_DIM // 2)) < nf

    def tables(pos, sel):
        ang = pos[:, None] * inv[None, :]
        ang = ang[:, f_idx]
        cos = jnp.where(sel, jnp.cos(ang), 0.0)
        sin = jnp.where(sel, jnp.sin(ang), 0.0)
        sin_a = jnp.where(first_quarter, -sin, 0.0)
        sin_b = jnp.where(first_quarter, 0.0, sin)
        return jnp.stack([cos, sin_a, sin_b]).astype(F32)

    row_t = tables(jnp.arange(n_rows, dtype=F32), is_row)
    col_t = tables(jnp.arange(GRID_W, dtype=F32), ~is_row)
    return row_t, col_t


def _mlstm(u, g_rows, gate_b, norm_g, rope_row, rope_col, n_ctx):
    b, t, _ = u.shape
    n_chunks = t // CHUNK
    gb = gate_b.reshape(-1)
    gbc = jnp.zeros((1, LANES), F32).at[0, :N_GATES].set(gb)
    gbr = gb.reshape(N_GATES, 1)
    kern = functools.partial(_mlstm_kernel, n_ctx_chunks=n_ctx // CHUNK, n_chunks=n_chunks)
    n_rows = rope_row.shape[1]
    return pl.pallas_call(
        kern,
        grid=(b,),
        in_specs=[
            pl.BlockSpec((1, t, 4 * M_WIDTH), lambda bi: (bi, 0, 0), pipeline_mode=pl.Buffered(1)),
            pl.BlockSpec((1, t, LANES), lambda bi: (bi, 0, U_G // LANES)),
            pl.BlockSpec((1, n_chunks, N_GATES, CHUNK), lambda bi: (bi, 0, 0, 0)),
            pl.BlockSpec((1, LANES), lambda bi: (0, 0)),
            pl.BlockSpec((N_GATES, 1), lambda bi: (0, 0)),
            pl.BlockSpec((1, M_WIDTH), lambda bi: (0, 0)),
            pl.BlockSpec((3, n_rows, M_WIDTH), lambda bi: (0, 0, 0)),
            pl.BlockSpec((3, GRID_W, M_WIDTH), lambda bi: (0, 0, 0)),
        ],
        out_specs=pl.BlockSpec((1, t, M_WIDTH), lambda bi: (bi, 0, 0)),
        out_shape=jax.ShapeDtypeStruct((b, t, M_WIDTH), F32),
        scratch_shapes=[
            pltpu.VMEM((t, M_WIDTH), BF16),
            pltpu.VMEM((t, M_WIDTH), BF16),
            pltpu.VMEM((t, M_WIDTH), F32),
            pltpu.VMEM((2, M_WIDTH, M_WIDTH + LANES), F32),
        ],
        compiler_params=_params(("arbitrary",)),
    )(u, u, g_rows, gbc, gbr, norm_g.reshape(1, M_WIDTH), rope_row, rope_col)


def _na_kernel(q_ref, k_ref, v_ref, qg_ref, kg_ref, bias_ref, o_ref, qn_s, kn_s, v_s, *, n_ctx, n_rows):
    t = q_ref.shape[1]
    lane = lax.broadcasted_iota(jnp.int32, (1, LANES), 1)
    half = [lane < HEAD_DIM, lane >= HEAD_DIM]
    hi = lax.broadcasted_iota(jnp.int32, (LANES, LANES), 0) // HEAD_DIM
    hj = lax.broadcasted_iota(jnp.int32, (LANES, LANES), 1) // HEAD_DIM
    same_head = (hi == hj).astype(BF16)
    qg = qg_ref[...] * HEAD_DIM ** -0.5
    kg = kg_ref[...]

    def prep(j, carry):
        r0 = pl.multiple_of(j * ROW_TILE, ROW_TILE)
        q = q_ref[0, pl.ds(r0, ROW_TILE), :]
        k = k_ref[0, pl.ds(r0, ROW_TILE), :]
        qms = _dot_f32_rhs01(q * q, same_head) * (1.0 / HEAD_DIM)
        kms = _dot_f32_rhs01(k * k, same_head) * (1.0 / HEAD_DIM)
        qn_s[pl.ds(r0, ROW_TILE), :] = (q * lax.rsqrt(qms + EPS) * qg).astype(BF16)
        kn_s[pl.ds(r0, ROW_TILE), :] = (k * lax.rsqrt(kms + EPS) * kg).astype(BF16)
        v_s[pl.ds(r0, ROW_TILE), :] = v_ref[0, pl.ds(r0, ROW_TILE), :].astype(BF16)
        return carry

    lax.fori_loop(0, t // ROW_TILE, prep, 0)

    def masked(x, hh):
        return jnp.where(half[hh], x, jnp.zeros_like(x))

    qc = qn_s[0:n_ctx, :]
    kc = kn_s[0:n_ctx, :]
    vc = v_s[0:n_ctx, :]
    outs = []
    for hh in range(2):
        s = _dot_nt(masked(qc, hh), kc)
        m = jnp.max(s, axis=-1, keepdims=True)
        p = jnp.exp(s - m)
        outs.append(_dot(p.astype(BF16), vc) / jnp.sum(p, axis=-1, keepdims=True))
    o_ref[0, 0:n_ctx, :] = jnp.where(half[0], outs[0], outs[1])

    band = WIN_R * GRID_W

    def row(r, carry):
        sr = jnp.clip(r - WIN_R // 2, 0, n_rows - WIN_R)
        case = jnp.where(r < WIN_R // 2, r, jnp.where(r <= n_rows - WIN_R // 2, WIN_R // 2, r - (n_rows - WIN_R)))
        q0 = pl.multiple_of(n_ctx + r * GRID_W, GRID_W)
        b0 = pl.multiple_of(n_ctx + sr * GRID_W, GRID_W)
        q_r = qn_s[pl.ds(q0, GRID_W), :]
        kb = kn_s[pl.ds(b0, band), :]
        vb = v_s[pl.ds(b0, band), :]
        kcx = kn_s[0:n_ctx, :]
        vcx = v_s[0:n_ctx, :]
        res = []
        for hh in range(2):
            qh = masked(q_r, hh)
            s_loc = _dot_nt(qh, kb) + bias_ref[0, hh, case]
            s_ctx = _dot_nt(qh, kcx)
            m = jnp.maximum(jnp.max(s_loc, axis=-1, keepdims=True), jnp.max(s_ctx, axis=-1, keepdims=True))
            p_loc = jnp.exp(s_loc - m)
            p_ctx = jnp.exp(s_ctx - m)
            den = jnp.sum(p_loc, axis=-1, keepdims=True) + jnp.sum(p_ctx, axis=-1, keepdims=True)
            res.append((_dot(p_loc.astype(BF16), vb) + _dot(p_ctx.astype(BF16), vcx)) / den)
        o_ref[0, pl.ds(q0, GRID_W), :] = jnp.where(half[0], res[0], res[1])
        return carry

    lax.fori_loop(0, n_rows, row, 0)


def _na_bias_table(rpb):
    cols = np.arange(GRID_W)
    col0 = np.clip(cols - WIN_C // 2, 0, GRID_W - WIN_C)
    valid = (cols[None, :] >= col0[:, None]) & (cols[None, :] < col0[:, None] + WIN_C)
    dc = np.clip(cols[None, :] - cols[:, None] + (WIN_C - 1), 0, RPB_C - 1)
    dr = np.arange(WIN_R)[None, :] + (WIN_R - 1) - np.arange(N_BIAS_CASES)[:, None]
    tab = rpb[:, dr[:, :, None, None], dc[None, None, :, :]]
    tab = jnp.where(valid[None, None, None], tab, NEG)
    tab = jnp.transpose(tab, (0, 1, 3, 2, 4))
    return tab.reshape(rpb.shape[0], N_BIAS_CASES, GRID_W, WIN_R * GRID_W).astype(F32)


def _na(u, qn_g, kn_g, rpb, n_ctx):
    b, t, _ = u.shape
    n_rows = (t - n_ctx) // GRID_W
    pairs = NA_HEADS // 2
    bias = _na_bias_table(rpb).reshape(pairs, 2, N_BIAS_CASES, GRID_W, WIN_R * GRID_W)
    qg = jnp.tile(qn_g, 2).reshape(1, LANES)
    kg = jnp.tile(kn_g, 2).reshape(1, LANES)
    kern = functools.partial(_na_kernel, n_ctx=n_ctx, n_rows=n_rows)
    col = lambda base: (lambda bi, hp: (bi, 0, base // LANES + hp))
    return pl.pallas_call(
        kern,
        grid=(b, pairs),
        in_specs=[
            pl.BlockSpec((1, t, LANES), col(U_NQ)),
            pl.BlockSpec((1, t, LANES), col(U_NK)),
            pl.BlockSpec((1, t, LANES), col(U_NV)),
            pl.BlockSpec((1, LANES), lambda bi, hp: (0, 0)),
            pl.BlockSpec((1, LANES), lambda bi, hp: (0, 0)),
            pl.BlockSpec((1, 2, N_BIAS_CASES, GRID_W, WIN_R * GRID_W), lambda bi, hp: (hp, 0, 0, 0, 0)),
        ],
        out_specs=pl.BlockSpec((1, t, LANES), lambda bi, hp: (bi, 0, hp)),
        out_shape=jax.ShapeDtypeStruct((b, t, NA_WIDTH), F32),
        scratch_shapes=[pltpu.VMEM((t, LANES), BF16)] * 3,
        compiler_params=_params(("arbitrary", "arbitrary")),
    )(u, u, u, qg, kg, bias)


def _lru_kernel(x_ref, g_ref, cw_ref, cb_ref, wa_ref, ba_ref, wx_ref, bx_ref, lam_ref, o_ref,
                xp_s, a_s, b_s, h_s, *, n_tiles):
    tl = LRU_TILE
    halo = LRU_HALO
    t = x_ref.shape[1]
    width = x_ref.shape[2]
    group = 8

    xp_s[0:halo, :] = jnp.zeros((halo, width), F32)
    xp_s[halo + t:halo + t + halo, :] = jnp.zeros((halo, width), F32)

    def fill(i, carry):
        r0 = pl.multiple_of(i * tl, tl)
        xp_s[pl.ds(pl.multiple_of(r0 + halo, halo), tl), :] = x_ref[0, pl.ds(r0, tl), :]
        return carry

    lax.fori_loop(0, n_tiles, fill, 0)

    row = lax.broadcasted_iota(jnp.int32, (tl, 1), 0)
    rowg = row % group
    cw = cw_ref[...]
    cb = cb_ref[...]

    def conv_tile(i):
        r0 = pl.multiple_of(i * tl, tl)
        win = xp_s[pl.ds(r0, tl + 2 * halo), :]
        n_win = tl + 2 * halo
        xm2 = pltpu.roll(win, 2, 0)[halo:halo + tl]
        xm1 = pltpu.roll(win, 1, 0)[halo:halo + tl]
        x0 = win[halo:halo + tl]
        xp1 = pltpu.roll(win, n_win - 1, 0)[halo:halo + tl]
        lo = i <= 1
        hi = i == 0
        xm2 = jnp.where(lo & (row < 2), 0.0, xm2)
        xm1 = jnp.where(lo & (row < 1), 0.0, xm1)
        xp1 = jnp.where(hi & (row == tl - 1), 0.0, xp1)
        return xm2 * cw[0:1] + xm1 * cw[1:2] + x0 * cw[2:3] + xp1 * cw[3:4] + cb

    def gates(i, d):
        xc = conv_tile(i)
        xb = xc.astype(BF16)
        r = jax.nn.sigmoid(_dot(xb, wa_ref[d]) + ba_ref[d:d + 1, :])
        ig = jax.nn.sigmoid(_dot(xb, wx_ref[d]) + bx_ref[d:d + 1, :])
        log_a = -LRU_C * r * _softplus(-lam_ref[d:d + 1, :])
        a = jnp.exp(log_a)
        bt = jnp.sqrt(-jnp.tanh(log_a) * (a * a + 1.0)) * (ig * xc)
        return a, bt

    def scan_tile(i, d, carry):
        a, bt = gates(i, d)
        for kk in (1, 2, 4):
            if d == 0:
                a_sh = pltpu.roll(a, kk, 0)
                b_sh = pltpu.roll(bt, kk, 0)
                mk = rowg >= kk
            else:
                a_sh = pltpu.roll(a, tl - kk, 0)
                b_sh = pltpu.roll(bt, tl - kk, 0)
                mk = rowg < group - kk
            bt = jnp.where(mk, a * b_sh + bt, bt)
            a = jnp.where(mk, a * a_sh, a)
        a_s[...] = a
        b_s[...] = bt
        n_groups = tl // group

        def grp(gidx, c):
            gg = gidx if d == 0 else n_groups - 1 - gidx
            g0 = pl.multiple_of(gg * group, group)
            h = b_s[pl.ds(g0, group), :] + a_s[pl.ds(g0, group), :] * c
            h_s[pl.ds(g0, group), :] = h
            return h[group - 1:group, :] if d == 0 else h[0:1, :]

        return lax.fori_loop(0, n_groups, grp, carry)

    def fwd(i, carry):
        carry = scan_tile(i, 0, carry)
        r0 = pl.multiple_of(i * tl, tl)
        o_ref[0, pl.ds(r0, tl), :] = h_s[...]
        return carry

    def bwd(i, carry):
        carry = scan_tile(i, 1, carry)
        r0 = pl.multiple_of(i * tl, tl)
        o_ref[0, pl.ds(r0, tl), :] = (o_ref[0, pl.ds(r0, tl), :] + h_s[...]) * jax.nn.gelu(g_ref[0, pl.ds(r0, tl), :])
        return carry

    h0 = jnp.zeros((1, width), F32)
    lax.fori_loop(0, n_tiles, fwd, h0)
    carry = bwd(jnp.int32(0), h0)
    lax.fori_loop(1, n_tiles, lambda j, c: bwd(n_tiles - j, c), carry)


def _block_diag_dense(w):
    nd, k, c, _ = w.shape
    eye = jnp.eye(k, dtype=w.dtype)
    return jnp.einsum('dkce,kj->dkcje', w, eye).reshape(nd, k * c, k * c)


def _lru(u, conv_w, conv_b, w_a, b_a, w_x, b_x, lam, n_ctx):
    b, t, _ = u.shape
    width = conv_w.shape[1]
    assert n_ctx == LRU_TILE and t % LRU_TILE == 0
    n_tiles = t // LRU_TILE
    wa = _block_diag_dense(w_a).astype(BF16)
    wx = _block_diag_dense(w_x).astype(BF16)
    kern = functools.partial(_lru_kernel, n_tiles=n_tiles)
    full = lambda shape: pl.BlockSpec(shape, lambda bi: (0,) * len(shape))
    return pl.pallas_call(
        kern,
        grid=(b,),
        in_specs=[
            pl.BlockSpec((1, t, width), lambda bi: (bi, 0, U_XR // width)),
            pl.BlockSpec((1, t, width), lambda bi: (bi, 0, U_GT // width)),
            full((CONV_W, width)),
            full((1, width)),
            full((2, width, width)),
            full((2, width)),
            full((2, width, width)),
            full((2, width)),
            full((2, width)),
        ],
        out_specs=pl.BlockSpec((1, t, width), lambda bi: (bi, 0, 0)),
        out_shape=jax.ShapeDtypeStruct((b, t, width), F32),
        scratch_shapes=[
            pltpu.VMEM((t + 2 * LRU_HALO, width), F32),
            pltpu.VMEM((LRU_TILE, width), F32),
            pltpu.VMEM((LRU_TILE, width), F32),
            pltpu.VMEM((LRU_TILE, width), F32),
        ],
        compiler_params=_params(("arbitrary",)),
    )(u, u, conv_w, conv_b.reshape(1, width), wa, b_a, wx, b_x, lam)


def _out_mlp_kernel(x_ref, ym_ref, yn_ref, yl_ref, mod_ref, g_ref, wo_ref, w1_ref, w2_ref, o_ref, *, hid_tile):
    mod = mod_ref[0, 0]
    att = _dot(ym_ref[0].astype(BF16), wo_ref[0:M_WIDTH, :])
    att = att + _dot(yn_ref[0].astype(BF16), wo_ref[M_WIDTH:M_WIDTH + NA_WIDTH, :])
    att = att + _dot(yl_ref[0].astype(BF16), wo_ref[M_WIDTH + NA_WIDTH:, :])
    x1 = x_ref[0] + mod[2:3, :] * att
    ms = jnp.mean(x1 * x1, axis=-1, keepdims=True)
    y = x1 * lax.rsqrt(ms + EPS) * g_ref[...]
    h = (y * (1.0 + mod[4:5, :]) + mod[3:4, :]).astype(BF16)
    hidden = w1_ref.shape[1]
    acc = None
    for j in range(hidden // hid_tile):
        a = jnp.maximum(_dot(h, w1_ref[:, j * hid_tile:(j + 1) * hid_tile]), 0.0)
        part = _dot((a * a).astype(BF16), w2_ref[j * hid_tile:(j + 1) * hid_tile, :])
        acc = part if acc is None else acc + part
    o_ref[0] = x1 + mod[5:6, :] * acc


def _out_mlp(xs, ym, yn, yl, modv, g2, w_out, w1, w2, n_ctx):
    b, t, d = xs.shape
    hidden = w1.shape[1]
    ctx_tiles = n_ctx // ROW_TILE
    tok = lambda w: pl.BlockSpec((1, ROW_TILE, w), lambda bi, i: (bi, i, 0))
    kern = functools.partial(_out_mlp_kernel, hid_tile=1024)
    return pl.pallas_call(
        kern,
        grid=(b, t // ROW_TILE),
        in_specs=[
            tok(d), tok(M_WIDTH), tok(NA_WIDTH), tok(d - M_WIDTH - NA_WIDTH),
            pl.BlockSpec((1, 1, 6, d), lambda bi, i: (bi, jnp.minimum(i // ctx_tiles, 1), 0, 0)),
            pl.BlockSpec((1, d), lambda bi, i: (0, 0)),
            _resident((d, d), lambda bi, i: (0, 0)),
            _resident((d, hidden), lambda bi, i: (0, 0)),
            _resident((hidden, d), lambda bi, i: (0, 0)),
        ],
        out_specs=tok(d),
        out_shape=jax.ShapeDtypeStruct((b, t, d), F32),
        compiler_params=_params(("arbitrary", "arbitrary")),
    )(xs, ym, yn, yl, modv, g2.reshape(1, d), w_out, w1, w2)


def _permute_w_in(w_in):
    m_end = 4 * M_WIDTH
    na_end = m_end + N_GATES + 3 * NA_WIDTH
    parts = [
        w_in[:, :m_end],
        w_in[:, na_end:],
        w_in[:, m_end + N_GATES:na_end],
        w_in[:, m_end:m_end + N_GATES],
        jnp.zeros((w_in.shape[0], U_COLS - U_G - N_GATES), w_in.dtype),
    ]
    return jnp.concatenate(parts, axis=1).astype(BF16)


def kernel(x, c, ctx, c_ctx, w_mod, b_mod, norm1_g, norm2_g, w_in, mlstm_gate_b, mlstm_norm_g, na_q_norm_g, na_k_norm_g, na_rpb, lru_conv_w, lru_conv_b, lru_w_a, lru_b_a, lru_w_x, lru_b_x, lru_lambda, w_out, w_mlp1, w_mlp2):
    b, n, d = x.shape
    n_ctx = ctx.shape[1]
    t = n_ctx + n
    depth = w_in.shape[0]
    assert n_ctx == ROW_TILE and n % ROW_TILE == 0 and n % GRID_W == 0 and GRID_W == CHUNK

    xs = jnp.concatenate([ctx, x], axis=1)
    cond_rows = 8 * ((b + 1 + 7) // 8)
    cond = jnp.zeros((cond_rows, d), F32).at[:b].set(c).at[b].set(c_ctx)
    mod_all = _modulation(cond, w_mod, b_mod)
    rope_row, rope_col = _rope_tables(n // GRID_W)
    n_chunks = t // CHUNK

    for l in range(depth):
        lat = mod_all[l, :b].reshape(b, 6, d)
        cx = jnp.broadcast_to(mod_all[l, b].reshape(1, 6, d), (b, 6, d))
        modv = jnp.stack([cx, lat], axis=1)
        u = _inproj(xs, modv, norm1_g[l], _permute_w_in(w_in[l]), n_ctx)
        g_rows = jnp.swapaxes(u[:, :, U_G:U_G + N_GATES].reshape(b, n_chunks, CHUNK, N_GATES), 2, 3)
        ym = _mlstm(u, g_rows, mlstm_gate_b[l], mlstm_norm_g[l], rope_row, rope_col, n_ctx)
        yn = _na(u, na_q_norm_g[l], na_k_norm_g[l], na_rpb[l], n_ctx)
        yl = _lru(u, lru_conv_w[l], lru_conv_b[l], lru_w_a[l], lru_b_a[l], lru_w_x[l], lru_b_x[l], lru_lambda[l], n_ctx)
        xs = _out_mlp(xs, ym, yn, yl, modv, norm2_g[l], w_out[l].astype(BF16), w_mlp1[l].astype(BF16),
                      w_mlp2[l].astype(BF16), n_ctx)
    return xs[:, n_ctx:, :]
```

```python
import functools
import math

import numpy as np
import jax
import jax.numpy as jnp
from jax import lax
from jax.experimental import pallas as pl
from jax.experimental.pallas import tpu as pltpu

F32 = jnp.float32
BF16 = jnp.bfloat16

HEAD_DIM = 64
M_HEADS = 4
M_WIDTH = M_HEADS * HEAD_DIM
NA_HEADS = 8
NA_WIDTH = NA_HEADS * HEAD_DIM
N_GATES = 4 * M_HEADS
GRID_W = 64
CHUNK = 64
WIN_R = 8
WIN_C = 16
RPB_R = 2 * WIN_R - 1
RPB_C = 2 * WIN_C - 1
NA_QROWS = 4
NA_BAND = NA_QROWS + WIN_R
NA_KEY_CHUNK = 128
NA_ONES_ROWS = 16
LOG2E = 1.4426950408889634
CONV_W = 4
LRU_C = 8.0
ROPE_BASE = 10000.0
EPS = 1e-6
NEG = -1e30

LANES = 128
ROW_TILE = 256
LRU_TILE = 256
LRU_HALO = 8
VMEM_LIMIT = 56 * 1024 * 1024

U_MQ, U_MK, U_MV, U_MO = 0, 256, 512, 768
U_XR, U_GT = 1024, 1280
U_NQ, U_NK, U_NV = 1536, 2048, 2560
U_G = 3072
U_COLS = 3200


def _dot(a, b):
    return jnp.dot(a, b, preferred_element_type=F32)


def _dot_nt(a, b):
    return lax.dot_general(a, b, (((1,), (1,)), ((), ())), preferred_element_type=F32)


def _dot_tn(a, b):
    return lax.dot_general(a, b, (((0,), (0,)), ((), ())), preferred_element_type=F32)


def _split3(x):
    hi = x.astype(BF16)
    r = x - hi.astype(F32)
    mid = r.astype(BF16)
    lo = (r - mid.astype(F32)).astype(BF16)
    return hi, mid, lo


def _dot_f32_rhs01(x, m01):
    hi, mid, lo = _split3(x)
    return _dot(hi, m01) + _dot(mid, m01) + _dot(lo, m01)


def _dot_f32_lhs01(m01, x):
    hi, mid, lo = _split3(x)
    return _dot(m01, hi) + _dot(m01, mid) + _dot(m01, lo)


def _log_sigmoid(x):
    return jnp.minimum(x, 0.0) - jnp.log1p(jnp.exp(-jnp.abs(x)))


def _softplus(x):
    return jnp.maximum(x, 0.0) + jnp.log1p(jnp.exp(-jnp.abs(x)))


def _params(sem):
    return pltpu.CompilerParams(dimension_semantics=sem, vmem_limit_bytes=VMEM_LIMIT)


def _resident(shape, index_map):
    return pl.BlockSpec(shape, index_map, pipeline_mode=pl.Buffered(1))


def _mod_kernel(a_ref, w_ref, b_ref, o_ref):
    a = a_ref[...]
    act = (a * jax.nn.sigmoid(a)).astype(BF16)
    o_ref[0] = _dot(act, w_ref[0].astype(BF16)) + b_ref[0]


def _modulation(cond, w_mod, b_mod):
    depth, d, n6 = w_mod.shape
    rows = cond.shape[0]
    tn = 1536
    return pl.pallas_call(
        _mod_kernel,
        grid=(depth, n6 // tn),
        in_specs=[
            pl.BlockSpec((rows, d), lambda l, j: (0, 0)),
            pl.BlockSpec((1, d, tn), lambda l, j: (l, 0, j)),
            pl.BlockSpec((1, 1, tn), lambda l, j: (l, 0, j)),
        ],
        out_specs=pl.BlockSpec((1, rows, tn), lambda l, j: (l, 0, j)),
        out_shape=jax.ShapeDtypeStruct((depth, rows, n6), F32),
        compiler_params=_params(("arbitrary", "arbitrary")),
    )(cond, w_mod, b_mod.reshape(depth, 1, n6))


def _inproj_kernel(x_ref, mod_ref, g_ref, w_ref, u_ref):
    x = x_ref[0]
    ms = jnp.mean(x * x, axis=-1, keepdims=True)
    y = x * lax.rsqrt(ms + EPS) * g_ref[...]
    mod = mod_ref[0, 0]
    h = (y * (1.0 + mod[1:2, :]) + mod[0:1, :]).astype(BF16)
    u_ref[0] = _dot(h, w_ref[...])


def _inproj(xs, modv, g1, w_in_p, n_ctx):
    b, t, d = xs.shape
    ctx_tiles = n_ctx // ROW_TILE
    return pl.pallas_call(
        _inproj_kernel,
        grid=(b, t // ROW_TILE),
        in_specs=[
            pl.BlockSpec((1, ROW_TILE, d), lambda bi, i: (bi, i, 0)),
            pl.BlockSpec((1, 1, 6, d), lambda bi, i: (bi, jnp.minimum(i // ctx_tiles, 1), 0, 0)),
            pl.BlockSpec((1, d), lambda bi, i: (0, 0)),
            _resident((d, U_COLS), lambda bi, i: (0, 0)),
        ],
        out_specs=pl.BlockSpec((1, ROW_TILE, U_COLS), lambda bi, i: (bi, i, 0)),
        out_shape=jax.ShapeDtypeStruct((b, t, U_COLS), F32),
        compiler_params=_params(("arbitrary", "arbitrary")),
    )(xs, modv, g1.reshape(1, d), w_in_p)


def _mlstm_kernel(u_ref, gc_ref, gr_ref, gbc_ref, gbr_ref, ng_ref, rrow_ref, rcol_ref, o_ref,
                  q_s, k_s, hb_s, c_s, *, n_ctx_chunks, n_chunks):
    L = CHUNK
    W = M_WIDTH
    lane = lax.broadcasted_iota(jnp.int32, (1, W), 1)
    head_masks = [(lane >= h * HEAD_DIM) & (lane < (h + 1) * HEAD_DIM) for h in range(M_HEADS)]

    def head_lanes(cols):
        out = cols[M_HEADS - 1]
        for h in range(M_HEADS - 2, -1, -1):
            out = jnp.where(lane < (h + 1) * HEAD_DIM, cols[h], out)
        return out

    def prep_ctx(j, carry):
        r0 = pl.multiple_of(j * L, L)
        q_s[pl.ds(r0, L), :] = u_ref[0, pl.ds(r0, L), U_MQ:U_MQ + W].astype(BF16)
        k_s[pl.ds(r0, L), :] = (u_ref[0, pl.ds(r0, L), U_MK:U_MK + W] * HEAD_DIM ** -0.5).astype(BF16)
        return carry

    def prep_lat(j, carry):
        r0 = pl.multiple_of((n_ctx_chunks + j) * L, L)
        cos = rrow_ref[0, pl.ds(j, 1), :] + rcol_ref[0]
        sin_a = rrow_ref[1, pl.ds(j, 1), :] + rcol_ref[1]
        sin_b = rrow_ref[2, pl.ds(j, 1), :] + rcol_ref[2]

        def rope(x):
            return x * cos + pltpu.roll(x, W - HEAD_DIM // 4, 1) * sin_a + pltpu.roll(x, HEAD_DIM // 4, 1) * sin_b

        q_s[pl.ds(r0, L), :] = rope(u_ref[0, pl.ds(r0, L), U_MQ:U_MQ + W]).astype(BF16)
        k_s[pl.ds(r0, L), :] = rope(u_ref[0, pl.ds(r0, L), U_MK:U_MK + W] * HEAD_DIM ** -0.5).astype(BF16)
        return carry

    lax.fori_loop(0, n_ctx_chunks, prep_ctx, 0)
    lax.fori_loop(0, n_chunks - n_ctx_chunks, prep_lat, 0)

    c_s[...] = jnp.zeros(c_s.shape, F32)

    ri = lax.broadcasted_iota(jnp.int32, (L, L), 0)
    ci = lax.broadcasted_iota(jnp.int32, (L, L), 1)
    ones_aug = jnp.ones((L, LANES), BF16)
    gb_c = gbc_ref[...]
    gb_r = gbr_ref[...]

    def dir_step(c, m_prev, dirn):
        if dirn == 0:
            keep = ci <= ri
            last = L - 1
        else:
            keep = ci >= ri
            last = 0
        tri = keep.astype(BF16)
        tri_t = (ri <= ci if dirn == 0 else ri >= ci).astype(BF16)
        gi = 2 * dirn * M_HEADS
        gf = gi + M_HEADS
        r0 = pl.multiple_of(c * L, L)
        q = q_s[pl.ds(r0, L), :]
        k = k_s[pl.ds(r0, L), :]
        v = u_ref[0, pl.ds(r0, L), U_MV:U_MV + W].astype(BF16)
        gc = gc_ref[0, pl.ds(r0, L), :] + gb_c
        gr = gr_ref[0, c] + gb_r
        cum_c = _dot_f32_lhs01(tri, _log_sigmoid(gc))
        cum_r = _dot_f32_rhs01(_log_sigmoid(gr), tri_t)

        cum_l = jnp.concatenate([cum_c[:, gf + h:gf + h + 1] for h in range(M_HEADS)], axis=0)
        src_r = gr[gi:gi + M_HEADS, :] - cum_r[gf:gf + M_HEADS, :]
        src = jnp.concatenate([jnp.broadcast_to(src_r[h:h + 1, :], (L, L)) for h in range(M_HEADS)], axis=0)
        keep4 = jnp.concatenate([keep] * M_HEADS, axis=0)
        logw = jnp.where(keep4, cum_l + src, NEG)
        inter = cum_l + m_prev
        m_t = jnp.maximum(inter, jnp.max(logw, axis=1, keepdims=True))
        w_inter = jnp.exp(inter - m_t)
        dmat = jnp.exp(logw - m_t)

        q_st = jnp.concatenate([jnp.where(head_masks[h], q, jnp.zeros_like(q)) for h in range(M_HEADS)], axis=0)
        v_aug = jnp.concatenate([v, ones_aug], axis=1)
        p = (_dot_nt(q_st, k) * dmat).astype(BF16)
        c_old = c_s[dirn]
        r = w_inter * _dot(q_st, c_old.astype(BF16)) + _dot(p, v_aug)
        denom = jnp.maximum(jnp.abs(r[:, W:W + 1]), jnp.exp(-m_t))
        hs = r[:, :W] / denom
        h_out = jnp.where(head_masks[0], hs[0:L], 0.0)
        for h in range(1, M_HEADS):
            h_out = h_out + jnp.where(head_masks[h], hs[h * L:(h + 1) * L], 0.0)

        m_last = [m_t[h * L + last:h * L + last + 1, :] for h in range(M_HEADS)]
        c_last = [cum_c[last:last + 1, gf + h:gf + h + 1] for h in range(M_HEADS)]
        m_new = jnp.concatenate([jnp.broadcast_to(m_last[h], (L, 1)) for h in range(M_HEADS)], axis=0)
        cl4 = jnp.concatenate([jnp.broadcast_to(c_last[h], (L, 1)) for h in range(M_HEADS)], axis=0)
        w_old = jnp.exp(cl4 + m_prev - m_new)
        w_src = [jnp.exp(c_last[h] - cum_c[:, gf + h:gf + h + 1] + gc[:, gi + h:gi + h + 1] - m_last[h])
                 for h in range(M_HEADS)]
        kw = (k.astype(F32) * head_lanes(w_src)).astype(BF16)
        c_s[dirn] = w_old * c_old + _dot_tn(kw, v_aug)
        return h_out, m_new

    def body(i, carry):
        m_f, m_b = carry
        c_f = i
        c_b = jnp.where(i < n_ctx_chunks, n_ctx_chunks - 1 - i, n_chunks - 1 + n_ctx_chunks - i)
        h_f, m_f = dir_step(c_f, m_f, 0)
        o_ref[0, pl.ds(pl.multiple_of(c_f * L, L), L), :] = h_f
        h_b, m_b = dir_step(c_b, m_b, 1)
        hb_s[pl.ds(pl.multiple_of(c_b * L, L), L), :] = h_b
        return m_f, m_b

    m0 = jnp.zeros((M_HEADS * L, 1), F32)
    lax.fori_loop(0, n_chunks, body, (m0, m0))

    hi = lax.broadcasted_iota(jnp.int32, (W, W), 0) // HEAD_DIM
    hj = lax.broadcasted_iota(jnp.int32, (W, W), 1) // HEAD_DIM
    same_head = (hi == hj).astype(BF16)
    ng = ng_ref[...]

    def post(j, carry):
        r0 = pl.multiple_of(j * L, L)
        h = o_ref[0, pl.ds(r0, L), :] + hb_s[pl.ds(r0, L), :]
        ms = _dot_f32_rhs01(h * h, same_head) * (1.0 / HEAD_DIM)
        hn = h * lax.rsqrt(ms + EPS) * ng
        o_ref[0, pl.ds(r0, L), :] = jax.nn.sigmoid(u_ref[0, pl.ds(r0, L), U_MO:U_MO + W]) * hn
        return carry

    lax.fori_loop(0, n_chunks, post, 0)


def _rope_tables(n_rows):
    nf = HEAD_DIM // 4
    inv = ROPE_BASE ** (-jnp.arange(nf, dtype=F32) / nf)
    d = np.arange(M_WIDTH) % HEAD_DIM
    f_idx = d % nf
    is_row = d < HEAD_DIM // 2
    first_quarter = (d % (HEAD_DIM // 2)) < nf

    def tables(pos, sel):
        ang = pos[:, None] * inv[None, :]
        ang = ang[:, f_idx]
        cos = jnp.where(sel, jnp.cos(ang), 0.0)
        sin = jnp.where(sel, jnp.sin(ang), 0.0)
        sin_a = jnp.where(first_quarter, -sin, 0.0)
        sin_b = jnp.where(first_quarter, 0.0, sin)
        return jnp.stack([cos, sin_a, sin_b]).astype(F32)

    row_t = tables(jnp.arange(n_rows, dtype=F32), is_row)
    col_t = tables(jnp.arange(GRID_W, dtype=F32), ~is_row)
    return row_t, col_t


def _mlstm(u, g_rows, gate_b, norm_g, rope_row, rope_col, n_ctx):
    b, t, _ = u.shape
    n_chunks = t // CHUNK
    gb = gate_b.reshape(-1)
    gbc = jnp.zeros((1, LANES), F32).at[0, :N_GATES].set(gb)
    gbr = gb.reshape(N_GATES, 1)
    kern = functools.partial(_mlstm_kernel, n_ctx_chunks=n_ctx // CHUNK, n_chunks=n_chunks)
    n_rows = rope_row.shape[1]
    return pl.pallas_call(
        kern,
        grid=(b,),
        in_specs=[
            pl.BlockSpec((1, t, 4 * M_WIDTH), lambda bi: (bi, 0, 0), pipeline_mode=pl.Buffered(1)),
            pl.BlockSpec((1, t, LANES), lambda bi: (bi, 0, U_G // LANES)),
            pl.BlockSpec((1, n_chunks, N_GATES, CHUNK), lambda bi: (bi, 0, 0, 0)),
            pl.BlockSpec((1, LANES), lambda bi: (0, 0)),
            pl.BlockSpec((N_GATES, 1), lambda bi: (0, 0)),
            pl.BlockSpec((1, M_WIDTH), lambda bi: (0, 0)),
            pl.BlockSpec((3, n_rows, M_WIDTH), lambda bi: (0, 0, 0)),
            pl.BlockSpec((3, GRID_W, M_WIDTH), lambda bi: (0, 0, 0)),
        ],
        out_specs=pl.BlockSpec((1, t, M_WIDTH), lambda bi: (bi, 0, 0)),
        out_shape=jax.ShapeDtypeStruct((b, t, M_WIDTH), F32),
        scratch_shapes=[
            pltpu.VMEM((t, M_WIDTH), BF16),
            pltpu.VMEM((t, M_WIDTH), BF16),
            pltpu.VMEM((t, M_WIDTH), F32),
            pltpu.VMEM((2, M_WIDTH, M_WIDTH + LANES), F32),
        ],
        compiler_params=_params(("arbitrary",)),
    )(u, u, g_rows, gbc, gbr, norm_g.reshape(1, M_WIDTH), rope_row, rope_col)


def _na_kernel(q_ref, k_ref, v_ref, qg_ref, kg_ref, bias_ref, o_ref, qt_s, kz_s, vt_s, s_s, *, n_ctx, n_rows):
    t = q_ref.shape[1]
    lane = lax.broadcasted_iota(jnp.int32, (1, LANES), 1)
    half = [lane < HEAD_DIM, lane >= HEAD_DIM]
    hi = lax.broadcasted_iota(jnp.int32, (LANES, LANES), 0) // HEAD_DIM
    hj = lax.broadcasted_iota(jnp.int32, (LANES, LANES), 1) // HEAD_DIM
    same_head = (hi == hj).astype(BF16)
    qg = qg_ref[...] * (HEAD_DIM ** -0.5 * LOG2E)
    kg = kg_ref[...]
    qblk = NA_QROWS * GRID_W
    kc = NA_KEY_CHUNK
    sub = 8

    for hh in range(2):
        vt_s[hh, HEAD_DIM:, :] = jnp.ones((vt_s.shape[1] - HEAD_DIM, t), BF16)

    def prep(j, carry):
        r0 = pl.multiple_of(j * ROW_TILE, ROW_TILE)
        q = q_ref[0, pl.ds(r0, ROW_TILE), :]
        k = k_ref[0, pl.ds(r0, ROW_TILE), :]
        qms = _dot_f32_rhs01(q * q, same_head) * (1.0 / HEAD_DIM)
        kms = _dot_f32_rhs01(k * k, same_head) * (1.0 / HEAD_DIM)
        qn = q * lax.rsqrt(qms + EPS) * qg
        kn = k * lax.rsqrt(kms + EPS) * kg
        qt_s[:, pl.ds(r0, ROW_TILE)] = qn.T.astype(BF16)
        vt = v_ref[0, pl.ds(r0, ROW_TILE), :].T
        for hh in range(2):
            kz_s[hh, pl.ds(r0, ROW_TILE), :] = jnp.where(half[hh], kn, 0.0).astype(BF16)
            vt_s[hh, 0:HEAD_DIM, pl.ds(r0, ROW_TILE)] = vt[hh * HEAD_DIM:(hh + 1) * HEAD_DIM].astype(BF16)
        return carry

    lax.fori_loop(0, t // ROW_TILE, prep, 0)

    def attend(q0, chunks):
        qt = qt_s[:, pl.ds(q0, qblk)]
        outs = []
        for hh in range(2):
            mx = None
            for ci, (k0, tab) in enumerate(chunks):
                s = _dot(kz_s[hh, pl.ds(k0, kc), :], qt)
                if tab is not None:
                    s = s + bias_ref[0, hh, tab[0], tab[1]:tab[1] + kc, :]
                s_s[hh, ci * kc:(ci + 1) * kc, :] = s
                cm = functools.reduce(jnp.maximum, [s[i * sub:(i + 1) * sub] for i in range(kc // sub)])
                mx = cm if mx is None else jnp.maximum(mx, cm)
            m = mx.max(axis=0, keepdims=True)
            acc = None
            for ci, (k0, _) in enumerate(chunks):
                p = jnp.exp2(s_s[hh, ci * kc:(ci + 1) * kc, :] - m).astype(BF16)
                pv = _dot(vt_s[hh, :, pl.ds(k0, kc)], p)
                acc = pv if acc is None else acc + pv
            outs.append(acc[0:HEAD_DIM] / acc[HEAD_DIM:HEAD_DIM + 1])
        return jnp.concatenate(outs, axis=0).T

    ctx_chunks = [(ci * kc, None) for ci in range(n_ctx // kc)]
    o_ref[0, 0:n_ctx, :] = attend(0, ctx_chunks)

    n_blocks = n_rows // NA_QROWS

    def block(bi, carry):
        r0 = bi * NA_QROWS
        sb = jnp.clip(r0 - WIN_R // 2, 0, n_rows - NA_BAND)
        case = jnp.where(bi == 0, 0, jnp.where(bi == n_blocks - 1, 2, 1))
        q0 = pl.multiple_of(n_ctx + r0 * GRID_W, qblk)
        k0 = pl.multiple_of(n_ctx + sb * GRID_W, qblk)
        local = [(pl.multiple_of(k0 + ci * kc, kc), (case, ci * kc)) for ci in range(NA_BAND * GRID_W // kc)]
        o_ref[0, pl.ds(q0, qblk), :] = attend(q0, local + ctx_chunks)
        return carry

    lax.fori_loop(0, n_blocks, block, 0)


def _na_bias_table(rpb, n_rows):
    n_heads = rpb.shape[0]
    cols = np.arange(GRID_W)
    col0 = np.clip(cols - WIN_C // 2, 0, GRID_W - WIN_C)
    valid = (cols[:, None] >= col0[None, :]) & (cols[:, None] < col0[None, :] + WIN_C)
    pad = GRID_W - WIN_C
    rp = jnp.pad(rpb, ((0, 0), (0, 0), (pad, pad)))
    toep = jnp.stack([rp[:, :, GRID_W - 1 - c:2 * GRID_W - 1 - c] for c in range(GRID_W)], axis=-1)
    toep = jnp.where(valid, toep, NEG)
    blank = jnp.full((n_heads, GRID_W, GRID_W), NEG, F32)
    last_off = NA_BAND - WIN_R
    cases = []
    for case in range(3):
        rows = []
        for jj in range(NA_BAND):
            tiles = []
            for i in range(NA_QROWS):
                off = (0, i, last_off)[case]
                rel = (jj - i, jj - i - WIN_R // 2, jj - i - NA_BAND + NA_QROWS)[case]
                inside = off <= jj < off + WIN_R
                tiles.append(toep[:, rel + WIN_R - 1] if inside else blank)
            rows.append(jnp.concatenate(tiles, axis=-1))
        cases.append(jnp.concatenate(rows, axis=1))
    return (jnp.stack(cases, axis=1) * LOG2E).astype(F32)


def _na(u, qn_g, kn_g, rpb, n_ctx):
    b, t, _ = u.shape
    n_rows = (t - n_ctx) // GRID_W
    assert n_rows % NA_QROWS == 0 and n_rows >= NA_BAND and n_ctx == NA_QROWS * GRID_W
    pairs = NA_HEADS // 2
    nk, nq = NA_BAND * GRID_W, NA_QROWS * GRID_W
    bias = _na_bias_table(rpb, n_rows).reshape(pairs, 2, 3, nk, nq)
    qg = jnp.tile(qn_g, 2).reshape(1, LANES)
    kg = jnp.tile(kn_g, 2).reshape(1, LANES)
    kern = functools.partial(_na_kernel, n_ctx=n_ctx, n_rows=n_rows)
    col = lambda base: (lambda hp, bi: (bi, 0, base // LANES + hp))
    return pl.pallas_call(
        kern,
        grid=(pairs, b),
        in_specs=[
            pl.BlockSpec((1, t, LANES), col(U_NQ)),
            pl.BlockSpec((1, t, LANES), col(U_NK)),
            pl.BlockSpec((1, t, LANES), col(U_NV)),
            pl.BlockSpec((1, LANES), lambda hp, bi: (0, 0)),
            pl.BlockSpec((1, LANES), lambda hp, bi: (0, 0)),
            pl.BlockSpec((1, 2, 3, nk, nq), lambda hp, bi: (hp, 0, 0, 0, 0)),
        ],
        out_specs=pl.BlockSpec((1, t, LANES), lambda hp, bi: (bi, 0, hp)),
        out_shape=jax.ShapeDtypeStruct((b, t, NA_WIDTH), F32),
        scratch_shapes=[
            pltpu.VMEM((LANES, t), BF16),
            pltpu.VMEM((2, t, LANES), BF16),
            pltpu.VMEM((2, HEAD_DIM + NA_ONES_ROWS, t), BF16),
            pltpu.VMEM((2, nk + n_ctx, nq), F32),
        ],
        compiler_params=_params(("arbitrary", "arbitrary")),
    )(u, u, u, qg, kg, bias)


def _lru_kernel(x_ref, g_ref, cw_ref, cb_ref, wa_ref, ba_ref, wx_ref, bx_ref, lam_ref, o_ref,
                xp_s, a_s, b_s, h_s, *, n_tiles):
    tl = LRU_TILE
    halo = LRU_HALO
    t = x_ref.shape[1]
    width = x_ref.shape[2]
    group = 8

    xp_s[0:halo, :] = jnp.zeros((halo, width), F32)
    xp_s[halo + t:halo + t + halo, :] = jnp.zeros((halo, width), F32)

    def fill(i, carry):
        r0 = pl.multiple_of(i * tl, tl)
        xp_s[pl.ds(pl.multiple_of(r0 + halo, halo), tl), :] = x_ref[0, pl.ds(r0, tl), :]
        return carry

    lax.fori_loop(0, n_tiles, fill, 0)

    row = lax.broadcasted_iota(jnp.int32, (tl, 1), 0)
    rowg = row % group
    cw = cw_ref[...]
    cb = cb_ref[...]

    def conv_tile(i):
        r0 = pl.multiple_of(i * tl, tl)
        win = xp_s[pl.ds(r0, tl + 2 * halo), :]
        n_win = tl + 2 * halo
        xm2 = pltpu.roll(win, 2, 0)[halo:halo + tl]
        xm1 = pltpu.roll(win, 1, 0)[halo:halo + tl]
        x0 = win[halo:halo + tl]
        xp1 = pltpu.roll(win, n_win - 1, 0)[halo:halo + tl]
        lo = i <= 1
        hi = i == 0
        xm2 = jnp.where(lo & (row < 2), 0.0, xm2)
        xm1 = jnp.where(lo & (row < 1), 0.0, xm1)
        xp1 = jnp.where(hi & (row == tl - 1), 0.0, xp1)
        return xm2 * cw[0:1] + xm1 * cw[1:2] + x0 * cw[2:3] + xp1 * cw[3:4] + cb

    def gates(i, d):
        xc = conv_tile(i)
        xb = xc.astype(BF16)
        r = jax.nn.sigmoid(_dot(xb, wa_ref[d]) + ba_ref[d:d + 1, :])
        ig = jax.nn.sigmoid(_dot(xb, wx_ref[d]) + bx_ref[d:d + 1, :])
        log_a = -LRU_C * r * _softplus(-lam_ref[d:d + 1, :])
        a = jnp.exp(log_a)
        bt = jnp.sqrt(-jnp.tanh(log_a) * (a * a + 1.0)) * (ig * xc)
        return a, bt

    def scan_tile(i, d, carry):
        a, bt = gates(i, d)
        for kk in (1, 2, 4):
            if d == 0:
                a_sh = pltpu.roll(a, kk, 0)
                b_sh = pltpu.roll(bt, kk, 0)
                mk = rowg >= kk
            else:
                a_sh = pltpu.roll(a, tl - kk, 0)
                b_sh = pltpu.roll(bt, tl - kk, 0)
                mk = rowg < group - kk
            bt = jnp.where(mk, a * b_sh + bt, bt)
            a = jnp.where(mk, a * a_sh, a)
        a_s[...] = a
        b_s[...] = bt
        n_groups = tl // group

        def grp(gidx, c):
            gg = gidx if d == 0 else n_groups - 1 - gidx
            g0 = pl.multiple_of(gg * group, group)
            h = b_s[pl.ds(g0, group), :] + a_s[pl.ds(g0, group), :] * c
            h_s[pl.ds(g0, group), :] = h
            return h[group - 1:group, :] if d == 0 else h[0:1, :]

        return lax.fori_loop(0, n_groups, grp, carry)

    def fwd(i, carry):
        carry = scan_tile(i, 0, carry)
        r0 = pl.multiple_of(i * tl, tl)
        o_ref[0, pl.ds(r0, tl), :] = h_s[...]
        return carry

    def bwd(i, carry):
        carry = scan_tile(i, 1, carry)
        r0 = pl.multiple_of(i * tl, tl)
        o_ref[0, pl.ds(r0, tl), :] = (o_ref[0, pl.ds(r0, tl), :] + h_s[...]) * jax.nn.gelu(g_ref[0, pl.ds(r0, tl), :])
        return carry

    h0 = jnp.zeros((1, width), F32)
    lax.fori_loop(0, n_tiles, fwd, h0)
    carry = bwd(jnp.int32(0), h0)
    lax.fori_loop(1, n_tiles, lambda j, c: bwd(n_tiles - j, c), carry)


def _block_diag_dense(w):
    nd, k, c, _ = w.shape
    eye = jnp.eye(k, dtype=w.dtype)
    return jnp.einsum('dkce,kj->dkcje', w, eye).reshape(nd, k * c, k * c)


def _lru(u, conv_w, conv_b, w_a, b_a, w_x, b_x, lam, n_ctx):
    b, t, _ = u.shape
    width = conv_w.shape[1]
    assert n_ctx == LRU_TILE and t % LRU_TILE == 0
    n_tiles = t // LRU_TILE
    wa = _block_diag_dense(w_a).astype(BF16)
    wx = _block_diag_dense(w_x).astype(BF16)
    kern = functools.partial(_lru_kernel, n_tiles=n_tiles)
    full = lambda shape: pl.BlockSpec(shape, lambda bi: (0,) * len(shape))
    return pl.pallas_call(
        kern,
        grid=(b,),
        in_specs=[
            pl.BlockSpec((1, t, width), lambda bi: (bi, 0, U_XR // width)),
            pl.BlockSpec((1, t, width), lambda bi: (bi, 0, U_GT // width)),
            full((CONV_W, width)),
            full((1, width)),
            full((2, width, width)),
            full((2, width)),
            full((2, width, width)),
            full((2, width)),
            full((2, width)),
        ],
        out_specs=pl.BlockSpec((1, t, width), lambda bi: (bi, 0, 0)),
        out_shape=jax.ShapeDtypeStruct((b, t, width), F32),
        scratch_shapes=[
            pltpu.VMEM((t + 2 * LRU_HALO, width), F32),
            pltpu.VMEM((LRU_TILE, width), F32),
            pltpu.VMEM((LRU_TILE, width), F32),
            pltpu.VMEM((LRU_TILE, width), F32),
        ],
        compiler_params=_params(("arbitrary",)),
    )(u, u, conv_w, conv_b.reshape(1, width), wa, b_a, wx, b_x, lam)


def _out_mlp_kernel(x_ref, ym_ref, yn_ref, yl_ref, mod_ref, g_ref, wo_ref, w1_ref, w2_ref, o_ref, *, hid_tile):
    mod = mod_ref[0, 0]
    att = _dot(ym_ref[0].astype(BF16), wo_ref[0:M_WIDTH, :])
    att = att + _dot(yn_ref[0].astype(BF16), wo_ref[M_WIDTH:M_WIDTH + NA_WIDTH, :])
    att = att + _dot(yl_ref[0].astype(BF16), wo_ref[M_WIDTH + NA_WIDTH:, :])
    x1 = x_ref[0] + mod[2:3, :] * att
    ms = jnp.mean(x1 * x1, axis=-1, keepdims=True)
    y = x1 * lax.rsqrt(ms + EPS) * g_ref[...]
    h = (y * (1.0 + mod[4:5, :]) + mod[3:4, :]).astype(BF16)
    hidden = w1_ref.shape[1]
    acc = None
    for j in range(hidden // hid_tile):
        a = jnp.maximum(_dot(h, w1_ref[:, j * hid_tile:(j + 1) * hid_tile]), 0.0)
        part = _dot((a * a).astype(BF16), w2_ref[j * hid_tile:(j + 1) * hid_tile, :])
        acc = part if acc is None else acc + part
    o_ref[0] = x1 + mod[5:6, :] * acc


def _out_mlp(xs, ym, yn, yl, modv, g2, w_out, w1, w2, n_ctx):
    b, t, d = xs.shape
    hidden = w1.shape[1]
    ctx_tiles = n_ctx // ROW_TILE
    tok = lambda w: pl.BlockSpec((1, ROW_TILE, w), lambda bi, i: (bi, i, 0))
    kern = functools.partial(_out_mlp_kernel, hid_tile=1024)
    return pl.pallas_call(
        kern,
        grid=(b, t // ROW_TILE),
        in_specs=[
            tok(d), tok(M_WIDTH), tok(NA_WIDTH), tok(d - M_WIDTH - NA_WIDTH),
            pl.BlockSpec((1, 1, 6, d), lambda bi, i: (bi, jnp.minimum(i // ctx_tiles, 1), 0, 0)),
            pl.BlockSpec((1, d), lambda bi, i: (0, 0)),
            _resident((d, d), lambda bi, i: (0, 0)),
            _resident((d, hidden), lambda bi, i: (0, 0)),
            _resident((hidden, d), lambda bi, i: (0, 0)),
        ],
        out_specs=tok(d),
        out_shape=jax.ShapeDtypeStruct((b, t, d), F32),
        compiler_params=_params(("arbitrary", "arbitrary")),
    )(xs, ym, yn, yl, modv, g2.reshape(1, d), w_out, w1, w2)


def _permute_w_in(w_in):
    m_end = 4 * M_WIDTH
    na_end = m_end + N_GATES + 3 * NA_WIDTH
    parts = [
        w_in[:, :m_end],
        w_in[:, na_end:],
        w_in[:, m_end + N_GATES:na_end],
        w_in[:, m_end:m_end + N_GATES],
        jnp.zeros((w_in.shape[0], U_COLS - U_G - N_GATES), w_in.dtype),
    ]
    return jnp.concatenate(parts, axis=1).astype(BF16)


def kernel(x, c, ctx, c_ctx, w_mod, b_mod, norm1_g, norm2_g, w_in, mlstm_gate_b, mlstm_norm_g, na_q_norm_g, na_k_norm_g, na_rpb, lru_conv_w, lru_conv_b, lru_w_a, lru_b_a, lru_w_x, lru_b_x, lru_lambda, w_out, w_mlp1, w_mlp2):
    b, n, d = x.shape
    n_ctx = ctx.shape[1]
    t = n_ctx + n
    depth = w_in.shape[0]
    assert n_ctx == ROW_TILE and n % ROW_TILE == 0 and n % GRID_W == 0 and GRID_W == CHUNK

    xs = jnp.concatenate([ctx, x], axis=1)
    cond_rows = 8 * ((b + 1 + 7) // 8)
    cond = jnp.zeros((cond_rows, d), F32).at[:b].set(c).at[b].set(c_ctx)
    mod_all = _modulation(cond, w_mod, b_mod)
    rope_row, rope_col = _rope_tables(n // GRID_W)
    n_chunks = t // CHUNK

    for l in range(depth):
        lat = mod_all[l, :b].reshape(b, 6, d)
        cx = jnp.broadcast_to(mod_all[l, b].reshape(1, 6, d), (b, 6, d))
        modv = jnp.stack([cx, lat], axis=1)
        u = _inproj(xs, modv, norm1_g[l], _permute_w_in(w_in[l]), n_ctx)
        g_rows = jnp.swapaxes(u[:, :, U_G:U_G + N_GATES].reshape(b, n_chunks, CHUNK, N_GATES), 2, 3)
        ym = _mlstm(u, g_rows, mlstm_gate_b[l], mlstm_norm_g[l], rope_row, rope_col, n_ctx)
        yn = _na(u, na_q_norm_g[l], na_k_norm_g[l], na_rpb[l], n_ctx)
        yl = _lru(u, lru_conv_w[l], lru_conv_b[l], lru_w_a[l], lru_b_a[l], lru_w_x[l], lru_b_x[l], lru_lambda[l], n_ctx)
        xs = _out_mlp(xs, ym, yn, yl, modv, norm2_g[l], w_out[l].astype(BF16), w_mlp1[l].astype(BF16),
                      w_mlp2[l].astype(BF16), n_ctx)
    return xs[:, n_ctx:, :]
```

```python
import functools
import math

import numpy as np
import jax
import jax.numpy as jnp
from jax import lax
from jax.experimental import pallas as pl
from jax.experimental.pallas import tpu as pltpu

F32 = jnp.float32
BF16 = jnp.bfloat16

HEAD_DIM = 64
M_HEADS = 4
M_WIDTH = M_HEADS * HEAD_DIM
NA_HEADS = 8
NA_WIDTH = NA_HEADS * HEAD_DIM
N_GATES = 4 * M_HEADS
GRID_W = 64
M_CHUNK = 128
M_AUG = HEAD_DIM + 16
WIN_R = 8
WIN_C = 16
RPB_R = 2 * WIN_R - 1
RPB_C = 2 * WIN_C - 1
NA_QROWS = 4
NA_BAND = NA_QROWS + WIN_R
NA_KEY_CHUNK = 128
NA_ONES_ROWS = 16
LOG2E = 1.4426950408889634
CONV_W = 4
LRU_C = 8.0
ROPE_BASE = 10000.0
EPS = 1e-6
NEG = -1e30

LANES = 128
ROW_TILE = 256
LRU_TILE = 256
LRU_HALO = 8
VMEM_LIMIT = 56 * 1024 * 1024

U_MQ, U_MK, U_MV, U_MO = 0, 256, 512, 768
U_XR, U_GT = 1024, 1280
U_NQ, U_NK, U_NV = 1536, 2048, 2560
U_G = 3072
U_COLS = 3200


def _dot(a, b):
    return jnp.dot(a, b, preferred_element_type=F32)


def _dot_nt(a, b):
    return lax.dot_general(a, b, (((1,), (1,)), ((), ())), preferred_element_type=F32)


def _dot_tn(a, b):
    return lax.dot_general(a, b, (((0,), (0,)), ((), ())), preferred_element_type=F32)


def _split3(x):
    hi = x.astype(BF16)
    r = x - hi.astype(F32)
    mid = r.astype(BF16)
    lo = (r - mid.astype(F32)).astype(BF16)
    return hi, mid, lo


def _dot_f32_rhs01(x, m01):
    hi, mid, lo = _split3(x)
    return _dot(hi, m01) + _dot(mid, m01) + _dot(lo, m01)


def _dot_f32_lhs01(m01, x):
    hi, mid, lo = _split3(x)
    return _dot(m01, hi) + _dot(m01, mid) + _dot(m01, lo)


def _log_sigmoid(x):
    return jnp.minimum(x, 0.0) - jnp.log1p(jnp.exp(-jnp.abs(x)))


def _softplus(x):
    return jnp.maximum(x, 0.0) + jnp.log1p(jnp.exp(-jnp.abs(x)))


def _params(sem):
    return pltpu.CompilerParams(dimension_semantics=sem, vmem_limit_bytes=VMEM_LIMIT)


def _resident(shape, index_map):
    return pl.BlockSpec(shape, index_map, pipeline_mode=pl.Buffered(1))


def _mod_kernel(a_ref, w_ref, b_ref, o_ref):
    a = a_ref[...]
    act = (a * jax.nn.sigmoid(a)).astype(BF16)
    o_ref[0] = _dot(act, w_ref[0].astype(BF16)) + b_ref[0]


def _modulation(cond, w_mod, b_mod):
    depth, d, n6 = w_mod.shape
    rows = cond.shape[0]
    tn = 1536
    return pl.pallas_call(
        _mod_kernel,
        grid=(depth, n6 // tn),
        in_specs=[
            pl.BlockSpec((rows, d), lambda l, j: (0, 0)),
            pl.BlockSpec((1, d, tn), lambda l, j: (l, 0, j)),
            pl.BlockSpec((1, 1, tn), lambda l, j: (l, 0, j)),
        ],
        out_specs=pl.BlockSpec((1, rows, tn), lambda l, j: (l, 0, j)),
        out_shape=jax.ShapeDtypeStruct((depth, rows, n6), F32),
        compiler_params=_params(("arbitrary", "arbitrary")),
    )(cond, w_mod, b_mod.reshape(depth, 1, n6))


def _inproj_kernel(x_ref, mod_ref, g_ref, w_ref, u_ref):
    x = x_ref[0]
    ms = jnp.mean(x * x, axis=-1, keepdims=True)
    y = x * lax.rsqrt(ms + EPS) * g_ref[...]
    mod = mod_ref[0, 0]
    h = (y * (1.0 + mod[1:2, :]) + mod[0:1, :]).astype(BF16)
    u_ref[0] = _dot(h, w_ref[...])


def _inproj(xs, modv, g1, w_in_p, n_ctx):
    b, t, d = xs.shape
    ctx_tiles = n_ctx // ROW_TILE
    return pl.pallas_call(
        _inproj_kernel,
        grid=(b, t // ROW_TILE),
        in_specs=[
            pl.BlockSpec((1, ROW_TILE, d), lambda bi, i: (bi, i, 0)),
            pl.BlockSpec((1, 1, 6, d), lambda bi, i: (bi, jnp.minimum(i // ctx_tiles, 1), 0, 0)),
            pl.BlockSpec((1, d), lambda bi, i: (0, 0)),
            _resident((d, U_COLS), lambda bi, i: (0, 0)),
        ],
        out_specs=pl.BlockSpec((1, ROW_TILE, U_COLS), lambda bi, i: (bi, i, 0)),
        out_shape=jax.ShapeDtypeStruct((b, t, U_COLS), F32),
        compiler_params=_params(("arbitrary", "arbitrary")),
    )(xs, modv, g1.reshape(1, d), w_in_p)


def _mlstm_kernel(u_ref, gr_ref, gb_ref, ng_ref, rrow_ref, rcol_ref, o_ref,
                  qt_s, k_s, vt_s, r_s, row_s, c_s, *, n_ctx_chunks, n_chunks):
    L = M_CHUNK
    W = M_WIDTH
    D = HEAD_DIM
    groups = L // GRID_W

    for h in range(M_HEADS):
        vt_s[h, D:, :] = jnp.ones((M_AUG - D, vt_s.shape[2]), BF16)
    c_s[...] = jnp.zeros(c_s.shape, F32)

    def stage(r0, q, k):
        qt_s[:, pl.ds(r0, L)] = q.T.astype(BF16)
        k_s[pl.ds(r0, L), :] = k.astype(BF16)
        vt = u_ref[0, pl.ds(r0, L), U_MV:U_MV + W].T
        for h in range(M_HEADS):
            vt_s[h, 0:D, pl.ds(r0, L)] = vt[h * D:(h + 1) * D].astype(BF16)

    def prep_ctx(j, carry):
        r0 = pl.multiple_of(j * L, L)
        stage(r0, u_ref[0, pl.ds(r0, L), U_MQ:U_MQ + W], u_ref[0, pl.ds(r0, L), U_MK:U_MK + W] * D ** -0.5)
        return carry

    def prep_lat(j, carry):
        r0 = pl.multiple_of((n_ctx_chunks + j) * L, L)

        def table(i):
            rows = [rrow_ref[i, pl.ds(j * groups + g, 1), :] + rcol_ref[i] for g in range(groups)]
            return jnp.concatenate(rows, axis=0)

        cos, sin_a, sin_b = table(0), table(1), table(2)

        def rope(x):
            return x * cos + pltpu.roll(x, W - D // 4, 1) * sin_a + pltpu.roll(x, D // 4, 1) * sin_b

        stage(r0, rope(u_ref[0, pl.ds(r0, L), U_MQ:U_MQ + W]), rope(u_ref[0, pl.ds(r0, L), U_MK:U_MK + W] * D ** -0.5))
        return carry

    lax.fori_loop(0, n_ctx_chunks, prep_ctx, 0)
    lax.fori_loop(0, n_chunks - n_ctx_chunks, prep_lat, 0)

    si = lax.broadcasted_iota(jnp.int32, (L, L), 0)
    li = lax.broadcasted_iota(jnp.int32, (L, L), 1)
    keeps = [si <= li, si >= li]
    gbias = [jnp.broadcast_to(gb_ref[8 * d:8 * d + 8, :], (8, L)) for d in range(2)]
    zrows = jnp.zeros((D, L), BF16)

    def qt_head(qtc, h):
        parts = [zrows] * M_HEADS
        parts[h] = qtc[h * D:(h + 1) * D]
        return jnp.concatenate(parts, axis=0)

    def local(c, carry):
        r0 = pl.multiple_of(c * L, L)
        kc = k_s[pl.ds(r0, L), :]
        qtc = qt_s[:, pl.ds(r0, L)]
        scores = [_dot(kc, qt_head(qtc, h)) for h in range(M_HEADS)]
        vtas = [vt_s[h, :, pl.ds(r0, L)] for h in range(M_HEADS)]
        stores = []
        for d in range(2):
            keep = keeps[d]
            last = L - 1 if d == 0 else 0
            g8 = gr_ref[0, 8 * d:8 * d + 8, pl.ds(r0, L)] + gbias[d]
            cum8 = _dot_f32_rhs01(_log_sigmoid(g8), keep.astype(BF16))
            cum4 = pltpu.roll(cum8, M_HEADS, 0)
            src_t = (g8 - cum4).T
            for h in range(M_HEADS):
                cl = cum4[h:h + 1]
                logw = jnp.where(keep, cl + jnp.broadcast_to(src_t[:, h:h + 1], (L, L)), NEG)
                a = logw.max(axis=0, keepdims=True)
                p = (scores[h] * jnp.exp(logw - a)).astype(BF16)
                a_last = jnp.broadcast_to(a[:, last:last + 1], (1, L))
                cl_last = jnp.broadcast_to(cl[:, last:last + 1], (1, L))
                w_src = jnp.exp(cl_last - cl + g8[h:h + 1] - a_last)
                stores.append((d, h, _dot(vtas[h], p), (cl, a, cl_last, a_last, w_src)))
        for d, h, intra, rows in stores:
            r_s[d, h * M_AUG:(h + 1) * M_AUG, pl.ds(r0, L)] = intra
            for kind, row in enumerate(rows):
                row_s[d, kind, h:h + 1, pl.ds(r0, L)] = row
        return carry

    lax.fori_loop(0, n_chunks, local, 0)

    def step(c, d, c_olds, m_prevs):
        r0 = pl.multiple_of(c * L, L)
        kc = k_s[pl.ds(r0, L), :]
        qtc = qt_s[:, pl.ds(r0, L)]
        h_ts, c_news, m_news = [], [], []
        for h in range(M_HEADS):
            cl, a, cl_last, a_last, w_src = [row_s[d, kind, h:h + 1, pl.ds(r0, L)] for kind in range(5)]
            m_prev = m_prevs[h]
            inter = cl + m_prev
            m_t = jnp.maximum(inter, a)
            r = (jnp.exp(inter - m_t) * _dot(c_olds[h].astype(BF16), qt_head(qtc, h))
                 + jnp.exp(a - m_t) * r_s[d, h * M_AUG:(h + 1) * M_AUG, pl.ds(r0, L)])
            denom = jnp.maximum(jnp.abs(r[D:D + 1]), jnp.exp(-m_t))
            h_ts.append(r[0:D] * (1.0 / denom))
            m_new = jnp.maximum(cl_last + m_prev, a_last)
            w_old = jnp.exp(cl_last + m_prev - m_new)
            w_upd = jnp.exp(a_last - m_new)
            vw = (vt_s[h, :, pl.ds(r0, L)].astype(F32) * w_src).astype(BF16)
            scale = lambda w: jnp.concatenate([w] * (W // L), axis=1)
            c_news.append(scale(w_old) * c_olds[h] + scale(w_upd) * _dot(vw, kc))
            m_news.append(m_new)
        return r0, h_ts, c_news, m_news

    def body(i, m_all):
        chunks = [i, jnp.where(i < n_ctx_chunks, n_ctx_chunks - 1 - i, n_chunks - 1 + n_ctx_chunks - i)]
        c_olds = [[c_s[d, h] for h in range(M_HEADS)] for d in range(2)]
        outs = [step(chunks[d], d, c_olds[d], m_all[d]) for d in range(2)]
        for d in range(2):
            r0, h_ts, c_news, _ = outs[d]
            for h in range(M_HEADS):
                r_s[d, h * M_AUG:h * M_AUG + D, pl.ds(r0, L)] = h_ts[h]
                c_s[d, h] = c_news[h]
        return tuple(tuple(outs[d][3]) for d in range(2))

    m0 = tuple(tuple(jnp.zeros((1, L), F32) for _ in range(M_HEADS)) for _ in range(2))
    lax.fori_loop(0, n_chunks, body, m0)

    ng = jnp.broadcast_to(ng_ref[...], (W, L))

    def post(j, carry):
        r0 = pl.multiple_of(j * L, L)
        normed = []
        for h in range(M_HEADS):
            ht = r_s[0, h * M_AUG:h * M_AUG + D, pl.ds(r0, L)] + r_s[1, h * M_AUG:h * M_AUG + D, pl.ds(r0, L)]
            ms = jnp.sum(ht * ht, axis=0, keepdims=True) * (1.0 / D)
            normed.append(ht * lax.rsqrt(ms + EPS))
        hn = jnp.concatenate(normed, axis=0) * ng
        o_ref[0, pl.ds(r0, L), :] = jax.nn.sigmoid(u_ref[0, pl.ds(r0, L), U_MO:U_MO + W]) * hn.T
        return carry

    lax.fori_loop(0, n_chunks, post, 0)


def _rope_tables(n_rows):
    nf = HEAD_DIM // 4
    inv = ROPE_BASE ** (-jnp.arange(nf, dtype=F32) / nf)
    d = np.arange(M_WIDTH) % HEAD_DIM
    f_idx = d % nf
    is_row = d < HEAD_DIM // 2
    first_quarter = (d % (HEAD_DIM // 2)) < nf

    def tables(pos, sel):
        ang = pos[:, None] * inv[None, :]
        ang = ang[:, f_idx]
        cos = jnp.where(sel, jnp.cos(ang), 0.0)
        sin = jnp.where(sel, jnp.sin(ang), 0.0)
        sin_a = jnp.where(first_quarter, -sin, 0.0)
        sin_b = jnp.where(first_quarter, 0.0, sin)
        return jnp.stack([cos, sin_a, sin_b]).astype(F32)

    row_t = tables(jnp.arange(n_rows, dtype=F32), is_row)
    col_t = tables(jnp.arange(GRID_W, dtype=F32), ~is_row)
    return row_t, col_t


def _mlstm(u, g_rows, gate_b, norm_g, rope_row, rope_col, n_ctx):
    b, t, _ = u.shape
    assert t % M_CHUNK == 0 and n_ctx % M_CHUNK == 0 and M_CHUNK % GRID_W == 0
    n_chunks = t // M_CHUNK
    kern = functools.partial(_mlstm_kernel, n_ctx_chunks=n_ctx // M_CHUNK, n_chunks=n_chunks)
    n_rows = rope_row.shape[1]
    return pl.pallas_call(
        kern,
        grid=(b,),
        in_specs=[
            pl.BlockSpec((1, t, 4 * M_WIDTH), lambda bi: (bi, 0, 0), pipeline_mode=pl.Buffered(1)),
            pl.BlockSpec((1, N_GATES, t), lambda bi: (bi, 0, 0)),
            pl.BlockSpec((N_GATES, 1), lambda bi: (0, 0)),
            pl.BlockSpec((M_WIDTH, 1), lambda bi: (0, 0)),
            pl.BlockSpec((3, n_rows, M_WIDTH), lambda bi: (0, 0, 0)),
            pl.BlockSpec((3, GRID_W, M_WIDTH), lambda bi: (0, 0, 0)),
        ],
        out_specs=pl.BlockSpec((1, t, M_WIDTH), lambda bi: (bi, 0, 0)),
        out_shape=jax.ShapeDtypeStruct((b, t, M_WIDTH), F32),
        scratch_shapes=[
            pltpu.VMEM((M_WIDTH, t), BF16),
            pltpu.VMEM((t, M_WIDTH), BF16),
            pltpu.VMEM((M_HEADS, M_AUG, t), BF16),
            pltpu.VMEM((2, M_HEADS * M_AUG, t), F32),
            pltpu.VMEM((2, 5, 8, t), F32),
            pltpu.VMEM((2, M_HEADS, M_AUG, M_WIDTH), F32),
        ],
        compiler_params=_params(("arbitrary",)),
    )(u, g_rows, gate_b.reshape(N_GATES, 1), norm_g.reshape(M_WIDTH, 1), rope_row, rope_col)


def _na_kernel(q_ref, k_ref, v_ref, qg_ref, kg_ref, bias_ref, o_ref, qt_s, kz_s, vt_s, s_s, *, n_ctx, n_rows):
    t = q_ref.shape[1]
    lane = lax.broadcasted_iota(jnp.int32, (1, LANES), 1)
    half = [lane < HEAD_DIM, lane >= HEAD_DIM]
    hi = lax.broadcasted_iota(jnp.int32, (LANES, LANES), 0) // HEAD_DIM
    hj = lax.broadcasted_iota(jnp.int32, (LANES, LANES), 1) // HEAD_DIM
    same_head = (hi == hj).astype(BF16)
    qg = qg_ref[...] * (HEAD_DIM ** -0.5 * LOG2E)
    kg = kg_ref[...]
    qblk = NA_QROWS * GRID_W
    kc = NA_KEY_CHUNK
    sub = 8

    for hh in range(2):
        vt_s[hh, HEAD_DIM:, :] = jnp.ones((vt_s.shape[1] - HEAD_DIM, t), BF16)

    def prep(j, carry):
        r0 = pl.multiple_of(j * ROW_TILE, ROW_TILE)
        q = q_ref[0, pl.ds(r0, ROW_TILE), :]
        k = k_ref[0, pl.ds(r0, ROW_TILE), :]
        qms = _dot_f32_rhs01(q * q, same_head) * (1.0 / HEAD_DIM)
        kms = _dot_f32_rhs01(k * k, same_head) * (1.0 / HEAD_DIM)
        qn = q * lax.rsqrt(qms + EPS) * qg
        kn = k * lax.rsqrt(kms + EPS) * kg
        qt_s[:, pl.ds(r0, ROW_TILE)] = qn.T.astype(BF16)
        vt = v_ref[0, pl.ds(r0, ROW_TILE), :].T
        for hh in range(2):
            kz_s[hh, pl.ds(r0, ROW_TILE), :] = jnp.where(half[hh], kn, 0.0).astype(BF16)
            vt_s[hh, 0:HEAD_DIM, pl.ds(r0, ROW_TILE)] = vt[hh * HEAD_DIM:(hh + 1) * HEAD_DIM].astype(BF16)
        return carry

    lax.fori_loop(0, t // ROW_TILE, prep, 0)

    def attend(q0, chunks):
        qt = qt_s[:, pl.ds(q0, qblk)]
        outs = []
        for hh in range(2):
            mx = None
            for ci, (k0, tab) in enumerate(chunks):
                s = _dot(kz_s[hh, pl.ds(k0, kc), :], qt)
                if tab is not None:
                    s = s + bias_ref[0, hh, tab[0], tab[1]:tab[1] + kc, :]
                s_s[hh, ci * kc:(ci + 1) * kc, :] = s
                cm = functools.reduce(jnp.maximum, [s[i * sub:(i + 1) * sub] for i in range(kc // sub)])
                mx = cm if mx is None else jnp.maximum(mx, cm)
            m = mx.max(axis=0, keepdims=True)
            acc = None
            for ci, (k0, _) in enumerate(chunks):
                p = jnp.exp2(s_s[hh, ci * kc:(ci + 1) * kc, :] - m).astype(BF16)
                pv = _dot(vt_s[hh, :, pl.ds(k0, kc)], p)
                acc = pv if acc is None else acc + pv
            outs.append(acc[0:HEAD_DIM] / acc[HEAD_DIM:HEAD_DIM + 1])
        return jnp.concatenate(outs, axis=0).T

    ctx_chunks = [(ci * kc, None) for ci in range(n_ctx // kc)]
    o_ref[0, 0:n_ctx, :] = attend(0, ctx_chunks)

    n_blocks = n_rows // NA_QROWS

    def block(bi, carry):
        r0 = bi * NA_QROWS
        sb = jnp.clip(r0 - WIN_R // 2, 0, n_rows - NA_BAND)
        case = jnp.where(bi == 0, 0, jnp.where(bi == n_blocks - 1, 2, 1))
        q0 = pl.multiple_of(n_ctx + r0 * GRID_W, qblk)
        k0 = pl.multiple_of(n_ctx + sb * GRID_W, qblk)
        local = [(pl.multiple_of(k0 + ci * kc, kc), (case, ci * kc)) for ci in range(NA_BAND * GRID_W // kc)]
        o_ref[0, pl.ds(q0, qblk), :] = attend(q0, local + ctx_chunks)
        return carry

    lax.fori_loop(0, n_blocks, block, 0)


def _na_bias_table(rpb, n_rows):
    n_heads = rpb.shape[0]
    cols = np.arange(GRID_W)
    col0 = np.clip(cols - WIN_C // 2, 0, GRID_W - WIN_C)
    valid = (cols[:, None] >= col0[None, :]) & (cols[:, None] < col0[None, :] + WIN_C)
    pad = GRID_W - WIN_C
    rp = jnp.pad(rpb, ((0, 0), (0, 0), (pad, pad)))
    toep = jnp.stack([rp[:, :, GRID_W - 1 - c:2 * GRID_W - 1 - c] for c in range(GRID_W)], axis=-1)
    toep = jnp.where(valid, toep, NEG)
    blank = jnp.full((n_heads, GRID_W, GRID_W), NEG, F32)
    last_off = NA_BAND - WIN_R
    cases = []
    for case in range(3):
        rows = []
        for jj in range(NA_BAND):
            tiles = []
            for i in range(NA_QROWS):
                off = (0, i, last_off)[case]
                rel = (jj - i, jj - i - WIN_R // 2, jj - i - NA_BAND + NA_QROWS)[case]
                inside = off <= jj < off + WIN_R
                tiles.append(toep[:, rel + WIN_R - 1] if inside else blank)
            rows.append(jnp.concatenate(tiles, axis=-1))
        cases.append(jnp.concatenate(rows, axis=1))
    return (jnp.stack(cases, axis=1) * LOG2E).astype(F32)


def _na(u, qn_g, kn_g, rpb, n_ctx):
    b, t, _ = u.shape
    n_rows = (t - n_ctx) // GRID_W
    assert n_rows % NA_QROWS == 0 and n_rows >= NA_BAND and n_ctx == NA_QROWS * GRID_W
    pairs = NA_HEADS // 2
    nk, nq = NA_BAND * GRID_W, NA_QROWS * GRID_W
    bias = _na_bias_table(rpb, n_rows).reshape(pairs, 2, 3, nk, nq)
    qg = jnp.tile(qn_g, 2).reshape(1, LANES)
    kg = jnp.tile(kn_g, 2).reshape(1, LANES)
    kern = functools.partial(_na_kernel, n_ctx=n_ctx, n_rows=n_rows)
    col = lambda base: (lambda hp, bi: (bi, 0, base // LANES + hp))
    return pl.pallas_call(
        kern,
        grid=(pairs, b),
        in_specs=[
            pl.BlockSpec((1, t, LANES), col(U_NQ)),
            pl.BlockSpec((1, t, LANES), col(U_NK)),
            pl.BlockSpec((1, t, LANES), col(U_NV)),
            pl.BlockSpec((1, LANES), lambda hp, bi: (0, 0)),
            pl.BlockSpec((1, LANES), lambda hp, bi: (0, 0)),
            pl.BlockSpec((1, 2, 3, nk, nq), lambda hp, bi: (hp, 0, 0, 0, 0)),
        ],
        out_specs=pl.BlockSpec((1, t, LANES), lambda hp, bi: (bi, 0, hp)),
        out_shape=jax.ShapeDtypeStruct((b, t, NA_WIDTH), F32),
        scratch_shapes=[
            pltpu.VMEM((LANES, t), BF16),
            pltpu.VMEM((2, t, LANES), BF16),
            pltpu.VMEM((2, HEAD_DIM + NA_ONES_ROWS, t), BF16),
            pltpu.VMEM((2, nk + n_ctx, nq), F32),
        ],
        compiler_params=_params(("arbitrary", "arbitrary")),
    )(u, u, u, qg, kg, bias)


def _lru_kernel(x_ref, g_ref, cw_ref, cb_ref, wa_ref, ba_ref, wx_ref, bx_ref, lam_ref, o_ref,
                xp_s, a_s, b_s, h_s, *, n_tiles):
    tl = LRU_TILE
    halo = LRU_HALO
    t = x_ref.shape[1]
    width = x_ref.shape[2]
    group = 8

    xp_s[0:halo, :] = jnp.zeros((halo, width), F32)
    xp_s[halo + t:halo + t + halo, :] = jnp.zeros((halo, width), F32)

    def fill(i, carry):
        r0 = pl.multiple_of(i * tl, tl)
        xp_s[pl.ds(pl.multiple_of(r0 + halo, halo), tl), :] = x_ref[0, pl.ds(r0, tl), :]
        return carry

    lax.fori_loop(0, n_tiles, fill, 0)

    row = lax.broadcasted_iota(jnp.int32, (tl, 1), 0)
    rowg = row % group
    cw = cw_ref[...]
    cb = cb_ref[...]

    def conv_tile(i):
        r0 = pl.multiple_of(i * tl, tl)
        win = xp_s[pl.ds(r0, tl + 2 * halo), :]
        n_win = tl + 2 * halo
        xm2 = pltpu.roll(win, 2, 0)[halo:halo + tl]
        xm1 = pltpu.roll(win, 1, 0)[halo:halo + tl]
        x0 = win[halo:halo + tl]
        xp1 = pltpu.roll(win, n_win - 1, 0)[halo:halo + tl]
        lo = i <= 1
        hi = i == 0
        xm2 = jnp.where(lo & (row < 2), 0.0, xm2)
        xm1 = jnp.where(lo & (row < 1), 0.0, xm1)
        xp1 = jnp.where(hi & (row == tl - 1), 0.0, xp1)
        return xm2 * cw[0:1] + xm1 * cw[1:2] + x0 * cw[2:3] + xp1 * cw[3:4] + cb

    def gates(i, d):
        xc = conv_tile(i)
        xb = xc.astype(BF16)
        r = jax.nn.sigmoid(_dot(xb, wa_ref[d]) + ba_ref[d:d + 1, :])
        ig = jax.nn.sigmoid(_dot(xb, wx_ref[d]) + bx_ref[d:d + 1, :])
        log_a = -LRU_C * r * _softplus(-lam_ref[d:d + 1, :])
        a = jnp.exp(log_a)
        bt = jnp.sqrt(-jnp.tanh(log_a) * (a * a + 1.0)) * (ig * xc)
        return a, bt

    def scan_tile(i, d, carry):
        a, bt = gates(i, d)
        for kk in (1, 2, 4):
            if d == 0:
                a_sh = pltpu.roll(a, kk, 0)
                b_sh = pltpu.roll(bt, kk, 0)
                mk = rowg >= kk
            else:
                a_sh = pltpu.roll(a, tl - kk, 0)
                b_sh = pltpu.roll(bt, tl - kk, 0)
                mk = rowg < group - kk
            bt = jnp.where(mk, a * b_sh + bt, bt)
            a = jnp.where(mk, a * a_sh, a)
        a_s[...] = a
        b_s[...] = bt
        n_groups = tl // group

        def grp(gidx, c):
            gg = gidx if d == 0 else n_groups - 1 - gidx
            g0 = pl.multiple_of(gg * group, group)
            h = b_s[pl.ds(g0, group), :] + a_s[pl.ds(g0, group), :] * c
            h_s[pl.ds(g0, group), :] = h
            return h[group - 1:group, :] if d == 0 else h[0:1, :]

        return lax.fori_loop(0, n_groups, grp, carry)

    def fwd(i, carry):
        carry = scan_tile(i, 0, carry)
        r0 = pl.multiple_of(i * tl, tl)
        o_ref[0, pl.ds(r0, tl), :] = h_s[...]
        return carry

    def bwd(i, carry):
        carry = scan_tile(i, 1, carry)
        r0 = pl.multiple_of(i * tl, tl)
        o_ref[0, pl.ds(r0, tl), :] = (o_ref[0, pl.ds(r0, tl), :] + h_s[...]) * jax.nn.gelu(g_ref[0, pl.ds(r0, tl), :])
        return carry

    h0 = jnp.zeros((1, width), F32)
    lax.fori_loop(0, n_tiles, fwd, h0)
    carry = bwd(jnp.int32(0), h0)
    lax.fori_loop(1, n_tiles, lambda j, c: bwd(n_tiles - j, c), carry)


def _block_diag_dense(w):
    nd, k, c, _ = w.shape
    eye = jnp.eye(k, dtype=w.dtype)
    return jnp.einsum('dkce,kj->dkcje', w, eye).reshape(nd, k * c, k * c)


def _lru(u, conv_w, conv_b, w_a, b_a, w_x, b_x, lam, n_ctx):
    b, t, _ = u.shape
    width = conv_w.shape[1]
    assert n_ctx == LRU_TILE and t % LRU_TILE == 0
    n_tiles = t // LRU_TILE
    wa = _block_diag_dense(w_a).astype(BF16)
    wx = _block_diag_dense(w_x).astype(BF16)
    kern = functools.partial(_lru_kernel, n_tiles=n_tiles)
    full = lambda shape: pl.BlockSpec(shape, lambda bi: (0,) * len(shape))
    return pl.pallas_call(
        kern,
        grid=(b,),
        in_specs=[
            pl.BlockSpec((1, t, width), lambda bi: (bi, 0, U_XR // width)),
            pl.BlockSpec((1, t, width), lambda bi: (bi, 0, U_GT // width)),
            full((CONV_W, width)),
            full((1, width)),
            full((2, width, width)),
            full((2, width)),
            full((2, width, width)),
            full((2, width)),
            full((2, width)),
        ],
        out_specs=pl.BlockSpec((1, t, width), lambda bi: (bi, 0, 0)),
        out_shape=jax.ShapeDtypeStruct((b, t, width), F32),
        scratch_shapes=[
            pltpu.VMEM((t + 2 * LRU_HALO, width), F32),
            pltpu.VMEM((LRU_TILE, width), F32),
            pltpu.VMEM((LRU_TILE, width), F32),
            pltpu.VMEM((LRU_TILE, width), F32),
        ],
        compiler_params=_params(("arbitrary",)),
    )(u, u, conv_w, conv_b.reshape(1, width), wa, b_a, wx, b_x, lam)


def _out_mlp_kernel(x_ref, ym_ref, yn_ref, yl_ref, mod_ref, g_ref, wo_ref, w1_ref, w2_ref, o_ref, *, hid_tile):
    mod = mod_ref[0, 0]
    att = _dot(ym_ref[0].astype(BF16), wo_ref[0:M_WIDTH, :])
    att = att + _dot(yn_ref[0].astype(BF16), wo_ref[M_WIDTH:M_WIDTH + NA_WIDTH, :])
    att = att + _dot(yl_ref[0].astype(BF16), wo_ref[M_WIDTH + NA_WIDTH:, :])
    x1 = x_ref[0] + mod[2:3, :] * att
    ms = jnp.mean(x1 * x1, axis=-1, keepdims=True)
    y = x1 * lax.rsqrt(ms + EPS) * g_ref[...]
    h = (y * (1.0 + mod[4:5, :]) + mod[3:4, :]).astype(BF16)
    hidden = w1_ref.shape[1]
    acc = None
    for j in range(hidden // hid_tile):
        a = jnp.maximum(_dot(h, w1_ref[:, j * hid_tile:(j + 1) * hid_tile]), 0.0)
        part = _dot((a * a).astype(BF16), w2_ref[j * hid_tile:(j + 1) * hid_tile, :])
        acc = part if acc is None else acc + part
    o_ref[0] = x1 + mod[5:6, :] * acc


def _out_mlp(xs, ym, yn, yl, modv, g2, w_out, w1, w2, n_ctx):
    b, t, d = xs.shape
    hidden = w1.shape[1]
    ctx_tiles = n_ctx // ROW_TILE
    tok = lambda w: pl.BlockSpec((1, ROW_TILE, w), lambda bi, i: (bi, i, 0))
    kern = functools.partial(_out_mlp_kernel, hid_tile=1024)
    return pl.pallas_call(
        kern,
        grid=(b, t // ROW_TILE),
        in_specs=[
            tok(d), tok(M_WIDTH), tok(NA_WIDTH), tok(d - M_WIDTH - NA_WIDTH),
            pl.BlockSpec((1, 1, 6, d), lambda bi, i: (bi, jnp.minimum(i // ctx_tiles, 1), 0, 0)),
            pl.BlockSpec((1, d), lambda bi, i: (0, 0)),
            _resident((d, d), lambda bi, i: (0, 0)),
            _resident((d, hidden), lambda bi, i: (0, 0)),
            _resident((hidden, d), lambda bi, i: (0, 0)),
        ],
        out_specs=tok(d),
        out_shape=jax.ShapeDtypeStruct((b, t, d), F32),
        compiler_params=_params(("arbitrary", "arbitrary")),
    )(xs, ym, yn, yl, modv, g2.reshape(1, d), w_out, w1, w2)


def _permute_w_in(w_in):
    m_end = 4 * M_WIDTH
    na_end = m_end + N_GATES + 3 * NA_WIDTH
    parts = [
        w_in[:, :m_end],
        w_in[:, na_end:],
        w_in[:, m_end + N_GATES:na_end],
        w_in[:, m_end:m_end + N_GATES],
        jnp.zeros((w_in.shape[0], U_COLS - U_G - N_GATES), w_in.dtype),
    ]
    return jnp.concatenate(parts, axis=1).astype(BF16)


def kernel(x, c, ctx, c_ctx, w_mod, b_mod, norm1_g, norm2_g, w_in, mlstm_gate_b, mlstm_norm_g, na_q_norm_g, na_k_norm_g, na_rpb, lru_conv_w, lru_conv_b, lru_w_a, lru_b_a, lru_w_x, lru_b_x, lru_lambda, w_out, w_mlp1, w_mlp2):
    b, n, d = x.shape
    n_ctx = ctx.shape[1]
    t = n_ctx + n
    depth = w_in.shape[0]
    assert n_ctx == ROW_TILE and n % ROW_TILE == 0 and n % GRID_W == 0

    xs = jnp.concatenate([ctx, x], axis=1)
    cond_rows = 8 * ((b + 1 + 7) // 8)
    cond = jnp.zeros((cond_rows, d), F32).at[:b].set(c).at[b].set(c_ctx)
    mod_all = _modulation(cond, w_mod, b_mod)
    rope_row, rope_col = _rope_tables(n // GRID_W)

    for l in range(depth):
        lat = mod_all[l, :b].reshape(b, 6, d)
        cx = jnp.broadcast_to(mod_all[l, b].reshape(1, 6, d), (b, 6, d))
        modv = jnp.stack([cx, lat], axis=1)
        u = _inproj(xs, modv, norm1_g[l], _permute_w_in(w_in[l]), n_ctx)
        g_rows = jnp.swapaxes(u[:, :, U_G:U_G + N_GATES], 1, 2)
        ym = _mlstm(u, g_rows, mlstm_gate_b[l], mlstm_norm_g[l], rope_row, rope_col, n_ctx)
        yn = _na(u, na_q_norm_g[l], na_k_norm_g[l], na_rpb[l], n_ctx)
        yl = _lru(u, lru_conv_w[l], lru_conv_b[l], lru_w_a[l], lru_b_a[l], lru_w_x[l], lru_b_x[l], lru_lambda[l], n_ctx)
        xs = _out_mlp(xs, ym, yn, yl, modv, norm2_g[l], w_out[l].astype(BF16), w_mlp1[l].astype(BF16),
                      w_mlp2[l].astype(BF16), n_ctx)
    return xs[:, n_ctx:, :]
```

```python
import functools
import math

import numpy as np
import jax
import jax.numpy as jnp
from jax import lax
from jax.experimental import pallas as pl
from jax.experimental.pallas import tpu as pltpu

F32 = jnp.float32
BF16 = jnp.bfloat16

HEAD_DIM = 64
M_HEADS = 4
M_WIDTH = M_HEADS * HEAD_DIM
NA_HEADS = 8
NA_WIDTH = NA_HEADS * HEAD_DIM
N_GATES = 4 * M_HEADS
GRID_W = 64
M_CHUNK = 128
M_AUG = HEAD_DIM + 16
M_LOCAL_GROUP = 2
WIN_R = 8
WIN_C = 16
RPB_R = 2 * WIN_R - 1
RPB_C = 2 * WIN_C - 1
NA_QROWS = 4
NA_BAND = NA_QROWS + WIN_R
NA_KEY_CHUNK = 128
NA_ONES_ROWS = 16
NA_GROUP = 4
LOG2E = 1.4426950408889634
CONV_W = 4
LRU_C = 8.0
ROPE_BASE = 10000.0
EPS = 1e-6
NEG = -1e30

LANES = 128
ROW_TILE = 256
LRU_TILE = 256
LRU_HALO = 8
VMEM_LIMIT = 56 * 1024 * 1024

U_MQ, U_MK, U_MV, U_MO = 0, 256, 512, 768
U_XR, U_GT = 1024, 1280
U_NQ, U_NK, U_NV = 1536, 2048, 2560
U_G = 3072
U_COLS = 3200


def _dot(a, b):
    return jnp.dot(a, b, preferred_element_type=F32)


def _dot_f32_rhs01(x, m01):
    hi = x.astype(BF16)
    lo = (x - hi.astype(F32)).astype(BF16)
    return _dot(hi, m01) + _dot(lo, m01)


def _log_sigmoid(x):
    return jnp.minimum(x, 0.0) - jnp.log1p(jnp.exp(-jnp.abs(x)))


def _softplus(x):
    return jnp.maximum(x, 0.0) + jnp.log1p(jnp.exp(-jnp.abs(x)))


def _params(sem):
    return pltpu.CompilerParams(dimension_semantics=sem, vmem_limit_bytes=VMEM_LIMIT)


def _resident(shape, index_map):
    return pl.BlockSpec(shape, index_map, pipeline_mode=pl.Buffered(1))


def _mod_kernel(a_ref, w_ref, b_ref, o_ref):
    a = a_ref[...]
    act = (a * jax.nn.sigmoid(a)).astype(BF16)
    o_ref[0] = _dot(act, w_ref[0].astype(BF16)) + b_ref[0]


def _modulation(cond, w_mod, b_mod):
    depth, d, n6 = w_mod.shape
    rows = cond.shape[0]
    tn = 1536
    return pl.pallas_call(
        _mod_kernel,
        grid=(depth, n6 // tn),
        in_specs=[
            pl.BlockSpec((rows, d), lambda l, j: (0, 0)),
            pl.BlockSpec((1, d, tn), lambda l, j: (l, 0, j)),
            pl.BlockSpec((1, 1, tn), lambda l, j: (l, 0, j)),
        ],
        out_specs=pl.BlockSpec((1, rows, tn), lambda l, j: (l, 0, j)),
        out_shape=jax.ShapeDtypeStruct((depth, rows, n6), F32),
        compiler_params=_params(("arbitrary", "arbitrary")),
    )(cond, w_mod, b_mod.reshape(depth, 1, n6))


def _inproj_kernel(x_ref, mod_ref, g_ref, w_ref, u_ref):
    x = x_ref[0]
    ms = jnp.mean(x * x, axis=-1, keepdims=True)
    y = x * lax.rsqrt(ms + EPS) * g_ref[...]
    mod = mod_ref[0, 0]
    h = (y * (1.0 + mod[1:2, :]) + mod[0:1, :]).astype(BF16)
    u_ref[0] = _dot(h, w_ref[...])


def _inproj(xs, modv, g1, w_in_p, n_ctx):
    b, t, d = xs.shape
    ctx_tiles = n_ctx // ROW_TILE
    return pl.pallas_call(
        _inproj_kernel,
        grid=(b, t // ROW_TILE),
        in_specs=[
            pl.BlockSpec((1, ROW_TILE, d), lambda bi, i: (bi, i, 0)),
            pl.BlockSpec((1, 1, 6, d), lambda bi, i: (bi, jnp.minimum(i // ctx_tiles, 1), 0, 0)),
            pl.BlockSpec((1, d), lambda bi, i: (0, 0)),
            _resident((d, U_COLS), lambda bi, i: (0, 0)),
        ],
        out_specs=pl.BlockSpec((1, ROW_TILE, U_COLS), lambda bi, i: (bi, i, 0)),
        out_shape=jax.ShapeDtypeStruct((b, t, U_COLS), F32),
        compiler_params=_params(("arbitrary", "arbitrary")),
    )(xs, modv, g1.reshape(1, d), w_in_p)


def _mlstm_kernel(u_ref, gr_ref, gb_ref, ng_ref, rrow_ref, rcol_ref, o_ref,
                  qt_s, k_s, vt_s, r_s, row_s, c_s, *, n_ctx_chunks, n_chunks):
    L = M_CHUNK
    W = M_WIDTH
    D = HEAD_DIM
    groups = L // GRID_W

    for h in range(M_HEADS):
        vt_s[h, D:, :] = jnp.ones((M_AUG - D, vt_s.shape[2]), BF16)
    c_s[...] = jnp.zeros(c_s.shape, F32)

    def stage(r0, q, k):
        qt_s[:, pl.ds(r0, L)] = q.T.astype(BF16)
        k_s[pl.ds(r0, L), :] = k.astype(BF16)
        vt = u_ref[0, pl.ds(r0, L), U_MV:U_MV + W].T
        for h in range(M_HEADS):
            vt_s[h, 0:D, pl.ds(r0, L)] = vt[h * D:(h + 1) * D].astype(BF16)

    def prep_ctx(j, carry):
        r0 = pl.multiple_of(j * L, L)
        stage(r0, u_ref[0, pl.ds(r0, L), U_MQ:U_MQ + W], u_ref[0, pl.ds(r0, L), U_MK:U_MK + W] * D ** -0.5)
        return carry

    def prep_lat(j, carry):
        r0 = pl.multiple_of((n_ctx_chunks + j) * L, L)

        def table(i):
            rows = [rrow_ref[i, pl.ds(j * groups + g, 1), :] + rcol_ref[i] for g in range(groups)]
            return jnp.concatenate(rows, axis=0)

        cos, sin_a, sin_b = table(0), table(1), table(2)

        def rope(x):
            return x * cos + pltpu.roll(x, W - D // 4, 1) * sin_a + pltpu.roll(x, D // 4, 1) * sin_b

        stage(r0, rope(u_ref[0, pl.ds(r0, L), U_MQ:U_MQ + W]), rope(u_ref[0, pl.ds(r0, L), U_MK:U_MK + W] * D ** -0.5))
        return carry

    lax.fori_loop(0, n_ctx_chunks, prep_ctx, 0)
    lax.fori_loop(0, n_chunks - n_ctx_chunks, prep_lat, 0)

    si = lax.broadcasted_iota(jnp.int32, (L, L), 0)
    li = lax.broadcasted_iota(jnp.int32, (L, L), 1)
    keeps = [si <= li, si >= li]
    gbias = [jnp.broadcast_to(gb_ref[8 * d:8 * d + 8, :], (8, L)) for d in range(2)]
    zrows = jnp.zeros((D, L), BF16)

    def qt_head(qtc, h):
        parts = [zrows] * M_HEADS
        parts[h] = qtc[h * D:(h + 1) * D]
        return jnp.concatenate(parts, axis=0)

    def local(ci, carry):
        r0s = [pl.multiple_of((ci * M_LOCAL_GROUP + j) * L, L) for j in range(M_LOCAL_GROUP)]
        scores = [[_dot(k_s[pl.ds(r0, L), :], qt_head(qt_s[:, pl.ds(r0, L)], h)) for h in range(M_HEADS)]
                  for r0 in r0s]
        vtas = [[vt_s[h, :, pl.ds(r0, L)] for h in range(M_HEADS)] for r0 in r0s]
        g8s = [[gr_ref[0, 8 * d:8 * d + 8, pl.ds(r0, L)] + gbias[d] for d in range(2)] for r0 in r0s]
        cum4s = [[pltpu.roll(_dot_f32_rhs01(_log_sigmoid(g8[d]), keeps[d].astype(BF16)), M_HEADS, 0)
                  for d in range(2)] for g8 in g8s]
        src_ts = [[(g8[d] - cum4[d]).T for d in range(2)] for g8, cum4 in zip(g8s, cum4s)]
        jdh = [(j, d, h) for j in range(M_LOCAL_GROUP) for d in range(2) for h in range(M_HEADS)]
        logws = [jnp.where(keeps[d], cum4s[j][d][h:h + 1] + jnp.broadcast_to(src_ts[j][d][:, h:h + 1], (L, L)), NEG)
                 for j, d, h in jdh]
        amaxs = [lw.max(axis=0, keepdims=True) for lw in logws]
        ps = [(scores[j][h] * jnp.exp(logw - a)).astype(BF16) for (j, d, h), logw, a in zip(jdh, logws, amaxs)]
        intras = [_dot(vtas[j][h], p) for (j, d, h), p in zip(jdh, ps)]
        for (j, d, h), a, intra in zip(jdh, amaxs, intras):
            last = L - 1 if d == 0 else 0
            cl = cum4s[j][d][h:h + 1]
            a_last = jnp.broadcast_to(a[:, last:last + 1], (1, L))
            cl_last = jnp.broadcast_to(cl[:, last:last + 1], (1, L))
            w_src = jnp.exp(cl_last - cl + g8s[j][d][h:h + 1] - a_last)
            r_s[d, h * M_AUG:(h + 1) * M_AUG, pl.ds(r0s[j], L)] = intra
            for kind, row in enumerate((cl, a, cl_last, a_last, w_src)):
                row_s[d, kind, h:h + 1, pl.ds(r0s[j], L)] = row
        return carry

    lax.fori_loop(0, n_chunks // M_LOCAL_GROUP, local, 0)

    def step(c, d, c_olds, m_prevs):
        r0 = pl.multiple_of(c * L, L)
        kc = k_s[pl.ds(r0, L), :]
        qtc = qt_s[:, pl.ds(r0, L)]
        h_ts, c_news, m_news = [], [], []
        for h in range(M_HEADS):
            cl, a, cl_last, a_last, w_src = [row_s[d, kind, h:h + 1, pl.ds(r0, L)] for kind in range(5)]
            m_prev = m_prevs[h]
            inter = cl + m_prev
            m_t = jnp.maximum(inter, a)
            r = (jnp.exp(inter - m_t) * _dot(c_olds[h].astype(BF16), qt_head(qtc, h))
                 + jnp.exp(a - m_t) * r_s[d, h * M_AUG:(h + 1) * M_AUG, pl.ds(r0, L)])
            denom = jnp.maximum(jnp.abs(r[D:D + 1]), jnp.exp(-m_t))
            h_ts.append(r[0:D] * (1.0 / denom))
            m_new = jnp.maximum(cl_last + m_prev, a_last)
            w_old = jnp.exp(cl_last + m_prev - m_new)
            w_upd = jnp.exp(a_last - m_new)
            vw = (vt_s[h, :, pl.ds(r0, L)].astype(F32) * w_src).astype(BF16)
            scale = lambda w: jnp.concatenate([w] * (W // L), axis=1)
            c_news.append(scale(w_old) * c_olds[h] + scale(w_upd) * _dot(vw, kc))
            m_news.append(m_new)
        return r0, h_ts, c_news, m_news

    def body(i, m_all):
        chunks = [i, jnp.where(i < n_ctx_chunks, n_ctx_chunks - 1 - i, n_chunks - 1 + n_ctx_chunks - i)]
        c_olds = [[c_s[d, h] for h in range(M_HEADS)] for d in range(2)]
        outs = [step(chunks[d], d, c_olds[d], m_all[d]) for d in range(2)]
        for d in range(2):
            r0, h_ts, c_news, _ = outs[d]
            for h in range(M_HEADS):
                r_s[d, h * M_AUG:h * M_AUG + D, pl.ds(r0, L)] = h_ts[h]
                c_s[d, h] = c_news[h]
        return tuple(tuple(outs[d][3]) for d in range(2))

    m0 = tuple(tuple(jnp.zeros((1, L), F32) for _ in range(M_HEADS)) for _ in range(2))
    lax.fori_loop(0, n_chunks, body, m0)

    ng = jnp.broadcast_to(ng_ref[...], (W, L))

    def post(j, carry):
        r0 = pl.multiple_of(j * L, L)
        normed = []
        for h in range(M_HEADS):
            ht = r_s[0, h * M_AUG:h * M_AUG + D, pl.ds(r0, L)] + r_s[1, h * M_AUG:h * M_AUG + D, pl.ds(r0, L)]
            ms = jnp.sum(ht * ht, axis=0, keepdims=True) * (1.0 / D)
            normed.append(ht * lax.rsqrt(ms + EPS))
        hn = jnp.concatenate(normed, axis=0) * ng
        o_ref[0, pl.ds(r0, L), :] = jax.nn.sigmoid(u_ref[0, pl.ds(r0, L), U_MO:U_MO + W]) * hn.T
        return carry

    lax.fori_loop(0, n_chunks, post, 0)


def _rope_tables(n_rows):
    nf = HEAD_DIM // 4
    inv = ROPE_BASE ** (-jnp.arange(nf, dtype=F32) / nf)
    d = np.arange(M_WIDTH) % HEAD_DIM
    f_idx = d % nf
    is_row = d < HEAD_DIM // 2
    first_quarter = (d % (HEAD_DIM // 2)) < nf

    def tables(pos, sel):
        ang = pos[:, None] * inv[None, :]
        ang = ang[:, f_idx]
        cos = jnp.where(sel, jnp.cos(ang), 0.0)
        sin = jnp.where(sel, jnp.sin(ang), 0.0)
        sin_a = jnp.where(first_quarter, -sin, 0.0)
        sin_b = jnp.where(first_quarter, 0.0, sin)
        return jnp.stack([cos, sin_a, sin_b]).astype(F32)

    row_t = tables(jnp.arange(n_rows, dtype=F32), is_row)
    col_t = tables(jnp.arange(GRID_W, dtype=F32), ~is_row)
    return row_t, col_t


def _mlstm(u, g_rows, gate_b, norm_g, rope_row, rope_col, n_ctx):
    b, t, _ = u.shape
    assert t % M_CHUNK == 0 and n_ctx % M_CHUNK == 0 and M_CHUNK % GRID_W == 0
    n_chunks = t // M_CHUNK
    kern = functools.partial(_mlstm_kernel, n_ctx_chunks=n_ctx // M_CHUNK, n_chunks=n_chunks)
    n_rows = rope_row.shape[1]
    return pl.pallas_call(
        kern,
        grid=(b,),
        in_specs=[
            pl.BlockSpec((1, t, 4 * M_WIDTH), lambda bi: (bi, 0, 0), pipeline_mode=pl.Buffered(1)),
            pl.BlockSpec((1, N_GATES, t), lambda bi: (bi, 0, 0)),
            pl.BlockSpec((N_GATES, 1), lambda bi: (0, 0)),
            pl.BlockSpec((M_WIDTH, 1), lambda bi: (0, 0)),
            pl.BlockSpec((3, n_rows, M_WIDTH), lambda bi: (0, 0, 0)),
            pl.BlockSpec((3, GRID_W, M_WIDTH), lambda bi: (0, 0, 0)),
        ],
        out_specs=pl.BlockSpec((1, t, M_WIDTH), lambda bi: (bi, 0, 0)),
        out_shape=jax.ShapeDtypeStruct((b, t, M_WIDTH), F32),
        scratch_shapes=[
            pltpu.VMEM((M_WIDTH, t), BF16),
            pltpu.VMEM((t, M_WIDTH), BF16),
            pltpu.VMEM((M_HEADS, M_AUG, t), BF16),
            pltpu.VMEM((2, M_HEADS * M_AUG, t), F32),
            pltpu.VMEM((2, 5, 8, t), F32),
            pltpu.VMEM((2, M_HEADS, M_AUG, M_WIDTH), F32),
        ],
        compiler_params=_params(("arbitrary",)),
    )(u, g_rows, gate_b.reshape(N_GATES, 1), norm_g.reshape(M_WIDTH, 1), rope_row, rope_col)


def _na_kernel(q_ref, k_ref, v_ref, qg_ref, kg_ref, bias_ref, o_ref, qt_s, kz_s, vt_s, s_s, *, n_ctx, n_rows):
    t = q_ref.shape[1]
    lane = lax.broadcasted_iota(jnp.int32, (1, LANES), 1)
    half = [lane < HEAD_DIM, lane >= HEAD_DIM]
    hi = lax.broadcasted_iota(jnp.int32, (LANES, LANES), 0) // HEAD_DIM
    hj = lax.broadcasted_iota(jnp.int32, (LANES, LANES), 1) // HEAD_DIM
    same_head = (hi == hj).astype(BF16)
    qg = qg_ref[...] * (HEAD_DIM ** -0.5 * LOG2E)
    kg = kg_ref[...]
    qblk = NA_QROWS * GRID_W
    kc = NA_KEY_CHUNK
    sub = 8

    for hh in range(2):
        vt_s[hh, HEAD_DIM:, :] = jnp.ones((vt_s.shape[1] - HEAD_DIM, t), BF16)

    def prep(j, carry):
        r0 = pl.multiple_of(j * ROW_TILE, ROW_TILE)
        q = q_ref[0, pl.ds(r0, ROW_TILE), :]
        k = k_ref[0, pl.ds(r0, ROW_TILE), :]
        qms = _dot_f32_rhs01(q * q, same_head) * (1.0 / HEAD_DIM)
        kms = _dot_f32_rhs01(k * k, same_head) * (1.0 / HEAD_DIM)
        qn = q * lax.rsqrt(qms + EPS) * qg
        kn = k * lax.rsqrt(kms + EPS) * kg
        qt_s[:, pl.ds(r0, ROW_TILE)] = qn.T.astype(BF16)
        vt = v_ref[0, pl.ds(r0, ROW_TILE), :].T
        for hh in range(2):
            kz_s[hh, pl.ds(r0, ROW_TILE), :] = jnp.where(half[hh], kn, 0.0).astype(BF16)
            vt_s[hh, 0:HEAD_DIM, pl.ds(r0, ROW_TILE)] = vt[hh * HEAD_DIM:(hh + 1) * HEAD_DIM].astype(BF16)
        return carry

    lax.fori_loop(0, t // ROW_TILE, prep, 0)

    def attend(blocks):
        jobs = [(b, hh) for b in range(len(blocks)) for hh in range(2)]
        qts = [qt_s[:, pl.ds(q0, qblk)] for q0, _ in blocks]
        n_chunks = len(blocks[0][1])
        mx = [None] * len(jobs)
        for ci in range(n_chunks):
            for ji, (b, hh) in enumerate(jobs):
                k0, tab = blocks[b][1][ci]
                s = _dot(kz_s[hh, pl.ds(k0, kc), :], qts[b])
                if tab is not None:
                    s = s + bias_ref[0, hh, tab[0], tab[1]:tab[1] + kc, :]
                s_s[ji, ci * kc:(ci + 1) * kc, :] = s
                cm = functools.reduce(jnp.maximum, [s[i * sub:(i + 1) * sub] for i in range(kc // sub)])
                mx[ji] = cm if mx[ji] is None else jnp.maximum(mx[ji], cm)
        ms = [m.max(axis=0, keepdims=True) for m in mx]
        accs = [None] * len(jobs)
        for ci in range(n_chunks):
            for ji, (b, hh) in enumerate(jobs):
                k0, _ = blocks[b][1][ci]
                p = jnp.exp2(s_s[ji, ci * kc:(ci + 1) * kc, :] - ms[ji]).astype(BF16)
                pv = _dot(vt_s[hh, :, pl.ds(k0, kc)], p)
                accs[ji] = pv if accs[ji] is None else accs[ji] + pv
        outs = []
        for b in range(len(blocks)):
            heads = [accs[2 * b + hh][0:HEAD_DIM] / accs[2 * b + hh][HEAD_DIM:HEAD_DIM + 1] for hh in range(2)]
            outs.append(jnp.concatenate(heads, axis=0).T)
        return outs

    ctx_chunks = [(ci * kc, None) for ci in range(n_ctx // kc)]
    o_ref[0, 0:n_ctx, :] = attend([(0, ctx_chunks)])[0]

    n_blocks = n_rows // NA_QROWS

    def group(gi, carry):
        blocks = []
        for j in range(NA_GROUP):
            bi = gi * NA_GROUP + j
            r0 = bi * NA_QROWS
            sb = jnp.clip(r0 - WIN_R // 2, 0, n_rows - NA_BAND)
            case = jnp.where(bi == 0, 0, jnp.where(bi == n_blocks - 1, 2, 1))
            q0 = pl.multiple_of(n_ctx + r0 * GRID_W, qblk)
            k0 = pl.multiple_of(n_ctx + sb * GRID_W, qblk)
            local = [(pl.multiple_of(k0 + ci * kc, kc), (case, ci * kc)) for ci in range(NA_BAND * GRID_W // kc)]
            blocks.append((q0, local + ctx_chunks))
        for (q0, _), out in zip(blocks, attend(blocks)):
            o_ref[0, pl.ds(q0, qblk), :] = out
        return carry

    lax.fori_loop(0, n_blocks // NA_GROUP, group, 0)


def _na_bias_table(rpb, n_rows):
    n_heads = rpb.shape[0]
    cols = np.arange(GRID_W)
    col0 = np.clip(cols - WIN_C // 2, 0, GRID_W - WIN_C)
    valid = (cols[:, None] >= col0[None, :]) & (cols[:, None] < col0[None, :] + WIN_C)
    pad = GRID_W - WIN_C
    rp = jnp.pad(rpb, ((0, 0), (0, 0), (pad, pad)))
    toep = jnp.stack([rp[:, :, GRID_W - 1 - c:2 * GRID_W - 1 - c] for c in range(GRID_W)], axis=-1)
    toep = jnp.where(valid, toep, NEG)
    blank = jnp.full((n_heads, GRID_W, GRID_W), NEG, F32)
    last_off = NA_BAND - WIN_R
    cases = []
    for case in range(3):
        rows = []
        for jj in range(NA_BAND):
            tiles = []
            for i in range(NA_QROWS):
                off = (0, i, last_off)[case]
                rel = (jj - i, jj - i - WIN_R // 2, jj - i - NA_BAND + NA_QROWS)[case]
                inside = off <= jj < off + WIN_R
                tiles.append(toep[:, rel + WIN_R - 1] if inside else blank)
            rows.append(jnp.concatenate(tiles, axis=-1))
        cases.append(jnp.concatenate(rows, axis=1))
    return (jnp.stack(cases, axis=1) * LOG2E).astype(F32)


def _na(u, qn_g, kn_g, rpb, n_ctx):
    b, t, _ = u.shape
    n_rows = (t - n_ctx) // GRID_W
    assert n_rows % (NA_QROWS * NA_GROUP) == 0 and n_rows >= NA_BAND and n_ctx == NA_QROWS * GRID_W
    pairs = NA_HEADS // 2
    nk, nq = NA_BAND * GRID_W, NA_QROWS * GRID_W
    bias = _na_bias_table(rpb, n_rows).reshape(pairs, 2, 3, nk, nq)
    qg = jnp.tile(qn_g, 2).reshape(1, LANES)
    kg = jnp.tile(kn_g, 2).reshape(1, LANES)
    kern = functools.partial(_na_kernel, n_ctx=n_ctx, n_rows=n_rows)
    col = lambda base: (lambda hp, bi: (bi, 0, base // LANES + hp))
    return pl.pallas_call(
        kern,
        grid=(pairs, b),
        in_specs=[
            pl.BlockSpec((1, t, LANES), col(U_NQ)),
            pl.BlockSpec((1, t, LANES), col(U_NK)),
            pl.BlockSpec((1, t, LANES), col(U_NV)),
            pl.BlockSpec((1, LANES), lambda hp, bi: (0, 0)),
            pl.BlockSpec((1, LANES), lambda hp, bi: (0, 0)),
            pl.BlockSpec((1, 2, 3, nk, nq), lambda hp, bi: (hp, 0, 0, 0, 0)),
        ],
        out_specs=pl.BlockSpec((1, t, LANES), lambda hp, bi: (bi, 0, hp)),
        out_shape=jax.ShapeDtypeStruct((b, t, NA_WIDTH), F32),
        scratch_shapes=[
            pltpu.VMEM((LANES, t), BF16),
            pltpu.VMEM((2, t, LANES), BF16),
            pltpu.VMEM((2, HEAD_DIM + NA_ONES_ROWS, t), BF16),
            pltpu.VMEM((2 * NA_GROUP, nk + n_ctx, nq), F32),
        ],
        compiler_params=_params(("arbitrary", "arbitrary")),
    )(u, u, u, qg, kg, bias)


def _lru_kernel(x_ref, g_ref, cw_ref, cb_ref, wa_ref, ba_ref, wx_ref, bx_ref, lam_ref, o_ref,
                xp_s, a_s, b_s, h_s, *, n_tiles):
    tl = LRU_TILE
    halo = LRU_HALO
    t = x_ref.shape[1]
    width = x_ref.shape[2]
    group = 8

    xp_s[0:halo, :] = jnp.zeros((halo, width), F32)
    xp_s[halo + t:halo + t + halo, :] = jnp.zeros((halo, width), F32)

    def fill(i, carry):
        r0 = pl.multiple_of(i * tl, tl)
        xp_s[pl.ds(pl.multiple_of(r0 + halo, halo), tl), :] = x_ref[0, pl.ds(r0, tl), :]
        return carry

    lax.fori_loop(0, n_tiles, fill, 0)

    row = lax.broadcasted_iota(jnp.int32, (tl, 1), 0)
    rowg = row % group
    cw = cw_ref[...]
    cb = cb_ref[...]

    def conv_tile(i):
        r0 = pl.multiple_of(i * tl, tl)
        win = xp_s[pl.ds(r0, tl + 2 * halo), :]
        n_win = tl + 2 * halo
        xm2 = pltpu.roll(win, 2, 0)[halo:halo + tl]
        xm1 = pltpu.roll(win, 1, 0)[halo:halo + tl]
        x0 = win[halo:halo + tl]
        xp1 = pltpu.roll(win, n_win - 1, 0)[halo:halo + tl]
        lo = i <= 1
        hi = i == 0
        xm2 = jnp.where(lo & (row < 2), 0.0, xm2)
        xm1 = jnp.where(lo & (row < 1), 0.0, xm1)
        xp1 = jnp.where(hi & (row == tl - 1), 0.0, xp1)
        return xm2 * cw[0:1] + xm1 * cw[1:2] + x0 * cw[2:3] + xp1 * cw[3:4] + cb

    def gates(i, d):
        xc = conv_tile(i)
        xb = xc.astype(BF16)
        r = jax.nn.sigmoid(_dot(xb, wa_ref[d]) + ba_ref[d:d + 1, :])
        ig = jax.nn.sigmoid(_dot(xb, wx_ref[d]) + bx_ref[d:d + 1, :])
        log_a = -LRU_C * r * _softplus(-lam_ref[d:d + 1, :])
        a = jnp.exp(log_a)
        bt = jnp.sqrt(-jnp.tanh(log_a) * (a * a + 1.0)) * (ig * xc)
        return a, bt

    def scan_tile(i, d, carry):
        a, bt = gates(i, d)
        for kk in (1, 2, 4):
            if d == 0:
                a_sh = pltpu.roll(a, kk, 0)
                b_sh = pltpu.roll(bt, kk, 0)
                mk = rowg >= kk
            else:
                a_sh = pltpu.roll(a, tl - kk, 0)
                b_sh = pltpu.roll(bt, tl - kk, 0)
                mk = rowg < group - kk
            bt = jnp.where(mk, a * b_sh + bt, bt)
            a = jnp.where(mk, a * a_sh, a)
        a_s[...] = a
        b_s[...] = bt
        n_groups = tl // group

        def grp(gidx, c):
            gg = gidx if d == 0 else n_groups - 1 - gidx
            g0 = pl.multiple_of(gg * group, group)
            h = b_s[pl.ds(g0, group), :] + a_s[pl.ds(g0, group), :] * c
            h_s[pl.ds(g0, group), :] = h
            return h[group - 1:group, :] if d == 0 else h[0:1, :]

        return lax.fori_loop(0, n_groups, grp, carry)

    def fwd(i, carry):
        carry = scan_tile(i, 0, carry)
        r0 = pl.multiple_of(i * tl, tl)
        o_ref[0, pl.ds(r0, tl), :] = h_s[...]
        return carry

    def bwd(i, carry):
        carry = scan_tile(i, 1, carry)
        r0 = pl.multiple_of(i * tl, tl)
        o_ref[0, pl.ds(r0, tl), :] = (o_ref[0, pl.ds(r0, tl), :] + h_s[...]) * jax.nn.gelu(g_ref[0, pl.ds(r0, tl), :])
        return carry

    h0 = jnp.zeros((1, width), F32)
    lax.fori_loop(0, n_tiles, fwd, h0)
    carry = bwd(jnp.int32(0), h0)
    lax.fori_loop(1, n_tiles, lambda j, c: bwd(n_tiles - j, c), carry)


def _block_diag_dense(w):
    nd, k, c, _ = w.shape
    eye = jnp.eye(k, dtype=w.dtype)
    return jnp.einsum('dkce,kj->dkcje', w, eye).reshape(nd, k * c, k * c)


def _lru(u, conv_w, conv_b, w_a, b_a, w_x, b_x, lam, n_ctx):
    b, t, _ = u.shape
    width = conv_w.shape[1]
    assert n_ctx == LRU_TILE and t % LRU_TILE == 0
    n_tiles = t // LRU_TILE
    wa = _block_diag_dense(w_a).astype(BF16)
    wx = _block_diag_dense(w_x).astype(BF16)
    kern = functools.partial(_lru_kernel, n_tiles=n_tiles)
    full = lambda shape: pl.BlockSpec(shape, lambda bi: (0,) * len(shape))
    return pl.pallas_call(
        kern,
        grid=(b,),
        in_specs=[
            pl.BlockSpec((1, t, width), lambda bi: (bi, 0, U_XR // width)),
            pl.BlockSpec((1, t, width), lambda bi: (bi, 0, U_GT // width)),
            full((CONV_W, width)),
            full((1, width)),
            full((2, width, width)),
            full((2, width)),
            full((2, width, width)),
            full((2, width)),
            full((2, width)),
        ],
        out_specs=pl.BlockSpec((1, t, width), lambda bi: (bi, 0, 0)),
        out_shape=jax.ShapeDtypeStruct((b, t, width), F32),
        scratch_shapes=[
            pltpu.VMEM((t + 2 * LRU_HALO, width), F32),
            pltpu.VMEM((LRU_TILE, width), F32),
            pltpu.VMEM((LRU_TILE, width), F32),
            pltpu.VMEM((LRU_TILE, width), F32),
        ],
        compiler_params=_params(("arbitrary",)),
    )(u, u, conv_w, conv_b.reshape(1, width), wa, b_a, wx, b_x, lam)


def _out_mlp_kernel(x_ref, ym_ref, yn_ref, yl_ref, mod_ref, g_ref, wo_ref, w1_ref, w2_ref, o_ref, *, hid_tile):
    mod = mod_ref[0, 0]
    att = _dot(ym_ref[0].astype(BF16), wo_ref[0:M_WIDTH, :])
    att = att + _dot(yn_ref[0].astype(BF16), wo_ref[M_WIDTH:M_WIDTH + NA_WIDTH, :])
    att = att + _dot(yl_ref[0].astype(BF16), wo_ref[M_WIDTH + NA_WIDTH:, :])
    x1 = x_ref[0] + mod[2:3, :] * att
    ms = jnp.mean(x1 * x1, axis=-1, keepdims=True)
    y = x1 * lax.rsqrt(ms + EPS) * g_ref[...]
    h = (y * (1.0 + mod[4:5, :]) + mod[3:4, :]).astype(BF16)
    hidden = w1_ref.shape[1]
    acc = None
    for j in range(hidden // hid_tile):
        a = jnp.maximum(_dot(h, w1_ref[:, j * hid_tile:(j + 1) * hid_tile]), 0.0)
        part = _dot((a * a).astype(BF16), w2_ref[j * hid_tile:(j + 1) * hid_tile, :])
        acc = part if acc is None else acc + part
    o_ref[0] = x1 + mod[5:6, :] * acc


def _out_mlp(xs, ym, yn, yl, modv, g2, w_out, w1, w2, n_ctx):
    b, t, d = xs.shape
    hidden = w1.shape[1]
    ctx_tiles = n_ctx // ROW_TILE
    tok = lambda w: pl.BlockSpec((1, ROW_TILE, w), lambda bi, i: (bi, i, 0))
    kern = functools.partial(_out_mlp_kernel, hid_tile=1024)
    return pl.pallas_call(
        kern,
        grid=(b, t // ROW_TILE),
        in_specs=[
            tok(d), tok(M_WIDTH), tok(NA_WIDTH), tok(d - M_WIDTH - NA_WIDTH),
            pl.BlockSpec((1, 1, 6, d), lambda bi, i: (bi, jnp.minimum(i // ctx_tiles, 1), 0, 0)),
            pl.BlockSpec((1, d), lambda bi, i: (0, 0)),
            _resident((d, d), lambda bi, i: (0, 0)),
            _resident((d, hidden), lambda bi, i: (0, 0)),
            _resident((hidden, d), lambda bi, i: (0, 0)),
        ],
        out_specs=tok(d),
        out_shape=jax.ShapeDtypeStruct((b, t, d), F32),
        compiler_params=_params(("arbitrary", "arbitrary")),
    )(xs, ym, yn, yl, modv, g2.reshape(1, d), w_out, w1, w2)


def _permute_w_in(w_in):
    m_end = 4 * M_WIDTH
    na_end = m_end + N_GATES + 3 * NA_WIDTH
    parts = [
        w_in[:, :m_end],
        w_in[:, na_end:],
        w_in[:, m_end + N_GATES:na_end],
        w_in[:, m_end:m_end + N_GATES],
        jnp.zeros((w_in.shape[0], U_COLS - U_G - N_GATES), w_in.dtype),
    ]
    return jnp.concatenate(parts, axis=1).astype(BF16)


def kernel(x, c, ctx, c_ctx, w_mod, b_mod, norm1_g, norm2_g, w_in, mlstm_gate_b, mlstm_norm_g, na_q_norm_g, na_k_norm_g, na_rpb, lru_conv_w, lru_conv_b, lru_w_a, lru_b_a, lru_w_x, lru_b_x, lru_lambda, w_out, w_mlp1, w_mlp2):
    b, n, d = x.shape
    n_ctx = ctx.shape[1]
    t = n_ctx + n
    depth = w_in.shape[0]
    assert n_ctx == ROW_TILE and n % ROW_TILE == 0 and n % GRID_W == 0

    xs = jnp.concatenate([ctx, x], axis=1)
    cond_rows = 8 * ((b + 1 + 7) // 8)
    cond = jnp.zeros((cond_rows, d), F32).at[:b].set(c).at[b].set(c_ctx)
    mod_all = _modulation(cond, w_mod, b_mod)
    rope_row, rope_col = _rope_tables(n // GRID_W)

    for l in range(depth):
        lat = mod_all[l, :b].reshape(b, 6, d)
        cx = jnp.broadcast_to(mod_all[l, b].reshape(1, 6, d), (b, 6, d))
        modv = jnp.stack([cx, lat], axis=1)
        u = _inproj(xs, modv, norm1_g[l], _permute_w_in(w_in[l]), n_ctx)
        g_rows = jnp.swapaxes(u[:, :, U_G:U_G + N_GATES], 1, 2)
        ym = _mlstm(u, g_rows, mlstm_gate_b[l], mlstm_norm_g[l], rope_row, rope_col, n_ctx)
        yn = _na(u, na_q_norm_g[l], na_k_norm_g[l], na_rpb[l], n_ctx)
        yl = _lru(u, lru_conv_w[l], lru_conv_b[l], lru_w_a[l], lru_b_a[l], lru_w_x[l], lru_b_x[l], lru_lambda[l], n_ctx)
        xs = _out_mlp(xs, ym, yn, yl, modv, norm2_g[l], w_out[l].astype(BF16), w_mlp1[l].astype(BF16),
                      w_mlp2[l].astype(BF16), n_ctx)
    return xs[:, n_ctx:, :]
```

```python
import functools
import math

import numpy as np
import jax
import jax.numpy as jnp
from jax import lax
from jax.experimental import pallas as pl
from jax.experimental.pallas import tpu as pltpu

F32 = jnp.float32
BF16 = jnp.bfloat16

HEAD_DIM = 64
M_HEADS = 4
M_WIDTH = M_HEADS * HEAD_DIM
NA_HEADS = 8
NA_WIDTH = NA_HEADS * HEAD_DIM
N_GATES = 4 * M_HEADS
GRID_W = 64
M_CHUNK = 128
M_AUG = HEAD_DIM + 16
M_LOCAL_GROUP = 2
WIN_R = 8
WIN_C = 16
RPB_R = 2 * WIN_R - 1
RPB_C = 2 * WIN_C - 1
NA_QROWS = 4
NA_BAND = NA_QROWS + WIN_R
NA_KEY_CHUNK = 128
NA_ONES_ROWS = 16
NA_GROUP = 4
LOG2E = 1.4426950408889634
CONV_W = 4
LRU_C = 8.0
ROPE_BASE = 10000.0
EPS = 1e-6
NEG = -1e30

LANES = 128
ROW_TILE = 256
LRU_TILE = 256
LRU_HALO = 8
VMEM_LIMIT = 56 * 1024 * 1024

U_MQ, U_MK, U_MV, U_MO = 0, 256, 512, 768
U_XR, U_GT = 1024, 1280
U_NQ, U_NK, U_NV = 1536, 2048, 2560
U_COLS = 3072


def _dot(a, b):
    return jnp.dot(a, b, preferred_element_type=F32)


def _dot_f32_rhs01(x, m01):
    hi = x.astype(BF16)
    lo = (x - hi.astype(F32)).astype(BF16)
    return _dot(hi, m01) + _dot(lo, m01)


def _log_sigmoid(x):
    return jnp.minimum(x, 0.0) - jnp.log1p(jnp.exp(-jnp.abs(x)))


def _softplus(x):
    return jnp.maximum(x, 0.0) + jnp.log1p(jnp.exp(-jnp.abs(x)))


def _params(sem):
    return pltpu.CompilerParams(dimension_semantics=sem, vmem_limit_bytes=VMEM_LIMIT)


def _resident(shape, index_map):
    return pl.BlockSpec(shape, index_map, pipeline_mode=pl.Buffered(1))


def _mod_kernel(a_ref, w_ref, b_ref, o_ref):
    a = a_ref[...]
    act = (a * jax.nn.sigmoid(a)).astype(BF16)
    o_ref[0] = _dot(act, w_ref[0].astype(BF16)) + b_ref[0]


def _modulation(cond, w_mod, b_mod):
    depth, d, n6 = w_mod.shape
    rows = cond.shape[0]
    tn = 1536
    return pl.pallas_call(
        _mod_kernel,
        grid=(depth, n6 // tn),
        in_specs=[
            pl.BlockSpec((rows, d), lambda l, j: (0, 0)),
            pl.BlockSpec((1, d, tn), lambda l, j: (l, 0, j)),
            pl.BlockSpec((1, 1, tn), lambda l, j: (l, 0, j)),
        ],
        out_specs=pl.BlockSpec((1, rows, tn), lambda l, j: (l, 0, j)),
        out_shape=jax.ShapeDtypeStruct((depth, rows, n6), F32),
        compiler_params=_params(("arbitrary", "arbitrary")),
    )(cond, w_mod, b_mod.reshape(depth, 1, n6))


def _stream_specs(n_ctx, d):
    assert n_ctx == ROW_TILE
    return [
        pl.BlockSpec((1, ROW_TILE, d), lambda bi, i: (bi, 0, 0)),
        pl.BlockSpec((1, ROW_TILE, d), lambda bi, i: (bi, jnp.maximum(i - 1, 0), 0)),
    ]


def _stream_tile(xc_ref, xl_ref):
    return jnp.where(pl.program_id(1) == 0, xc_ref[0], xl_ref[0])


def _inproj_kernel(xc_ref, xl_ref, mod_ref, g_ref, w_ref, wg_ref, u_ref, gt_ref):
    x = _stream_tile(xc_ref, xl_ref)
    ms = jnp.mean(x * x, axis=-1, keepdims=True)
    y = x * lax.rsqrt(ms + EPS) * g_ref[...]
    mod = mod_ref[0, 0]
    h = (y * (1.0 + mod[1:2, :]) + mod[0:1, :]).astype(BF16)
    u_ref[0] = _dot(h, w_ref[...])
    gt_ref[0] = lax.dot_general(wg_ref[...], h, (((1,), (1,)), ((), ())), preferred_element_type=F32)


def _inproj(xc, xl, modv, g1, w_in_p, w_gate_t):
    b, n_ctx, d = xc.shape
    t = n_ctx + xl.shape[1]
    return pl.pallas_call(
        _inproj_kernel,
        grid=(b, t // ROW_TILE),
        in_specs=_stream_specs(n_ctx, d) + [
            pl.BlockSpec((1, 1, 6, d), lambda bi, i: (bi, jnp.minimum(i, 1), 0, 0)),
            pl.BlockSpec((1, d), lambda bi, i: (0, 0)),
            _resident((d, U_COLS), lambda bi, i: (0, 0)),
            _resident((N_GATES, d), lambda bi, i: (0, 0)),
        ],
        out_specs=[
            pl.BlockSpec((1, ROW_TILE, U_COLS), lambda bi, i: (bi, i, 0)),
            pl.BlockSpec((1, N_GATES, ROW_TILE), lambda bi, i: (bi, 0, i)),
        ],
        out_shape=[
            jax.ShapeDtypeStruct((b, t, U_COLS), F32),
            jax.ShapeDtypeStruct((b, N_GATES, t), F32),
        ],
        compiler_params=_params(("arbitrary", "arbitrary")),
    )(xc, xl, modv, g1.reshape(1, d), w_in_p, w_gate_t)


def _mlstm_kernel(u_ref, gr_ref, gb_ref, ng_ref, rrow_ref, rcol_ref, o_ref,
                  qt_s, k_s, vt_s, r_s, row_s, c_s, *, n_ctx_chunks, n_chunks):
    L = M_CHUNK
    W = M_WIDTH
    D = HEAD_DIM
    groups = L // GRID_W

    for h in range(M_HEADS):
        vt_s[h, D:, :] = jnp.ones((M_AUG - D, vt_s.shape[2]), BF16)
    c_s[...] = jnp.zeros(c_s.shape, F32)

    def stage(r0, q, k):
        qt_s[:, pl.ds(r0, L)] = q.T.astype(BF16)
        k_s[pl.ds(r0, L), :] = k.astype(BF16)
        vt = u_ref[0, pl.ds(r0, L), U_MV:U_MV + W].T
        for h in range(M_HEADS):
            vt_s[h, 0:D, pl.ds(r0, L)] = vt[h * D:(h + 1) * D].astype(BF16)

    def prep_ctx(j, carry):
        r0 = pl.multiple_of(j * L, L)
        stage(r0, u_ref[0, pl.ds(r0, L), U_MQ:U_MQ + W], u_ref[0, pl.ds(r0, L), U_MK:U_MK + W] * D ** -0.5)
        return carry

    def prep_lat(j, carry):
        r0 = pl.multiple_of((n_ctx_chunks + j) * L, L)

        def table(i):
            rows = [rrow_ref[i, pl.ds(j * groups + g, 1), :] + rcol_ref[i] for g in range(groups)]
            return jnp.concatenate(rows, axis=0)

        cos, sin_a, sin_b = table(0), table(1), table(2)

        def rope(x):
            return x * cos + pltpu.roll(x, W - D // 4, 1) * sin_a + pltpu.roll(x, D // 4, 1) * sin_b

        stage(r0, rope(u_ref[0, pl.ds(r0, L), U_MQ:U_MQ + W]), rope(u_ref[0, pl.ds(r0, L), U_MK:U_MK + W] * D ** -0.5))
        return carry

    lax.fori_loop(0, n_ctx_chunks, prep_ctx, 0)
    lax.fori_loop(0, n_chunks - n_ctx_chunks, prep_lat, 0)

    si = lax.broadcasted_iota(jnp.int32, (L, L), 0)
    li = lax.broadcasted_iota(jnp.int32, (L, L), 1)
    keeps = [si <= li, si >= li]
    gbias = [jnp.broadcast_to(gb_ref[8 * d:8 * d + 8, :], (8, L)) for d in range(2)]
    zrows = jnp.zeros((D, L), BF16)

    def qt_head(qtc, h):
        parts = [zrows] * M_HEADS
        parts[h] = qtc[h * D:(h + 1) * D]
        return jnp.concatenate(parts, axis=0)

    def local(ci, carry):
        r0s = [pl.multiple_of((ci * M_LOCAL_GROUP + j) * L, L) for j in range(M_LOCAL_GROUP)]
        scores = [[_dot(k_s[pl.ds(r0, L), :], qt_head(qt_s[:, pl.ds(r0, L)], h)) for h in range(M_HEADS)]
                  for r0 in r0s]
        vtas = [[vt_s[h, :, pl.ds(r0, L)] for h in range(M_HEADS)] for r0 in r0s]
        g8s = [[gr_ref[0, 8 * d:8 * d + 8, pl.ds(r0, L)] + gbias[d] for d in range(2)] for r0 in r0s]
        cum4s = [[pltpu.roll(_dot_f32_rhs01(_log_sigmoid(g8[d]), keeps[d].astype(BF16)), M_HEADS, 0)
                  for d in range(2)] for g8 in g8s]
        src_ts = [[(g8[d] - cum4[d]).T for d in range(2)] for g8, cum4 in zip(g8s, cum4s)]
        jdh = [(j, d, h) for j in range(M_LOCAL_GROUP) for d in range(2) for h in range(M_HEADS)]
        logws = [jnp.where(keeps[d], cum4s[j][d][h:h + 1] + jnp.broadcast_to(src_ts[j][d][:, h:h + 1], (L, L)), NEG)
                 for j, d, h in jdh]
        amaxs = [lw.max(axis=0, keepdims=True) for lw in logws]
        ps = [(scores[j][h] * jnp.exp(logw - a)).astype(BF16) for (j, d, h), logw, a in zip(jdh, logws, amaxs)]
        intras = [_dot(vtas[j][h], p) for (j, d, h), p in zip(jdh, ps)]
        for (j, d, h), a, intra in zip(jdh, amaxs, intras):
            last = L - 1 if d == 0 else 0
            cl = cum4s[j][d][h:h + 1]
            a_last = jnp.broadcast_to(a[:, last:last + 1], (1, L))
            cl_last = jnp.broadcast_to(cl[:, last:last + 1], (1, L))
            w_src = jnp.exp(cl_last - cl + g8s[j][d][h:h + 1] - a_last)
            r_s[d, h * M_AUG:(h + 1) * M_AUG, pl.ds(r0s[j], L)] = intra
            for kind, row in enumerate((cl, a, cl_last, a_last, w_src)):
                row_s[d, kind, h:h + 1, pl.ds(r0s[j], L)] = row
        return carry

    lax.fori_loop(0, n_chunks // M_LOCAL_GROUP, local, 0)

    def step(c, d, c_olds, m_prevs):
        r0 = pl.multiple_of(c * L, L)
        kc = k_s[pl.ds(r0, L), :]
        qtc = qt_s[:, pl.ds(r0, L)]
        h_ts, c_news, m_news = [], [], []
        for h in range(M_HEADS):
            cl, a, cl_last, a_last, w_src = [row_s[d, kind, h:h + 1, pl.ds(r0, L)] for kind in range(5)]
            m_prev = m_prevs[h]
            inter = cl + m_prev
            m_t = jnp.maximum(inter, a)
            r = (jnp.exp(inter - m_t) * _dot(c_olds[h].astype(BF16), qt_head(qtc, h))
                 + jnp.exp(a - m_t) * r_s[d, h * M_AUG:(h + 1) * M_AUG, pl.ds(r0, L)])
            denom = jnp.maximum(jnp.abs(r[D:D + 1]), jnp.exp(-m_t))
            h_ts.append(r[0:D] * (1.0 / denom))
            m_new = jnp.maximum(cl_last + m_prev, a_last)
            w_old = jnp.exp(cl_last + m_prev - m_new)
            w_upd = jnp.exp(a_last - m_new)
            vw = (vt_s[h, :, pl.ds(r0, L)].astype(F32) * w_src).astype(BF16)
            scale = lambda w: jnp.concatenate([w] * (W // L), axis=1)
            c_news.append(scale(w_old) * c_olds[h] + scale(w_upd) * _dot(vw, kc))
            m_news.append(m_new)
        return r0, h_ts, c_news, m_news

    def body(i, m_all):
        chunks = [i, jnp.where(i < n_ctx_chunks, n_ctx_chunks - 1 - i, n_chunks - 1 + n_ctx_chunks - i)]
        c_olds = [[c_s[d, h] for h in range(M_HEADS)] for d in range(2)]
        outs = [step(chunks[d], d, c_olds[d], m_all[d]) for d in range(2)]
        for d in range(2):
            r0, h_ts, c_news, _ = outs[d]
            for h in range(M_HEADS):
                r_s[d, h * M_AUG:h * M_AUG + D, pl.ds(r0, L)] = h_ts[h]
                c_s[d, h] = c_news[h]
        return tuple(tuple(outs[d][3]) for d in range(2))

    m0 = tuple(tuple(jnp.zeros((1, L), F32) for _ in range(M_HEADS)) for _ in range(2))
    lax.fori_loop(0, n_chunks, body, m0)

    ng = jnp.broadcast_to(ng_ref[...], (W, L))

    def post(j, carry):
        r0 = pl.multiple_of(j * L, L)
        normed = []
        for h in range(M_HEADS):
            ht = r_s[0, h * M_AUG:h * M_AUG + D, pl.ds(r0, L)] + r_s[1, h * M_AUG:h * M_AUG + D, pl.ds(r0, L)]
            ms = jnp.sum(ht * ht, axis=0, keepdims=True) * (1.0 / D)
            normed.append(ht * lax.rsqrt(ms + EPS))
        hn = jnp.concatenate(normed, axis=0) * ng
        o_ref[0, pl.ds(r0, L), :] = jax.nn.sigmoid(u_ref[0, pl.ds(r0, L), U_MO:U_MO + W]) * hn.T
        return carry

    lax.fori_loop(0, n_chunks, post, 0)


def _rope_tables(n_rows):
    nf = HEAD_DIM // 4
    inv = ROPE_BASE ** (-jnp.arange(nf, dtype=F32) / nf)
    d = np.arange(M_WIDTH) % HEAD_DIM
    f_idx = d % nf
    is_row = d < HEAD_DIM // 2
    first_quarter = (d % (HEAD_DIM // 2)) < nf

    def tables(pos, sel):
        ang = pos[:, None] * inv[None, :]
        ang = ang[:, f_idx]
        cos = jnp.where(sel, jnp.cos(ang), 0.0)
        sin = jnp.where(sel, jnp.sin(ang), 0.0)
        sin_a = jnp.where(first_quarter, -sin, 0.0)
        sin_b = jnp.where(first_quarter, 0.0, sin)
        return jnp.stack([cos, sin_a, sin_b]).astype(F32)

    row_t = tables(jnp.arange(n_rows, dtype=F32), is_row)
    col_t = tables(jnp.arange(GRID_W, dtype=F32), ~is_row)
    return row_t, col_t


def _mlstm(u, g_rows, gate_b, norm_g, rope_row, rope_col, n_ctx):
    b, t, _ = u.shape
    assert t % M_CHUNK == 0 and n_ctx % M_CHUNK == 0 and M_CHUNK % GRID_W == 0
    n_chunks = t // M_CHUNK
    kern = functools.partial(_mlstm_kernel, n_ctx_chunks=n_ctx // M_CHUNK, n_chunks=n_chunks)
    n_rows = rope_row.shape[1]
    return pl.pallas_call(
        kern,
        grid=(b,),
        in_specs=[
            pl.BlockSpec((1, t, 4 * M_WIDTH), lambda bi: (bi, 0, 0), pipeline_mode=pl.Buffered(1)),
            pl.BlockSpec((1, N_GATES, t), lambda bi: (bi, 0, 0)),
            pl.BlockSpec((N_GATES, 1), lambda bi: (0, 0)),
            pl.BlockSpec((M_WIDTH, 1), lambda bi: (0, 0)),
            pl.BlockSpec((3, n_rows, M_WIDTH), lambda bi: (0, 0, 0)),
            pl.BlockSpec((3, GRID_W, M_WIDTH), lambda bi: (0, 0, 0)),
        ],
        out_specs=pl.BlockSpec((1, t, M_WIDTH), lambda bi: (bi, 0, 0)),
        out_shape=jax.ShapeDtypeStruct((b, t, M_WIDTH), F32),
        scratch_shapes=[
            pltpu.VMEM((M_WIDTH, t), BF16),
            pltpu.VMEM((t, M_WIDTH), BF16),
            pltpu.VMEM((M_HEADS, M_AUG, t), BF16),
            pltpu.VMEM((2, M_HEADS * M_AUG, t), F32),
            pltpu.VMEM((2, 5, 8, t), F32),
            pltpu.VMEM((2, M_HEADS, M_AUG, M_WIDTH), F32),
        ],
        compiler_params=_params(("arbitrary",)),
    )(u, g_rows, gate_b.reshape(N_GATES, 1), norm_g.reshape(M_WIDTH, 1), rope_row, rope_col)


def _na_kernel(q_ref, k_ref, v_ref, qg_ref, kg_ref, bias_ref, o_ref, qt_s, kz_s, vt_s, s_s, *, n_ctx, n_rows):
    t = q_ref.shape[1]
    lane = lax.broadcasted_iota(jnp.int32, (1, LANES), 1)
    half = [lane < HEAD_DIM, lane >= HEAD_DIM]
    hi = lax.broadcasted_iota(jnp.int32, (LANES, LANES), 0) // HEAD_DIM
    hj = lax.broadcasted_iota(jnp.int32, (LANES, LANES), 1) // HEAD_DIM
    same_head = (hi == hj).astype(BF16)
    qg = qg_ref[...] * (HEAD_DIM ** -0.5 * LOG2E)
    kg = kg_ref[...]
    qblk = NA_QROWS * GRID_W
    kc = NA_KEY_CHUNK
    sub = 8

    for hh in range(2):
        vt_s[hh, HEAD_DIM:, :] = jnp.ones((vt_s.shape[1] - HEAD_DIM, t), BF16)

    def prep(j, carry):
        r0 = pl.multiple_of(j * ROW_TILE, ROW_TILE)
        q = q_ref[0, pl.ds(r0, ROW_TILE), :]
        k = k_ref[0, pl.ds(r0, ROW_TILE), :]
        qms = _dot_f32_rhs01(q * q, same_head) * (1.0 / HEAD_DIM)
        kms = _dot_f32_rhs01(k * k, same_head) * (1.0 / HEAD_DIM)
        qn = q * lax.rsqrt(qms + EPS) * qg
        kn = k * lax.rsqrt(kms + EPS) * kg
        qt_s[:, pl.ds(r0, ROW_TILE)] = qn.T.astype(BF16)
        vt = v_ref[0, pl.ds(r0, ROW_TILE), :].T
        for hh in range(2):
            kz_s[hh, pl.ds(r0, ROW_TILE), :] = jnp.where(half[hh], kn, 0.0).astype(BF16)
            vt_s[hh, 0:HEAD_DIM, pl.ds(r0, ROW_TILE)] = vt[hh * HEAD_DIM:(hh + 1) * HEAD_DIM].astype(BF16)
        return carry

    lax.fori_loop(0, t // ROW_TILE, prep, 0)

    def attend(blocks):
        jobs = [(b, hh) for b in range(len(blocks)) for hh in range(2)]
        qts = [qt_s[:, pl.ds(q0, qblk)] for q0, _ in blocks]
        n_chunks = len(blocks[0][1])
        mx = [None] * len(jobs)
        for ci in range(n_chunks):
            for ji, (b, hh) in enumerate(jobs):
                k0, tab = blocks[b][1][ci]
                s = _dot(kz_s[hh, pl.ds(k0, kc), :], qts[b])
                if tab is not None:
                    s = s + bias_ref[0, hh, tab[0], tab[1]:tab[1] + kc, :]
                s_s[ji, ci * kc:(ci + 1) * kc, :] = s
                cm = functools.reduce(jnp.maximum, [s[i * sub:(i + 1) * sub] for i in range(kc // sub)])
                mx[ji] = cm if mx[ji] is None else jnp.maximum(mx[ji], cm)
        ms = [m.max(axis=0, keepdims=True) for m in mx]
        accs = [None] * len(jobs)
        for ci in range(n_chunks):
            for ji, (b, hh) in enumerate(jobs):
                k0, _ = blocks[b][1][ci]
                p = jnp.exp2(s_s[ji, ci * kc:(ci + 1) * kc, :] - ms[ji]).astype(BF16)
                pv = _dot(vt_s[hh, :, pl.ds(k0, kc)], p)
                accs[ji] = pv if accs[ji] is None else accs[ji] + pv
        outs = []
        for b in range(len(blocks)):
            heads = [accs[2 * b + hh][0:HEAD_DIM] / accs[2 * b + hh][HEAD_DIM:HEAD_DIM + 1] for hh in range(2)]
            outs.append(jnp.concatenate(heads, axis=0).T)
        return outs

    ctx_chunks = [(ci * kc, None) for ci in range(n_ctx // kc)]
    o_ref[0, 0:n_ctx, :] = attend([(0, ctx_chunks)])[0]

    n_blocks = n_rows // NA_QROWS

    def group(gi, carry):
        blocks = []
        for j in range(NA_GROUP):
            bi = gi * NA_GROUP + j
            r0 = bi * NA_QROWS
            sb = jnp.clip(r0 - WIN_R // 2, 0, n_rows - NA_BAND)
            case = jnp.where(bi == 0, 0, jnp.where(bi == n_blocks - 1, 2, 1))
            q0 = pl.multiple_of(n_ctx + r0 * GRID_W, qblk)
            k0 = pl.multiple_of(n_ctx + sb * GRID_W, qblk)
            local = [(pl.multiple_of(k0 + ci * kc, kc), (case, ci * kc)) for ci in range(NA_BAND * GRID_W // kc)]
            blocks.append((q0, local + ctx_chunks))
        for (q0, _), out in zip(blocks, attend(blocks)):
            o_ref[0, pl.ds(q0, qblk), :] = out
        return carry

    lax.fori_loop(0, n_blocks // NA_GROUP, group, 0)


def _na_bias_tables(rpb):
    cols = np.arange(GRID_W)
    col0 = np.clip(cols - WIN_C // 2, 0, GRID_W - WIN_C)
    valid = (cols[:, None] >= col0[None, :]) & (cols[:, None] < col0[None, :] + WIN_C)
    dc = cols[:, None] - cols[None, :] + WIN_C - 1
    col_sel = ((dc[..., None] == np.arange(RPB_C)) & valid[..., None]).astype(np.float32)
    row_sel = np.zeros((3, NA_BAND, NA_QROWS, RPB_R), np.float32)
    for case in range(3):
        for jj in range(NA_BAND):
            for i in range(NA_QROWS):
                off = (0, i, NA_BAND - WIN_R)[case]
                rel = (jj - i, jj - i - WIN_R // 2, jj - i - NA_BAND + NA_QROWS)[case]
                if off <= jj < off + WIN_R:
                    row_sel[case, jj, i, rel + WIN_R - 1] = 1.0
    in_window = row_sel.sum(-1)[:, :, None, :, None] * valid[None, None, :, None, :]
    tab = jnp.einsum('kjir,lhrd,xcd->lhkjxic', row_sel, rpb, col_sel, precision=lax.Precision.HIGHEST)
    tab = jnp.where(in_window > 0, tab * LOG2E, NEG * LOG2E).astype(F32)
    n_layers, n_heads = rpb.shape[:2]
    return tab.reshape(n_layers, n_heads, 3, NA_BAND * GRID_W, NA_QROWS * GRID_W)


def _na(u, qn_g, kn_g, bias, n_ctx):
    b, t, _ = u.shape
    n_rows = (t - n_ctx) // GRID_W
    assert n_rows % (NA_QROWS * NA_GROUP) == 0 and n_rows >= NA_BAND and n_ctx == NA_QROWS * GRID_W
    pairs = NA_HEADS // 2
    nk, nq = NA_BAND * GRID_W, NA_QROWS * GRID_W
    bias = bias.reshape(pairs, 2, 3, nk, nq)
    qg = jnp.tile(qn_g, 2).reshape(1, LANES)
    kg = jnp.tile(kn_g, 2).reshape(1, LANES)
    kern = functools.partial(_na_kernel, n_ctx=n_ctx, n_rows=n_rows)
    col = lambda base: (lambda hp, bi: (bi, 0, base // LANES + hp))
    return pl.pallas_call(
        kern,
        grid=(pairs, b),
        in_specs=[
            pl.BlockSpec((1, t, LANES), col(U_NQ)),
            pl.BlockSpec((1, t, LANES), col(U_NK)),
            pl.BlockSpec((1, t, LANES), col(U_NV)),
            pl.BlockSpec((1, LANES), lambda hp, bi: (0, 0)),
            pl.BlockSpec((1, LANES), lambda hp, bi: (0, 0)),
            pl.BlockSpec((1, 2, 3, nk, nq), lambda hp, bi: (hp, 0, 0, 0, 0)),
        ],
        out_specs=pl.BlockSpec((1, t, LANES), lambda hp, bi: (bi, 0, hp)),
        out_shape=jax.ShapeDtypeStruct((b, t, NA_WIDTH), F32),
        scratch_shapes=[
            pltpu.VMEM((LANES, t), BF16),
            pltpu.VMEM((2, t, LANES), BF16),
            pltpu.VMEM((2, HEAD_DIM + NA_ONES_ROWS, t), BF16),
            pltpu.VMEM((2 * NA_GROUP, nk + n_ctx, nq), F32),
        ],
        compiler_params=_params(("arbitrary", "arbitrary")),
    )(u, u, u, qg, kg, bias)


def _lru_kernel(x_ref, g_ref, cw_ref, cb_ref, wa_ref, ba_ref, wx_ref, bx_ref, lam_ref, o_ref,
                xp_s, a_s, b_s, h_s, *, n_tiles):
    tl = LRU_TILE
    halo = LRU_HALO
    t = x_ref.shape[1]
    width = x_ref.shape[2]
    group = 8

    xp_s[0:halo, :] = jnp.zeros((halo, width), F32)
    xp_s[halo + t:halo + t + halo, :] = jnp.zeros((halo, width), F32)

    def fill(i, carry):
        r0 = pl.multiple_of(i * tl, tl)
        xp_s[pl.ds(pl.multiple_of(r0 + halo, halo), tl), :] = x_ref[0, pl.ds(r0, tl), :]
        return carry

    lax.fori_loop(0, n_tiles, fill, 0)

    row = lax.broadcasted_iota(jnp.int32, (tl, 1), 0)
    rowg = row % group
    cw = cw_ref[...]
    cb = cb_ref[...]

    def conv_tile(i):
        r0 = pl.multiple_of(i * tl, tl)
        win = xp_s[pl.ds(r0, tl + 2 * halo), :]
        n_win = tl + 2 * halo
        xm2 = pltpu.roll(win, 2, 0)[halo:halo + tl]
        xm1 = pltpu.roll(win, 1, 0)[halo:halo + tl]
        x0 = win[halo:halo + tl]
        xp1 = pltpu.roll(win, n_win - 1, 0)[halo:halo + tl]
        lo = i <= 1
        hi = i == 0
        xm2 = jnp.where(lo & (row < 2), 0.0, xm2)
        xm1 = jnp.where(lo & (row < 1), 0.0, xm1)
        xp1 = jnp.where(hi & (row == tl - 1), 0.0, xp1)
        return xm2 * cw[0:1] + xm1 * cw[1:2] + x0 * cw[2:3] + xp1 * cw[3:4] + cb

    def gates(i, d):
        xc = conv_tile(i)
        xb = xc.astype(BF16)
        r = jax.nn.sigmoid(_dot(xb, wa_ref[d]) + ba_ref[d:d + 1, :])
        ig = jax.nn.sigmoid(_dot(xb, wx_ref[d]) + bx_ref[d:d + 1, :])
        log_a = -LRU_C * r * _softplus(-lam_ref[d:d + 1, :])
        a = jnp.exp(log_a)
        bt = jnp.sqrt(-jnp.tanh(log_a) * (a * a + 1.0)) * (ig * xc)
        return a, bt

    def scan_tile(i, d, carry):
        a, bt = gates(i, d)
        for kk in (1, 2, 4):
            if d == 0:
                a_sh = pltpu.roll(a, kk, 0)
                b_sh = pltpu.roll(bt, kk, 0)
                mk = rowg >= kk
            else:
                a_sh = pltpu.roll(a, tl - kk, 0)
                b_sh = pltpu.roll(bt, tl - kk, 0)
                mk = rowg < group - kk
            bt = jnp.where(mk, a * b_sh + bt, bt)
            a = jnp.where(mk, a * a_sh, a)
        a_s[...] = a
        b_s[...] = bt
        n_groups = tl // group

        def grp(gidx, c):
            gg = gidx if d == 0 else n_groups - 1 - gidx
            g0 = pl.multiple_of(gg * group, group)
            h = b_s[pl.ds(g0, group), :] + a_s[pl.ds(g0, group), :] * c
            h_s[pl.ds(g0, group), :] = h
            return h[group - 1:group, :] if d == 0 else h[0:1, :]

        return lax.fori_loop(0, n_groups, grp, carry)

    def fwd(i, carry):
        carry = scan_tile(i, 0, carry)
        r0 = pl.multiple_of(i * tl, tl)
        o_ref[0, pl.ds(r0, tl), :] = h_s[...]
        return carry

    def bwd(i, carry):
        carry = scan_tile(i, 1, carry)
        r0 = pl.multiple_of(i * tl, tl)
        o_ref[0, pl.ds(r0, tl), :] = (o_ref[0, pl.ds(r0, tl), :] + h_s[...]) * jax.nn.gelu(g_ref[0, pl.ds(r0, tl), :])
        return carry

    h0 = jnp.zeros((1, width), F32)
    lax.fori_loop(0, n_tiles, fwd, h0)
    carry = bwd(jnp.int32(0), h0)
    lax.fori_loop(1, n_tiles, lambda j, c: bwd(n_tiles - j, c), carry)


def _block_diag_dense(w):
    nd, k, c, _ = w.shape
    eye = jnp.eye(k, dtype=w.dtype)
    return jnp.einsum('dkce,kj->dkcje', w, eye).reshape(nd, k * c, k * c)


def _lru(u, conv_w, conv_b, w_a, b_a, w_x, b_x, lam, n_ctx):
    b, t, _ = u.shape
    width = conv_w.shape[1]
    assert n_ctx == LRU_TILE and t % LRU_TILE == 0
    n_tiles = t // LRU_TILE
    wa = _block_diag_dense(w_a).astype(BF16)
    wx = _block_diag_dense(w_x).astype(BF16)
    kern = functools.partial(_lru_kernel, n_tiles=n_tiles)
    full = lambda shape: pl.BlockSpec(shape, lambda bi: (0,) * len(shape))
    return pl.pallas_call(
        kern,
        grid=(b,),
        in_specs=[
            pl.BlockSpec((1, t, width), lambda bi: (bi, 0, U_XR // width)),
            pl.BlockSpec((1, t, width), lambda bi: (bi, 0, U_GT // width)),
            full((CONV_W, width)),
            full((1, width)),
            full((2, width, width)),
            full((2, width)),
            full((2, width, width)),
            full((2, width)),
            full((2, width)),
        ],
        out_specs=pl.BlockSpec((1, t, width), lambda bi: (bi, 0, 0)),
        out_shape=jax.ShapeDtypeStruct((b, t, width), F32),
        scratch_shapes=[
            pltpu.VMEM((t + 2 * LRU_HALO, width), F32),
            pltpu.VMEM((LRU_TILE, width), F32),
            pltpu.VMEM((LRU_TILE, width), F32),
            pltpu.VMEM((LRU_TILE, width), F32),
        ],
        compiler_params=_params(("arbitrary",)),
    )(u, u, conv_w, conv_b.reshape(1, width), wa, b_a, wx, b_x, lam)


def _out_mlp_kernel(xc_ref, xl_ref, ym_ref, yn_ref, yl_ref, mod_ref, g_ref, wo_ref, w1_ref, w2_ref, oc_ref, ol_ref,
                    *, hid_tile):
    mod = mod_ref[0, 0]
    att = _dot(ym_ref[0].astype(BF16), wo_ref[0:M_WIDTH, :])
    att = att + _dot(yn_ref[0].astype(BF16), wo_ref[M_WIDTH:M_WIDTH + NA_WIDTH, :])
    att = att + _dot(yl_ref[0].astype(BF16), wo_ref[M_WIDTH + NA_WIDTH:, :])
    x1 = _stream_tile(xc_ref, xl_ref) + mod[2:3, :] * att
    ms = jnp.mean(x1 * x1, axis=-1, keepdims=True)
    y = x1 * lax.rsqrt(ms + EPS) * g_ref[...]
    h = (y * (1.0 + mod[4:5, :]) + mod[3:4, :]).astype(BF16)
    hidden = w1_ref.shape[1]
    acc = None
    for j in range(hidden // hid_tile):
        a = jnp.maximum(_dot(h, w1_ref[:, j * hid_tile:(j + 1) * hid_tile]), 0.0)
        part = _dot((a * a).astype(BF16), w2_ref[j * hid_tile:(j + 1) * hid_tile, :])
        acc = part if acc is None else acc + part
    out = x1 + mod[5:6, :] * acc

    @pl.when(pl.program_id(1) == 0)
    def _():
        oc_ref[0] = out

    @pl.when(pl.program_id(1) > 0)
    def _():
        ol_ref[0] = out


def _out_mlp(xc, xl, ym, yn, yl, modv, g2, w_out, w1, w2):
    b, n_ctx, d = xc.shape
    t = n_ctx + xl.shape[1]
    hidden = w1.shape[1]
    tok = lambda w: pl.BlockSpec((1, ROW_TILE, w), lambda bi, i: (bi, i, 0))
    kern = functools.partial(_out_mlp_kernel, hid_tile=1024)
    return pl.pallas_call(
        kern,
        grid=(b, t // ROW_TILE),
        in_specs=_stream_specs(n_ctx, d) + [
            tok(M_WIDTH), tok(NA_WIDTH), tok(d - M_WIDTH - NA_WIDTH),
            pl.BlockSpec((1, 1, 6, d), lambda bi, i: (bi, jnp.minimum(i, 1), 0, 0)),
            pl.BlockSpec((1, d), lambda bi, i: (0, 0)),
            _resident((d, d), lambda bi, i: (0, 0)),
            _resident((d, hidden), lambda bi, i: (0, 0)),
            _resident((hidden, d), lambda bi, i: (0, 0)),
        ],
        out_specs=_stream_specs(n_ctx, d),
        out_shape=[jax.ShapeDtypeStruct(xc.shape, F32), jax.ShapeDtypeStruct(xl.shape, F32)],
        compiler_params=_params(("arbitrary", "arbitrary")),
    )(xc, xl, ym, yn, yl, modv, g2.reshape(1, d), w_out, w1, w2)


def _split_w_in(w_in):
    m_end = 4 * M_WIDTH
    na_end = m_end + N_GATES + 3 * NA_WIDTH
    main = jnp.concatenate([w_in[..., :m_end], w_in[..., na_end:], w_in[..., m_end + N_GATES:na_end]], axis=-1)
    gates = jnp.swapaxes(w_in[..., m_end:m_end + N_GATES], -1, -2)
    return main.astype(BF16), gates.astype(BF16)


def kernel(x, c, ctx, c_ctx, w_mod, b_mod, norm1_g, norm2_g, w_in, mlstm_gate_b, mlstm_norm_g, na_q_norm_g, na_k_norm_g, na_rpb, lru_conv_w, lru_conv_b, lru_w_a, lru_b_a, lru_w_x, lru_b_x, lru_lambda, w_out, w_mlp1, w_mlp2):
    b, n, d = x.shape
    n_ctx = ctx.shape[1]
    depth = w_in.shape[0]
    assert n_ctx == ROW_TILE and n % ROW_TILE == 0 and n % GRID_W == 0

    cond_rows = 8 * ((b + 1 + 7) // 8)
    cond = jnp.zeros((cond_rows, d), F32).at[:b].set(c).at[b].set(c_ctx)
    mod_all = _modulation(cond, w_mod, b_mod)
    rope_row, rope_col = _rope_tables(n // GRID_W)
    w_in_p, w_gate_t = _split_w_in(w_in)
    na_bias = _na_bias_tables(na_rpb)
    w_out_b, w1_b, w2_b = w_out.astype(BF16), w_mlp1.astype(BF16), w_mlp2.astype(BF16)

    xc, xl = ctx, x
    for l in range(depth):
        lat = mod_all[l, :b].reshape(b, 6, d)
        cx = jnp.broadcast_to(mod_all[l, b].reshape(1, 6, d), (b, 6, d))
        modv = jnp.stack([cx, lat], axis=1)
        u, g_rows = _inproj(xc, xl, modv, norm1_g[l], w_in_p[l], w_gate_t[l])
        ym = _mlstm(u, g_rows, mlstm_gate_b[l], mlstm_norm_g[l], rope_row, rope_col, n_ctx)
        yn = _na(u, na_q_norm_g[l], na_k_norm_g[l], na_bias[l], n_ctx)
        yl = _lru(u, lru_conv_w[l], lru_conv_b[l], lru_w_a[l], lru_b_a[l], lru_w_x[l], lru_b_x[l], lru_lambda[l], n_ctx)
        xc, xl = _out_mlp(xc, xl, ym, yn, yl, modv, norm2_g[l], w_out_b[l], w1_b[l], w2_b[l])
    return xl
```

```python
import functools
import math

import numpy as np
import jax
import jax.numpy as jnp
from jax import lax
from jax.experimental import pallas as pl
from jax.experimental.pallas import tpu as pltpu

F32 = jnp.float32
BF16 = jnp.bfloat16

HEAD_DIM = 64
M_HEADS = 4
M_WIDTH = M_HEADS * HEAD_DIM
NA_HEADS = 8
NA_WIDTH = NA_HEADS * HEAD_DIM
N_GATES = 4 * M_HEADS
GRID_W = 64
M_CHUNK = 128
M_AUG = HEAD_DIM + 16
M_LOCAL_GROUP = 2
WIN_R = 8
WIN_C = 16
RPB_R = 2 * WIN_R - 1
RPB_C = 2 * WIN_C - 1
NA_QROWS = 4
NA_BAND = NA_QROWS + WIN_R
NA_KEY_CHUNK = 128
NA_ONES_ROWS = 16
NA_GROUP = 4
LOG2E = 1.4426950408889634
CONV_W = 4
LRU_C = 8.0
ROPE_BASE = 10000.0
EPS = 1e-6
NEG = -1e30

LANES = 128
ROW_TILE = 256
LRU_TILE = 256
LRU_HALO = 8
VMEM_LIMIT = 56 * 1024 * 1024

U_MQ, U_MK, U_MV, U_MO = 0, 256, 512, 768
U_XR, U_GT = 1024, 1280
U_NQ, U_NK, U_NV = 1536, 2048, 2560
U_COLS = 3072


def _dot(a, b):
    return jnp.dot(a, b, preferred_element_type=F32)


def _dot_f32_rhs01(x, m01):
    hi = x.astype(BF16)
    lo = (x - hi.astype(F32)).astype(BF16)
    return _dot(hi, m01) + _dot(lo, m01)


def _sigmoid(x):
    return 0.5 * jnp.tanh(0.5 * x) + 0.5


def _log_sigmoid(x):
    return jnp.minimum(x, 0.0) - jnp.log1p(jnp.exp(-jnp.abs(x)))


def _softplus(x):
    return jnp.maximum(x, 0.0) + jnp.log1p(jnp.exp(-jnp.abs(x)))


def _params(sem):
    return pltpu.CompilerParams(dimension_semantics=sem, vmem_limit_bytes=VMEM_LIMIT)


def _resident(shape, index_map):
    return pl.BlockSpec(shape, index_map, pipeline_mode=pl.Buffered(1))


def _mod_kernel(a_ref, w_ref, b_ref, o_ref):
    a = a_ref[...]
    act = (a * jax.nn.sigmoid(a)).astype(BF16)
    o_ref[0] = _dot(act, w_ref[0].astype(BF16)) + b_ref[0]


def _modulation(cond, w_mod, b_mod):
    depth, d, n6 = w_mod.shape
    rows = cond.shape[0]
    tn = 1536
    return pl.pallas_call(
        _mod_kernel,
        grid=(depth, n6 // tn),
        in_specs=[
            pl.BlockSpec((rows, d), lambda l, j: (0, 0)),
            pl.BlockSpec((1, d, tn), lambda l, j: (l, 0, j)),
            pl.BlockSpec((1, 1, tn), lambda l, j: (l, 0, j)),
        ],
        out_specs=pl.BlockSpec((1, rows, tn), lambda l, j: (l, 0, j)),
        out_shape=jax.ShapeDtypeStruct((depth, rows, n6), F32),
        compiler_params=_params(("arbitrary", "arbitrary")),
    )(cond, w_mod, b_mod.reshape(depth, 1, n6))


def _stream_specs(n_ctx, d):
    assert n_ctx == ROW_TILE
    return [
        pl.BlockSpec((1, ROW_TILE, d), lambda bi, i: (bi, 0, 0)),
        pl.BlockSpec((1, ROW_TILE, d), lambda bi, i: (bi, jnp.maximum(i - 1, 0), 0)),
    ]


def _on_stream_tile(body, xc_ref, xl_ref, *outs):
    @pl.when(pl.program_id(1) == 0)
    def _():
        body(xc_ref, *[o[0] for o in outs])

    @pl.when(pl.program_id(1) > 0)
    def _():
        body(xl_ref, *[o[1] for o in outs])


def _inproj_kernel(xc_ref, xl_ref, mod_ref, g_ref, w_ref, wg_ref, u_ref, gt_ref):
    def body(x_ref):
        x = x_ref[0]
        ms = jnp.mean(x * x, axis=-1, keepdims=True)
        y = x * lax.rsqrt(ms + EPS) * g_ref[...]
        mod = mod_ref[0, 0]
        h = (y * (1.0 + mod[1:2, :]) + mod[0:1, :]).astype(BF16)
        u_ref[0] = _dot(h, w_ref[...])
        gt_ref[0] = lax.dot_general(wg_ref[...], h, (((1,), (1,)), ((), ())), preferred_element_type=F32)

    _on_stream_tile(body, xc_ref, xl_ref)


def _inproj(xc, xl, modv, g1, w_in_p, w_gate_t):
    b, n_ctx, d = xc.shape
    t = n_ctx + xl.shape[1]
    return pl.pallas_call(
        _inproj_kernel,
        grid=(b, t // ROW_TILE),
        in_specs=_stream_specs(n_ctx, d) + [
            pl.BlockSpec((1, 1, 6, d), lambda bi, i: (bi, jnp.minimum(i, 1), 0, 0)),
            pl.BlockSpec((1, d), lambda bi, i: (0, 0)),
            _resident((d, U_COLS), lambda bi, i: (0, 0)),
            _resident((N_GATES, d), lambda bi, i: (0, 0)),
        ],
        out_specs=[
            pl.BlockSpec((1, ROW_TILE, U_COLS), lambda bi, i: (bi, i, 0)),
            pl.BlockSpec((1, N_GATES, ROW_TILE), lambda bi, i: (bi, 0, i)),
        ],
        out_shape=[
            jax.ShapeDtypeStruct((b, t, U_COLS), F32),
            jax.ShapeDtypeStruct((b, N_GATES, t), F32),
        ],
        compiler_params=_params(("arbitrary", "arbitrary")),
    )(xc, xl, modv, g1.reshape(1, d), w_in_p, w_gate_t)


def _mlstm_kernel(u_ref, gr_ref, gb_ref, ng_ref, rrow_ref, rcol_ref, o_ref,
                  qt_s, k_s, vt_s, r_s, row_s, c_s, *, n_ctx_chunks, n_chunks):
    L = M_CHUNK
    W = M_WIDTH
    D = HEAD_DIM
    groups = L // GRID_W

    for h in range(M_HEADS):
        vt_s[h, D:, :] = jnp.ones((M_AUG - D, vt_s.shape[2]), BF16)
    c_s[...] = jnp.zeros(c_s.shape, F32)

    def stage(r0, q, k):
        qt_s[:, pl.ds(r0, L)] = q.T.astype(BF16)
        k_s[pl.ds(r0, L), :] = k.astype(BF16)
        vt = u_ref[0, pl.ds(r0, L), U_MV:U_MV + W].T
        for h in range(M_HEADS):
            vt_s[h, 0:D, pl.ds(r0, L)] = vt[h * D:(h + 1) * D].astype(BF16)

    def prep_ctx(j, carry):
        r0 = pl.multiple_of(j * L, L)
        stage(r0, u_ref[0, pl.ds(r0, L), U_MQ:U_MQ + W], u_ref[0, pl.ds(r0, L), U_MK:U_MK + W] * D ** -0.5)
        return carry

    def prep_lat(j, carry):
        r0 = pl.multiple_of((n_ctx_chunks + j) * L, L)

        def table(i):
            rows = [rrow_ref[i, pl.ds(j * groups + g, 1), :] + rcol_ref[i] for g in range(groups)]
            return jnp.concatenate(rows, axis=0)

        cos, sin_a, sin_b = table(0), table(1), table(2)

        def rope(x):
            return x * cos + pltpu.roll(x, W - D // 4, 1) * sin_a + pltpu.roll(x, D // 4, 1) * sin_b

        stage(r0, rope(u_ref[0, pl.ds(r0, L), U_MQ:U_MQ + W]), rope(u_ref[0, pl.ds(r0, L), U_MK:U_MK + W] * D ** -0.5))
        return carry

    lax.fori_loop(0, n_ctx_chunks, prep_ctx, 0)
    lax.fori_loop(0, n_chunks - n_ctx_chunks, prep_lat, 0)

    si = lax.broadcasted_iota(jnp.int32, (L, L), 0)
    li = lax.broadcasted_iota(jnp.int32, (L, L), 1)
    keeps = [si <= li, si >= li]
    gbias = [jnp.broadcast_to(gb_ref[8 * d:8 * d + 8, :], (8, L)) for d in range(2)]
    zrows = jnp.zeros((D, L), BF16)

    def qt_head(qtc, h):
        parts = [zrows] * M_HEADS
        parts[h] = qtc[h * D:(h + 1) * D]
        return jnp.concatenate(parts, axis=0)

    def local(ci, carry):
        r0s = [pl.multiple_of((ci * M_LOCAL_GROUP + j) * L, L) for j in range(M_LOCAL_GROUP)]
        scores = [[_dot(k_s[pl.ds(r0, L), :], qt_head(qt_s[:, pl.ds(r0, L)], h)) for h in range(M_HEADS)]
                  for r0 in r0s]
        vtas = [[vt_s[h, :, pl.ds(r0, L)] for h in range(M_HEADS)] for r0 in r0s]
        g8s = [[gr_ref[0, 8 * d:8 * d + 8, pl.ds(r0, L)] + gbias[d] for d in range(2)] for r0 in r0s]
        cum4s = [[pltpu.roll(_dot_f32_rhs01(_log_sigmoid(g8[d]), keeps[d].astype(BF16)), M_HEADS, 0)
                  for d in range(2)] for g8 in g8s]
        src_ts = [[(g8[d] - cum4[d]).T for d in range(2)] for g8, cum4 in zip(g8s, cum4s)]
        jdh = [(j, d, h) for j in range(M_LOCAL_GROUP) for d in range(2) for h in range(M_HEADS)]
        logws = [jnp.where(keeps[d], cum4s[j][d][h:h + 1] + jnp.broadcast_to(src_ts[j][d][:, h:h + 1], (L, L)), NEG)
                 for j, d, h in jdh]
        amaxs = [lw.max(axis=0, keepdims=True) for lw in logws]
        ps = [(scores[j][h] * jnp.exp(logw - a)).astype(BF16) for (j, d, h), logw, a in zip(jdh, logws, amaxs)]
        intras = [_dot(vtas[j][h], p) for (j, d, h), p in zip(jdh, ps)]
        for (j, d, h), a, intra in zip(jdh, amaxs, intras):
            last = L - 1 if d == 0 else 0
            cl = cum4s[j][d][h:h + 1]
            a_last = jnp.broadcast_to(a[:, last:last + 1], (1, L))
            cl_last = jnp.broadcast_to(cl[:, last:last + 1], (1, L))
            w_src = jnp.exp(cl_last - cl + g8s[j][d][h:h + 1] - a_last)
            r_s[d, h * M_AUG:(h + 1) * M_AUG, pl.ds(r0s[j], L)] = intra
            for kind, row in enumerate((cl, a, cl_last, a_last, w_src)):
                row_s[d, kind, h:h + 1, pl.ds(r0s[j], L)] = row
        return carry

    lax.fori_loop(0, n_chunks // M_LOCAL_GROUP, local, 0)

    def step(c, d, c_olds, m_prevs):
        r0 = pl.multiple_of(c * L, L)
        kc = k_s[pl.ds(r0, L), :]
        qtc = qt_s[:, pl.ds(r0, L)]
        h_ts, c_news, m_news = [], [], []
        for h in range(M_HEADS):
            cl, a, cl_last, a_last, w_src = [row_s[d, kind, h:h + 1, pl.ds(r0, L)] for kind in range(5)]
            m_prev = m_prevs[h]
            inter = cl + m_prev
            m_t = jnp.maximum(inter, a)
            r = (jnp.exp(inter - m_t) * _dot(c_olds[h].astype(BF16), qt_head(qtc, h))
                 + jnp.exp(a - m_t) * r_s[d, h * M_AUG:(h + 1) * M_AUG, pl.ds(r0, L)])
            denom = jnp.maximum(jnp.abs(r[D:D + 1]), jnp.exp(-m_t))
            h_ts.append(r[0:D] * (1.0 / denom))
            m_new = jnp.maximum(cl_last + m_prev, a_last)
            w_old = jnp.exp(cl_last + m_prev - m_new)
            w_upd = jnp.exp(a_last - m_new)
            vw = (vt_s[h, :, pl.ds(r0, L)].astype(F32) * w_src).astype(BF16)
            scale = lambda w: jnp.concatenate([w] * (W // L), axis=1)
            c_news.append(scale(w_old) * c_olds[h] + scale(w_upd) * _dot(vw, kc))
            m_news.append(m_new)
        return r0, h_ts, c_news, m_news

    def body(i, m_all):
        chunks = [i, jnp.where(i < n_ctx_chunks, n_ctx_chunks - 1 - i, n_chunks - 1 + n_ctx_chunks - i)]
        c_olds = [[c_s[d, h] for h in range(M_HEADS)] for d in range(2)]
        outs = [step(chunks[d], d, c_olds[d], m_all[d]) for d in range(2)]
        for d in range(2):
            r0, h_ts, c_news, _ = outs[d]
            for h in range(M_HEADS):
                r_s[d, h * M_AUG:h * M_AUG + D, pl.ds(r0, L)] = h_ts[h]
                c_s[d, h] = c_news[h]
        return tuple(tuple(outs[d][3]) for d in range(2))

    m0 = tuple(tuple(jnp.zeros((1, L), F32) for _ in range(M_HEADS)) for _ in range(2))
    lax.fori_loop(0, n_chunks, body, m0)

    ng = jnp.broadcast_to(ng_ref[...], (W, L))

    def post(j, carry):
        r0 = pl.multiple_of(j * L, L)
        normed = []
        for h in range(M_HEADS):
            ht = r_s[0, h * M_AUG:h * M_AUG + D, pl.ds(r0, L)] + r_s[1, h * M_AUG:h * M_AUG + D, pl.ds(r0, L)]
            ms = jnp.sum(ht * ht, axis=0, keepdims=True) * (1.0 / D)
            normed.append(ht * lax.rsqrt(ms + EPS))
        hn = jnp.concatenate(normed, axis=0) * ng
        o_ref[0, pl.ds(r0, L), :] = jax.nn.sigmoid(u_ref[0, pl.ds(r0, L), U_MO:U_MO + W]) * hn.T
        return carry

    lax.fori_loop(0, n_chunks, post, 0)


def _rope_tables(n_rows):
    nf = HEAD_DIM // 4
    inv = ROPE_BASE ** (-jnp.arange(nf, dtype=F32) / nf)
    d = np.arange(M_WIDTH) % HEAD_DIM
    f_idx = d % nf
    is_row = d < HEAD_DIM // 2
    first_quarter = (d % (HEAD_DIM // 2)) < nf

    def tables(pos, sel):
        ang = pos[:, None] * inv[None, :]
        ang = ang[:, f_idx]
        cos = jnp.where(sel, jnp.cos(ang), 0.0)
        sin = jnp.where(sel, jnp.sin(ang), 0.0)
        sin_a = jnp.where(first_quarter, -sin, 0.0)
        sin_b = jnp.where(first_quarter, 0.0, sin)
        return jnp.stack([cos, sin_a, sin_b]).astype(F32)

    row_t = tables(jnp.arange(n_rows, dtype=F32), is_row)
    col_t = tables(jnp.arange(GRID_W, dtype=F32), ~is_row)
    return row_t, col_t


def _mlstm(u, g_rows, gate_b, norm_g, rope_row, rope_col, n_ctx):
    b, t, _ = u.shape
    assert t % M_CHUNK == 0 and n_ctx % M_CHUNK == 0 and M_CHUNK % GRID_W == 0
    n_chunks = t // M_CHUNK
    kern = functools.partial(_mlstm_kernel, n_ctx_chunks=n_ctx // M_CHUNK, n_chunks=n_chunks)
    n_rows = rope_row.shape[1]
    return pl.pallas_call(
        kern,
        grid=(b,),
        in_specs=[
            pl.BlockSpec((1, t, 4 * M_WIDTH), lambda bi: (bi, 0, 0), pipeline_mode=pl.Buffered(1)),
            pl.BlockSpec((1, N_GATES, t), lambda bi: (bi, 0, 0)),
            pl.BlockSpec((N_GATES, 1), lambda bi: (0, 0)),
            pl.BlockSpec((M_WIDTH, 1), lambda bi: (0, 0)),
            pl.BlockSpec((3, n_rows, M_WIDTH), lambda bi: (0, 0, 0)),
            pl.BlockSpec((3, GRID_W, M_WIDTH), lambda bi: (0, 0, 0)),
        ],
        out_specs=pl.BlockSpec((1, t, M_WIDTH), lambda bi: (bi, 0, 0)),
        out_shape=jax.ShapeDtypeStruct((b, t, M_WIDTH), F32),
        scratch_shapes=[
            pltpu.VMEM((M_WIDTH, t), BF16),
            pltpu.VMEM((t, M_WIDTH), BF16),
            pltpu.VMEM((M_HEADS, M_AUG, t), BF16),
            pltpu.VMEM((2, M_HEADS * M_AUG, t), F32),
            pltpu.VMEM((2, 5, 8, t), F32),
            pltpu.VMEM((2, M_HEADS, M_AUG, M_WIDTH), F32),
        ],
        compiler_params=_params(("arbitrary",)),
    )(u, g_rows, gate_b.reshape(N_GATES, 1), norm_g.reshape(M_WIDTH, 1), rope_row, rope_col)


def _na_kernel(q_ref, k_ref, v_ref, qg_ref, kg_ref, bias_ref, o_ref, qt_s, kz_s, vt_s, s_s, *, n_ctx, n_rows):
    t = q_ref.shape[1]
    lane = lax.broadcasted_iota(jnp.int32, (1, LANES), 1)
    half = [lane < HEAD_DIM, lane >= HEAD_DIM]
    hi = lax.broadcasted_iota(jnp.int32, (LANES, LANES), 0) // HEAD_DIM
    hj = lax.broadcasted_iota(jnp.int32, (LANES, LANES), 1) // HEAD_DIM
    same_head = (hi == hj).astype(BF16)
    qg = qg_ref[...] * (HEAD_DIM ** -0.5 * LOG2E)
    kg = kg_ref[...]
    qblk = NA_QROWS * GRID_W
    kc = NA_KEY_CHUNK
    sub = 8

    for hh in range(2):
        vt_s[hh, HEAD_DIM:, :] = jnp.ones((vt_s.shape[1] - HEAD_DIM, t), BF16)

    def prep(j, carry):
        r0 = pl.multiple_of(j * ROW_TILE, ROW_TILE)
        q = q_ref[0, pl.ds(r0, ROW_TILE), :]
        k = k_ref[0, pl.ds(r0, ROW_TILE), :]
        qms = _dot_f32_rhs01(q * q, same_head) * (1.0 / HEAD_DIM)
        kms = _dot_f32_rhs01(k * k, same_head) * (1.0 / HEAD_DIM)
        qn = q * lax.rsqrt(qms + EPS) * qg
        kn = k * lax.rsqrt(kms + EPS) * kg
        qt_s[:, pl.ds(r0, ROW_TILE)] = qn.T.astype(BF16)
        vt = v_ref[0, pl.ds(r0, ROW_TILE), :].T
        for hh in range(2):
            kz_s[hh, pl.ds(r0, ROW_TILE), :] = jnp.where(half[hh], kn, 0.0).astype(BF16)
            vt_s[hh, 0:HEAD_DIM, pl.ds(r0, ROW_TILE)] = vt[hh * HEAD_DIM:(hh + 1) * HEAD_DIM].astype(BF16)
        return carry

    lax.fori_loop(0, t // ROW_TILE, prep, 0)

    def attend(blocks):
        jobs = [(b, hh) for b in range(len(blocks)) for hh in range(2)]
        qts = [qt_s[:, pl.ds(q0, qblk)] for q0, _ in blocks]
        n_chunks = len(blocks[0][1])
        mx = [None] * len(jobs)
        for ci in range(n_chunks):
            for ji, (b, hh) in enumerate(jobs):
                k0, tab = blocks[b][1][ci]
                s = _dot(kz_s[hh, pl.ds(k0, kc), :], qts[b])
                if tab is not None:
                    s = s + bias_ref[0, hh, tab[0], tab[1]:tab[1] + kc, :]
                s_s[ji, ci * kc:(ci + 1) * kc, :] = s
                cm = functools.reduce(jnp.maximum, [s[i * sub:(i + 1) * sub] for i in range(kc // sub)])
                mx[ji] = cm if mx[ji] is None else jnp.maximum(mx[ji], cm)
        ms = [m.max(axis=0, keepdims=True) for m in mx]
        accs = [None] * len(jobs)
        for ci in range(n_chunks):
            for ji, (b, hh) in enumerate(jobs):
                k0, _ = blocks[b][1][ci]
                p = jnp.exp2(s_s[ji, ci * kc:(ci + 1) * kc, :] - ms[ji]).astype(BF16)
                pv = _dot(vt_s[hh, :, pl.ds(k0, kc)], p)
                accs[ji] = pv if accs[ji] is None else accs[ji] + pv
        outs = []
        for b in range(len(blocks)):
            heads = [accs[2 * b + hh][0:HEAD_DIM] / accs[2 * b + hh][HEAD_DIM:HEAD_DIM + 1] for hh in range(2)]
            outs.append(jnp.concatenate(heads, axis=0).T)
        return outs

    ctx_chunks = [(ci * kc, None) for ci in range(n_ctx // kc)]
    o_ref[0, 0:n_ctx, :] = attend([(0, ctx_chunks)])[0]

    n_blocks = n_rows // NA_QROWS

    def group(gi, carry):
        blocks = []
        for j in range(NA_GROUP):
            bi = gi * NA_GROUP + j
            r0 = bi * NA_QROWS
            sb = jnp.clip(r0 - WIN_R // 2, 0, n_rows - NA_BAND)
            case = jnp.where(bi == 0, 0, jnp.where(bi == n_blocks - 1, 2, 1))
            q0 = pl.multiple_of(n_ctx + r0 * GRID_W, qblk)
            k0 = pl.multiple_of(n_ctx + sb * GRID_W, qblk)
            local = [(pl.multiple_of(k0 + ci * kc, kc), (case, ci * kc)) for ci in range(NA_BAND * GRID_W // kc)]
            blocks.append((q0, local + ctx_chunks))
        for (q0, _), out in zip(blocks, attend(blocks)):
            o_ref[0, pl.ds(q0, qblk), :] = out
        return carry

    lax.fori_loop(0, n_blocks // NA_GROUP, group, 0)


def _na_bias_tables(rpb):
    n_layers, n_heads = rpb.shape[:2]
    cols = np.arange(GRID_W)
    col0 = np.clip(cols - WIN_C // 2, 0, GRID_W - WIN_C)
    valid = (cols[:, None] >= col0[None, :]) & (cols[:, None] < col0[None, :] + WIN_C)
    dc = cols[:, None] - cols[None, :] + WIN_C - 1
    col_sel = ((dc[..., None] == np.arange(RPB_C)) & valid[..., None]).astype(np.float32)
    toep = jnp.einsum('lhrd,xcd->lhrxc', rpb, col_sel, precision=lax.Precision.HIGHEST)
    toep = jnp.where(valid, toep * LOG2E, NEG * LOG2E).astype(F32)
    nk, nq = NA_BAND * GRID_W, NA_QROWS * GRID_W
    return pl.pallas_call(
        _na_bias_kernel,
        grid=(n_layers * n_heads,),
        in_specs=[pl.BlockSpec((1, RPB_R, GRID_W, GRID_W), lambda i: (i, 0, 0, 0))],
        out_specs=pl.BlockSpec((1, 3, nk, nq), lambda i: (i, 0, 0, 0)),
        out_shape=jax.ShapeDtypeStruct((n_layers * n_heads, 3, nk, nq), F32),
        compiler_params=_params(("arbitrary",)),
    )(toep.reshape(n_layers * n_heads, RPB_R, GRID_W, GRID_W)).reshape(n_layers, n_heads, 3, nk, nq)


def _na_bias_kernel(t_ref, o_ref):
    blank = jnp.full((GRID_W, GRID_W), NEG * LOG2E, F32)
    for case in range(3):
        for jj in range(NA_BAND):
            tiles = []
            for i in range(NA_QROWS):
                off = (0, i, NA_BAND - WIN_R)[case]
                rel = (jj - i, jj - i - WIN_R // 2, jj - i - NA_BAND + NA_QROWS)[case]
                tiles.append(t_ref[0, rel + WIN_R - 1] if off <= jj < off + WIN_R else blank)
            o_ref[0, case, jj * GRID_W:(jj + 1) * GRID_W, :] = jnp.concatenate(tiles, axis=1)


def _na(u, qn_g, kn_g, bias, n_ctx):
    b, t, _ = u.shape
    n_rows = (t - n_ctx) // GRID_W
    assert n_rows % (NA_QROWS * NA_GROUP) == 0 and n_rows >= NA_BAND and n_ctx == NA_QROWS * GRID_W
    pairs = NA_HEADS // 2
    nk, nq = NA_BAND * GRID_W, NA_QROWS * GRID_W
    bias = bias.reshape(pairs, 2, 3, nk, nq)
    qg = jnp.tile(qn_g, 2).reshape(1, LANES)
    kg = jnp.tile(kn_g, 2).reshape(1, LANES)
    kern = functools.partial(_na_kernel, n_ctx=n_ctx, n_rows=n_rows)
    col = lambda base: (lambda hp, bi: (bi, 0, base // LANES + hp))
    return pl.pallas_call(
        kern,
        grid=(pairs, b),
        in_specs=[
            pl.BlockSpec((1, t, LANES), col(U_NQ)),
            pl.BlockSpec((1, t, LANES), col(U_NK)),
            pl.BlockSpec((1, t, LANES), col(U_NV)),
            pl.BlockSpec((1, LANES), lambda hp, bi: (0, 0)),
            pl.BlockSpec((1, LANES), lambda hp, bi: (0, 0)),
            pl.BlockSpec((1, 2, 3, nk, nq), lambda hp, bi: (hp, 0, 0, 0, 0)),
        ],
        out_specs=pl.BlockSpec((1, t, LANES), lambda hp, bi: (bi, 0, hp)),
        out_shape=jax.ShapeDtypeStruct((b, t, NA_WIDTH), F32),
        scratch_shapes=[
            pltpu.VMEM((LANES, t), BF16),
            pltpu.VMEM((2, t, LANES), BF16),
            pltpu.VMEM((2, HEAD_DIM + NA_ONES_ROWS, t), BF16),
            pltpu.VMEM((2 * NA_GROUP, nk + n_ctx, nq), F32),
        ],
        compiler_params=_params(("arbitrary", "arbitrary")),
    )(u, u, u, qg, kg, bias)


def _lru_kernel(x_ref, g_ref, cw_ref, cb_ref, wa_ref, ba_ref, wx_ref, bx_ref, lam_ref, o_ref,
                xp_s, a_s, b_s, h_s, *, n_tiles):
    tl = LRU_TILE
    halo = LRU_HALO
    t = x_ref.shape[1]
    width = x_ref.shape[2]
    group = 8

    xp_s[0:halo, :] = jnp.zeros((halo, width), F32)
    xp_s[halo + t:halo + t + halo, :] = jnp.zeros((halo, width), F32)

    def fill(i, carry):
        r0 = pl.multiple_of(i * tl, tl)
        xp_s[pl.ds(pl.multiple_of(r0 + halo, halo), tl), :] = x_ref[0, pl.ds(r0, tl), :]
        return carry

    lax.fori_loop(0, n_tiles, fill, 0)

    row = lax.broadcasted_iota(jnp.int32, (tl, 1), 0)
    rowg = row % group
    cw = cw_ref[...]
    cb = cb_ref[...]

    def conv_tile(i):
        r0 = pl.multiple_of(i * tl, tl)
        win = xp_s[pl.ds(r0, tl + 2 * halo), :]
        n_win = tl + 2 * halo
        xm2 = pltpu.roll(win, 2, 0)[halo:halo + tl]
        xm1 = pltpu.roll(win, 1, 0)[halo:halo + tl]
        x0 = win[halo:halo + tl]
        xp1 = pltpu.roll(win, n_win - 1, 0)[halo:halo + tl]
        lo = i <= 1
        hi = i == 0
        xm2 = jnp.where(lo & (row < 2), 0.0, xm2)
        xm1 = jnp.where(lo & (row < 1), 0.0, xm1)
        xp1 = jnp.where(hi & (row == tl - 1), 0.0, xp1)
        return xm2 * cw[0:1] + xm1 * cw[1:2] + x0 * cw[2:3] + xp1 * cw[3:4] + cb

    def gates(i, d):
        xc = conv_tile(i)
        xb = xc.astype(BF16)
        r = _sigmoid(_dot(xb, wa_ref[d]) + ba_ref[d:d + 1, :])
        ig = _sigmoid(_dot(xb, wx_ref[d]) + bx_ref[d:d + 1, :])
        log_a = -LRU_C * r * _softplus(-lam_ref[d:d + 1, :])
        a = jnp.exp(log_a)
        bt = jnp.sqrt(-jnp.tanh(log_a) * (a * a + 1.0)) * (ig * xc)
        return a, bt

    def scan_tile(i, d, carry):
        a, bt = gates(i, d)
        for kk in (1, 2, 4):
            if d == 0:
                a_sh = pltpu.roll(a, kk, 0)
                b_sh = pltpu.roll(bt, kk, 0)
                mk = rowg >= kk
            else:
                a_sh = pltpu.roll(a, tl - kk, 0)
                b_sh = pltpu.roll(bt, tl - kk, 0)
                mk = rowg < group - kk
            bt = jnp.where(mk, a * b_sh + bt, bt)
            a = jnp.where(mk, a * a_sh, a)
        a_s[...] = a
        b_s[...] = bt
        n_groups = tl // group

        def grp(gidx, c):
            gg = gidx if d == 0 else n_groups - 1 - gidx
            g0 = pl.multiple_of(gg * group, group)
            h = b_s[pl.ds(g0, group), :] + a_s[pl.ds(g0, group), :] * c
            h_s[pl.ds(g0, group), :] = h
            return h[group - 1:group, :] if d == 0 else h[0:1, :]

        return lax.fori_loop(0, n_groups, grp, carry)

    def fwd(i, carry):
        carry = scan_tile(i, 0, carry)
        r0 = pl.multiple_of(i * tl, tl)
        o_ref[0, pl.ds(r0, tl), :] = h_s[...]
        return carry

    def bwd(i, carry):
        carry = scan_tile(i, 1, carry)
        r0 = pl.multiple_of(i * tl, tl)
        o_ref[0, pl.ds(r0, tl), :] = (o_ref[0, pl.ds(r0, tl), :] + h_s[...]) * jax.nn.gelu(g_ref[0, pl.ds(r0, tl), :])
        return carry

    h0 = jnp.zeros((1, width), F32)
    lax.fori_loop(0, n_tiles, fwd, h0)
    carry = bwd(jnp.int32(0), h0)
    lax.fori_loop(1, n_tiles, lambda j, c: bwd(n_tiles - j, c), carry)


def _block_diag_dense(w):
    nd, k, c, _ = w.shape
    eye = jnp.eye(k, dtype=w.dtype)
    return jnp.einsum('dkce,kj->dkcje', w, eye).reshape(nd, k * c, k * c)


def _lru(u, conv_w, conv_b, w_a, b_a, w_x, b_x, lam, n_ctx):
    b, t, _ = u.shape
    width = conv_w.shape[1]
    assert n_ctx == LRU_TILE and t % LRU_TILE == 0
    n_tiles = t // LRU_TILE
    wa = _block_diag_dense(w_a).astype(BF16)
    wx = _block_diag_dense(w_x).astype(BF16)
    kern = functools.partial(_lru_kernel, n_tiles=n_tiles)
    full = lambda shape: pl.BlockSpec(shape, lambda bi: (0,) * len(shape))
    return pl.pallas_call(
        kern,
        grid=(b,),
        in_specs=[
            pl.BlockSpec((1, t, width), lambda bi: (bi, 0, U_XR // width)),
            pl.BlockSpec((1, t, width), lambda bi: (bi, 0, U_GT // width)),
            full((CONV_W, width)),
            full((1, width)),
            full((2, width, width)),
            full((2, width)),
            full((2, width, width)),
            full((2, width)),
            full((2, width)),
        ],
        out_specs=pl.BlockSpec((1, t, width), lambda bi: (bi, 0, 0)),
        out_shape=jax.ShapeDtypeStruct((b, t, width), F32),
        scratch_shapes=[
            pltpu.VMEM((t + 2 * LRU_HALO, width), F32),
            pltpu.VMEM((LRU_TILE, width), F32),
            pltpu.VMEM((LRU_TILE, width), F32),
            pltpu.VMEM((LRU_TILE, width), F32),
        ],
        compiler_params=_params(("arbitrary",)),
    )(u, u, conv_w, conv_b.reshape(1, width), wa, b_a, wx, b_x, lam)


def _out_mlp_kernel(xc_ref, xl_ref, ym_ref, yn_ref, yl_ref, mod_ref, g_ref, wo_ref, w1_ref, w2_ref, oc_ref, ol_ref,
                    *, hid_tile):
    def body(x_ref, o_ref):
        mod = mod_ref[0, 0]
        att = _dot(ym_ref[0].astype(BF16), wo_ref[0:M_WIDTH, :])
        att = att + _dot(yn_ref[0].astype(BF16), wo_ref[M_WIDTH:M_WIDTH + NA_WIDTH, :])
        att = att + _dot(yl_ref[0].astype(BF16), wo_ref[M_WIDTH + NA_WIDTH:, :])
        x1 = x_ref[0] + mod[2:3, :] * att
        ms = jnp.mean(x1 * x1, axis=-1, keepdims=True)
        y = x1 * lax.rsqrt(ms + EPS) * g_ref[...]
        h = (y * (1.0 + mod[4:5, :]) + mod[3:4, :]).astype(BF16)
        hidden = w1_ref.shape[1]
        acc = None
        for j in range(hidden // hid_tile):
            a = jnp.maximum(_dot(h, w1_ref[:, j * hid_tile:(j + 1) * hid_tile]), 0.0)
            part = _dot((a * a).astype(BF16), w2_ref[j * hid_tile:(j + 1) * hid_tile, :])
            acc = part if acc is None else acc + part
        o_ref[0] = x1 + mod[5:6, :] * acc

    _on_stream_tile(body, xc_ref, xl_ref, (oc_ref, ol_ref))


def _out_mlp(xc, xl, ym, yn, yl, modv, g2, w_out, w1, w2):
    b, n_ctx, d = xc.shape
    t = n_ctx + xl.shape[1]
    hidden = w1.shape[1]
    tok = lambda w: pl.BlockSpec((1, ROW_TILE, w), lambda bi, i: (bi, i, 0))
    kern = functools.partial(_out_mlp_kernel, hid_tile=1024)
    return pl.pallas_call(
        kern,
        grid=(b, t // ROW_TILE),
        in_specs=_stream_specs(n_ctx, d) + [
            tok(M_WIDTH), tok(NA_WIDTH), tok(d - M_WIDTH - NA_WIDTH),
            pl.BlockSpec((1, 1, 6, d), lambda bi, i: (bi, jnp.minimum(i, 1), 0, 0)),
            pl.BlockSpec((1, d), lambda bi, i: (0, 0)),
            _resident((d, d), lambda bi, i: (0, 0)),
            _resident((d, hidden), lambda bi, i: (0, 0)),
            _resident((hidden, d), lambda bi, i: (0, 0)),
        ],
        out_specs=_stream_specs(n_ctx, d),
        out_shape=[jax.ShapeDtypeStruct(xc.shape, F32), jax.ShapeDtypeStruct(xl.shape, F32)],
        compiler_params=_params(("arbitrary", "arbitrary")),
    )(xc, xl, ym, yn, yl, modv, g2.reshape(1, d), w_out, w1, w2)


def _split_w_in(w_in):
    m_end = 4 * M_WIDTH
    na_end = m_end + N_GATES + 3 * NA_WIDTH
    main = jnp.concatenate([w_in[..., :m_end], w_in[..., na_end:], w_in[..., m_end + N_GATES:na_end]], axis=-1)
    gates = jnp.swapaxes(w_in[..., m_end:m_end + N_GATES], -1, -2)
    return main.astype(BF16), gates.astype(BF16)


def kernel(x, c, ctx, c_ctx, w_mod, b_mod, norm1_g, norm2_g, w_in, mlstm_gate_b, mlstm_norm_g, na_q_norm_g, na_k_norm_g, na_rpb, lru_conv_w, lru_conv_b, lru_w_a, lru_b_a, lru_w_x, lru_b_x, lru_lambda, w_out, w_mlp1, w_mlp2):
    b, n, d = x.shape
    n_ctx = ctx.shape[1]
    depth = w_in.shape[0]
    assert n_ctx == ROW_TILE and n % ROW_TILE == 0 and n % GRID_W == 0

    cond_rows = 8 * ((b + 1 + 7) // 8)
    cond = jnp.zeros((cond_rows, d), F32).at[:b].set(c).at[b].set(c_ctx)
    mod_all = _modulation(cond, w_mod, b_mod)
    rope_row, rope_col = _rope_tables(n // GRID_W)
    w_in_p, w_gate_t = _split_w_in(w_in)
    na_bias = _na_bias_tables(na_rpb)
    w_out_b, w1_b, w2_b = w_out.astype(BF16), w_mlp1.astype(BF16), w_mlp2.astype(BF16)

    xc, xl = ctx, x
    for l in range(depth):
        lat = mod_all[l, :b].reshape(b, 6, d)
        cx = jnp.broadcast_to(mod_all[l, b].reshape(1, 6, d), (b, 6, d))
        modv = jnp.stack([cx, lat], axis=1)
        u, g_rows = _inproj(xc, xl, modv, norm1_g[l], w_in_p[l], w_gate_t[l])
        ym = _mlstm(u, g_rows, mlstm_gate_b[l], mlstm_norm_g[l], rope_row, rope_col, n_ctx)
        yn = _na(u, na_q_norm_g[l], na_k_norm_g[l], na_bias[l], n_ctx)
        yl = _lru(u, lru_conv_w[l], lru_conv_b[l], lru_w_a[l], lru_b_a[l], lru_w_x[l], lru_b_x[l], lru_lambda[l], n_ctx)
        xc, xl = _out_mlp(xc, xl, ym, yn, yl, modv, norm2_g[l], w_out_b[l], w1_b[l], w2_b[l])
    return xl
```

```python
import functools
import math

import numpy as np
import jax
import jax.numpy as jnp
from jax import lax
from jax.experimental import pallas as pl
from jax.experimental.pallas import tpu as pltpu

F32 = jnp.float32
BF16 = jnp.bfloat16

HEAD_DIM = 64
M_HEADS = 4
M_WIDTH = M_HEADS * HEAD_DIM
NA_HEADS = 8
NA_WIDTH = NA_HEADS * HEAD_DIM
N_GATES = 4 * M_HEADS
GRID_W = 64
M_CHUNK = 128
M_AUG = HEAD_DIM + 16
M_LOCAL_GROUP = 2
M_LATENT_GROUP = 4
WIN_R = 8
WIN_C = 16
RPB_R = 2 * WIN_R - 1
RPB_C = 2 * WIN_C - 1
NA_QROWS = 4
NA_BAND = NA_QROWS + WIN_R
NA_KEY_CHUNK = 128
NA_ONES_ROWS = 16
NA_GROUP = 8
LOG2E = 1.4426950408889634
CONV_W = 4
LRU_C = 8.0
ROPE_BASE = 10000.0
EPS = 1e-6
NEG = -1e30

LANES = 128
ROW_TILE = 256
LRU_TILE = 256
LRU_HALO = 8
VMEM_LIMIT = 56 * 1024 * 1024

U_MQ, U_MK, U_MV, U_MO = 0, 256, 512, 768
U_XR, U_GT = 1024, 1280
U_NQ, U_NK, U_NV = 1536, 2048, 2560
U_COLS = 3072


def _dot(a, b):
    return jnp.dot(a, b, preferred_element_type=F32)


def _dot_f32_rhs01(x, m01):
    hi = x.astype(BF16)
    lo = (x - hi.astype(F32)).astype(BF16)
    return _dot(hi, m01) + _dot(lo, m01)


def _sigmoid(x):
    return 0.5 * jnp.tanh(0.5 * x) + 0.5


def _log_sigmoid(x):
    return jnp.minimum(x, 0.0) - jnp.log1p(jnp.exp(-jnp.abs(x)))


def _softplus(x):
    return jnp.maximum(x, 0.0) + jnp.log1p(jnp.exp(-jnp.abs(x)))


def _params(sem):
    return pltpu.CompilerParams(dimension_semantics=sem, vmem_limit_bytes=VMEM_LIMIT)


def _resident(shape, index_map):
    return pl.BlockSpec(shape, index_map, pipeline_mode=pl.Buffered(1))


def _mod_kernel(a_ref, w_ref, b_ref, o_ref):
    a = a_ref[...]
    act = (a * jax.nn.sigmoid(a)).astype(BF16)
    o_ref[0] = _dot(act, w_ref[0].astype(BF16)) + b_ref[0]


def _modulation(cond, w_mod, b_mod):
    depth, d, n6 = w_mod.shape
    rows = cond.shape[0]
    tn = 1536
    return pl.pallas_call(
        _mod_kernel,
        grid=(depth, n6 // tn),
        in_specs=[
            pl.BlockSpec((rows, d), lambda l, j: (0, 0)),
            pl.BlockSpec((1, d, tn), lambda l, j: (l, 0, j)),
            pl.BlockSpec((1, 1, tn), lambda l, j: (l, 0, j)),
        ],
        out_specs=pl.BlockSpec((1, rows, tn), lambda l, j: (l, 0, j)),
        out_shape=jax.ShapeDtypeStruct((depth, rows, n6), F32),
        compiler_params=_params(("arbitrary", "arbitrary")),
    )(cond, w_mod, b_mod.reshape(depth, 1, n6))


def _stream_specs(n_ctx, d):
    assert n_ctx == ROW_TILE
    return [
        pl.BlockSpec((1, ROW_TILE, d), lambda bi, i: (bi, 0, 0)),
        pl.BlockSpec((1, ROW_TILE, d), lambda bi, i: (bi, jnp.maximum(i - 1, 0), 0)),
    ]


def _on_stream_tile(body, xc_ref, xl_ref, *outs):
    @pl.when(pl.program_id(1) == 0)
    def _():
        body(xc_ref, *[o[0] for o in outs])

    @pl.when(pl.program_id(1) > 0)
    def _():
        body(xl_ref, *[o[1] for o in outs])


def _inproj_kernel(xc_ref, xl_ref, mod_ref, g_ref, w_ref, wg_ref, u_ref, gt_ref):
    def body(x_ref):
        x = x_ref[0]
        ms = jnp.mean(x * x, axis=-1, keepdims=True)
        y = x * lax.rsqrt(ms + EPS) * g_ref[...]
        mod = mod_ref[0, 0]
        h = (y * (1.0 + mod[1:2, :]) + mod[0:1, :]).astype(BF16)
        u_ref[0] = _dot(h, w_ref[...])
        gt_ref[0] = lax.dot_general(wg_ref[...], h, (((1,), (1,)), ((), ())), preferred_element_type=F32)

    _on_stream_tile(body, xc_ref, xl_ref)


def _inproj(xc, xl, modv, g1, w_in_p, w_gate_t):
    b, n_ctx, d = xc.shape
    t = n_ctx + xl.shape[1]
    return pl.pallas_call(
        _inproj_kernel,
        grid=(b, t // ROW_TILE),
        in_specs=_stream_specs(n_ctx, d) + [
            pl.BlockSpec((1, 1, 6, d), lambda bi, i: (bi, jnp.minimum(i, 1), 0, 0)),
            pl.BlockSpec((1, d), lambda bi, i: (0, 0)),
            _resident((d, U_COLS), lambda bi, i: (0, 0)),
            _resident((N_GATES, d), lambda bi, i: (0, 0)),
        ],
        out_specs=[
            pl.BlockSpec((1, ROW_TILE, U_COLS), lambda bi, i: (bi, i, 0)),
            pl.BlockSpec((1, N_GATES, ROW_TILE), lambda bi, i: (bi, 0, i)),
        ],
        out_shape=[
            jax.ShapeDtypeStruct((b, t, U_COLS), F32),
            jax.ShapeDtypeStruct((b, N_GATES, t), F32),
        ],
        compiler_params=_params(("arbitrary", "arbitrary")),
    )(xc, xl, modv, g1.reshape(1, d), w_in_p, w_gate_t)


def _mlstm_kernel(u_ref, gr_ref, gb_ref, ng_ref, rrow_ref, rcol_ref, o_ref,
                  qt_s, k_s, vt_s, r_s, row_s, c_s, *, n_ctx_chunks, n_chunks):
    L = M_CHUNK
    W = M_WIDTH
    D = HEAD_DIM
    groups = L // GRID_W

    for h in range(M_HEADS):
        vt_s[h, D:, :] = jnp.ones((M_AUG - D, vt_s.shape[2]), BF16)
    c_s[...] = jnp.zeros(c_s.shape, F32)

    def prep(cs, latent):
        r0s = [pl.multiple_of(c * L, L) for c in cs]
        qs = [u_ref[0, pl.ds(r0, L), U_MQ:U_MQ + W] for r0 in r0s]
        ks = [u_ref[0, pl.ds(r0, L), U_MK:U_MK + W] * D ** -0.5 for r0 in r0s]
        vts = [u_ref[0, pl.ds(r0, L), U_MV:U_MV + W].T for r0 in r0s]
        if latent:
            def table(c, i):
                first = (c - n_ctx_chunks) * groups
                return jnp.concatenate([rrow_ref[i, pl.ds(first + g, 1), :] + rcol_ref[i] for g in range(groups)], axis=0)

            def rope(x, c):
                return (x * table(c, 0) + pltpu.roll(x, W - D // 4, 1) * table(c, 1)
                        + pltpu.roll(x, D // 4, 1) * table(c, 2))

            qs = [rope(q, c) for q, c in zip(qs, cs)]
            ks = [rope(k, c) for k, c in zip(ks, cs)]
        qts = [q.T for q in qs]
        for r0, qt, k, vt in zip(r0s, qts, ks, vts):
            qt_s[:, pl.ds(r0, L)] = qt.astype(BF16)
            k_s[pl.ds(r0, L), :] = k.astype(BF16)
            for h in range(M_HEADS):
                vt_s[h, 0:D, pl.ds(r0, L)] = vt[h * D:(h + 1) * D].astype(BF16)

    assert n_ctx_chunks % M_LOCAL_GROUP == 0 and n_chunks % M_LOCAL_GROUP == 0

    def prep_group(latent):
        def body(j, carry):
            first = j * M_LOCAL_GROUP + (n_ctx_chunks if latent else 0)
            prep([first + i for i in range(M_LOCAL_GROUP)], latent)
            return carry
        return body

    lax.fori_loop(0, n_ctx_chunks // M_LOCAL_GROUP, prep_group(False), 0)
    lax.fori_loop(0, (n_chunks - n_ctx_chunks) // M_LOCAL_GROUP, prep_group(True), 0)

    si = lax.broadcasted_iota(jnp.int32, (L, L), 0)
    li = lax.broadcasted_iota(jnp.int32, (L, L), 1)
    keeps = [si <= li, si >= li]
    gbias = [jnp.broadcast_to(gb_ref[8 * d:8 * d + 8, :], (8, L)) for d in range(2)]
    zrows = jnp.zeros((D, L), BF16)

    def qt_head(qtc, h):
        parts = [zrows] * M_HEADS
        parts[h] = qtc[h * D:(h + 1) * D]
        return jnp.concatenate(parts, axis=0)

    def local(first, n_group):
        r0s = [pl.multiple_of((first + j) * L, L) for j in range(n_group)]
        scores = [[_dot(k_s[pl.ds(r0, L), :], qt_head(qt_s[:, pl.ds(r0, L)], h)) for h in range(M_HEADS)]
                  for r0 in r0s]
        vtas = [[vt_s[h, :, pl.ds(r0, L)] for h in range(M_HEADS)] for r0 in r0s]
        g8s = [[gr_ref[0, 8 * d:8 * d + 8, pl.ds(r0, L)] + gbias[d] for d in range(2)] for r0 in r0s]
        cum4s = [[pltpu.roll(_dot_f32_rhs01(_log_sigmoid(g8[d]), keeps[d].astype(BF16)), M_HEADS, 0)
                  for d in range(2)] for g8 in g8s]
        src_ts = [[(g8[d] - cum4[d]).T for d in range(2)] for g8, cum4 in zip(g8s, cum4s)]
        jdh = [(j, d, h) for j in range(n_group) for d in range(2) for h in range(M_HEADS)]
        logws = [jnp.where(keeps[d], cum4s[j][d][h:h + 1] + jnp.broadcast_to(src_ts[j][d][:, h:h + 1], (L, L)), NEG)
                 for j, d, h in jdh]
        amaxs = [lw.max(axis=0, keepdims=True) for lw in logws]
        ps = [(scores[j][h] * jnp.exp(logw - a)).astype(BF16) for (j, d, h), logw, a in zip(jdh, logws, amaxs)]
        intras = [_dot(vtas[j][h], p) for (j, d, h), p in zip(jdh, ps)]
        for (j, d, h), a, intra in zip(jdh, amaxs, intras):
            last = L - 1 if d == 0 else 0
            cl = cum4s[j][d][h:h + 1]
            a_last = jnp.broadcast_to(a[:, last:last + 1], (1, L))
            cl_last = jnp.broadcast_to(cl[:, last:last + 1], (1, L))
            w_src = jnp.exp(cl_last - cl + g8s[j][d][h:h + 1] - a_last)
            r_s[d, h * M_AUG:(h + 1) * M_AUG, pl.ds(r0s[j], L)] = intra
            for kind, row in enumerate((cl, a, cl_last, a_last, w_src)):
                row_s[d, kind, h:h + 1, pl.ds(r0s[j], L)] = row

    def local_loop(first, count, n_group):
        assert count % n_group == 0

        def body(i, carry):
            local(first + i * n_group, n_group)
            return carry

        lax.fori_loop(0, count // n_group, body, 0)

    local_loop(0, n_ctx_chunks, M_LOCAL_GROUP)
    local_loop(n_ctx_chunks, n_chunks - n_ctx_chunks, M_LATENT_GROUP)

    def step(c, d, c_olds, m_prevs):
        r0 = pl.multiple_of(c * L, L)
        kc = k_s[pl.ds(r0, L), :]
        qtc = qt_s[:, pl.ds(r0, L)]
        h_ts, c_news, m_news = [], [], []
        for h in range(M_HEADS):
            cl, a, cl_last, a_last, w_src = [row_s[d, kind, h:h + 1, pl.ds(r0, L)] for kind in range(5)]
            m_prev = m_prevs[h]
            inter = cl + m_prev
            m_t = jnp.maximum(inter, a)
            r = (jnp.exp(inter - m_t) * _dot(c_olds[h].astype(BF16), qt_head(qtc, h))
                 + jnp.exp(a - m_t) * r_s[d, h * M_AUG:(h + 1) * M_AUG, pl.ds(r0, L)])
            denom = jnp.maximum(jnp.abs(r[D:D + 1]), jnp.exp(-m_t))
            h_ts.append(r[0:D] * (1.0 / denom))
            m_new = jnp.maximum(cl_last + m_prev, a_last)
            w_old = jnp.exp(cl_last + m_prev - m_new)
            w_upd = jnp.exp(a_last - m_new)
            vw = (vt_s[h, :, pl.ds(r0, L)].astype(F32) * w_src).astype(BF16)
            scale = lambda w: jnp.concatenate([w] * (W // L), axis=1)
            c_news.append(scale(w_old) * c_olds[h] + scale(w_upd) * _dot(vw, kc))
            m_news.append(m_new)
        return r0, h_ts, c_news, m_news

    def body(i, m_all):
        chunks = [i, jnp.where(i < n_ctx_chunks, n_ctx_chunks - 1 - i, n_chunks - 1 + n_ctx_chunks - i)]
        c_olds = [[c_s[d, h] for h in range(M_HEADS)] for d in range(2)]
        outs = [step(chunks[d], d, c_olds[d], m_all[d]) for d in range(2)]
        for d in range(2):
            r0, h_ts, c_news, _ = outs[d]
            for h in range(M_HEADS):
                r_s[d, h * M_AUG:h * M_AUG + D, pl.ds(r0, L)] = h_ts[h]
                c_s[d, h] = c_news[h]
        return tuple(tuple(outs[d][3]) for d in range(2))

    m0 = tuple(tuple(jnp.zeros((1, L), F32) for _ in range(M_HEADS)) for _ in range(2))
    lax.fori_loop(0, n_chunks, body, m0)

    ng = jnp.broadcast_to(ng_ref[...], (W, L))

    def post(j, carry):
        r0s = [pl.multiple_of((j * M_LOCAL_GROUP + i) * L, L) for i in range(M_LOCAL_GROUP)]
        hts = [[r_s[0, h * M_AUG:h * M_AUG + D, pl.ds(r0, L)] + r_s[1, h * M_AUG:h * M_AUG + D, pl.ds(r0, L)]
                for h in range(M_HEADS)] for r0 in r0s]
        scales = [[lax.rsqrt(jnp.sum(ht * ht, axis=0, keepdims=True) * (1.0 / D) + EPS) for ht in hs] for hs in hts]
        hnts = [(jnp.concatenate([ht * sc for ht, sc in zip(hs, scs)], axis=0) * ng).T for hs, scs in zip(hts, scales)]
        for r0, hnt in zip(r0s, hnts):
            o_ref[0, pl.ds(r0, L), :] = _sigmoid(u_ref[0, pl.ds(r0, L), U_MO:U_MO + W]) * hnt
        return carry

    lax.fori_loop(0, n_chunks // M_LOCAL_GROUP, post, 0)


def _rope_tables(n_rows):
    nf = HEAD_DIM // 4
    inv = ROPE_BASE ** (-jnp.arange(nf, dtype=F32) / nf)
    d = np.arange(M_WIDTH) % HEAD_DIM
    f_idx = d % nf
    is_row = d < HEAD_DIM // 2
    first_quarter = (d % (HEAD_DIM // 2)) < nf

    def tables(pos, sel):
        ang = pos[:, None] * inv[None, :]
        ang = ang[:, f_idx]
        cos = jnp.where(sel, jnp.cos(ang), 0.0)
        sin = jnp.where(sel, jnp.sin(ang), 0.0)
        sin_a = jnp.where(first_quarter, -sin, 0.0)
        sin_b = jnp.where(first_quarter, 0.0, sin)
        return jnp.stack([cos, sin_a, sin_b]).astype(F32)

    row_t = tables(jnp.arange(n_rows, dtype=F32), is_row)
    col_t = tables(jnp.arange(GRID_W, dtype=F32), ~is_row)
    return row_t, col_t


def _mlstm(u, g_rows, gate_b, norm_g, rope_row, rope_col, n_ctx):
    b, t, _ = u.shape
    assert t % M_CHUNK == 0 and n_ctx % M_CHUNK == 0 and M_CHUNK % GRID_W == 0
    n_chunks = t // M_CHUNK
    kern = functools.partial(_mlstm_kernel, n_ctx_chunks=n_ctx // M_CHUNK, n_chunks=n_chunks)
    n_rows = rope_row.shape[1]
    return pl.pallas_call(
        kern,
        grid=(b,),
        in_specs=[
            pl.BlockSpec((1, t, 4 * M_WIDTH), lambda bi: (bi, 0, 0), pipeline_mode=pl.Buffered(1)),
            pl.BlockSpec((1, N_GATES, t), lambda bi: (bi, 0, 0)),
            pl.BlockSpec((N_GATES, 1), lambda bi: (0, 0)),
            pl.BlockSpec((M_WIDTH, 1), lambda bi: (0, 0)),
            pl.BlockSpec((3, n_rows, M_WIDTH), lambda bi: (0, 0, 0)),
            pl.BlockSpec((3, GRID_W, M_WIDTH), lambda bi: (0, 0, 0)),
        ],
        out_specs=pl.BlockSpec((1, t, M_WIDTH), lambda bi: (bi, 0, 0)),
        out_shape=jax.ShapeDtypeStruct((b, t, M_WIDTH), F32),
        scratch_shapes=[
            pltpu.VMEM((M_WIDTH, t), BF16),
            pltpu.VMEM((t, M_WIDTH), BF16),
            pltpu.VMEM((M_HEADS, M_AUG, t), BF16),
            pltpu.VMEM((2, M_HEADS * M_AUG, t), F32),
            pltpu.VMEM((2, 5, 8, t), F32),
            pltpu.VMEM((2, M_HEADS, M_AUG, M_WIDTH), F32),
        ],
        compiler_params=_params(("arbitrary",)),
    )(u, g_rows, gate_b.reshape(N_GATES, 1), norm_g.reshape(M_WIDTH, 1), rope_row, rope_col)


def _na_kernel(q_ref, k_ref, v_ref, qg_ref, kg_ref, bias_ref, o_ref, qt_s, kz_s, vt_s, s_s, *, n_ctx, n_rows):
    t = q_ref.shape[1]
    lane = lax.broadcasted_iota(jnp.int32, (1, LANES), 1)
    half = [lane < HEAD_DIM, lane >= HEAD_DIM]
    hi = lax.broadcasted_iota(jnp.int32, (LANES, LANES), 0) // HEAD_DIM
    hj = lax.broadcasted_iota(jnp.int32, (LANES, LANES), 1) // HEAD_DIM
    same_head = (hi == hj).astype(BF16)
    qg = qg_ref[...] * (HEAD_DIM ** -0.5 * LOG2E)
    kg = kg_ref[...]
    qblk = NA_QROWS * GRID_W
    kc = NA_KEY_CHUNK
    sub = 8

    for hh in range(2):
        vt_s[hh, HEAD_DIM:, :] = jnp.ones((vt_s.shape[1] - HEAD_DIM, t), BF16)

    def prep(r0s):
        qs = [q_ref[0, pl.ds(r0, ROW_TILE), :] for r0 in r0s]
        ks = [k_ref[0, pl.ds(r0, ROW_TILE), :] for r0 in r0s]
        vts = [v_ref[0, pl.ds(r0, ROW_TILE), :].T for r0 in r0s]
        qms = [_dot_f32_rhs01(q * q, same_head) * (1.0 / HEAD_DIM) for q in qs]
        kms = [_dot_f32_rhs01(k * k, same_head) * (1.0 / HEAD_DIM) for k in ks]
        qnts = [(q * lax.rsqrt(m + EPS) * qg).T for q, m in zip(qs, qms)]
        kns = [k * lax.rsqrt(m + EPS) * kg for k, m in zip(ks, kms)]
        for r0, qnt, kn, vt in zip(r0s, qnts, kns, vts):
            qt_s[:, pl.ds(r0, ROW_TILE)] = qnt.astype(BF16)
            for hh in range(2):
                kz_s[hh, pl.ds(r0, ROW_TILE), :] = jnp.where(half[hh], kn, 0.0).astype(BF16)
                vt_s[hh, 0:HEAD_DIM, pl.ds(r0, ROW_TILE)] = vt[hh * HEAD_DIM:(hh + 1) * HEAD_DIM].astype(BF16)

    n_tiles = t // ROW_TILE

    def prep_pair(j, carry):
        prep([pl.multiple_of((2 * j + i) * ROW_TILE, ROW_TILE) for i in range(2)])
        return carry

    lax.fori_loop(0, n_tiles // 2, prep_pair, 0)
    if n_tiles % 2:
        prep([(n_tiles - 1) * ROW_TILE])

    def attend(blocks):
        jobs = [(b, hh) for b in range(len(blocks)) for hh in range(2)]
        qts = [qt_s[:, pl.ds(q0, qblk)] for q0, _ in blocks]
        n_chunks = len(blocks[0][1])
        mx = [None] * len(jobs)
        for ci in range(n_chunks):
            for ji, (b, hh) in enumerate(jobs):
                k0, tab = blocks[b][1][ci]
                s = _dot(kz_s[hh, pl.ds(k0, kc), :], qts[b])
                if tab is not None:
                    s = s + bias_ref[0, hh, tab[0], tab[1]:tab[1] + kc, :]
                s_s[ji, ci * kc:(ci + 1) * kc, :] = s
                cm = functools.reduce(jnp.maximum, [s[i * sub:(i + 1) * sub] for i in range(kc // sub)])
                mx[ji] = cm if mx[ji] is None else jnp.maximum(mx[ji], cm)
        ms = [m.max(axis=0, keepdims=True) for m in mx]
        accs = [None] * len(jobs)
        for ci in range(n_chunks):
            for ji, (b, hh) in enumerate(jobs):
                k0, _ = blocks[b][1][ci]
                p = jnp.exp2(s_s[ji, ci * kc:(ci + 1) * kc, :] - ms[ji]).astype(BF16)
                pv = _dot(vt_s[hh, :, pl.ds(k0, kc)], p)
                accs[ji] = pv if accs[ji] is None else accs[ji] + pv
        outs = []
        for b in range(len(blocks)):
            heads = [accs[2 * b + hh][0:HEAD_DIM] / accs[2 * b + hh][HEAD_DIM:HEAD_DIM + 1] for hh in range(2)]
            outs.append(jnp.concatenate(heads, axis=0).T)
        return outs

    ctx_chunks = [(ci * kc, None) for ci in range(n_ctx // kc)]
    o_ref[0, 0:n_ctx, :] = attend([(0, ctx_chunks)])[0]

    n_blocks = n_rows // NA_QROWS

    def group(gi, carry):
        blocks = []
        for j in range(NA_GROUP):
            bi = gi * NA_GROUP + j
            r0 = bi * NA_QROWS
            sb = jnp.clip(r0 - WIN_R // 2, 0, n_rows - NA_BAND)
            case = jnp.where(bi == 0, 0, jnp.where(bi == n_blocks - 1, 2, 1))
            q0 = pl.multiple_of(n_ctx + r0 * GRID_W, qblk)
            k0 = pl.multiple_of(n_ctx + sb * GRID_W, qblk)
            local = [(pl.multiple_of(k0 + ci * kc, kc), (case, ci * kc)) for ci in range(NA_BAND * GRID_W // kc)]
            blocks.append((q0, local + ctx_chunks))
        for (q0, _), out in zip(blocks, attend(blocks)):
            o_ref[0, pl.ds(q0, qblk), :] = out
        return carry

    lax.fori_loop(0, n_blocks // NA_GROUP, group, 0)


def _na_bias_tables(rpb):
    n_layers, n_heads = rpb.shape[:2]
    cols = np.arange(GRID_W)
    col0 = np.clip(cols - WIN_C // 2, 0, GRID_W - WIN_C)
    valid = (cols[:, None] >= col0[None, :]) & (cols[:, None] < col0[None, :] + WIN_C)
    dc = cols[:, None] - cols[None, :] + WIN_C - 1
    col_sel = ((dc[..., None] == np.arange(RPB_C)) & valid[..., None]).astype(np.float32)
    toep = jnp.einsum('lhrd,xcd->lhrxc', rpb, col_sel, precision=lax.Precision.HIGHEST)
    toep = jnp.where(valid, toep * LOG2E, NEG * LOG2E).astype(F32)
    nk, nq = NA_BAND * GRID_W, NA_QROWS * GRID_W
    return pl.pallas_call(
        _na_bias_kernel,
        grid=(n_layers * n_heads,),
        in_specs=[pl.BlockSpec((1, RPB_R, GRID_W, GRID_W), lambda i: (i, 0, 0, 0))],
        out_specs=pl.BlockSpec((1, 3, nk, nq), lambda i: (i, 0, 0, 0)),
        out_shape=jax.ShapeDtypeStruct((n_layers * n_heads, 3, nk, nq), F32),
        compiler_params=_params(("arbitrary",)),
    )(toep.reshape(n_layers * n_heads, RPB_R, GRID_W, GRID_W)).reshape(n_layers, n_heads, 3, nk, nq)


def _na_bias_kernel(t_ref, o_ref):
    blank = jnp.full((GRID_W, GRID_W), NEG * LOG2E, F32)
    for case in range(3):
        for jj in range(NA_BAND):
            tiles = []
            for i in range(NA_QROWS):
                off = (0, i, NA_BAND - WIN_R)[case]
                rel = (jj - i, jj - i - WIN_R // 2, jj - i - NA_BAND + NA_QROWS)[case]
                tiles.append(t_ref[0, rel + WIN_R - 1] if off <= jj < off + WIN_R else blank)
            o_ref[0, case, jj * GRID_W:(jj + 1) * GRID_W, :] = jnp.concatenate(tiles, axis=1)


def _na(u, qn_g, kn_g, bias, n_ctx):
    b, t, _ = u.shape
    n_rows = (t - n_ctx) // GRID_W
    assert n_rows % (NA_QROWS * NA_GROUP) == 0 and n_rows >= NA_BAND and n_ctx == NA_QROWS * GRID_W
    pairs = NA_HEADS // 2
    nk, nq = NA_BAND * GRID_W, NA_QROWS * GRID_W
    bias = bias.reshape(pairs, 2, 3, nk, nq)
    qg = jnp.tile(qn_g, 2).reshape(1, LANES)
    kg = jnp.tile(kn_g, 2).reshape(1, LANES)
    kern = functools.partial(_na_kernel, n_ctx=n_ctx, n_rows=n_rows)
    col = lambda base: (lambda hp, bi: (bi, 0, base // LANES + hp))
    return pl.pallas_call(
        kern,
        grid=(pairs, b),
        in_specs=[
            pl.BlockSpec((1, t, LANES), col(U_NQ)),
            pl.BlockSpec((1, t, LANES), col(U_NK)),
            pl.BlockSpec((1, t, LANES), col(U_NV)),
            pl.BlockSpec((1, LANES), lambda hp, bi: (0, 0)),
            pl.BlockSpec((1, LANES), lambda hp, bi: (0, 0)),
            pl.BlockSpec((1, 2, 3, nk, nq), lambda hp, bi: (hp, 0, 0, 0, 0)),
        ],
        out_specs=pl.BlockSpec((1, t, LANES), lambda hp, bi: (bi, 0, hp)),
        out_shape=jax.ShapeDtypeStruct((b, t, NA_WIDTH), F32),
        scratch_shapes=[
            pltpu.VMEM((LANES, t), BF16),
            pltpu.VMEM((2, t, LANES), BF16),
            pltpu.VMEM((2, HEAD_DIM + NA_ONES_ROWS, t), BF16),
            pltpu.VMEM((2 * NA_GROUP, nk + n_ctx, nq), F32),
        ],
        compiler_params=_params(("arbitrary", "arbitrary")),
    )(u, u, u, qg, kg, bias)


def _lru_kernel(x_ref, g_ref, cw_ref, cb_ref, wa_ref, ba_ref, wx_ref, bx_ref, lam_ref, o_ref,
                xp_s, xc_s, a_s, b_s, h_s, *, n_tiles):
    tl = LRU_TILE
    halo = LRU_HALO
    t = x_ref.shape[1]
    width = x_ref.shape[2]
    group = 8

    xp_s[0:halo, :] = jnp.zeros((halo, width), F32)
    xp_s[halo + t:halo + t + halo, :] = jnp.zeros((halo, width), F32)

    def fill(i, carry):
        r0 = pl.multiple_of(i * tl, tl)
        xp_s[pl.ds(pl.multiple_of(r0 + halo, halo), tl), :] = x_ref[0, pl.ds(r0, tl), :]
        return carry

    lax.fori_loop(0, n_tiles, fill, 0)

    row = lax.broadcasted_iota(jnp.int32, (tl, 1), 0)
    rowg = row % group
    cw = cw_ref[...]
    cb = cb_ref[...]

    def conv_tile(i):
        r0 = pl.multiple_of(i * tl, tl)
        win = xp_s[pl.ds(r0, tl + 2 * halo), :]
        n_win = tl + 2 * halo
        xm2 = pltpu.roll(win, 2, 0)[halo:halo + tl]
        xm1 = pltpu.roll(win, 1, 0)[halo:halo + tl]
        x0 = win[halo:halo + tl]
        xp1 = pltpu.roll(win, n_win - 1, 0)[halo:halo + tl]
        lo = i <= 1
        hi = i == 0
        xm2 = jnp.where(lo & (row < 2), 0.0, xm2)
        xm1 = jnp.where(lo & (row < 1), 0.0, xm1)
        xp1 = jnp.where(hi & (row == tl - 1), 0.0, xp1)
        return xm2 * cw[0:1] + xm1 * cw[1:2] + x0 * cw[2:3] + xp1 * cw[3:4] + cb

    def gates(i, d):
        r0 = pl.multiple_of(i * tl, tl)
        if d == 0:
            xc = conv_tile(i)
            xc_s[pl.ds(r0, tl), :] = xc
        else:
            xc = xc_s[pl.ds(r0, tl), :]
        xb = xc.astype(BF16)
        r = _sigmoid(_dot(xb, wa_ref[d]) + ba_ref[d:d + 1, :])
        ig = _sigmoid(_dot(xb, wx_ref[d]) + bx_ref[d:d + 1, :])
        log_a = -LRU_C * r * _softplus(-lam_ref[d:d + 1, :])
        a = jnp.exp(log_a)
        bt = jnp.sqrt(-jnp.tanh(log_a) * (a * a + 1.0)) * (ig * xc)
        return a, bt

    def scan_tile(i, d, carry):
        a, bt = gates(i, d)

        def shift(x, kk):
            x3 = x.reshape(tl // group, group, width)
            return pltpu.roll(x3, kk if d == 0 else group - kk, 1).reshape(tl, width)

        for kk in (1, 2, 4):
            mk = rowg >= kk if d == 0 else rowg < group - kk
            bt = jnp.where(mk, a * shift(bt, kk) + bt, bt)
            a = jnp.where(mk, a * shift(a, kk), a)
        a_s[...] = a
        b_s[...] = bt
        n_groups = tl // group

        def grp(gidx, c):
            gg = gidx if d == 0 else n_groups - 1 - gidx
            g0 = pl.multiple_of(gg * group, group)
            h = b_s[pl.ds(g0, group), :] + a_s[pl.ds(g0, group), :] * c
            h_s[pl.ds(g0, group), :] = h
            return h[group - 1:group, :] if d == 0 else h[0:1, :]

        return lax.fori_loop(0, n_groups, grp, carry, unroll=8)

    def fwd(i, carry):
        carry = scan_tile(i, 0, carry)
        r0 = pl.multiple_of(i * tl, tl)
        o_ref[0, pl.ds(r0, tl), :] = h_s[...]
        return carry

    def bwd(i, carry):
        carry = scan_tile(i, 1, carry)
        r0 = pl.multiple_of(i * tl, tl)
        o_ref[0, pl.ds(r0, tl), :] = (o_ref[0, pl.ds(r0, tl), :] + h_s[...]) * jax.nn.gelu(g_ref[0, pl.ds(r0, tl), :])
        return carry

    h0 = jnp.zeros((1, width), F32)
    lax.fori_loop(0, n_tiles, fwd, h0)
    carry = bwd(jnp.int32(0), h0)
    lax.fori_loop(1, n_tiles, lambda j, c: bwd(n_tiles - j, c), carry)


def _block_diag_dense(w):
    nd, k, c, _ = w.shape
    eye = jnp.eye(k, dtype=w.dtype)
    return jnp.einsum('dkce,kj->dkcje', w, eye).reshape(nd, k * c, k * c)


def _lru(u, conv_w, conv_b, w_a, b_a, w_x, b_x, lam, n_ctx):
    b, t, _ = u.shape
    width = conv_w.shape[1]
    assert n_ctx == LRU_TILE and t % LRU_TILE == 0
    n_tiles = t // LRU_TILE
    wa = _block_diag_dense(w_a).astype(BF16)
    wx = _block_diag_dense(w_x).astype(BF16)
    kern = functools.partial(_lru_kernel, n_tiles=n_tiles)
    full = lambda shape: pl.BlockSpec(shape, lambda bi: (0,) * len(shape))
    return pl.pallas_call(
        kern,
        grid=(b,),
        in_specs=[
            pl.BlockSpec((1, t, width), lambda bi: (bi, 0, U_XR // width)),
            pl.BlockSpec((1, t, width), lambda bi: (bi, 0, U_GT // width)),
            full((CONV_W, width)),
            full((1, width)),
            full((2, width, width)),
            full((2, width)),
            full((2, width, width)),
            full((2, width)),
            full((2, width)),
        ],
        out_specs=pl.BlockSpec((1, t, width), lambda bi: (bi, 0, 0)),
        out_shape=jax.ShapeDtypeStruct((b, t, width), F32),
        scratch_shapes=[
            pltpu.VMEM((t + 2 * LRU_HALO, width), F32),
            pltpu.VMEM((t, width), F32),
            pltpu.VMEM((LRU_TILE, width), F32),
            pltpu.VMEM((LRU_TILE, width), F32),
            pltpu.VMEM((LRU_TILE, width), F32),
        ],
        compiler_params=_params(("arbitrary",)),
    )(u, u, conv_w, conv_b.reshape(1, width), wa, b_a, wx, b_x, lam)


def _out_mlp_kernel(xc_ref, xl_ref, ym_ref, yn_ref, yl_ref, mod_ref, g_ref, wo_ref, w1_ref, w2_ref, oc_ref, ol_ref,
                    *, hid_tile):
    def body(x_ref, o_ref):
        mod = mod_ref[0, 0]
        att = _dot(ym_ref[0].astype(BF16), wo_ref[0:M_WIDTH, :])
        att = att + _dot(yn_ref[0].astype(BF16), wo_ref[M_WIDTH:M_WIDTH + NA_WIDTH, :])
        att = att + _dot(yl_ref[0].astype(BF16), wo_ref[M_WIDTH + NA_WIDTH:, :])
        x1 = x_ref[0] + mod[2:3, :] * att
        ms = jnp.mean(x1 * x1, axis=-1, keepdims=True)
        y = x1 * lax.rsqrt(ms + EPS) * g_ref[...]
        h = (y * (1.0 + mod[4:5, :]) + mod[3:4, :]).astype(BF16)
        hidden = w1_ref.shape[1]
        acc = None
        for j in range(hidden // hid_tile):
            a = jnp.maximum(_dot(h, w1_ref[:, j * hid_tile:(j + 1) * hid_tile]), 0.0)
            part = _dot((a * a).astype(BF16), w2_ref[j * hid_tile:(j + 1) * hid_tile, :])
            acc = part if acc is None else acc + part
        o_ref[0] = x1 + mod[5:6, :] * acc

    _on_stream_tile(body, xc_ref, xl_ref, (oc_ref, ol_ref))


def _out_mlp(xc, xl, ym, yn, yl, modv, g2, w_out, w1, w2):
    b, n_ctx, d = xc.shape
    t = n_ctx + xl.shape[1]
    hidden = w1.shape[1]
    tok = lambda w: pl.BlockSpec((1, ROW_TILE, w), lambda bi, i: (bi, i, 0))
    kern = functools.partial(_out_mlp_kernel, hid_tile=1024)
    return pl.pallas_call(
        kern,
        grid=(b, t // ROW_TILE),
        in_specs=_stream_specs(n_ctx, d) + [
            tok(M_WIDTH), tok(NA_WIDTH), tok(d - M_WIDTH - NA_WIDTH),
            pl.BlockSpec((1, 1, 6, d), lambda bi, i: (bi, jnp.minimum(i, 1), 0, 0)),
            pl.BlockSpec((1, d), lambda bi, i: (0, 0)),
            _resident((d, d), lambda bi, i: (0, 0)),
            _resident((d, hidden), lambda bi, i: (0, 0)),
            _resident((hidden, d), lambda bi, i: (0, 0)),
        ],
        out_specs=_stream_specs(n_ctx, d),
        out_shape=[jax.ShapeDtypeStruct(xc.shape, F32), jax.ShapeDtypeStruct(xl.shape, F32)],
        compiler_params=_params(("arbitrary", "arbitrary")),
    )(xc, xl, ym, yn, yl, modv, g2.reshape(1, d), w_out, w1, w2)


def _split_w_in(w_in):
    m_end = 4 * M_WIDTH
    na_end = m_end + N_GATES + 3 * NA_WIDTH
    main = jnp.concatenate([w_in[..., :m_end], w_in[..., na_end:], w_in[..., m_end + N_GATES:na_end]], axis=-1)
    gates = jnp.swapaxes(w_in[..., m_end:m_end + N_GATES], -1, -2)
    return main.astype(BF16), gates.astype(BF16)


def kernel(x, c, ctx, c_ctx, w_mod, b_mod, norm1_g, norm2_g, w_in, mlstm_gate_b, mlstm_norm_g, na_q_norm_g, na_k_norm_g, na_rpb, lru_conv_w, lru_conv_b, lru_w_a, lru_b_a, lru_w_x, lru_b_x, lru_lambda, w_out, w_mlp1, w_mlp2):
    b, n, d = x.shape
    n_ctx = ctx.shape[1]
    depth = w_in.shape[0]
    assert n_ctx == ROW_TILE and n % ROW_TILE == 0 and n % GRID_W == 0

    cond_rows = 8 * ((b + 1 + 7) // 8)
    cond = jnp.zeros((cond_rows, d), F32).at[:b].set(c).at[b].set(c_ctx)
    mod_all = _modulation(cond, w_mod, b_mod)
    rope_row, rope_col = _rope_tables(n // GRID_W)
    w_in_p, w_gate_t = _split_w_in(w_in)
    na_bias = _na_bias_tables(na_rpb)
    w_out_b, w1_b, w2_b = w_out.astype(BF16), w_mlp1.astype(BF16), w_mlp2.astype(BF16)

    xc, xl = ctx, x
    for l in range(depth):
        lat = mod_all[l, :b].reshape(b, 6, d)
        cx = jnp.broadcast_to(mod_all[l, b].reshape(1, 6, d), (b, 6, d))
        modv = jnp.stack([cx, lat], axis=1)
        u, g_rows = _inproj(xc, xl, modv, norm1_g[l], w_in_p[l], w_gate_t[l])
        ym = _mlstm(u, g_rows, mlstm_gate_b[l], mlstm_norm_g[l], rope_row, rope_col, n_ctx)
        yn = _na(u, na_q_norm_g[l], na_k_norm_g[l], na_bias[l], n_ctx)
        yl = _lru(u, lru_conv_w[l], lru_conv_b[l], lru_w_a[l], lru_b_a[l], lru_w_x[l], lru_b_x[l], lru_lambda[l], n_ctx)
        xc, xl = _out_mlp(xc, xl, ym, yn, yl, modv, norm2_g[l], w_out_b[l], w1_b[l], w2_b[l])
    return xl
```

```python
import functools
import math

import numpy as np
import jax
import jax.numpy as jnp
from jax import lax
from jax.experimental import pallas as pl
from jax.experimental.pallas import tpu as pltpu

F32 = jnp.float32
BF16 = jnp.bfloat16

HEAD_DIM = 64
M_HEADS = 4
M_WIDTH = M_HEADS * HEAD_DIM
NA_HEADS = 8
NA_WIDTH = NA_HEADS * HEAD_DIM
N_GATES = 4 * M_HEADS
GRID_W = 64
M_CHUNK = 128
M_AUG = HEAD_DIM + 16
M_LOCAL_GROUP = 2
M_LATENT_GROUP = 4
WIN_R = 8
WIN_C = 16
RPB_R = 2 * WIN_R - 1
RPB_C = 2 * WIN_C - 1
NA_QROWS = 4
NA_BAND = NA_QROWS + WIN_R
NA_KEY_CHUNK = 128
NA_ONES_ROWS = 16
NA_GROUP = 8
LOG2E = 1.4426950408889634
CONV_W = 4
LRU_C = 8.0
ROPE_BASE = 10000.0
EPS = 1e-6
NEG = -1e30

LANES = 128
ROW_TILE = 256
LRU_TILE = 256
LRU_HALO = 8
VMEM_LIMIT = 56 * 1024 * 1024

U_MQ, U_MK, U_MV, U_MO = 0, 256, 512, 768
U_XR, U_GT = 1024, 1280
U_NQ, U_NK, U_NV = 1536, 2048, 2560
U_COLS = 3072
LRU_WIDTH = U_NQ - U_GT
NA_COL_BLOCKS = (U_COLS - U_NQ) // LANES


def _dot(a, b):
    return jnp.dot(a, b, preferred_element_type=F32)


def _dot_f32_rhs01(x, m01):
    hi = x.astype(BF16)
    lo = (x - hi.astype(F32)).astype(BF16)
    return _dot(hi, m01) + _dot(lo, m01)


def _sigmoid(x):
    return 0.5 * jnp.tanh(0.5 * x) + 0.5


def _log_sigmoid(x):
    return jnp.minimum(x, 0.0) - jnp.log1p(jnp.exp(-jnp.abs(x)))


def _softplus(x):
    return jnp.maximum(x, 0.0) + jnp.log1p(jnp.exp(-jnp.abs(x)))


def _params(sem):
    return pltpu.CompilerParams(dimension_semantics=sem, vmem_limit_bytes=VMEM_LIMIT)


def _resident(shape, index_map):
    return pl.BlockSpec(shape, index_map, pipeline_mode=pl.Buffered(1))


def _mod_kernel(a_ref, w_ref, b_ref, o_ref):
    a = a_ref[...]
    act = (a * jax.nn.sigmoid(a)).astype(BF16)
    o_ref[0] = _dot(act, w_ref[0].astype(BF16)) + b_ref[0]


def _modulation(cond, w_mod, b_mod):
    depth, d, n6 = w_mod.shape
    rows = cond.shape[0]
    tn = 1536
    return pl.pallas_call(
        _mod_kernel,
        grid=(depth, n6 // tn),
        in_specs=[
            pl.BlockSpec((rows, d), lambda l, j: (0, 0)),
            pl.BlockSpec((1, d, tn), lambda l, j: (l, 0, j)),
            pl.BlockSpec((1, 1, tn), lambda l, j: (l, 0, j)),
        ],
        out_specs=pl.BlockSpec((1, rows, tn), lambda l, j: (l, 0, j)),
        out_shape=jax.ShapeDtypeStruct((depth, rows, n6), F32),
        compiler_params=_params(("arbitrary", "arbitrary")),
    )(cond, w_mod, b_mod.reshape(depth, 1, n6))


def _stream_specs(n_ctx, d):
    assert n_ctx == ROW_TILE
    return [
        pl.BlockSpec((1, ROW_TILE, d), lambda bi, i: (bi, 0, 0)),
        pl.BlockSpec((1, ROW_TILE, d), lambda bi, i: (bi, jnp.maximum(i - 1, 0), 0)),
    ]


def _on_stream_tile(body, xc_ref, xl_ref, *outs):
    @pl.when(pl.program_id(1) == 0)
    def _():
        body(xc_ref, *[o[0] for o in outs])

    @pl.when(pl.program_id(1) > 0)
    def _():
        body(xl_ref, *[o[1] for o in outs])


def _inproj_kernel(xc_ref, xl_ref, mod_ref, g_ref, w_ref, wg_ref, um_ref, ul_ref, un_ref, gt_ref):
    def body(x_ref):
        x = x_ref[0]
        ms = jnp.mean(x * x, axis=-1, keepdims=True)
        y = x * lax.rsqrt(ms + EPS) * g_ref[...]
        mod = mod_ref[0, 0]
        h = (y * (1.0 + mod[1:2, :]) + mod[0:1, :]).astype(BF16)
        u = _dot(h, w_ref[...])
        um_ref[0] = u[:, 0:U_XR]
        for j in range(ul_ref.shape[1]):
            ul_ref[0, j] = u[:, U_XR + j * LRU_WIDTH:U_XR + (j + 1) * LRU_WIDTH]
        for j in range(un_ref.shape[1]):
            un_ref[0, j] = u[:, U_NQ + j * LANES:U_NQ + (j + 1) * LANES]
        gt_ref[0] = lax.dot_general(wg_ref[...], h, (((1,), (1,)), ((), ())), preferred_element_type=F32)

    _on_stream_tile(body, xc_ref, xl_ref)


def _inproj(xc, xl, modv, g1, w_in_p, w_gate_t):
    b, n_ctx, d = xc.shape
    t = n_ctx + xl.shape[1]
    return pl.pallas_call(
        _inproj_kernel,
        grid=(b, t // ROW_TILE),
        in_specs=_stream_specs(n_ctx, d) + [
            pl.BlockSpec((1, 1, 6, d), lambda bi, i: (bi, jnp.minimum(i, 1), 0, 0)),
            pl.BlockSpec((1, d), lambda bi, i: (0, 0)),
            _resident((d, U_COLS), lambda bi, i: (0, 0)),
            _resident((N_GATES, d), lambda bi, i: (0, 0)),
        ],
        out_specs=[
            pl.BlockSpec((1, ROW_TILE, U_XR), lambda bi, i: (bi, i, 0)),
            pl.BlockSpec((1, 2, ROW_TILE, LRU_WIDTH), lambda bi, i: (bi, 0, i, 0)),
            pl.BlockSpec((1, NA_COL_BLOCKS, ROW_TILE, LANES), lambda bi, i: (bi, 0, i, 0)),
            pl.BlockSpec((1, N_GATES, ROW_TILE), lambda bi, i: (bi, 0, i)),
        ],
        out_shape=[
            jax.ShapeDtypeStruct((b, t, U_XR), F32),
            jax.ShapeDtypeStruct((b, 2, t, LRU_WIDTH), F32),
            jax.ShapeDtypeStruct((b, NA_COL_BLOCKS, t, LANES), F32),
            jax.ShapeDtypeStruct((b, N_GATES, t), F32),
        ],
        compiler_params=_params(("arbitrary", "arbitrary")),
    )(xc, xl, modv, g1.reshape(1, d), w_in_p, w_gate_t)


def _mlstm_kernel(u_ref, gr_ref, gb_ref, ng_ref, rrow_ref, rcol_ref, o_ref,
                  qt_s, k_s, vt_s, r_s, row_s, c_s, *, n_ctx_chunks, n_chunks):
    L = M_CHUNK
    W = M_WIDTH
    D = HEAD_DIM
    groups = L // GRID_W

    for h in range(M_HEADS):
        vt_s[h, D:, :] = jnp.ones((M_AUG - D, vt_s.shape[2]), BF16)
    c_s[...] = jnp.zeros(c_s.shape, F32)

    def prep(cs, latent):
        r0s = [pl.multiple_of(c * L, L) for c in cs]
        qs = [u_ref[0, pl.ds(r0, L), U_MQ:U_MQ + W] for r0 in r0s]
        ks = [u_ref[0, pl.ds(r0, L), U_MK:U_MK + W] * D ** -0.5 for r0 in r0s]
        vts = [u_ref[0, pl.ds(r0, L), U_MV:U_MV + W].T for r0 in r0s]
        if latent:
            def table(c, i):
                first = (c - n_ctx_chunks) * groups
                return jnp.concatenate([rrow_ref[i, pl.ds(first + g, 1), :] + rcol_ref[i] for g in range(groups)], axis=0)

            def rope(x, c):
                return (x * table(c, 0) + pltpu.roll(x, W - D // 4, 1) * table(c, 1)
                        + pltpu.roll(x, D // 4, 1) * table(c, 2))

            qs = [rope(q, c) for q, c in zip(qs, cs)]
            ks = [rope(k, c) for k, c in zip(ks, cs)]
        qts = [q.T for q in qs]
        for r0, qt, k, vt in zip(r0s, qts, ks, vts):
            qt_s[:, pl.ds(r0, L)] = qt.astype(BF16)
            k_s[pl.ds(r0, L), :] = k.astype(BF16)
            for h in range(M_HEADS):
                vt_s[h, 0:D, pl.ds(r0, L)] = vt[h * D:(h + 1) * D].astype(BF16)

    assert n_ctx_chunks % M_LOCAL_GROUP == 0 and n_chunks % M_LOCAL_GROUP == 0

    def prep_group(latent):
        def body(j, carry):
            first = j * M_LOCAL_GROUP + (n_ctx_chunks if latent else 0)
            prep([first + i for i in range(M_LOCAL_GROUP)], latent)
            return carry
        return body

    lax.fori_loop(0, n_ctx_chunks // M_LOCAL_GROUP, prep_group(False), 0)
    lax.fori_loop(0, (n_chunks - n_ctx_chunks) // M_LOCAL_GROUP, prep_group(True), 0)

    si = lax.broadcasted_iota(jnp.int32, (L, L), 0)
    li = lax.broadcasted_iota(jnp.int32, (L, L), 1)
    keeps = [si <= li, si >= li]
    gbias = [jnp.broadcast_to(gb_ref[8 * d:8 * d + 8, :], (8, L)) for d in range(2)]
    zrows = jnp.zeros((D, L), BF16)

    def qt_head(qtc, h):
        parts = [zrows] * M_HEADS
        parts[h] = qtc[h * D:(h + 1) * D]
        return jnp.concatenate(parts, axis=0)

    def local(first, n_group):
        r0s = [pl.multiple_of((first + j) * L, L) for j in range(n_group)]
        scores = [[_dot(k_s[pl.ds(r0, L), :], qt_head(qt_s[:, pl.ds(r0, L)], h)) for h in range(M_HEADS)]
                  for r0 in r0s]
        vtas = [[vt_s[h, :, pl.ds(r0, L)] for h in range(M_HEADS)] for r0 in r0s]
        g8s = [[gr_ref[0, 8 * d:8 * d + 8, pl.ds(r0, L)] + gbias[d] for d in range(2)] for r0 in r0s]
        cum4s = [[pltpu.roll(_dot_f32_rhs01(_log_sigmoid(g8[d]), keeps[d].astype(BF16)), M_HEADS, 0)
                  for d in range(2)] for g8 in g8s]
        src_ts = [[(g8[d] - cum4[d]).T for d in range(2)] for g8, cum4 in zip(g8s, cum4s)]
        jdh = [(j, d, h) for j in range(n_group) for d in range(2) for h in range(M_HEADS)]
        logws = [jnp.where(keeps[d], cum4s[j][d][h:h + 1] + jnp.broadcast_to(src_ts[j][d][:, h:h + 1], (L, L)), NEG)
                 for j, d, h in jdh]
        amaxs = [lw.max(axis=0, keepdims=True) for lw in logws]
        ps = [(scores[j][h] * jnp.exp(logw - a)).astype(BF16) for (j, d, h), logw, a in zip(jdh, logws, amaxs)]
        intras = [_dot(vtas[j][h], p) for (j, d, h), p in zip(jdh, ps)]
        for (j, d, h), a, intra in zip(jdh, amaxs, intras):
            last = L - 1 if d == 0 else 0
            cl = cum4s[j][d][h:h + 1]
            a_last = jnp.broadcast_to(a[:, last:last + 1], (1, L))
            cl_last = jnp.broadcast_to(cl[:, last:last + 1], (1, L))
            w_src = jnp.exp(cl_last - cl + g8s[j][d][h:h + 1] - a_last)
            r_s[d, h * M_AUG:(h + 1) * M_AUG, pl.ds(r0s[j], L)] = intra
            for kind, row in enumerate((cl, a, cl_last, a_last, w_src)):
                row_s[d, kind, h:h + 1, pl.ds(r0s[j], L)] = row

    def local_loop(first, count, n_group):
        assert count % n_group == 0

        def body(i, carry):
            local(first + i * n_group, n_group)
            return carry

        lax.fori_loop(0, count // n_group, body, 0)

    local_loop(0, n_ctx_chunks, M_LOCAL_GROUP)
    local_loop(n_ctx_chunks, n_chunks - n_ctx_chunks, M_LATENT_GROUP)

    def step(c, d, c_olds, m_prevs):
        r0 = pl.multiple_of(c * L, L)
        kc = k_s[pl.ds(r0, L), :]
        qtc = qt_s[:, pl.ds(r0, L)]
        h_ts, c_news, m_news = [], [], []
        for h in range(M_HEADS):
            cl, a, cl_last, a_last, w_src = [row_s[d, kind, h:h + 1, pl.ds(r0, L)] for kind in range(5)]
            m_prev = m_prevs[h]
            inter = cl + m_prev
            m_t = jnp.maximum(inter, a)
            r = (jnp.exp(inter - m_t) * _dot(c_olds[h].astype(BF16), qt_head(qtc, h))
                 + jnp.exp(a - m_t) * r_s[d, h * M_AUG:(h + 1) * M_AUG, pl.ds(r0, L)])
            denom = jnp.maximum(jnp.abs(r[D:D + 1]), jnp.exp(-m_t))
            h_ts.append(r[0:D] * (1.0 / denom))
            m_new = jnp.maximum(cl_last + m_prev, a_last)
            w_old = jnp.exp(cl_last + m_prev - m_new)
            w_upd = jnp.exp(a_last - m_new)
            vw = (vt_s[h, :, pl.ds(r0, L)].astype(F32) * w_src).astype(BF16)
            scale = lambda w: jnp.concatenate([w] * (W // L), axis=1)
            c_news.append(scale(w_old) * c_olds[h] + scale(w_upd) * _dot(vw, kc))
            m_news.append(m_new)
        return r0, h_ts, c_news, m_news

    def body(i, m_all):
        chunks = [i, jnp.where(i < n_ctx_chunks, n_ctx_chunks - 1 - i, n_chunks - 1 + n_ctx_chunks - i)]
        c_olds = [[c_s[d, h] for h in range(M_HEADS)] for d in range(2)]
        outs = [step(chunks[d], d, c_olds[d], m_all[d]) for d in range(2)]
        for d in range(2):
            r0, h_ts, c_news, _ = outs[d]
            for h in range(M_HEADS):
                r_s[d, h * M_AUG:h * M_AUG + D, pl.ds(r0, L)] = h_ts[h]
                c_s[d, h] = c_news[h]
        return tuple(tuple(outs[d][3]) for d in range(2))

    m0 = tuple(tuple(jnp.zeros((1, L), F32) for _ in range(M_HEADS)) for _ in range(2))
    lax.fori_loop(0, n_chunks, body, m0)

    ng = jnp.broadcast_to(ng_ref[...], (W, L))

    def post(j, carry):
        r0s = [pl.multiple_of((j * M_LOCAL_GROUP + i) * L, L) for i in range(M_LOCAL_GROUP)]
        hts = [[r_s[0, h * M_AUG:h * M_AUG + D, pl.ds(r0, L)] + r_s[1, h * M_AUG:h * M_AUG + D, pl.ds(r0, L)]
                for h in range(M_HEADS)] for r0 in r0s]
        scales = [[lax.rsqrt(jnp.sum(ht * ht, axis=0, keepdims=True) * (1.0 / D) + EPS) for ht in hs] for hs in hts]
        hnts = [(jnp.concatenate([ht * sc for ht, sc in zip(hs, scs)], axis=0) * ng).T for hs, scs in zip(hts, scales)]
        for r0, hnt in zip(r0s, hnts):
            o_ref[0, pl.ds(r0, L), :] = _sigmoid(u_ref[0, pl.ds(r0, L), U_MO:U_MO + W]) * hnt
        return carry

    lax.fori_loop(0, n_chunks // M_LOCAL_GROUP, post, 0)


def _rope_tables(n_rows):
    nf = HEAD_DIM // 4
    inv = ROPE_BASE ** (-jnp.arange(nf, dtype=F32) / nf)
    d = np.arange(M_WIDTH) % HEAD_DIM
    f_idx = d % nf
    is_row = d < HEAD_DIM // 2
    first_quarter = (d % (HEAD_DIM // 2)) < nf

    def tables(pos, sel):
        ang = pos[:, None] * inv[None, :]
        ang = ang[:, f_idx]
        cos = jnp.where(sel, jnp.cos(ang), 0.0)
        sin = jnp.where(sel, jnp.sin(ang), 0.0)
        sin_a = jnp.where(first_quarter, -sin, 0.0)
        sin_b = jnp.where(first_quarter, 0.0, sin)
        return jnp.stack([cos, sin_a, sin_b]).astype(F32)

    row_t = tables(jnp.arange(n_rows, dtype=F32), is_row)
    col_t = tables(jnp.arange(GRID_W, dtype=F32), ~is_row)
    return row_t, col_t


def _mlstm(u, g_rows, gate_b, norm_g, rope_row, rope_col, n_ctx):
    b, t, _ = u.shape
    assert t % M_CHUNK == 0 and n_ctx % M_CHUNK == 0 and M_CHUNK % GRID_W == 0
    n_chunks = t // M_CHUNK
    kern = functools.partial(_mlstm_kernel, n_ctx_chunks=n_ctx // M_CHUNK, n_chunks=n_chunks)
    n_rows = rope_row.shape[1]
    return pl.pallas_call(
        kern,
        grid=(b,),
        in_specs=[
            pl.BlockSpec((1, t, 4 * M_WIDTH), lambda bi: (bi, 0, 0), pipeline_mode=pl.Buffered(1)),
            pl.BlockSpec((1, N_GATES, t), lambda bi: (bi, 0, 0)),
            pl.BlockSpec((N_GATES, 1), lambda bi: (0, 0)),
            pl.BlockSpec((M_WIDTH, 1), lambda bi: (0, 0)),
            pl.BlockSpec((3, n_rows, M_WIDTH), lambda bi: (0, 0, 0)),
            pl.BlockSpec((3, GRID_W, M_WIDTH), lambda bi: (0, 0, 0)),
        ],
        out_specs=pl.BlockSpec((1, t, M_WIDTH), lambda bi: (bi, 0, 0)),
        out_shape=jax.ShapeDtypeStruct((b, t, M_WIDTH), F32),
        scratch_shapes=[
            pltpu.VMEM((M_WIDTH, t), BF16),
            pltpu.VMEM((t, M_WIDTH), BF16),
            pltpu.VMEM((M_HEADS, M_AUG, t), BF16),
            pltpu.VMEM((2, M_HEADS * M_AUG, t), F32),
            pltpu.VMEM((2, 5, 8, t), F32),
            pltpu.VMEM((2, M_HEADS, M_AUG, M_WIDTH), F32),
        ],
        compiler_params=_params(("arbitrary",)),
    )(u, g_rows, gate_b.reshape(N_GATES, 1), norm_g.reshape(M_WIDTH, 1), rope_row, rope_col)


def _na_kernel(q_ref, k_ref, v_ref, qg_ref, kg_ref, bias_ref, o_ref, qt_s, kz_s, vt_s, s_s, *, n_ctx, n_rows):
    t = q_ref.shape[1]
    lane = lax.broadcasted_iota(jnp.int32, (1, LANES), 1)
    half = [lane < HEAD_DIM, lane >= HEAD_DIM]
    hi = lax.broadcasted_iota(jnp.int32, (LANES, LANES), 0) // HEAD_DIM
    hj = lax.broadcasted_iota(jnp.int32, (LANES, LANES), 1) // HEAD_DIM
    same_head = (hi == hj).astype(BF16)
    qg = qg_ref[...] * (HEAD_DIM ** -0.5 * LOG2E)
    kg = kg_ref[...]
    qblk = NA_QROWS * GRID_W
    kc = NA_KEY_CHUNK
    sub = 8

    for hh in range(2):
        vt_s[hh, HEAD_DIM:, :] = jnp.ones((vt_s.shape[1] - HEAD_DIM, t), BF16)

    def prep(r0s):
        qs = [q_ref[0, pl.ds(r0, ROW_TILE), :] for r0 in r0s]
        ks = [k_ref[0, pl.ds(r0, ROW_TILE), :] for r0 in r0s]
        vts = [v_ref[0, pl.ds(r0, ROW_TILE), :].T for r0 in r0s]
        qms = [_dot_f32_rhs01(q * q, same_head) * (1.0 / HEAD_DIM) for q in qs]
        kms = [_dot_f32_rhs01(k * k, same_head) * (1.0 / HEAD_DIM) for k in ks]
        qnts = [(q * lax.rsqrt(m + EPS) * qg).T for q, m in zip(qs, qms)]
        kns = [k * lax.rsqrt(m + EPS) * kg for k, m in zip(ks, kms)]
        for r0, qnt, kn, vt in zip(r0s, qnts, kns, vts):
            qt_s[:, pl.ds(r0, ROW_TILE)] = qnt.astype(BF16)
            for hh in range(2):
                kz_s[hh, pl.ds(r0, ROW_TILE), :] = jnp.where(half[hh], kn, 0.0).astype(BF16)
                vt_s[hh, 0:HEAD_DIM, pl.ds(r0, ROW_TILE)] = vt[hh * HEAD_DIM:(hh + 1) * HEAD_DIM].astype(BF16)

    n_tiles = t // ROW_TILE

    def prep_pair(j, carry):
        prep([pl.multiple_of((2 * j + i) * ROW_TILE, ROW_TILE) for i in range(2)])
        return carry

    lax.fori_loop(0, n_tiles // 2, prep_pair, 0)
    if n_tiles % 2:
        prep([(n_tiles - 1) * ROW_TILE])

    def attend(blocks):
        jobs = [(b, hh) for b in range(len(blocks)) for hh in range(2)]
        qts = [qt_s[:, pl.ds(q0, qblk)] for q0, _ in blocks]
        n_chunks = len(blocks[0][1])
        mx = [None] * len(jobs)
        for ci in range(n_chunks):
            for ji, (b, hh) in enumerate(jobs):
                k0, tab = blocks[b][1][ci]
                s = _dot(kz_s[hh, pl.ds(k0, kc), :], qts[b])
                if tab is not None:
                    s = s + bias_ref[0, hh, tab[0], tab[1]:tab[1] + kc, :]
                s_s[ji, ci * kc:(ci + 1) * kc, :] = s
                cm = functools.reduce(jnp.maximum, [s[i * sub:(i + 1) * sub] for i in range(kc // sub)])
                mx[ji] = cm if mx[ji] is None else jnp.maximum(mx[ji], cm)
        ms = [m.max(axis=0, keepdims=True) for m in mx]
        accs = [None] * len(jobs)
        for ci in range(n_chunks):
            for ji, (b, hh) in enumerate(jobs):
                k0, _ = blocks[b][1][ci]
                p = jnp.exp2(s_s[ji, ci * kc:(ci + 1) * kc, :] - ms[ji]).astype(BF16)
                pv = _dot(vt_s[hh, :, pl.ds(k0, kc)], p)
                accs[ji] = pv if accs[ji] is None else accs[ji] + pv
        outs = []
        for b in range(len(blocks)):
            heads = [accs[2 * b + hh][0:HEAD_DIM] / accs[2 * b + hh][HEAD_DIM:HEAD_DIM + 1] for hh in range(2)]
            outs.append(jnp.concatenate(heads, axis=0).T)
        return outs

    ctx_chunks = [(ci * kc, None) for ci in range(n_ctx // kc)]
    o_ref[0, 0:n_ctx, :] = attend([(0, ctx_chunks)])[0]

    n_blocks = n_rows // NA_QROWS

    def group(gi, carry):
        blocks = []
        for j in range(NA_GROUP):
            bi = gi * NA_GROUP + j
            r0 = bi * NA_QROWS
            sb = jnp.clip(r0 - WIN_R // 2, 0, n_rows - NA_BAND)
            case = jnp.where(bi == 0, 0, jnp.where(bi == n_blocks - 1, 2, 1))
            q0 = pl.multiple_of(n_ctx + r0 * GRID_W, qblk)
            k0 = pl.multiple_of(n_ctx + sb * GRID_W, qblk)
            local = [(pl.multiple_of(k0 + ci * kc, kc), (case, ci * kc)) for ci in range(NA_BAND * GRID_W // kc)]
            blocks.append((q0, local + ctx_chunks))
        for (q0, _), out in zip(blocks, attend(blocks)):
            o_ref[0, pl.ds(q0, qblk), :] = out
        return carry

    lax.fori_loop(0, n_blocks // NA_GROUP, group, 0)


def _na_bias_tables(rpb):
    n_layers, n_heads = rpb.shape[:2]
    cols = np.arange(GRID_W)
    col0 = np.clip(cols - WIN_C // 2, 0, GRID_W - WIN_C)
    valid = (cols[:, None] >= col0[None, :]) & (cols[:, None] < col0[None, :] + WIN_C)
    dc = cols[:, None] - cols[None, :] + WIN_C - 1
    col_sel = ((dc[..., None] == np.arange(RPB_C)) & valid[..., None]).astype(np.float32)
    toep = jnp.einsum('lhrd,xcd->lhrxc', rpb, col_sel, precision=lax.Precision.HIGHEST)
    toep = jnp.where(valid, toep * LOG2E, NEG * LOG2E).astype(F32)
    nk, nq = NA_BAND * GRID_W, NA_QROWS * GRID_W
    return pl.pallas_call(
        _na_bias_kernel,
        grid=(n_layers * n_heads,),
        in_specs=[pl.BlockSpec((1, RPB_R, GRID_W, GRID_W), lambda i: (i, 0, 0, 0))],
        out_specs=pl.BlockSpec((1, 3, nk, nq), lambda i: (i, 0, 0, 0)),
        out_shape=jax.ShapeDtypeStruct((n_layers * n_heads, 3, nk, nq), F32),
        compiler_params=_params(("arbitrary",)),
    )(toep.reshape(n_layers * n_heads, RPB_R, GRID_W, GRID_W)).reshape(n_layers, n_heads, 3, nk, nq)


def _na_bias_kernel(t_ref, o_ref):
    blank = jnp.full((GRID_W, GRID_W), NEG * LOG2E, F32)
    for case in range(3):
        for jj in range(NA_BAND):
            tiles = []
            for i in range(NA_QROWS):
                off = (0, i, NA_BAND - WIN_R)[case]
                rel = (jj - i, jj - i - WIN_R // 2, jj - i - NA_BAND + NA_QROWS)[case]
                tiles.append(t_ref[0, rel + WIN_R - 1] if off <= jj < off + WIN_R else blank)
            o_ref[0, case, jj * GRID_W:(jj + 1) * GRID_W, :] = jnp.concatenate(tiles, axis=1)


def _na(u, qn_g, kn_g, bias, n_ctx):
    b, _, t, _ = u.shape
    n_rows = (t - n_ctx) // GRID_W
    assert n_rows % (NA_QROWS * NA_GROUP) == 0 and n_rows >= NA_BAND and n_ctx == NA_QROWS * GRID_W
    pairs = NA_HEADS // 2
    nk, nq = NA_BAND * GRID_W, NA_QROWS * GRID_W
    bias = bias.reshape(pairs, 2, 3, nk, nq)
    qg = jnp.tile(qn_g, 2).reshape(1, LANES)
    kg = jnp.tile(kn_g, 2).reshape(1, LANES)
    kern = functools.partial(_na_kernel, n_ctx=n_ctx, n_rows=n_rows)
    col = lambda which: pl.BlockSpec((1, None, t, LANES), lambda hp, bi: (bi, which * pairs + hp, 0, 0))
    return pl.pallas_call(
        kern,
        grid=(pairs, b),
        in_specs=[
            col(0), col(1), col(2),
            pl.BlockSpec((1, LANES), lambda hp, bi: (0, 0)),
            pl.BlockSpec((1, LANES), lambda hp, bi: (0, 0)),
            pl.BlockSpec((1, 2, 3, nk, nq), lambda hp, bi: (hp, 0, 0, 0, 0)),
        ],
        out_specs=col(0),
        out_shape=jax.ShapeDtypeStruct((b, pairs, t, LANES), F32),
        scratch_shapes=[
            pltpu.VMEM((LANES, t), BF16),
            pltpu.VMEM((2, t, LANES), BF16),
            pltpu.VMEM((2, HEAD_DIM + NA_ONES_ROWS, t), BF16),
            pltpu.VMEM((2 * NA_GROUP, nk + n_ctx, nq), F32),
        ],
        compiler_params=_params(("arbitrary", "arbitrary")),
    )(u, u, u, qg, kg, bias)


def _lru_kernel(x_ref, g_ref, cw_ref, cb_ref, wa_ref, ba_ref, wx_ref, bx_ref, lam_ref, o_ref,
                xp_s, xc_s, a_s, b_s, h_s, *, n_tiles):
    tl = LRU_TILE
    halo = LRU_HALO
    t = x_ref.shape[1]
    width = x_ref.shape[2]
    group = 8

    xp_s[0:halo, :] = jnp.zeros((halo, width), F32)
    xp_s[halo + t:halo + t + halo, :] = jnp.zeros((halo, width), F32)

    def fill(i, carry):
        r0 = pl.multiple_of(i * tl, tl)
        xp_s[pl.ds(pl.multiple_of(r0 + halo, halo), tl), :] = x_ref[0, pl.ds(r0, tl), :]
        return carry

    lax.fori_loop(0, n_tiles, fill, 0)

    row = lax.broadcasted_iota(jnp.int32, (tl, 1), 0)
    rowg = row % group
    cw = cw_ref[...]
    cb = cb_ref[...]

    def conv_tile(i):
        r0 = pl.multiple_of(i * tl, tl)
        win = xp_s[pl.ds(r0, tl + 2 * halo), :]
        n_win = tl + 2 * halo
        xm2 = pltpu.roll(win, 2, 0)[halo:halo + tl]
        xm1 = pltpu.roll(win, 1, 0)[halo:halo + tl]
        x0 = win[halo:halo + tl]
        xp1 = pltpu.roll(win, n_win - 1, 0)[halo:halo + tl]
        lo = i <= 1
        hi = i == 0
        xm2 = jnp.where(lo & (row < 2), 0.0, xm2)
        xm1 = jnp.where(lo & (row < 1), 0.0, xm1)
        xp1 = jnp.where(hi & (row == tl - 1), 0.0, xp1)
        return xm2 * cw[0:1] + xm1 * cw[1:2] + x0 * cw[2:3] + xp1 * cw[3:4] + cb

    def gates(i, d):
        r0 = pl.multiple_of(i * tl, tl)
        if d == 0:
            xc = conv_tile(i)
            xc_s[pl.ds(r0, tl), :] = xc
        else:
            xc = xc_s[pl.ds(r0, tl), :]
        xb = xc.astype(BF16)
        r = _sigmoid(_dot(xb, wa_ref[d]) + ba_ref[d:d + 1, :])
        ig = _sigmoid(_dot(xb, wx_ref[d]) + bx_ref[d:d + 1, :])
        log_a = -LRU_C * r * _softplus(-lam_ref[d:d + 1, :])
        a = jnp.exp(log_a)
        bt = jnp.sqrt(-jnp.tanh(log_a) * (a * a + 1.0)) * (ig * xc)
        return a, bt

    def scan_tile(i, d, carry):
        a, bt = gates(i, d)

        def shift(x, kk):
            x3 = x.reshape(tl // group, group, width)
            return pltpu.roll(x3, kk if d == 0 else group - kk, 1).reshape(tl, width)

        for kk in (1, 2, 4):
            mk = rowg >= kk if d == 0 else rowg < group - kk
            bt = jnp.where(mk, a * shift(bt, kk) + bt, bt)
            a = jnp.where(mk, a * shift(a, kk), a)
        a_s[...] = a
        b_s[...] = bt
        n_groups = tl // group

        def grp(gidx, c):
            gg = gidx if d == 0 else n_groups - 1 - gidx
            g0 = pl.multiple_of(gg * group, group)
            h = b_s[pl.ds(g0, group), :] + a_s[pl.ds(g0, group), :] * c
            h_s[pl.ds(g0, group), :] = h
            return h[group - 1:group, :] if d == 0 else h[0:1, :]

        return lax.fori_loop(0, n_groups, grp, carry, unroll=8)

    def fwd(i, carry):
        carry = scan_tile(i, 0, carry)
        r0 = pl.multiple_of(i * tl, tl)
        o_ref[0, pl.ds(r0, tl), :] = h_s[...]
        return carry

    def bwd(i, carry):
        carry = scan_tile(i, 1, carry)
        r0 = pl.multiple_of(i * tl, tl)
        o_ref[0, pl.ds(r0, tl), :] = (o_ref[0, pl.ds(r0, tl), :] + h_s[...]) * jax.nn.gelu(g_ref[0, pl.ds(r0, tl), :])
        return carry

    h0 = jnp.zeros((1, width), F32)
    lax.fori_loop(0, n_tiles, fwd, h0)
    carry = bwd(jnp.int32(0), h0)
    lax.fori_loop(1, n_tiles, lambda j, c: bwd(n_tiles - j, c), carry)


def _block_diag_dense(w):
    nd, k, c, _ = w.shape
    eye = jnp.eye(k, dtype=w.dtype)
    return jnp.einsum('dkce,kj->dkcje', w, eye).reshape(nd, k * c, k * c)


def _lru(u, conv_w, conv_b, w_a, b_a, w_x, b_x, lam, n_ctx):
    b, _, t, _ = u.shape
    width = conv_w.shape[1]
    assert n_ctx == LRU_TILE and t % LRU_TILE == 0
    n_tiles = t // LRU_TILE
    wa = _block_diag_dense(w_a).astype(BF16)
    wx = _block_diag_dense(w_x).astype(BF16)
    kern = functools.partial(_lru_kernel, n_tiles=n_tiles)
    full = lambda shape: pl.BlockSpec(shape, lambda bi: (0,) * len(shape))
    return pl.pallas_call(
        kern,
        grid=(b,),
        in_specs=[
            pl.BlockSpec((1, None, t, width), lambda bi: (bi, 0, 0, 0)),
            pl.BlockSpec((1, None, t, width), lambda bi: (bi, 1, 0, 0)),
            full((CONV_W, width)),
            full((1, width)),
            full((2, width, width)),
            full((2, width)),
            full((2, width, width)),
            full((2, width)),
            full((2, width)),
        ],
        out_specs=pl.BlockSpec((1, t, width), lambda bi: (bi, 0, 0)),
        out_shape=jax.ShapeDtypeStruct((b, t, width), F32),
        scratch_shapes=[
            pltpu.VMEM((t + 2 * LRU_HALO, width), F32),
            pltpu.VMEM((t, width), F32),
            pltpu.VMEM((LRU_TILE, width), F32),
            pltpu.VMEM((LRU_TILE, width), F32),
            pltpu.VMEM((LRU_TILE, width), F32),
        ],
        compiler_params=_params(("arbitrary",)),
    )(u, u, conv_w, conv_b.reshape(1, width), wa, b_a, wx, b_x, lam)


def _out_mlp_kernel(xc_ref, xl_ref, ym_ref, yn_ref, yl_ref, mod_ref, g_ref, wo_ref, w1_ref, w2_ref, oc_ref, ol_ref,
                    *, hid_tile):
    def body(x_ref, o_ref):
        mod = mod_ref[0, 0]
        att = _dot(ym_ref[0].astype(BF16), wo_ref[0:M_WIDTH, :])
        yn = jnp.concatenate([yn_ref[0, j] for j in range(yn_ref.shape[1])], axis=1)
        att = att + _dot(yn.astype(BF16), wo_ref[M_WIDTH:M_WIDTH + NA_WIDTH, :])
        att = att + _dot(yl_ref[0].astype(BF16), wo_ref[M_WIDTH + NA_WIDTH:, :])
        x1 = x_ref[0] + mod[2:3, :] * att
        ms = jnp.mean(x1 * x1, axis=-1, keepdims=True)
        y = x1 * lax.rsqrt(ms + EPS) * g_ref[...]
        h = (y * (1.0 + mod[4:5, :]) + mod[3:4, :]).astype(BF16)
        hidden = w1_ref.shape[1]
        acc = None
        for j in range(hidden // hid_tile):
            a = jnp.maximum(_dot(h, w1_ref[:, j * hid_tile:(j + 1) * hid_tile]), 0.0)
            part = _dot((a * a).astype(BF16), w2_ref[j * hid_tile:(j + 1) * hid_tile, :])
            acc = part if acc is None else acc + part
        o_ref[0] = x1 + mod[5:6, :] * acc

    _on_stream_tile(body, xc_ref, xl_ref, (oc_ref, ol_ref))


def _out_mlp(xc, xl, ym, yn, yl, modv, g2, w_out, w1, w2):
    b, n_ctx, d = xc.shape
    t = n_ctx + xl.shape[1]
    hidden = w1.shape[1]
    tok = lambda w: pl.BlockSpec((1, ROW_TILE, w), lambda bi, i: (bi, i, 0))
    kern = functools.partial(_out_mlp_kernel, hid_tile=1024)
    return pl.pallas_call(
        kern,
        grid=(b, t // ROW_TILE),
        in_specs=_stream_specs(n_ctx, d) + [
            tok(M_WIDTH),
            pl.BlockSpec((1, NA_HEADS // 2, ROW_TILE, LANES), lambda bi, i: (bi, 0, i, 0)),
            tok(d - M_WIDTH - NA_WIDTH),
            pl.BlockSpec((1, 1, 6, d), lambda bi, i: (bi, jnp.minimum(i, 1), 0, 0)),
            pl.BlockSpec((1, d), lambda bi, i: (0, 0)),
            _resident((d, d), lambda bi, i: (0, 0)),
            _resident((d, hidden), lambda bi, i: (0, 0)),
            _resident((hidden, d), lambda bi, i: (0, 0)),
        ],
        out_specs=_stream_specs(n_ctx, d),
        out_shape=[jax.ShapeDtypeStruct(xc.shape, F32), jax.ShapeDtypeStruct(xl.shape, F32)],
        compiler_params=_params(("arbitrary", "arbitrary")),
    )(xc, xl, ym, yn, yl, modv, g2.reshape(1, d), w_out, w1, w2)


def _split_w_in(w_in):
    m_end = 4 * M_WIDTH
    na_end = m_end + N_GATES + 3 * NA_WIDTH
    main = jnp.concatenate([w_in[..., :m_end], w_in[..., na_end:], w_in[..., m_end + N_GATES:na_end]], axis=-1)
    gates = jnp.swapaxes(w_in[..., m_end:m_end + N_GATES], -1, -2)
    return main.astype(BF16), gates.astype(BF16)


def kernel(x, c, ctx, c_ctx, w_mod, b_mod, norm1_g, norm2_g, w_in, mlstm_gate_b, mlstm_norm_g, na_q_norm_g, na_k_norm_g, na_rpb, lru_conv_w, lru_conv_b, lru_w_a, lru_b_a, lru_w_x, lru_b_x, lru_lambda, w_out, w_mlp1, w_mlp2):
    b, n, d = x.shape
    n_ctx = ctx.shape[1]
    depth = w_in.shape[0]
    assert n_ctx == ROW_TILE and n % ROW_TILE == 0 and n % GRID_W == 0

    cond_rows = 8 * ((b + 1 + 7) // 8)
    cond = jnp.zeros((cond_rows, d), F32).at[:b].set(c).at[b].set(c_ctx)
    mod_all = _modulation(cond, w_mod, b_mod)
    rope_row, rope_col = _rope_tables(n // GRID_W)
    w_in_p, w_gate_t = _split_w_in(w_in)
    na_bias = _na_bias_tables(na_rpb)
    w_out_b, w1_b, w2_b = w_out.astype(BF16), w_mlp1.astype(BF16), w_mlp2.astype(BF16)

    xc, xl = ctx, x
    for l in range(depth):
        lat = mod_all[l, :b].reshape(b, 6, d)
        cx = jnp.broadcast_to(mod_all[l, b].reshape(1, 6, d), (b, 6, d))
        modv = jnp.stack([cx, lat], axis=1)
        um, ul, un, g_rows = _inproj(xc, xl, modv, norm1_g[l], w_in_p[l], w_gate_t[l])
        ym = _mlstm(um, g_rows, mlstm_gate_b[l], mlstm_norm_g[l], rope_row, rope_col, n_ctx)
        yn = _na(un, na_q_norm_g[l], na_k_norm_g[l], na_bias[l], n_ctx)
        yl = _lru(ul, lru_conv_w[l], lru_conv_b[l], lru_w_a[l], lru_b_a[l], lru_w_x[l], lru_b_x[l], lru_lambda[l], n_ctx)
        xc, xl = _out_mlp(xc, xl, ym, yn, yl, modv, norm2_g[l], w_out_b[l], w1_b[l], w2_b[l])
    return xl
```

```python
import functools
import math

import numpy as np
import jax
import jax.numpy as jnp
from jax import lax
from jax.experimental import pallas as pl
from jax.experimental.pallas import tpu as pltpu

F32 = jnp.float32
BF16 = jnp.bfloat16

HEAD_DIM = 64
M_HEADS = 4
M_WIDTH = M_HEADS * HEAD_DIM
NA_HEADS = 8
NA_WIDTH = NA_HEADS * HEAD_DIM
N_GATES = 4 * M_HEADS
GRID_W = 64
M_CHUNK = 128
M_AUG = HEAD_DIM + 16
M_LOCAL_GROUP = 2
M_LATENT_GROUP = 4
WIN_R = 8
WIN_C = 16
RPB_R = 2 * WIN_R - 1
RPB_C = 2 * WIN_C - 1
NA_QROWS = 4
NA_BAND = NA_QROWS + WIN_R
NA_KEY_CHUNK = 128
NA_ONES_ROWS = 16
NA_GROUP = 8
LOG2E = 1.4426950408889634
CONV_W = 4
LRU_C = 8.0
ROPE_BASE = 10000.0
EPS = 1e-6
NEG = -1e30

LANES = 128
ROW_TILE = 256
LRU_TILE = 256
LRU_HALO = 8
VMEM_LIMIT = 56 * 1024 * 1024

U_MQ, U_MK, U_MV, U_MO = 0, 256, 512, 768
U_XR, U_GT = 1024, 1280
U_NQ, U_NK, U_NV = 1536, 2048, 2560
U_COLS = 3072
LRU_WIDTH = U_NQ - U_GT
NA_COL_BLOCKS = (U_COLS - U_NQ) // LANES


def _dot(a, b):
    return jnp.dot(a, b, preferred_element_type=F32)


def _dot_f32_rhs01(x, m01):
    hi = x.astype(BF16)
    lo = (x - hi.astype(F32)).astype(BF16)
    return _dot(hi, m01) + _dot(lo, m01)


def _sigmoid(x):
    return 0.5 * jnp.tanh(0.5 * x) + 0.5


def _log_sigmoid(x):
    return jnp.minimum(x, 0.0) - jnp.log1p(jnp.exp(-jnp.abs(x)))


def _softplus(x):
    return jnp.maximum(x, 0.0) + jnp.log1p(jnp.exp(-jnp.abs(x)))


def _params(sem):
    return pltpu.CompilerParams(dimension_semantics=sem, vmem_limit_bytes=VMEM_LIMIT)


def _resident(shape, index_map):
    return pl.BlockSpec(shape, index_map, pipeline_mode=pl.Buffered(1))


def _mod_kernel(a_ref, w_ref, b_ref, o_ref):
    a = a_ref[...]
    act = (a * jax.nn.sigmoid(a)).astype(BF16)
    o_ref[0] = _dot(act, w_ref[0].astype(BF16)) + b_ref[0]


def _modulation(cond, w_mod, b_mod):
    depth, d, n6 = w_mod.shape
    rows = cond.shape[0]
    tn = 1536
    return pl.pallas_call(
        _mod_kernel,
        grid=(depth, n6 // tn),
        in_specs=[
            pl.BlockSpec((rows, d), lambda l, j: (0, 0)),
            pl.BlockSpec((1, d, tn), lambda l, j: (l, 0, j)),
            pl.BlockSpec((1, 1, tn), lambda l, j: (l, 0, j)),
        ],
        out_specs=pl.BlockSpec((1, rows, tn), lambda l, j: (l, 0, j)),
        out_shape=jax.ShapeDtypeStruct((depth, rows, n6), F32),
        compiler_params=_params(("arbitrary", "arbitrary")),
    )(cond, w_mod, b_mod.reshape(depth, 1, n6))


def _stream_specs(n_ctx, d):
    assert n_ctx == ROW_TILE
    return [
        pl.BlockSpec((1, ROW_TILE, d), lambda bi, i: (bi, 0, 0)),
        pl.BlockSpec((1, ROW_TILE, d), lambda bi, i: (bi, jnp.maximum(i - 1, 0), 0)),
    ]


def _on_stream_tile(body, xc_ref, xl_ref, *outs):
    @pl.when(pl.program_id(1) == 0)
    def _():
        body(xc_ref, *[o[0] for o in outs])

    @pl.when(pl.program_id(1) > 0)
    def _():
        body(xl_ref, *[o[1] for o in outs])


def _inproj_kernel(xc_ref, xl_ref, mod_ref, g_ref, w_ref, wg_ref, um_ref, ul_ref, un_ref, gt_ref):
    def body(x_ref):
        x = x_ref[0]
        ms = jnp.mean(x * x, axis=-1, keepdims=True)
        y = x * lax.rsqrt(ms + EPS) * g_ref[...]
        mod = mod_ref[0, 0]
        h = (y * (1.0 + mod[1:2, :]) + mod[0:1, :]).astype(BF16)
        u = _dot(h, w_ref[...])
        um_ref[0] = u[:, 0:U_XR]
        for j in range(ul_ref.shape[1]):
            ul_ref[0, j] = u[:, U_XR + j * LRU_WIDTH:U_XR + (j + 1) * LRU_WIDTH]
        for j in range(un_ref.shape[1]):
            un_ref[0, j] = u[:, U_NQ + j * LANES:U_NQ + (j + 1) * LANES]
        gt_ref[0] = lax.dot_general(wg_ref[...], h, (((1,), (1,)), ((), ())), preferred_element_type=F32)

    _on_stream_tile(body, xc_ref, xl_ref)


def _inproj(xc, xl, modv, g1, w_in_p, w_gate_t, layer):
    b, n_ctx, d = xc.shape
    t = n_ctx + xl.shape[1]
    return pl.pallas_call(
        _inproj_kernel,
        grid=(b, t // ROW_TILE),
        in_specs=_stream_specs(n_ctx, d) + [
            pl.BlockSpec((1, 1, 6, d), lambda bi, i: (bi, jnp.minimum(i, 1), 0, 0)),
            pl.BlockSpec((1, d), lambda bi, i: (0, 0)),
            _resident((None, d, U_COLS), lambda bi, i: (layer, 0, 0)),
            _resident((None, N_GATES, d), lambda bi, i: (layer, 0, 0)),
        ],
        out_specs=[
            pl.BlockSpec((1, ROW_TILE, U_XR), lambda bi, i: (bi, i, 0)),
            pl.BlockSpec((1, 2, ROW_TILE, LRU_WIDTH), lambda bi, i: (bi, 0, i, 0)),
            pl.BlockSpec((1, NA_COL_BLOCKS, ROW_TILE, LANES), lambda bi, i: (bi, 0, i, 0)),
            pl.BlockSpec((1, N_GATES, ROW_TILE), lambda bi, i: (bi, 0, i)),
        ],
        out_shape=[
            jax.ShapeDtypeStruct((b, t, U_XR), F32),
            jax.ShapeDtypeStruct((b, 2, t, LRU_WIDTH), F32),
            jax.ShapeDtypeStruct((b, NA_COL_BLOCKS, t, LANES), F32),
            jax.ShapeDtypeStruct((b, N_GATES, t), F32),
        ],
        compiler_params=_params(("arbitrary", "arbitrary")),
    )(xc, xl, modv, g1.reshape(1, d), w_in_p, w_gate_t)


def _mlstm_kernel(u_ref, gr_ref, gb_ref, ng_ref, rrow_ref, rcol_ref, o_ref,
                  qt_s, k_s, vt_s, r_s, row_s, c_s, *, n_ctx_chunks, n_chunks):
    L = M_CHUNK
    W = M_WIDTH
    D = HEAD_DIM
    groups = L // GRID_W

    for h in range(M_HEADS):
        vt_s[h, D:, :] = jnp.ones((M_AUG - D, vt_s.shape[2]), BF16)
    c_s[...] = jnp.zeros(c_s.shape, F32)

    def prep(cs, latent):
        r0s = [pl.multiple_of(c * L, L) for c in cs]
        qs = [u_ref[0, pl.ds(r0, L), U_MQ:U_MQ + W] for r0 in r0s]
        ks = [u_ref[0, pl.ds(r0, L), U_MK:U_MK + W] * D ** -0.5 for r0 in r0s]
        vts = [u_ref[0, pl.ds(r0, L), U_MV:U_MV + W].T for r0 in r0s]
        if latent:
            def table(c, i):
                first = (c - n_ctx_chunks) * groups
                return jnp.concatenate([rrow_ref[i, pl.ds(first + g, 1), :] + rcol_ref[i] for g in range(groups)], axis=0)

            def rope(x, c):
                return (x * table(c, 0) + pltpu.roll(x, W - D // 4, 1) * table(c, 1)
                        + pltpu.roll(x, D // 4, 1) * table(c, 2))

            qs = [rope(q, c) for q, c in zip(qs, cs)]
            ks = [rope(k, c) for k, c in zip(ks, cs)]
        qts = [q.T for q in qs]
        for r0, qt, k, vt in zip(r0s, qts, ks, vts):
            qt_s[:, pl.ds(r0, L)] = qt.astype(BF16)
            k_s[pl.ds(r0, L), :] = k.astype(BF16)
            for h in range(M_HEADS):
                vt_s[h, 0:D, pl.ds(r0, L)] = vt[h * D:(h + 1) * D].astype(BF16)

    assert n_ctx_chunks % M_LOCAL_GROUP == 0 and n_chunks % M_LOCAL_GROUP == 0

    def prep_group(latent):
        def body(j, carry):
            first = j * M_LOCAL_GROUP + (n_ctx_chunks if latent else 0)
            prep([first + i for i in range(M_LOCAL_GROUP)], latent)
            return carry
        return body

    lax.fori_loop(0, n_ctx_chunks // M_LOCAL_GROUP, prep_group(False), 0)
    lax.fori_loop(0, (n_chunks - n_ctx_chunks) // M_LOCAL_GROUP, prep_group(True), 0)

    si = lax.broadcasted_iota(jnp.int32, (L, L), 0)
    li = lax.broadcasted_iota(jnp.int32, (L, L), 1)
    keeps = [si <= li, si >= li]
    gbias = [jnp.broadcast_to(gb_ref[8 * d:8 * d + 8, :], (8, L)) for d in range(2)]
    zrows = jnp.zeros((D, L), BF16)

    def qt_head(qtc, h):
        parts = [zrows] * M_HEADS
        parts[h] = qtc[h * D:(h + 1) * D]
        return jnp.concatenate(parts, axis=0)

    def local(first, n_group):
        r0s = [pl.multiple_of((first + j) * L, L) for j in range(n_group)]
        scores = [[_dot(k_s[pl.ds(r0, L), :], qt_head(qt_s[:, pl.ds(r0, L)], h)) for h in range(M_HEADS)]
                  for r0 in r0s]
        vtas = [[vt_s[h, :, pl.ds(r0, L)] for h in range(M_HEADS)] for r0 in r0s]
        g8s = [[gr_ref[0, 8 * d:8 * d + 8, pl.ds(r0, L)] + gbias[d] for d in range(2)] for r0 in r0s]
        cum4s = [[pltpu.roll(_dot_f32_rhs01(_log_sigmoid(g8[d]), keeps[d].astype(BF16)), M_HEADS, 0)
                  for d in range(2)] for g8 in g8s]
        src_ts = [[(g8[d] - cum4[d]).T for d in range(2)] for g8, cum4 in zip(g8s, cum4s)]
        jdh = [(j, d, h) for j in range(n_group) for d in range(2) for h in range(M_HEADS)]
        logws = [jnp.where(keeps[d], cum4s[j][d][h:h + 1] + jnp.broadcast_to(src_ts[j][d][:, h:h + 1], (L, L)), NEG)
                 for j, d, h in jdh]
        amaxs = [lw.max(axis=0, keepdims=True) for lw in logws]
        ps = [(scores[j][h] * jnp.exp(logw - a)).astype(BF16) for (j, d, h), logw, a in zip(jdh, logws, amaxs)]
        intras = [_dot(vtas[j][h], p) for (j, d, h), p in zip(jdh, ps)]
        for (j, d, h), a, intra in zip(jdh, amaxs, intras):
            last = L - 1 if d == 0 else 0
            cl = cum4s[j][d][h:h + 1]
            a_last = jnp.broadcast_to(a[:, last:last + 1], (1, L))
            cl_last = jnp.broadcast_to(cl[:, last:last + 1], (1, L))
            w_src = jnp.exp(cl_last - cl + g8s[j][d][h:h + 1] - a_last)
            r_s[d, h * M_AUG:(h + 1) * M_AUG, pl.ds(r0s[j], L)] = intra
            for kind, row in enumerate((cl, a, cl_last, a_last, w_src)):
                row_s[d, kind, h:h + 1, pl.ds(r0s[j], L)] = row

    def local_loop(first, count, n_group):
        assert count % n_group == 0

        def body(i, carry):
            local(first + i * n_group, n_group)
            return carry

        lax.fori_loop(0, count // n_group, body, 0)

    local_loop(0, n_ctx_chunks, M_LOCAL_GROUP)
    local_loop(n_ctx_chunks, n_chunks - n_ctx_chunks, M_LATENT_GROUP)

    def step(c, d, c_olds, m_prevs):
        r0 = pl.multiple_of(c * L, L)
        kc = k_s[pl.ds(r0, L), :]
        qtc = qt_s[:, pl.ds(r0, L)]
        h_ts, c_news, m_news = [], [], []
        for h in range(M_HEADS):
            cl, a, cl_last, a_last, w_src = [row_s[d, kind, h:h + 1, pl.ds(r0, L)] for kind in range(5)]
            m_prev = m_prevs[h]
            inter = cl + m_prev
            m_t = jnp.maximum(inter, a)
            r = (jnp.exp(inter - m_t) * _dot(c_olds[h].astype(BF16), qt_head(qtc, h))
                 + jnp.exp(a - m_t) * r_s[d, h * M_AUG:(h + 1) * M_AUG, pl.ds(r0, L)])
            denom = jnp.maximum(jnp.abs(r[D:D + 1]), jnp.exp(-m_t))
            h_ts.append(r[0:D] * (1.0 / denom))
            m_new = jnp.maximum(cl_last + m_prev, a_last)
            w_old = jnp.exp(cl_last + m_prev - m_new)
            w_upd = jnp.exp(a_last - m_new)
            vw = (vt_s[h, :, pl.ds(r0, L)].astype(F32) * w_src).astype(BF16)
            scale = lambda w: jnp.concatenate([w] * (W // L), axis=1)
            c_news.append(scale(w_old) * c_olds[h] + scale(w_upd) * _dot(vw, kc))
            m_news.append(m_new)
        return r0, h_ts, c_news, m_news

    def body(i, m_all):
        chunks = [i, jnp.where(i < n_ctx_chunks, n_ctx_chunks - 1 - i, n_chunks - 1 + n_ctx_chunks - i)]
        c_olds = [[c_s[d, h] for h in range(M_HEADS)] for d in range(2)]
        outs = [step(chunks[d], d, c_olds[d], m_all[d]) for d in range(2)]
        for d in range(2):
            r0, h_ts, c_news, _ = outs[d]
            for h in range(M_HEADS):
                r_s[d, h * M_AUG:h * M_AUG + D, pl.ds(r0, L)] = h_ts[h]
                c_s[d, h] = c_news[h]
        return tuple(tuple(outs[d][3]) for d in range(2))

    m0 = tuple(tuple(jnp.zeros((1, L), F32) for _ in range(M_HEADS)) for _ in range(2))
    lax.fori_loop(0, n_chunks, body, m0)

    ng = jnp.broadcast_to(ng_ref[...], (W, L))

    def post(j, carry):
        r0s = [pl.multiple_of((j * M_LOCAL_GROUP + i) * L, L) for i in range(M_LOCAL_GROUP)]
        hts = [[r_s[0, h * M_AUG:h * M_AUG + D, pl.ds(r0, L)] + r_s[1, h * M_AUG:h * M_AUG + D, pl.ds(r0, L)]
                for h in range(M_HEADS)] for r0 in r0s]
        scales = [[lax.rsqrt(jnp.sum(ht * ht, axis=0, keepdims=True) * (1.0 / D) + EPS) for ht in hs] for hs in hts]
        hnts = [(jnp.concatenate([ht * sc for ht, sc in zip(hs, scs)], axis=0) * ng).T for hs, scs in zip(hts, scales)]
        for r0, hnt in zip(r0s, hnts):
            o_ref[0, pl.ds(r0, L), :] = (_sigmoid(u_ref[0, pl.ds(r0, L), U_MO:U_MO + W]) * hnt).astype(o_ref.dtype)
        return carry

    lax.fori_loop(0, n_chunks // M_LOCAL_GROUP, post, 0)


def _rope_tables(n_rows):
    nf = HEAD_DIM // 4
    inv = ROPE_BASE ** (-jnp.arange(nf, dtype=F32) / nf)
    d = np.arange(M_WIDTH) % HEAD_DIM
    f_idx = d % nf
    is_row = d < HEAD_DIM // 2
    first_quarter = (d % (HEAD_DIM // 2)) < nf

    def tables(pos, sel):
        ang = pos[:, None] * inv[None, :]
        ang = ang[:, f_idx]
        cos = jnp.where(sel, jnp.cos(ang), 0.0)
        sin = jnp.where(sel, jnp.sin(ang), 0.0)
        sin_a = jnp.where(first_quarter, -sin, 0.0)
        sin_b = jnp.where(first_quarter, 0.0, sin)
        return jnp.stack([cos, sin_a, sin_b]).astype(F32)

    row_t = tables(jnp.arange(n_rows, dtype=F32), is_row)
    col_t = tables(jnp.arange(GRID_W, dtype=F32), ~is_row)
    return row_t, col_t


def _mlstm(u, g_rows, gate_b, norm_g, rope_row, rope_col, n_ctx):
    b, t, _ = u.shape
    assert t % M_CHUNK == 0 and n_ctx % M_CHUNK == 0 and M_CHUNK % GRID_W == 0
    n_chunks = t // M_CHUNK
    kern = functools.partial(_mlstm_kernel, n_ctx_chunks=n_ctx // M_CHUNK, n_chunks=n_chunks)
    n_rows = rope_row.shape[1]
    return pl.pallas_call(
        kern,
        grid=(b,),
        in_specs=[
            pl.BlockSpec((1, t, 4 * M_WIDTH), lambda bi: (bi, 0, 0), pipeline_mode=pl.Buffered(1)),
            pl.BlockSpec((1, N_GATES, t), lambda bi: (bi, 0, 0)),
            pl.BlockSpec((N_GATES, 1), lambda bi: (0, 0)),
            pl.BlockSpec((M_WIDTH, 1), lambda bi: (0, 0)),
            pl.BlockSpec((3, n_rows, M_WIDTH), lambda bi: (0, 0, 0)),
            pl.BlockSpec((3, GRID_W, M_WIDTH), lambda bi: (0, 0, 0)),
        ],
        out_specs=pl.BlockSpec((1, t, M_WIDTH), lambda bi: (bi, 0, 0)),
        out_shape=jax.ShapeDtypeStruct((b, t, M_WIDTH), BF16),
        scratch_shapes=[
            pltpu.VMEM((M_WIDTH, t), BF16),
            pltpu.VMEM((t, M_WIDTH), BF16),
            pltpu.VMEM((M_HEADS, M_AUG, t), BF16),
            pltpu.VMEM((2, M_HEADS * M_AUG, t), F32),
            pltpu.VMEM((2, 5, 8, t), F32),
            pltpu.VMEM((2, M_HEADS, M_AUG, M_WIDTH), F32),
        ],
        compiler_params=_params(("arbitrary",)),
    )(u, g_rows, gate_b.reshape(N_GATES, 1), norm_g.reshape(M_WIDTH, 1), rope_row, rope_col)


def _na_kernel(q_ref, k_ref, v_ref, qg_ref, kg_ref, bias_ref, o_ref, qt_s, kz_s, vt_s, s_s, *, n_ctx, n_rows):
    t = q_ref.shape[1]
    lane = lax.broadcasted_iota(jnp.int32, (1, LANES), 1)
    half = [lane < HEAD_DIM, lane >= HEAD_DIM]
    hi = lax.broadcasted_iota(jnp.int32, (LANES, LANES), 0) // HEAD_DIM
    hj = lax.broadcasted_iota(jnp.int32, (LANES, LANES), 1) // HEAD_DIM
    same_head = (hi == hj).astype(BF16)
    qg = qg_ref[...] * (HEAD_DIM ** -0.5 * LOG2E)
    kg = kg_ref[...]
    qblk = NA_QROWS * GRID_W
    kc = NA_KEY_CHUNK
    sub = 8

    for hh in range(2):
        vt_s[hh, HEAD_DIM:, :] = jnp.ones((vt_s.shape[1] - HEAD_DIM, t), BF16)

    def prep(r0s):
        qs = [q_ref[0, pl.ds(r0, ROW_TILE), :] for r0 in r0s]
        ks = [k_ref[0, pl.ds(r0, ROW_TILE), :] for r0 in r0s]
        vts = [v_ref[0, pl.ds(r0, ROW_TILE), :].T for r0 in r0s]
        qms = [_dot_f32_rhs01(q * q, same_head) * (1.0 / HEAD_DIM) for q in qs]
        kms = [_dot_f32_rhs01(k * k, same_head) * (1.0 / HEAD_DIM) for k in ks]
        qnts = [(q * lax.rsqrt(m + EPS) * qg).T for q, m in zip(qs, qms)]
        kns = [k * lax.rsqrt(m + EPS) * kg for k, m in zip(ks, kms)]
        for r0, qnt, kn, vt in zip(r0s, qnts, kns, vts):
            qt_s[:, pl.ds(r0, ROW_TILE)] = qnt.astype(BF16)
            for hh in range(2):
                kz_s[hh, pl.ds(r0, ROW_TILE), :] = jnp.where(half[hh], kn, 0.0).astype(BF16)
                vt_s[hh, 0:HEAD_DIM, pl.ds(r0, ROW_TILE)] = vt[hh * HEAD_DIM:(hh + 1) * HEAD_DIM].astype(BF16)

    n_tiles = t // ROW_TILE

    def prep_pair(j, carry):
        prep([pl.multiple_of((2 * j + i) * ROW_TILE, ROW_TILE) for i in range(2)])
        return carry

    lax.fori_loop(0, n_tiles // 2, prep_pair, 0)
    if n_tiles % 2:
        prep([(n_tiles - 1) * ROW_TILE])

    def attend(blocks):
        jobs = [(b, hh) for b in range(len(blocks)) for hh in range(2)]
        qts = [qt_s[:, pl.ds(q0, qblk)] for q0, _ in blocks]
        n_chunks = len(blocks[0][1])
        mx = [None] * len(jobs)
        for ci in range(n_chunks):
            for ji, (b, hh) in enumerate(jobs):
                k0, tab = blocks[b][1][ci]
                s = _dot(kz_s[hh, pl.ds(k0, kc), :], qts[b])
                if tab is not None:
                    s = s + bias_ref[0, hh, tab[0], tab[1]:tab[1] + kc, :]
                s_s[ji, ci * kc:(ci + 1) * kc, :] = s
                cm = functools.reduce(jnp.maximum, [s[i * sub:(i + 1) * sub] for i in range(kc // sub)])
                mx[ji] = cm if mx[ji] is None else jnp.maximum(mx[ji], cm)
        ms = [m.max(axis=0, keepdims=True) for m in mx]
        accs = [None] * len(jobs)
        for ci in range(n_chunks):
            for ji, (b, hh) in enumerate(jobs):
                k0, _ = blocks[b][1][ci]
                p = jnp.exp2(s_s[ji, ci * kc:(ci + 1) * kc, :] - ms[ji]).astype(BF16)
                pv = _dot(vt_s[hh, :, pl.ds(k0, kc)], p)
                accs[ji] = pv if accs[ji] is None else accs[ji] + pv
        outs = []
        for b in range(len(blocks)):
            heads = [accs[2 * b + hh][0:HEAD_DIM] / accs[2 * b + hh][HEAD_DIM:HEAD_DIM + 1] for hh in range(2)]
            outs.append(jnp.concatenate(heads, axis=0).T)
        return outs

    ctx_chunks = [(ci * kc, None) for ci in range(n_ctx // kc)]
    o_ref[0, 0:n_ctx, :] = attend([(0, ctx_chunks)])[0].astype(o_ref.dtype)

    n_blocks = n_rows // NA_QROWS

    def group(gi, carry):
        blocks = []
        for j in range(NA_GROUP):
            bi = gi * NA_GROUP + j
            r0 = bi * NA_QROWS
            sb = jnp.clip(r0 - WIN_R // 2, 0, n_rows - NA_BAND)
            case = jnp.where(bi == 0, 0, jnp.where(bi == n_blocks - 1, 2, 1))
            q0 = pl.multiple_of(n_ctx + r0 * GRID_W, qblk)
            k0 = pl.multiple_of(n_ctx + sb * GRID_W, qblk)
            local = [(pl.multiple_of(k0 + ci * kc, kc), (case, ci * kc)) for ci in range(NA_BAND * GRID_W // kc)]
            blocks.append((q0, local + ctx_chunks))
        for (q0, _), out in zip(blocks, attend(blocks)):
            o_ref[0, pl.ds(q0, qblk), :] = out.astype(o_ref.dtype)
        return carry

    lax.fori_loop(0, n_blocks // NA_GROUP, group, 0)


def _na_bias_tables(rpb):
    n_layers, n_heads = rpb.shape[:2]
    cols = np.arange(GRID_W)
    col0 = np.clip(cols - WIN_C // 2, 0, GRID_W - WIN_C)
    valid = (cols[:, None] >= col0[None, :]) & (cols[:, None] < col0[None, :] + WIN_C)
    dc = cols[:, None] - cols[None, :] + WIN_C - 1
    col_sel = ((dc[..., None] == np.arange(RPB_C)) & valid[..., None]).astype(np.float32)
    toep = jnp.einsum('lhrd,xcd->lhrxc', rpb, col_sel, precision=lax.Precision.HIGHEST)
    toep = jnp.where(valid, toep * LOG2E, NEG * LOG2E).astype(F32)
    nk, nq = NA_BAND * GRID_W, NA_QROWS * GRID_W
    return pl.pallas_call(
        _na_bias_kernel,
        grid=(n_layers * n_heads,),
        in_specs=[pl.BlockSpec((1, RPB_R, GRID_W, GRID_W), lambda i: (i, 0, 0, 0))],
        out_specs=pl.BlockSpec((1, 3, nk, nq), lambda i: (i, 0, 0, 0)),
        out_shape=jax.ShapeDtypeStruct((n_layers * n_heads, 3, nk, nq), F32),
        compiler_params=_params(("arbitrary",)),
    )(toep.reshape(n_layers * n_heads, RPB_R, GRID_W, GRID_W)).reshape(n_layers, n_heads, 3, nk, nq)


def _na_bias_kernel(t_ref, o_ref):
    blank = jnp.full((GRID_W, GRID_W), NEG * LOG2E, F32)
    for case in range(3):
        for jj in range(NA_BAND):
            tiles = []
            for i in range(NA_QROWS):
                off = (0, i, NA_BAND - WIN_R)[case]
                rel = (jj - i, jj - i - WIN_R // 2, jj - i - NA_BAND + NA_QROWS)[case]
                tiles.append(t_ref[0, rel + WIN_R - 1] if off <= jj < off + WIN_R else blank)
            o_ref[0, case, jj * GRID_W:(jj + 1) * GRID_W, :] = jnp.concatenate(tiles, axis=1)


def _na(u, qn_g, kn_g, bias, layer, n_ctx):
    b, _, t, _ = u.shape
    n_rows = (t - n_ctx) // GRID_W
    assert n_rows % (NA_QROWS * NA_GROUP) == 0 and n_rows >= NA_BAND and n_ctx == NA_QROWS * GRID_W
    pairs = NA_HEADS // 2
    nk, nq = NA_BAND * GRID_W, NA_QROWS * GRID_W
    bias = bias.reshape(bias.shape[0], pairs, 2, 3, nk, nq)
    qg = jnp.tile(qn_g, 2).reshape(1, LANES)
    kg = jnp.tile(kn_g, 2).reshape(1, LANES)
    kern = functools.partial(_na_kernel, n_ctx=n_ctx, n_rows=n_rows)
    col = lambda which: pl.BlockSpec((1, None, t, LANES), lambda hp, bi: (bi, which * pairs + hp, 0, 0))
    return pl.pallas_call(
        kern,
        grid=(pairs, b),
        in_specs=[
            col(0), col(1), col(2),
            pl.BlockSpec((1, LANES), lambda hp, bi: (0, 0)),
            pl.BlockSpec((1, LANES), lambda hp, bi: (0, 0)),
            pl.BlockSpec((None, 1, 2, 3, nk, nq), lambda hp, bi: (layer, hp, 0, 0, 0, 0)),
        ],
        out_specs=col(0),
        out_shape=jax.ShapeDtypeStruct((b, pairs, t, LANES), BF16),
        scratch_shapes=[
            pltpu.VMEM((LANES, t), BF16),
            pltpu.VMEM((2, t, LANES), BF16),
            pltpu.VMEM((2, HEAD_DIM + NA_ONES_ROWS, t), BF16),
            pltpu.VMEM((2 * NA_GROUP, nk + n_ctx, nq), F32),
        ],
        compiler_params=_params(("arbitrary", "arbitrary")),
    )(u, u, u, qg, kg, bias)


def _lru_kernel(x_ref, g_ref, cw_ref, cb_ref, wa_ref, ba_ref, wx_ref, bx_ref, lam_ref, o_ref,
                xp_s, xc_s, hf_s, a_s, b_s, h_s, *, n_tiles):
    tl = LRU_TILE
    halo = LRU_HALO
    t = x_ref.shape[1]
    width = x_ref.shape[2]
    group = 8

    xp_s[0:halo, :] = jnp.zeros((halo, width), F32)
    xp_s[halo + t:halo + t + halo, :] = jnp.zeros((halo, width), F32)

    def fill(i, carry):
        r0 = pl.multiple_of(i * tl, tl)
        xp_s[pl.ds(pl.multiple_of(r0 + halo, halo), tl), :] = x_ref[0, pl.ds(r0, tl), :]
        return carry

    lax.fori_loop(0, n_tiles, fill, 0)

    row = lax.broadcasted_iota(jnp.int32, (tl, 1), 0)
    rowg = row % group
    cw = cw_ref[...]
    cb = cb_ref[...]

    def conv_tile(i):
        r0 = pl.multiple_of(i * tl, tl)
        win = xp_s[pl.ds(r0, tl + 2 * halo), :]
        n_win = tl + 2 * halo
        xm2 = pltpu.roll(win, 2, 0)[halo:halo + tl]
        xm1 = pltpu.roll(win, 1, 0)[halo:halo + tl]
        x0 = win[halo:halo + tl]
        xp1 = pltpu.roll(win, n_win - 1, 0)[halo:halo + tl]
        lo = i <= 1
        hi = i == 0
        xm2 = jnp.where(lo & (row < 2), 0.0, xm2)
        xm1 = jnp.where(lo & (row < 1), 0.0, xm1)
        xp1 = jnp.where(hi & (row == tl - 1), 0.0, xp1)
        return xm2 * cw[0:1] + xm1 * cw[1:2] + x0 * cw[2:3] + xp1 * cw[3:4] + cb

    def gates(i, d):
        r0 = pl.multiple_of(i * tl, tl)
        if d == 0:
            xc = conv_tile(i)
            xc_s[pl.ds(r0, tl), :] = xc
        else:
            xc = xc_s[pl.ds(r0, tl), :]
        xb = xc.astype(BF16)
        r = _sigmoid(_dot(xb, wa_ref[d]) + ba_ref[d:d + 1, :])
        ig = _sigmoid(_dot(xb, wx_ref[d]) + bx_ref[d:d + 1, :])
        log_a = -LRU_C * r * _softplus(-lam_ref[d:d + 1, :])
        a = jnp.exp(log_a)
        bt = jnp.sqrt(-jnp.tanh(log_a) * (a * a + 1.0)) * (ig * xc)
        return a, bt

    def scan_tile(i, d, carry):
        a, bt = gates(i, d)

        def shift(x, kk):
            x3 = x.reshape(tl // group, group, width)
            return pltpu.roll(x3, kk if d == 0 else group - kk, 1).reshape(tl, width)

        for kk in (1, 2, 4):
            mk = rowg >= kk if d == 0 else rowg < group - kk
            bt = jnp.where(mk, a * shift(bt, kk) + bt, bt)
            a = jnp.where(mk, a * shift(a, kk), a)
        a_s[...] = a
        b_s[...] = bt
        n_groups = tl // group

        def grp(gidx, c):
            gg = gidx if d == 0 else n_groups - 1 - gidx
            g0 = pl.multiple_of(gg * group, group)
            h = b_s[pl.ds(g0, group), :] + a_s[pl.ds(g0, group), :] * c
            h_s[pl.ds(g0, group), :] = h
            return h[group - 1:group, :] if d == 0 else h[0:1, :]

        return lax.fori_loop(0, n_groups, grp, carry, unroll=8)

    def fwd(i, carry):
        carry = scan_tile(i, 0, carry)
        r0 = pl.multiple_of(i * tl, tl)
        hf_s[pl.ds(r0, tl), :] = h_s[...]
        return carry

    def bwd(i, carry):
        carry = scan_tile(i, 1, carry)
        r0 = pl.multiple_of(i * tl, tl)
        y = (hf_s[pl.ds(r0, tl), :] + h_s[...]) * jax.nn.gelu(g_ref[0, pl.ds(r0, tl), :])
        o_ref[0, pl.ds(r0, tl), :] = y.astype(o_ref.dtype)
        return carry

    h0 = jnp.zeros((1, width), F32)
    lax.fori_loop(0, n_tiles, fwd, h0)
    carry = bwd(jnp.int32(0), h0)
    lax.fori_loop(1, n_tiles, lambda j, c: bwd(n_tiles - j, c), carry)


def _block_diag_dense(w):
    nd, k, c, _ = w.shape
    eye = jnp.eye(k, dtype=w.dtype)
    return jnp.einsum('dkce,kj->dkcje', w, eye).reshape(nd, k * c, k * c)


def _lru(u, conv_w, conv_b, w_a, b_a, w_x, b_x, lam, n_ctx):
    b, _, t, _ = u.shape
    width = conv_w.shape[1]
    assert n_ctx == LRU_TILE and t % LRU_TILE == 0
    n_tiles = t // LRU_TILE
    wa = _block_diag_dense(w_a).astype(BF16)
    wx = _block_diag_dense(w_x).astype(BF16)
    kern = functools.partial(_lru_kernel, n_tiles=n_tiles)
    full = lambda shape: pl.BlockSpec(shape, lambda bi: (0,) * len(shape))
    return pl.pallas_call(
        kern,
        grid=(b,),
        in_specs=[
            pl.BlockSpec((1, None, t, width), lambda bi: (bi, 0, 0, 0)),
            pl.BlockSpec((1, None, t, width), lambda bi: (bi, 1, 0, 0)),
            full((CONV_W, width)),
            full((1, width)),
            full((2, width, width)),
            full((2, width)),
            full((2, width, width)),
            full((2, width)),
            full((2, width)),
        ],
        out_specs=pl.BlockSpec((1, t, width), lambda bi: (bi, 0, 0)),
        out_shape=jax.ShapeDtypeStruct((b, t, width), BF16),
        scratch_shapes=[
            pltpu.VMEM((t + 2 * LRU_HALO, width), F32),
            pltpu.VMEM((t, width), F32),
            pltpu.VMEM((t, width), F32),
            pltpu.VMEM((LRU_TILE, width), F32),
            pltpu.VMEM((LRU_TILE, width), F32),
            pltpu.VMEM((LRU_TILE, width), F32),
        ],
        compiler_params=_params(("arbitrary",)),
    )(u, u, conv_w, conv_b.reshape(1, width), wa, b_a, wx, b_x, lam)


def _out_mlp_kernel(xc_ref, xl_ref, ym_ref, yn_ref, yl_ref, mod_ref, g_ref, wo_ref, w1_ref, w2_ref, oc_ref, ol_ref,
                    *, hid_tile):
    def body(x_ref, o_ref):
        mod = mod_ref[0, 0]
        att = _dot(ym_ref[0].astype(BF16), wo_ref[0:M_WIDTH, :])
        yn = jnp.concatenate([yn_ref[0, j] for j in range(yn_ref.shape[1])], axis=1)
        att = att + _dot(yn.astype(BF16), wo_ref[M_WIDTH:M_WIDTH + NA_WIDTH, :])
        att = att + _dot(yl_ref[0].astype(BF16), wo_ref[M_WIDTH + NA_WIDTH:, :])
        x1 = x_ref[0] + mod[2:3, :] * att
        ms = jnp.mean(x1 * x1, axis=-1, keepdims=True)
        y = x1 * lax.rsqrt(ms + EPS) * g_ref[...]
        h = (y * (1.0 + mod[4:5, :]) + mod[3:4, :]).astype(BF16)
        hidden = w1_ref.shape[1]
        acc = None
        for j in range(hidden // hid_tile):
            a = jnp.maximum(_dot(h, w1_ref[:, j * hid_tile:(j + 1) * hid_tile]), 0.0)
            part = _dot((a * a).astype(BF16), w2_ref[j * hid_tile:(j + 1) * hid_tile, :])
            acc = part if acc is None else acc + part
        o_ref[0] = x1 + mod[5:6, :] * acc

    _on_stream_tile(body, xc_ref, xl_ref, (oc_ref, ol_ref))


def _out_mlp(xc, xl, ym, yn, yl, modv, g2, w_out, w1, w2, layer):
    b, n_ctx, d = xc.shape
    t = n_ctx + xl.shape[1]
    hidden = w1.shape[2]
    tok = lambda w: pl.BlockSpec((1, ROW_TILE, w), lambda bi, i: (bi, i, 0))
    kern = functools.partial(_out_mlp_kernel, hid_tile=1024)
    return pl.pallas_call(
        kern,
        grid=(b, t // ROW_TILE),
        in_specs=_stream_specs(n_ctx, d) + [
            tok(M_WIDTH),
            pl.BlockSpec((1, NA_HEADS // 2, ROW_TILE, LANES), lambda bi, i: (bi, 0, i, 0)),
            tok(d - M_WIDTH - NA_WIDTH),
            pl.BlockSpec((1, 1, 6, d), lambda bi, i: (bi, jnp.minimum(i, 1), 0, 0)),
            pl.BlockSpec((1, d), lambda bi, i: (0, 0)),
            _resident((None, d, d), lambda bi, i: (layer, 0, 0)),
            _resident((None, d, hidden), lambda bi, i: (layer, 0, 0)),
            _resident((None, hidden, d), lambda bi, i: (layer, 0, 0)),
        ],
        out_specs=_stream_specs(n_ctx, d),
        out_shape=[jax.ShapeDtypeStruct(xc.shape, F32), jax.ShapeDtypeStruct(xl.shape, F32)],
        compiler_params=_params(("arbitrary", "arbitrary")),
    )(xc, xl, ym, yn, yl, modv, g2.reshape(1, d), w_out, w1, w2)


def _split_w_in(w_in):
    m_end = 4 * M_WIDTH
    na_end = m_end + N_GATES + 3 * NA_WIDTH
    main = jnp.concatenate([w_in[..., :m_end], w_in[..., na_end:], w_in[..., m_end + N_GATES:na_end]], axis=-1)
    gates = jnp.swapaxes(w_in[..., m_end:m_end + N_GATES], -1, -2)
    return main.astype(BF16), gates.astype(BF16)


def kernel(x, c, ctx, c_ctx, w_mod, b_mod, norm1_g, norm2_g, w_in, mlstm_gate_b, mlstm_norm_g, na_q_norm_g, na_k_norm_g, na_rpb, lru_conv_w, lru_conv_b, lru_w_a, lru_b_a, lru_w_x, lru_b_x, lru_lambda, w_out, w_mlp1, w_mlp2):
    b, n, d = x.shape
    n_ctx = ctx.shape[1]
    depth = w_in.shape[0]
    assert n_ctx == ROW_TILE and n % ROW_TILE == 0 and n % GRID_W == 0

    cond_rows = 8 * ((b + 1 + 7) // 8)
    cond = jnp.zeros((cond_rows, d), F32).at[:b].set(c).at[b].set(c_ctx)
    mod_all = _modulation(cond, w_mod, b_mod)
    rope_row, rope_col = _rope_tables(n // GRID_W)
    w_in_p, w_gate_t = _split_w_in(w_in)
    na_bias = _na_bias_tables(na_rpb)
    w_out_b, w1_b, w2_b = w_out.astype(BF16), w_mlp1.astype(BF16), w_mlp2.astype(BF16)

    xc, xl = ctx, x
    for l in range(depth):
        lat = mod_all[l, :b].reshape(b, 6, d)
        cx = jnp.broadcast_to(mod_all[l, b].reshape(1, 6, d), (b, 6, d))
        modv = jnp.stack([cx, lat], axis=1)
        um, ul, un, g_rows = _inproj(xc, xl, modv, norm1_g[l], w_in_p, w_gate_t, l)
        ym = _mlstm(um, g_rows, mlstm_gate_b[l], mlstm_norm_g[l], rope_row, rope_col, n_ctx)
        yn = _na(un, na_q_norm_g[l], na_k_norm_g[l], na_bias, l, n_ctx)
        yl = _lru(ul, lru_conv_w[l], lru_conv_b[l], lru_w_a[l], lru_b_a[l], lru_w_x[l], lru_b_x[l], lru_lambda[l], n_ctx)
        xc, xl = _out_mlp(xc, xl, ym, yn, yl, modv, norm2_g[l], w_out_b, w1_b, w2_b, l)
    return xl
```

```python
import functools
import math

import numpy as np
import jax
import jax.numpy as jnp
from jax import lax
from jax.experimental import pallas as pl
from jax.experimental.pallas import tpu as pltpu

F32 = jnp.float32
BF16 = jnp.bfloat16

HEAD_DIM = 64
M_HEADS = 4
M_WIDTH = M_HEADS * HEAD_DIM
NA_HEADS = 8
NA_WIDTH = NA_HEADS * HEAD_DIM
N_GATES = 4 * M_HEADS
GRID_W = 64
M_CHUNK = 128
M_AUG = HEAD_DIM + 16
M_LOCAL_GROUP = 2
M_LATENT_GROUP = 4
WIN_R = 8
WIN_C = 16
RPB_R = 2 * WIN_R - 1
RPB_C = 2 * WIN_C - 1
NA_QROWS = 4
NA_BAND = NA_QROWS + WIN_R
NA_KEY_CHUNK = 128
NA_ONES_ROWS = 16
NA_GROUP = 8
LOG2E = 1.4426950408889634
CONV_W = 4
LRU_C = 8.0
ROPE_BASE = 10000.0
EPS = 1e-6
NEG = -1e30

LANES = 128
ROW_TILE = 256
LRU_TILE = 256
LRU_HALO = 8
VMEM_LIMIT = 56 * 1024 * 1024

U_MQ, U_MK, U_MV, U_MO = 0, 256, 512, 768
U_XR, U_GT = 1024, 1280
U_NQ, U_NK, U_NV = 1536, 2048, 2560
U_COLS = 3072
LRU_WIDTH = U_NQ - U_GT
NA_COL_BLOCKS = (U_COLS - U_NQ) // LANES


def _dot(a, b):
    return jnp.dot(a, b, preferred_element_type=F32)


def _dot_f32_rhs01(x, m01):
    hi = x.astype(BF16)
    lo = (x - hi.astype(F32)).astype(BF16)
    return _dot(hi, m01) + _dot(lo, m01)


def _sigmoid(x):
    return 0.5 * jnp.tanh(0.5 * x) + 0.5


def _log_sigmoid(x):
    return jnp.minimum(x, 0.0) - jnp.log1p(jnp.exp(-jnp.abs(x)))


def _softplus(x):
    return jnp.maximum(x, 0.0) + jnp.log1p(jnp.exp(-jnp.abs(x)))


def _params(sem):
    return pltpu.CompilerParams(dimension_semantics=sem, vmem_limit_bytes=VMEM_LIMIT)


def _resident(shape, index_map):
    return pl.BlockSpec(shape, index_map, pipeline_mode=pl.Buffered(1))


def _mod_kernel(a_ref, w_ref, b_ref, o_ref):
    a = a_ref[...]
    act = (a * jax.nn.sigmoid(a)).astype(BF16)
    o_ref[0] = _dot(act, w_ref[0].astype(BF16)) + b_ref[0]


def _modulation(cond, w_mod, b_mod):
    depth, d, n6 = w_mod.shape
    rows = cond.shape[0]
    tn = 1536
    return pl.pallas_call(
        _mod_kernel,
        grid=(depth, n6 // tn),
        in_specs=[
            pl.BlockSpec((rows, d), lambda l, j: (0, 0)),
            pl.BlockSpec((1, d, tn), lambda l, j: (l, 0, j)),
            pl.BlockSpec((1, 1, tn), lambda l, j: (l, 0, j)),
        ],
        out_specs=pl.BlockSpec((1, rows, tn), lambda l, j: (l, 0, j)),
        out_shape=jax.ShapeDtypeStruct((depth, rows, n6), F32),
        compiler_params=_params(("arbitrary", "arbitrary")),
    )(cond, w_mod, b_mod.reshape(depth, 1, n6))


def _stream_specs(n_ctx, d):
    assert n_ctx == ROW_TILE
    return [
        pl.BlockSpec((1, ROW_TILE, d), lambda bi, i: (bi, 0, 0)),
        pl.BlockSpec((1, ROW_TILE, d), lambda bi, i: (bi, jnp.maximum(i - 1, 0), 0)),
    ]


def _on_stream_tile(body, xc_ref, xl_ref, *outs):
    @pl.when(pl.program_id(1) == 0)
    def _():
        body(xc_ref, *[o[0] for o in outs])

    @pl.when(pl.program_id(1) > 0)
    def _():
        body(xl_ref, *[o[1] for o in outs])


def _inproj_kernel(xc_ref, xl_ref, mod_ref, g_ref, w_ref, wg_ref, um_ref, ul_ref, un_ref, gt_ref):
    def body(x_ref):
        x = x_ref[0]
        ms = jnp.mean(x * x, axis=-1, keepdims=True)
        y = x * lax.rsqrt(ms + EPS) * g_ref[...]
        mod = mod_ref[0, 0]
        h = (y * (1.0 + mod[1:2, :]) + mod[0:1, :]).astype(BF16)
        u = _dot(h, w_ref[...])
        um_ref[0] = u[:, 0:U_XR]
        for j in range(ul_ref.shape[1]):
            ul_ref[0, j] = u[:, U_XR + j * LRU_WIDTH:U_XR + (j + 1) * LRU_WIDTH]
        for j in range(un_ref.shape[1]):
            un_ref[0, j] = u[:, U_NQ + j * LANES:U_NQ + (j + 1) * LANES]
        gt_ref[0] = lax.dot_general(wg_ref[...], h, (((1,), (1,)), ((), ())), preferred_element_type=F32)

    _on_stream_tile(body, xc_ref, xl_ref)


def _inproj(xc, xl, modv, g1, w_in_p, w_gate_t, layer):
    b, n_ctx, d = xc.shape
    t = n_ctx + xl.shape[1]
    return pl.pallas_call(
        _inproj_kernel,
        grid=(b, t // ROW_TILE),
        in_specs=_stream_specs(n_ctx, d) + [
            pl.BlockSpec((1, 1, 6, d), lambda bi, i: (bi, jnp.minimum(i, 1), 0, 0)),
            pl.BlockSpec((1, d), lambda bi, i: (0, 0)),
            _resident((None, d, U_COLS), lambda bi, i: (layer, 0, 0)),
            _resident((None, N_GATES, d), lambda bi, i: (layer, 0, 0)),
        ],
        out_specs=[
            pl.BlockSpec((1, ROW_TILE, U_XR), lambda bi, i: (bi, i, 0)),
            pl.BlockSpec((1, 2, ROW_TILE, LRU_WIDTH), lambda bi, i: (bi, 0, i, 0)),
            pl.BlockSpec((1, NA_COL_BLOCKS, ROW_TILE, LANES), lambda bi, i: (bi, 0, i, 0)),
            pl.BlockSpec((1, N_GATES, ROW_TILE), lambda bi, i: (bi, 0, i)),
        ],
        out_shape=[
            jax.ShapeDtypeStruct((b, t, U_XR), F32),
            jax.ShapeDtypeStruct((b, 2, t, LRU_WIDTH), F32),
            jax.ShapeDtypeStruct((b, NA_COL_BLOCKS, t, LANES), F32),
            jax.ShapeDtypeStruct((b, N_GATES, t), F32),
        ],
        compiler_params=_params(("arbitrary", "arbitrary")),
    )(xc, xl, modv, g1.reshape(1, d), w_in_p, w_gate_t)


def _mlstm_kernel(u_ref, gr_ref, gb_ref, ng_ref, rrow_ref, rcol_ref, o_ref,
                  qt_s, k_s, vt_s, r_s, row_s, c_s, *, n_ctx_chunks, n_chunks):
    L = M_CHUNK
    W = M_WIDTH
    D = HEAD_DIM
    groups = L // GRID_W

    for h in range(M_HEADS):
        vt_s[h, D:, :] = jnp.ones((M_AUG - D, vt_s.shape[2]), BF16)
    c_s[...] = jnp.zeros(c_s.shape, F32)

    def prep(cs, latent):
        r0s = [pl.multiple_of(c * L, L) for c in cs]
        qs = [u_ref[0, pl.ds(r0, L), U_MQ:U_MQ + W] for r0 in r0s]
        ks = [u_ref[0, pl.ds(r0, L), U_MK:U_MK + W] * D ** -0.5 for r0 in r0s]
        vts = [u_ref[0, pl.ds(r0, L), U_MV:U_MV + W].T for r0 in r0s]
        if latent:
            def table(c, i):
                first = (c - n_ctx_chunks) * groups
                return jnp.concatenate([rrow_ref[i, pl.ds(first + g, 1), :] + rcol_ref[i] for g in range(groups)], axis=0)

            def rope(x, c):
                return (x * table(c, 0) + pltpu.roll(x, W - D // 4, 1) * table(c, 1)
                        + pltpu.roll(x, D // 4, 1) * table(c, 2))

            qs = [rope(q, c) for q, c in zip(qs, cs)]
            ks = [rope(k, c) for k, c in zip(ks, cs)]
        qts = [q.T for q in qs]
        for r0, qt, k, vt in zip(r0s, qts, ks, vts):
            qt_s[:, pl.ds(r0, L)] = qt.astype(BF16)
            k_s[pl.ds(r0, L), :] = k.astype(BF16)
            for h in range(M_HEADS):
                vt_s[h, 0:D, pl.ds(r0, L)] = vt[h * D:(h + 1) * D].astype(BF16)

    assert n_ctx_chunks % M_LOCAL_GROUP == 0 and n_chunks % M_LOCAL_GROUP == 0

    def prep_group(latent):
        def body(j, carry):
            first = j * M_LOCAL_GROUP + (n_ctx_chunks if latent else 0)
            prep([first + i for i in range(M_LOCAL_GROUP)], latent)
            return carry
        return body

    lax.fori_loop(0, n_ctx_chunks // M_LOCAL_GROUP, prep_group(False), 0)
    lax.fori_loop(0, (n_chunks - n_ctx_chunks) // M_LOCAL_GROUP, prep_group(True), 0)

    si = lax.broadcasted_iota(jnp.int32, (L, L), 0)
    li = lax.broadcasted_iota(jnp.int32, (L, L), 1)
    keeps = [si <= li, si >= li]
    gbias = [jnp.broadcast_to(gb_ref[8 * d:8 * d + 8, :], (8, L)) for d in range(2)]
    zrows = jnp.zeros((D, L), BF16)

    def qt_head(qtc, h):
        parts = [zrows] * M_HEADS
        parts[h] = qtc[h * D:(h + 1) * D]
        return jnp.concatenate(parts, axis=0)

    def local(first, n_group):
        r0s = [pl.multiple_of((first + j) * L, L) for j in range(n_group)]
        scores = [[_dot(k_s[pl.ds(r0, L), :], qt_head(qt_s[:, pl.ds(r0, L)], h)) for h in range(M_HEADS)]
                  for r0 in r0s]
        vtas = [[vt_s[h, :, pl.ds(r0, L)] for h in range(M_HEADS)] for r0 in r0s]
        g8s = [[gr_ref[0, 8 * d:8 * d + 8, pl.ds(r0, L)] + gbias[d] for d in range(2)] for r0 in r0s]
        cum4s = [[pltpu.roll(_dot_f32_rhs01(_log_sigmoid(g8[d]), keeps[d].astype(BF16)), M_HEADS, 0)
                  for d in range(2)] for g8 in g8s]
        src_ts = [[(g8[d] - cum4[d]).T for d in range(2)] for g8, cum4 in zip(g8s, cum4s)]
        jdh = [(j, d, h) for j in range(n_group) for d in range(2) for h in range(M_HEADS)]
        logws = [jnp.where(keeps[d], cum4s[j][d][h:h + 1] + jnp.broadcast_to(src_ts[j][d][:, h:h + 1], (L, L)), NEG)
                 for j, d, h in jdh]
        amaxs = [lw.max(axis=0, keepdims=True) for lw in logws]
        ps = [(scores[j][h] * jnp.exp(logw - a)).astype(BF16) for (j, d, h), logw, a in zip(jdh, logws, amaxs)]
        intras = [_dot(vtas[j][h], p) for (j, d, h), p in zip(jdh, ps)]
        for (j, d, h), a, intra in zip(jdh, amaxs, intras):
            last = L - 1 if d == 0 else 0
            cl = cum4s[j][d][h:h + 1]
            a_last = jnp.broadcast_to(a[:, last:last + 1], (1, L))
            cl_last = jnp.broadcast_to(cl[:, last:last + 1], (1, L))
            w_src = jnp.exp(cl_last - cl + g8s[j][d][h:h + 1] - a_last)
            r_s[d, h * M_AUG:(h + 1) * M_AUG, pl.ds(r0s[j], L)] = intra
            for kind, row in enumerate((cl, a, cl_last, a_last, w_src)):
                row_s[d, kind, h:h + 1, pl.ds(r0s[j], L)] = row

    def local_loop(first, count, n_group):
        assert count % n_group == 0

        def body(i, carry):
            local(first + i * n_group, n_group)
            return carry

        lax.fori_loop(0, count // n_group, body, 0)

    local_loop(0, n_ctx_chunks, M_LOCAL_GROUP)
    local_loop(n_ctx_chunks, n_chunks - n_ctx_chunks, M_LATENT_GROUP)

    def step(c, d, c_olds, m_prevs):
        r0 = pl.multiple_of(c * L, L)
        kc = k_s[pl.ds(r0, L), :]
        qtc = qt_s[:, pl.ds(r0, L)]
        h_ts, c_news, m_news = [], [], []
        for h in range(M_HEADS):
            cl, a, cl_last, a_last, w_src = [row_s[d, kind, h:h + 1, pl.ds(r0, L)] for kind in range(5)]
            m_prev = m_prevs[h]
            inter = cl + m_prev
            m_t = jnp.maximum(inter, a)
            r = (jnp.exp(inter - m_t) * _dot(c_olds[h].astype(BF16), qt_head(qtc, h))
                 + jnp.exp(a - m_t) * r_s[d, h * M_AUG:(h + 1) * M_AUG, pl.ds(r0, L)])
            denom = jnp.maximum(jnp.abs(r[D:D + 1]), jnp.exp(-m_t))
            h_ts.append(r[0:D] * (1.0 / denom))
            m_new = jnp.maximum(cl_last + m_prev, a_last)
            w_old = jnp.exp(cl_last + m_prev - m_new)
            w_upd = jnp.exp(a_last - m_new)
            vw = (vt_s[h, :, pl.ds(r0, L)].astype(F32) * w_src).astype(BF16)
            scale = lambda w: jnp.concatenate([w] * (W // L), axis=1)
            c_news.append(scale(w_old) * c_olds[h] + scale(w_upd) * _dot(vw, kc))
            m_news.append(m_new)
        return r0, h_ts, c_news, m_news

    def body(i, m_all):
        chunks = [i, jnp.where(i < n_ctx_chunks, n_ctx_chunks - 1 - i, n_chunks - 1 + n_ctx_chunks - i)]
        c_olds = [[c_s[d, h] for h in range(M_HEADS)] for d in range(2)]
        outs = [step(chunks[d], d, c_olds[d], m_all[d]) for d in range(2)]
        for d in range(2):
            r0, h_ts, c_news, _ = outs[d]
            for h in range(M_HEADS):
                r_s[d, h * M_AUG:h * M_AUG + D, pl.ds(r0, L)] = h_ts[h]
                c_s[d, h] = c_news[h]
        return tuple(tuple(outs[d][3]) for d in range(2))

    m0 = tuple(tuple(jnp.zeros((1, L), F32) for _ in range(M_HEADS)) for _ in range(2))
    lax.fori_loop(0, n_chunks, body, m0)

    ng = jnp.broadcast_to(ng_ref[...], (W, L))

    def post(j, carry):
        r0s = [pl.multiple_of((j * M_LOCAL_GROUP + i) * L, L) for i in range(M_LOCAL_GROUP)]
        hts = [[r_s[0, h * M_AUG:h * M_AUG + D, pl.ds(r0, L)] + r_s[1, h * M_AUG:h * M_AUG + D, pl.ds(r0, L)]
                for h in range(M_HEADS)] for r0 in r0s]
        scales = [[lax.rsqrt(jnp.sum(ht * ht, axis=0, keepdims=True) * (1.0 / D) + EPS) for ht in hs] for hs in hts]
        hnts = [(jnp.concatenate([ht * sc for ht, sc in zip(hs, scs)], axis=0) * ng).T for hs, scs in zip(hts, scales)]
        for r0, hnt in zip(r0s, hnts):
            o_ref[0, pl.ds(r0, L), :] = (_sigmoid(u_ref[0, pl.ds(r0, L), U_MO:U_MO + W]) * hnt).astype(o_ref.dtype)
        return carry

    lax.fori_loop(0, n_chunks // M_LOCAL_GROUP, post, 0)


def _rope_tables(n_rows):
    nf = HEAD_DIM // 4
    inv = ROPE_BASE ** (-jnp.arange(nf, dtype=F32) / nf)
    d = np.arange(M_WIDTH) % HEAD_DIM
    f_idx = d % nf
    is_row = d < HEAD_DIM // 2
    first_quarter = (d % (HEAD_DIM // 2)) < nf

    def tables(pos, sel):
        ang = pos[:, None] * inv[None, :]
        ang = ang[:, f_idx]
        cos = jnp.where(sel, jnp.cos(ang), 0.0)
        sin = jnp.where(sel, jnp.sin(ang), 0.0)
        sin_a = jnp.where(first_quarter, -sin, 0.0)
        sin_b = jnp.where(first_quarter, 0.0, sin)
        return jnp.stack([cos, sin_a, sin_b]).astype(F32)

    row_t = tables(jnp.arange(n_rows, dtype=F32), is_row)
    col_t = tables(jnp.arange(GRID_W, dtype=F32), ~is_row)
    return row_t, col_t


def _mlstm(u, g_rows, gate_b, norm_g, rope_row, rope_col, n_ctx):
    b, t, _ = u.shape
    assert t % M_CHUNK == 0 and n_ctx % M_CHUNK == 0 and M_CHUNK % GRID_W == 0
    n_chunks = t // M_CHUNK
    kern = functools.partial(_mlstm_kernel, n_ctx_chunks=n_ctx // M_CHUNK, n_chunks=n_chunks)
    n_rows = rope_row.shape[1]
    return pl.pallas_call(
        kern,
        grid=(b,),
        in_specs=[
            pl.BlockSpec((1, t, 4 * M_WIDTH), lambda bi: (bi, 0, 0), pipeline_mode=pl.Buffered(1)),
            pl.BlockSpec((1, N_GATES, t), lambda bi: (bi, 0, 0)),
            pl.BlockSpec((N_GATES, 1), lambda bi: (0, 0)),
            pl.BlockSpec((M_WIDTH, 1), lambda bi: (0, 0)),
            pl.BlockSpec((3, n_rows, M_WIDTH), lambda bi: (0, 0, 0)),
            pl.BlockSpec((3, GRID_W, M_WIDTH), lambda bi: (0, 0, 0)),
        ],
        out_specs=pl.BlockSpec((1, t, M_WIDTH), lambda bi: (bi, 0, 0)),
        out_shape=jax.ShapeDtypeStruct((b, t, M_WIDTH), BF16),
        scratch_shapes=[
            pltpu.VMEM((M_WIDTH, t), BF16),
            pltpu.VMEM((t, M_WIDTH), BF16),
            pltpu.VMEM((M_HEADS, M_AUG, t), BF16),
            pltpu.VMEM((2, M_HEADS * M_AUG, t), F32),
            pltpu.VMEM((2, 5, 8, t), F32),
            pltpu.VMEM((2, M_HEADS, M_AUG, M_WIDTH), F32),
        ],
        compiler_params=_params(("arbitrary",)),
    )(u, g_rows, gate_b.reshape(N_GATES, 1), norm_g.reshape(M_WIDTH, 1), rope_row, rope_col)


def _na_kernel(q_ref, k_ref, v_ref, qg_ref, kg_ref, bias_ref, o_ref, qt_s, kz_s, vt_s, s_s, *, n_ctx, n_rows):
    t = q_ref.shape[1]
    lane = lax.broadcasted_iota(jnp.int32, (1, LANES), 1)
    half = [lane < HEAD_DIM, lane >= HEAD_DIM]
    hi = lax.broadcasted_iota(jnp.int32, (LANES, LANES), 0) // HEAD_DIM
    hj = lax.broadcasted_iota(jnp.int32, (LANES, LANES), 1) // HEAD_DIM
    same_head = (hi == hj).astype(BF16)
    qg = qg_ref[...] * (HEAD_DIM ** -0.5 * LOG2E)
    kg = kg_ref[...]
    qblk = NA_QROWS * GRID_W
    kc = NA_KEY_CHUNK
    sub = 8

    for hh in range(2):
        vt_s[hh, HEAD_DIM:, :] = jnp.ones((vt_s.shape[1] - HEAD_DIM, t), BF16)

    def prep(r0s):
        qs = [q_ref[0, pl.ds(r0, ROW_TILE), :] for r0 in r0s]
        ks = [k_ref[0, pl.ds(r0, ROW_TILE), :] for r0 in r0s]
        vts = [v_ref[0, pl.ds(r0, ROW_TILE), :].T for r0 in r0s]
        qms = [_dot_f32_rhs01(q * q, same_head) * (1.0 / HEAD_DIM) for q in qs]
        kms = [_dot_f32_rhs01(k * k, same_head) * (1.0 / HEAD_DIM) for k in ks]
        qnts = [(q * lax.rsqrt(m + EPS) * qg).T for q, m in zip(qs, qms)]
        kns = [k * lax.rsqrt(m + EPS) * kg for k, m in zip(ks, kms)]
        for r0, qnt, kn, vt in zip(r0s, qnts, kns, vts):
            qt_s[:, pl.ds(r0, ROW_TILE)] = qnt.astype(BF16)
            for hh in range(2):
                kz_s[hh, pl.ds(r0, ROW_TILE), :] = jnp.where(half[hh], kn, 0.0).astype(BF16)
                vt_s[hh, 0:HEAD_DIM, pl.ds(r0, ROW_TILE)] = vt[hh * HEAD_DIM:(hh + 1) * HEAD_DIM].astype(BF16)

    n_tiles = t // ROW_TILE

    def prep_pair(j, carry):
        prep([pl.multiple_of((2 * j + i) * ROW_TILE, ROW_TILE) for i in range(2)])
        return carry

    lax.fori_loop(0, n_tiles // 2, prep_pair, 0)
    if n_tiles % 2:
        prep([(n_tiles - 1) * ROW_TILE])

    def attend(blocks):
        jobs = [(b, hh) for b in range(len(blocks)) for hh in range(2)]
        qts = [qt_s[:, pl.ds(q0, qblk)] for q0, _ in blocks]
        n_chunks = len(blocks[0][1])
        mx = [None] * len(jobs)
        for ci in range(n_chunks):
            for ji, (b, hh) in enumerate(jobs):
                k0, tab = blocks[b][1][ci]
                s = _dot(kz_s[hh, pl.ds(k0, kc), :], qts[b])
                if tab is not None:
                    s = s + bias_ref[0, hh, tab[0], tab[1]:tab[1] + kc, :]
                s_s[ji, ci * kc:(ci + 1) * kc, :] = s
                cm = functools.reduce(jnp.maximum, [s[i * sub:(i + 1) * sub] for i in range(kc // sub)])
                mx[ji] = cm if mx[ji] is None else jnp.maximum(mx[ji], cm)
        ms = [m.max(axis=0, keepdims=True) for m in mx]
        accs = [None] * len(jobs)
        for ci in range(n_chunks):
            for ji, (b, hh) in enumerate(jobs):
                k0, _ = blocks[b][1][ci]
                p = jnp.exp2(s_s[ji, ci * kc:(ci + 1) * kc, :] - ms[ji]).astype(BF16)
                pv = _dot(vt_s[hh, :, pl.ds(k0, kc)], p)
                accs[ji] = pv if accs[ji] is None else accs[ji] + pv
        outs = []
        for b in range(len(blocks)):
            heads = [accs[2 * b + hh][0:HEAD_DIM] / accs[2 * b + hh][HEAD_DIM:HEAD_DIM + 1] for hh in range(2)]
            outs.append(jnp.concatenate(heads, axis=0).T)
        return outs

    ctx_chunks = [(ci * kc, None) for ci in range(n_ctx // kc)]
    o_ref[0, 0:n_ctx, :] = attend([(0, ctx_chunks)])[0].astype(o_ref.dtype)

    n_blocks = n_rows // NA_QROWS

    def group(gi, carry):
        blocks = []
        for j in range(NA_GROUP):
            bi = gi * NA_GROUP + j
            r0 = bi * NA_QROWS
            sb = jnp.clip(r0 - WIN_R // 2, 0, n_rows - NA_BAND)
            case = jnp.where(bi == 0, 0, jnp.where(bi == n_blocks - 1, 2, 1))
            q0 = pl.multiple_of(n_ctx + r0 * GRID_W, qblk)
            k0 = pl.multiple_of(n_ctx + sb * GRID_W, qblk)
            local = [(pl.multiple_of(k0 + ci * kc, kc), (case, ci * kc)) for ci in range(NA_BAND * GRID_W // kc)]
            blocks.append((q0, local + ctx_chunks))
        for (q0, _), out in zip(blocks, attend(blocks)):
            o_ref[0, pl.ds(q0, qblk), :] = out.astype(o_ref.dtype)
        return carry

    lax.fori_loop(0, n_blocks // NA_GROUP, group, 0)


def _na_bias_tables(rpb):
    n_layers, n_heads = rpb.shape[:2]
    cols = np.arange(GRID_W)
    col0 = np.clip(cols - WIN_C // 2, 0, GRID_W - WIN_C)
    valid = (cols[:, None] >= col0[None, :]) & (cols[:, None] < col0[None, :] + WIN_C)
    dc = cols[:, None] - cols[None, :] + WIN_C - 1
    col_sel = ((dc[..., None] == np.arange(RPB_C)) & valid[..., None]).astype(np.float32)
    toep = jnp.einsum('lhrd,xcd->lhrxc', rpb, col_sel, precision=lax.Precision.HIGHEST)
    toep = jnp.where(valid, toep * LOG2E, NEG * LOG2E).astype(F32)
    nk, nq = NA_BAND * GRID_W, NA_QROWS * GRID_W
    return pl.pallas_call(
        _na_bias_kernel,
        grid=(n_layers * n_heads,),
        in_specs=[pl.BlockSpec((1, RPB_R, GRID_W, GRID_W), lambda i: (i, 0, 0, 0))],
        out_specs=pl.BlockSpec((1, 3, nk, nq), lambda i: (i, 0, 0, 0)),
        out_shape=jax.ShapeDtypeStruct((n_layers * n_heads, 3, nk, nq), F32),
        compiler_params=_params(("arbitrary",)),
    )(toep.reshape(n_layers * n_heads, RPB_R, GRID_W, GRID_W)).reshape(n_layers, n_heads, 3, nk, nq)


def _na_bias_kernel(t_ref, o_ref):
    blank = jnp.full((GRID_W, GRID_W), NEG * LOG2E, F32)
    for case in range(3):
        for jj in range(NA_BAND):
            tiles = []
            for i in range(NA_QROWS):
                off = (0, i, NA_BAND - WIN_R)[case]
                rel = (jj - i, jj - i - WIN_R // 2, jj - i - NA_BAND + NA_QROWS)[case]
                tiles.append(t_ref[0, rel + WIN_R - 1] if off <= jj < off + WIN_R else blank)
            o_ref[0, case, jj * GRID_W:(jj + 1) * GRID_W, :] = jnp.concatenate(tiles, axis=1)


def _na(u, qn_g, kn_g, bias, layer, n_ctx):
    b, _, t, _ = u.shape
    n_rows = (t - n_ctx) // GRID_W
    assert n_rows % (NA_QROWS * NA_GROUP) == 0 and n_rows >= NA_BAND and n_ctx == NA_QROWS * GRID_W
    pairs = NA_HEADS // 2
    nk, nq = NA_BAND * GRID_W, NA_QROWS * GRID_W
    bias = bias.reshape(bias.shape[0], pairs, 2, 3, nk, nq)
    qg = jnp.tile(qn_g, 2).reshape(1, LANES)
    kg = jnp.tile(kn_g, 2).reshape(1, LANES)
    kern = functools.partial(_na_kernel, n_ctx=n_ctx, n_rows=n_rows)
    col = lambda which: pl.BlockSpec((1, None, t, LANES), lambda hp, bi: (bi, which * pairs + hp, 0, 0))
    return pl.pallas_call(
        kern,
        grid=(pairs, b),
        in_specs=[
            col(0), col(1), col(2),
            pl.BlockSpec((1, LANES), lambda hp, bi: (0, 0)),
            pl.BlockSpec((1, LANES), lambda hp, bi: (0, 0)),
            pl.BlockSpec((None, 1, 2, 3, nk, nq), lambda hp, bi: (layer, hp, 0, 0, 0, 0)),
        ],
        out_specs=col(0),
        out_shape=jax.ShapeDtypeStruct((b, pairs, t, LANES), BF16),
        scratch_shapes=[
            pltpu.VMEM((LANES, t), BF16),
            pltpu.VMEM((2, t, LANES), BF16),
            pltpu.VMEM((2, HEAD_DIM + NA_ONES_ROWS, t), BF16),
            pltpu.VMEM((2 * NA_GROUP, nk + n_ctx, nq), F32),
        ],
        compiler_params=_params(("arbitrary", "arbitrary")),
    )(u, u, u, qg, kg, bias)


def _lru_kernel(x_ref, g_ref, cw_ref, cb_ref, wa_ref, ba_ref, wx_ref, bx_ref, lam_ref, o_ref,
                xp_s, xc_s, hf_s, a_s, b_s, h_s, *, n_tiles):
    tl = LRU_TILE
    halo = LRU_HALO
    t = x_ref.shape[1]
    width = x_ref.shape[2]
    group = 8

    xp_s[0:halo, :] = jnp.zeros((halo, width), F32)
    xp_s[halo + t:halo + t + halo, :] = jnp.zeros((halo, width), F32)

    def fill(i, carry):
        r0 = pl.multiple_of(i * tl, tl)
        xp_s[pl.ds(pl.multiple_of(r0 + halo, halo), tl), :] = x_ref[0, pl.ds(r0, tl), :]
        return carry

    lax.fori_loop(0, n_tiles, fill, 0)

    row = lax.broadcasted_iota(jnp.int32, (tl, 1), 0)
    rowg = row % group
    cw = cw_ref[...]
    cb = cb_ref[...]

    def conv_tile(i):
        r0 = pl.multiple_of(i * tl, tl)
        win = xp_s[pl.ds(r0, tl + 2 * halo), :]
        n_win = tl + 2 * halo
        xm2 = pltpu.roll(win, 2, 0)[halo:halo + tl]
        xm1 = pltpu.roll(win, 1, 0)[halo:halo + tl]
        x0 = win[halo:halo + tl]
        xp1 = pltpu.roll(win, n_win - 1, 0)[halo:halo + tl]
        lo = i <= 1
        hi = i == 0
        xm2 = jnp.where(lo & (row < 2), 0.0, xm2)
        xm1 = jnp.where(lo & (row < 1), 0.0, xm1)
        xp1 = jnp.where(hi & (row == tl - 1), 0.0, xp1)
        return xm2 * cw[0:1] + xm1 * cw[1:2] + x0 * cw[2:3] + xp1 * cw[3:4] + cb

    def gates(i, d):
        r0 = pl.multiple_of(i * tl, tl)
        if d == 0:
            xc = conv_tile(i)
            xc_s[pl.ds(r0, tl), :] = xc
        else:
            xc = xc_s[pl.ds(r0, tl), :]
        xb = xc.astype(BF16)
        r = _sigmoid(_dot(xb, wa_ref[d]) + ba_ref[d:d + 1, :])
        ig = _sigmoid(_dot(xb, wx_ref[d]) + bx_ref[d:d + 1, :])
        log_a = -LRU_C * r * _softplus(-lam_ref[d:d + 1, :])
        a = jnp.exp(log_a)
        bt = jnp.sqrt(-jnp.tanh(log_a) * (a * a + 1.0)) * (ig * xc)
        return a, bt

    def scan_tile(i, d, carry):
        a, bt = gates(i, d)

        def shift(x, kk):
            x3 = x.reshape(tl // group, group, width)
            return pltpu.roll(x3, kk if d == 0 else group - kk, 1).reshape(tl, width)

        for kk in (1, 2, 4):
            mk = rowg >= kk if d == 0 else rowg < group - kk
            bt = jnp.where(mk, a * shift(bt, kk) + bt, bt)
            a = jnp.where(mk, a * shift(a, kk), a)
        a_s[...] = a
        b_s[...] = bt
        n_groups = tl // group

        def grp(gidx, c):
            gg = gidx if d == 0 else n_groups - 1 - gidx
            g0 = pl.multiple_of(gg * group, group)
            h = b_s[pl.ds(g0, group), :] + a_s[pl.ds(g0, group), :] * c
            h_s[pl.ds(g0, group), :] = h
            return h[group - 1:group, :] if d == 0 else h[0:1, :]

        return lax.fori_loop(0, n_groups, grp, carry, unroll=8)

    def fwd(i, carry):
        carry = scan_tile(i, 0, carry)
        r0 = pl.multiple_of(i * tl, tl)
        hf_s[pl.ds(r0, tl), :] = h_s[...]
        return carry

    def bwd(i, carry):
        carry = scan_tile(i, 1, carry)
        r0 = pl.multiple_of(i * tl, tl)
        y = (hf_s[pl.ds(r0, tl), :] + h_s[...]) * jax.nn.gelu(g_ref[0, pl.ds(r0, tl), :])
        o_ref[0, pl.ds(r0, tl), :] = y.astype(o_ref.dtype)
        return carry

    h0 = jnp.zeros((1, width), F32)
    lax.fori_loop(0, n_tiles, fwd, h0)
    carry = bwd(jnp.int32(0), h0)
    lax.fori_loop(1, n_tiles, lambda j, c: bwd(n_tiles - j, c), carry)


def _block_diag_dense(w):
    nd, k, c, _ = w.shape
    eye = jnp.eye(k, dtype=w.dtype)
    return jnp.einsum('dkce,kj->dkcje', w, eye).reshape(nd, k * c, k * c)


def _lru(u, conv_w, conv_b, w_a, b_a, w_x, b_x, lam, n_ctx):
    b, _, t, _ = u.shape
    width = conv_w.shape[1]
    assert n_ctx == LRU_TILE and t % LRU_TILE == 0
    n_tiles = t // LRU_TILE
    wa = _block_diag_dense(w_a).astype(BF16)
    wx = _block_diag_dense(w_x).astype(BF16)
    kern = functools.partial(_lru_kernel, n_tiles=n_tiles)
    full = lambda shape: pl.BlockSpec(shape, lambda bi: (0,) * len(shape))
    return pl.pallas_call(
        kern,
        grid=(b,),
        in_specs=[
            pl.BlockSpec((1, None, t, width), lambda bi: (bi, 0, 0, 0)),
            pl.BlockSpec((1, None, t, width), lambda bi: (bi, 1, 0, 0)),
            full((CONV_W, width)),
            full((1, width)),
            full((2, width, width)),
            full((2, width)),
            full((2, width, width)),
            full((2, width)),
            full((2, width)),
        ],
        out_specs=pl.BlockSpec((1, t, width), lambda bi: (bi, 0, 0)),
        out_shape=jax.ShapeDtypeStruct((b, t, width), BF16),
        scratch_shapes=[
            pltpu.VMEM((t + 2 * LRU_HALO, width), F32),
            pltpu.VMEM((t, width), F32),
            pltpu.VMEM((t, width), F32),
            pltpu.VMEM((LRU_TILE, width), F32),
            pltpu.VMEM((LRU_TILE, width), F32),
            pltpu.VMEM((LRU_TILE, width), F32),
        ],
        compiler_params=_params(("arbitrary",)),
    )(u, u, conv_w, conv_b.reshape(1, width), wa, b_a, wx, b_x, lam)


def _out_mlp_kernel(xc_ref, xl_ref, yma_ref, ymb_ref, yna_ref, ynb_ref, yla_ref, ylb_ref, mod_ref, g_ref,
                    wo_ref, w1_ref, w2_ref, oc_ref, ol_ref, *, hid_tile):
    def body(x, ym, yn, yl, o_ref):
        mod = mod_ref[0, 0]
        att = _dot(ym.astype(BF16), wo_ref[0:M_WIDTH, :])
        att = att + _dot(yn.astype(BF16), wo_ref[M_WIDTH:M_WIDTH + NA_WIDTH, :])
        att = att + _dot(yl.astype(BF16), wo_ref[M_WIDTH + NA_WIDTH:, :])
        x1 = x + mod[2:3, :] * att
        ms = jnp.mean(x1 * x1, axis=-1, keepdims=True)
        y = x1 * lax.rsqrt(ms + EPS) * g_ref[...]
        h = (y * (1.0 + mod[4:5, :]) + mod[3:4, :]).astype(BF16)
        hidden = w1_ref.shape[1]
        acc = None
        for j in range(hidden // hid_tile):
            a = jnp.maximum(_dot(h, w1_ref[:, j * hid_tile:(j + 1) * hid_tile]), 0.0)
            part = _dot((a * a).astype(BF16), w2_ref[j * hid_tile:(j + 1) * hid_tile, :])
            acc = part if acc is None else acc + part
        o_ref[0] = x1 + mod[5:6, :] * acc

    def heads(ref):
        return jnp.concatenate([ref[0, j] for j in range(ref.shape[1])], axis=1)

    def rows(a, b):
        return jnp.concatenate([a, b], axis=0)

    @pl.when(pl.program_id(1) == 0)
    def _():
        body(xc_ref[0], yma_ref[0], heads(yna_ref), yla_ref[0], oc_ref)

    @pl.when(pl.program_id(1) > 0)
    def _():
        body(xl_ref[0], rows(yma_ref[0], ymb_ref[0]), rows(heads(yna_ref), heads(ynb_ref)),
             rows(yla_ref[0], ylb_ref[0]), ol_ref)


def _out_mlp(xc, xl, ym, yn, yl, modv, g2, w_out, w1, w2, layer):
    b, n_ctx, d = xc.shape
    n_lat = xl.shape[1]
    lat_tile = 2 * ROW_TILE
    assert n_ctx == ROW_TILE and n_lat % lat_tile == 0
    hidden = w1.shape[2]
    blk_a = lambda s: jnp.maximum(2 * s - 1, 0)
    blk_b = lambda s: 2 * s
    tok = lambda w, blk: pl.BlockSpec((1, ROW_TILE, w), lambda bi, s: (bi, blk(s), 0))
    pairs = lambda blk: pl.BlockSpec((1, NA_HEADS // 2, ROW_TILE, LANES), lambda bi, s: (bi, 0, blk(s), 0))
    stream = [
        pl.BlockSpec((1, ROW_TILE, d), lambda bi, s: (bi, 0, 0)),
        pl.BlockSpec((1, lat_tile, d), lambda bi, s: (bi, jnp.maximum(s - 1, 0), 0)),
    ]
    kern = functools.partial(_out_mlp_kernel, hid_tile=1024)
    return pl.pallas_call(
        kern,
        grid=(b, 1 + n_lat // lat_tile),
        in_specs=stream + [
            tok(M_WIDTH, blk_a), tok(M_WIDTH, blk_b), pairs(blk_a), pairs(blk_b),
            tok(d - M_WIDTH - NA_WIDTH, blk_a), tok(d - M_WIDTH - NA_WIDTH, blk_b),
            pl.BlockSpec((1, 1, 6, d), lambda bi, s: (bi, jnp.minimum(s, 1), 0, 0)),
            pl.BlockSpec((1, d), lambda bi, s: (0, 0)),
            _resident((None, d, d), lambda bi, s: (layer, 0, 0)),
            _resident((None, d, hidden), lambda bi, s: (layer, 0, 0)),
            _resident((None, hidden, d), lambda bi, s: (layer, 0, 0)),
        ],
        out_specs=stream,
        out_shape=[jax.ShapeDtypeStruct(xc.shape, F32), jax.ShapeDtypeStruct(xl.shape, F32)],
        compiler_params=_params(("arbitrary", "arbitrary")),
    )(xc, xl, ym, ym, yn, yn, yl, yl, modv, g2.reshape(1, d), w_out, w1, w2)


def _split_w_in(w_in):
    m_end = 4 * M_WIDTH
    na_end = m_end + N_GATES + 3 * NA_WIDTH
    main = jnp.concatenate([w_in[..., :m_end], w_in[..., na_end:], w_in[..., m_end + N_GATES:na_end]], axis=-1)
    gates = jnp.swapaxes(w_in[..., m_end:m_end + N_GATES], -1, -2)
    return main.astype(BF16), gates.astype(BF16)


def kernel(x, c, ctx, c_ctx, w_mod, b_mod, norm1_g, norm2_g, w_in, mlstm_gate_b, mlstm_norm_g, na_q_norm_g, na_k_norm_g, na_rpb, lru_conv_w, lru_conv_b, lru_w_a, lru_b_a, lru_w_x, lru_b_x, lru_lambda, w_out, w_mlp1, w_mlp2):
    b, n, d = x.shape
    n_ctx = ctx.shape[1]
    depth = w_in.shape[0]
    assert n_ctx == ROW_TILE and n % ROW_TILE == 0 and n % GRID_W == 0

    cond_rows = 8 * ((b + 1 + 7) // 8)
    cond = jnp.zeros((cond_rows, d), F32).at[:b].set(c).at[b].set(c_ctx)
    mod_all = _modulation(cond, w_mod, b_mod)
    rope_row, rope_col = _rope_tables(n // GRID_W)
    w_in_p, w_gate_t = _split_w_in(w_in)
    na_bias = _na_bias_tables(na_rpb)
    w_out_b, w1_b, w2_b = w_out.astype(BF16), w_mlp1.astype(BF16), w_mlp2.astype(BF16)

    xc, xl = ctx, x
    for l in range(depth):
        lat = mod_all[l, :b].reshape(b, 6, d)
        cx = jnp.broadcast_to(mod_all[l, b].reshape(1, 6, d), (b, 6, d))
        modv = jnp.stack([cx, lat], axis=1)
        um, ul, un, g_rows = _inproj(xc, xl, modv, norm1_g[l], w_in_p, w_gate_t, l)
        ym = _mlstm(um, g_rows, mlstm_gate_b[l], mlstm_norm_g[l], rope_row, rope_col, n_ctx)
        yn = _na(un, na_q_norm_g[l], na_k_norm_g[l], na_bias, l, n_ctx)
        yl = _lru(ul, lru_conv_w[l], lru_conv_b[l], lru_w_a[l], lru_b_a[l], lru_w_x[l], lru_b_x[l], lru_lambda[l], n_ctx)
        xc, xl = _out_mlp(xc, xl, ym, yn, yl, modv, norm2_g[l], w_out_b, w1_b, w2_b, l)
    return xl
```

```python
import functools
import math

import numpy as np
import jax
import jax.numpy as jnp
from jax import lax
from jax.experimental import pallas as pl
from jax.experimental.pallas import tpu as pltpu

F32 = jnp.float32
BF16 = jnp.bfloat16

HEAD_DIM = 64
M_HEADS = 4
M_WIDTH = M_HEADS * HEAD_DIM
NA_HEADS = 8
NA_WIDTH = NA_HEADS * HEAD_DIM
N_GATES = 4 * M_HEADS
GRID_W = 64
M_CHUNK = 128
M_AUG = HEAD_DIM + 16
M_LOCAL_GROUP = 2
M_LATENT_GROUP = 4
WIN_R = 8
WIN_C = 16
RPB_R = 2 * WIN_R - 1
RPB_C = 2 * WIN_C - 1
NA_QROWS = 4
NA_BAND = NA_QROWS + WIN_R
NA_KEY_CHUNK = 128
NA_ONES_ROWS = 16
NA_GROUP = 8
LOG2E = 1.4426950408889634
CONV_W = 4
LRU_C = 8.0
ROPE_BASE = 10000.0
EPS = 1e-6
NEG = -1e30

LANES = 128
ROW_TILE = 256
MLP_SUB = 4
LRU_TILE = 256
LRU_HALO = 8
VMEM_LIMIT = 56 * 1024 * 1024

U_MQ, U_MK, U_MV, U_MO = 0, 256, 512, 768
U_XR, U_GT = 1024, 1280
U_NQ, U_NK, U_NV = 1536, 2048, 2560
U_COLS = 3072
LRU_WIDTH = U_NQ - U_GT
NA_COL_BLOCKS = (U_COLS - U_NQ) // LANES


def _dot(a, b):
    return jnp.dot(a, b, preferred_element_type=F32)


def _dot_f32_rhs01(x, m01):
    hi = x.astype(BF16)
    lo = (x - hi.astype(F32)).astype(BF16)
    return _dot(hi, m01) + _dot(lo, m01)


def _sigmoid(x):
    return 0.5 * jnp.tanh(0.5 * x) + 0.5


def _log_sigmoid(x):
    return jnp.minimum(x, 0.0) - jnp.log1p(jnp.exp(-jnp.abs(x)))


def _softplus(x):
    return jnp.maximum(x, 0.0) + jnp.log1p(jnp.exp(-jnp.abs(x)))


def _params(sem):
    return pltpu.CompilerParams(dimension_semantics=sem, vmem_limit_bytes=VMEM_LIMIT)


def _resident(shape, index_map):
    return pl.BlockSpec(shape, index_map, pipeline_mode=pl.Buffered(1))


def _mod_kernel(a_ref, w_ref, b_ref, o_ref):
    a = a_ref[...]
    act = (a * jax.nn.sigmoid(a)).astype(BF16)
    o_ref[0] = _dot(act, w_ref[0].astype(BF16)) + b_ref[0]


def _modulation(cond, w_mod, b_mod):
    depth, d, n6 = w_mod.shape
    rows = cond.shape[0]
    tn = 1536
    return pl.pallas_call(
        _mod_kernel,
        grid=(depth, n6 // tn),
        in_specs=[
            pl.BlockSpec((rows, d), lambda l, j: (0, 0)),
            pl.BlockSpec((1, d, tn), lambda l, j: (l, 0, j)),
            pl.BlockSpec((1, 1, tn), lambda l, j: (l, 0, j)),
        ],
        out_specs=pl.BlockSpec((1, rows, tn), lambda l, j: (l, 0, j)),
        out_shape=jax.ShapeDtypeStruct((depth, rows, n6), F32),
        compiler_params=_params(("arbitrary", "arbitrary")),
    )(cond, w_mod, b_mod.reshape(depth, 1, n6))


def _stream_specs(n_ctx, d):
    assert n_ctx == ROW_TILE
    return [
        pl.BlockSpec((1, ROW_TILE, d), lambda bi, i: (bi, 0, 0)),
        pl.BlockSpec((1, ROW_TILE, d), lambda bi, i: (bi, jnp.maximum(i - 1, 0), 0)),
    ]


def _on_stream_tile(body, xc_ref, xl_ref, *outs):
    @pl.when(pl.program_id(1) == 0)
    def _():
        body(xc_ref, *[o[0] for o in outs])

    @pl.when(pl.program_id(1) > 0)
    def _():
        body(xl_ref, *[o[1] for o in outs])


def _inproj_kernel(xc_ref, xl_ref, mod_ref, g_ref, w_ref, wg_ref, um_ref, ul_ref, un_ref, gt_ref):
    def body(x_ref):
        x = x_ref[0]
        ms = jnp.mean(x * x, axis=-1, keepdims=True)
        y = x * lax.rsqrt(ms + EPS) * g_ref[...]
        mod = mod_ref[0, 0]
        h = (y * (1.0 + mod[1:2, :]) + mod[0:1, :]).astype(BF16)
        u = _dot(h, w_ref[...])
        um_ref[0] = u[:, 0:U_XR]
        for j in range(ul_ref.shape[1]):
            ul_ref[0, j] = u[:, U_XR + j * LRU_WIDTH:U_XR + (j + 1) * LRU_WIDTH]
        for j in range(un_ref.shape[1]):
            un_ref[0, j] = u[:, U_NQ + j * LANES:U_NQ + (j + 1) * LANES]
        gt_ref[0] = lax.dot_general(wg_ref[...], h, (((1,), (1,)), ((), ())), preferred_element_type=F32)

    _on_stream_tile(body, xc_ref, xl_ref)


def _inproj(xc, xl, modv, g1, w_in_p, w_gate_t, layer):
    b, n_ctx, d = xc.shape
    t = n_ctx + xl.shape[1]
    return pl.pallas_call(
        _inproj_kernel,
        grid=(b, t // ROW_TILE),
        in_specs=_stream_specs(n_ctx, d) + [
            pl.BlockSpec((1, 1, 6, d), lambda bi, i: (bi, jnp.minimum(i, 1), 0, 0)),
            pl.BlockSpec((1, d), lambda bi, i: (0, 0)),
            _resident((None, d, U_COLS), lambda bi, i: (layer, 0, 0)),
            _resident((None, N_GATES, d), lambda bi, i: (layer, 0, 0)),
        ],
        out_specs=[
            pl.BlockSpec((1, ROW_TILE, U_XR), lambda bi, i: (bi, i, 0)),
            pl.BlockSpec((1, 2, ROW_TILE, LRU_WIDTH), lambda bi, i: (bi, 0, i, 0)),
            pl.BlockSpec((1, NA_COL_BLOCKS, ROW_TILE, LANES), lambda bi, i: (bi, 0, i, 0)),
            pl.BlockSpec((1, N_GATES, ROW_TILE), lambda bi, i: (bi, 0, i)),
        ],
        out_shape=[
            jax.ShapeDtypeStruct((b, t, U_XR), F32),
            jax.ShapeDtypeStruct((b, 2, t, LRU_WIDTH), F32),
            jax.ShapeDtypeStruct((b, NA_COL_BLOCKS, t, LANES), F32),
            jax.ShapeDtypeStruct((b, N_GATES, t), F32),
        ],
        compiler_params=_params(("arbitrary", "arbitrary")),
    )(xc, xl, modv, g1.reshape(1, d), w_in_p, w_gate_t)


def _mlstm_kernel(u_ref, og_ref, gr_ref, gb_ref, ng_ref, rrow_ref, rcol_ref, o_ref,
                  qt_s, k_s, vt_s, r_s, row_s, c_s, *, n_ctx_chunks, n_chunks):
    L = M_CHUNK
    W = M_WIDTH
    D = HEAD_DIM
    groups = L // GRID_W

    for h in range(M_HEADS):
        vt_s[h, D:, :] = jnp.ones((M_AUG - D, vt_s.shape[2]), BF16)
    c_s[...] = jnp.zeros(c_s.shape, F32)

    def prep(cs, latent):
        r0s = [pl.multiple_of(c * L, L) for c in cs]
        qs = [u_ref[0, pl.ds(r0, L), U_MQ:U_MQ + W] for r0 in r0s]
        ks = [u_ref[0, pl.ds(r0, L), U_MK:U_MK + W] * D ** -0.5 for r0 in r0s]
        vts = [u_ref[0, pl.ds(r0, L), U_MV:U_MV + W].T for r0 in r0s]
        if latent:
            def table(c, i):
                first = (c - n_ctx_chunks) * groups
                return jnp.concatenate([rrow_ref[i, pl.ds(first + g, 1), :] + rcol_ref[i] for g in range(groups)], axis=0)

            def rope(x, c):
                return (x * table(c, 0) + pltpu.roll(x, W - D // 4, 1) * table(c, 1)
                        + pltpu.roll(x, D // 4, 1) * table(c, 2))

            qs = [rope(q, c) for q, c in zip(qs, cs)]
            ks = [rope(k, c) for k, c in zip(ks, cs)]
        qts = [q.T for q in qs]
        for r0, qt, k, vt in zip(r0s, qts, ks, vts):
            qt_s[:, pl.ds(r0, L)] = qt.astype(BF16)
            k_s[pl.ds(r0, L), :] = k.astype(BF16)
            for h in range(M_HEADS):
                vt_s[h, 0:D, pl.ds(r0, L)] = vt[h * D:(h + 1) * D].astype(BF16)

    assert n_ctx_chunks % M_LOCAL_GROUP == 0 and n_chunks % M_LOCAL_GROUP == 0

    def prep_group(latent):
        def body(j, carry):
            first = j * M_LOCAL_GROUP + (n_ctx_chunks if latent else 0)
            prep([first + i for i in range(M_LOCAL_GROUP)], latent)
            return carry
        return body

    lax.fori_loop(0, n_ctx_chunks // M_LOCAL_GROUP, prep_group(False), 0)
    lax.fori_loop(0, (n_chunks - n_ctx_chunks) // M_LOCAL_GROUP, prep_group(True), 0)

    si = lax.broadcasted_iota(jnp.int32, (L, L), 0)
    li = lax.broadcasted_iota(jnp.int32, (L, L), 1)
    keeps = [si <= li, si >= li]
    gbias = [jnp.broadcast_to(gb_ref[8 * d:8 * d + 8, :], (8, L)) for d in range(2)]
    zrows = jnp.zeros((D, L), BF16)

    def qt_head(qtc, h):
        parts = [zrows] * M_HEADS
        parts[h] = qtc[h * D:(h + 1) * D]
        return jnp.concatenate(parts, axis=0)

    def local(first, n_group):
        r0s = [pl.multiple_of((first + j) * L, L) for j in range(n_group)]
        scores = [[_dot(k_s[pl.ds(r0, L), :], qt_head(qt_s[:, pl.ds(r0, L)], h)) for h in range(M_HEADS)]
                  for r0 in r0s]
        vtas = [[vt_s[h, :, pl.ds(r0, L)] for h in range(M_HEADS)] for r0 in r0s]
        g8s = [[gr_ref[0, 8 * d:8 * d + 8, pl.ds(r0, L)] + gbias[d] for d in range(2)] for r0 in r0s]
        cum4s = [[pltpu.roll(_dot_f32_rhs01(_log_sigmoid(g8[d]), keeps[d].astype(BF16)), M_HEADS, 0)
                  for d in range(2)] for g8 in g8s]
        src_ts = [[(g8[d] - cum4[d]).T for d in range(2)] for g8, cum4 in zip(g8s, cum4s)]
        jdh = [(j, d, h) for j in range(n_group) for d in range(2) for h in range(M_HEADS)]
        logws = [jnp.where(keeps[d], cum4s[j][d][h:h + 1] + jnp.broadcast_to(src_ts[j][d][:, h:h + 1], (L, L)), NEG)
                 for j, d, h in jdh]
        amaxs = [lw.max(axis=0, keepdims=True) for lw in logws]
        ps = [(scores[j][h] * jnp.exp(logw - a)).astype(BF16) for (j, d, h), logw, a in zip(jdh, logws, amaxs)]
        intras = [_dot(vtas[j][h], p) for (j, d, h), p in zip(jdh, ps)]
        for (j, d, h), a, intra in zip(jdh, amaxs, intras):
            last = L - 1 if d == 0 else 0
            cl = cum4s[j][d][h:h + 1]
            a_last = jnp.broadcast_to(a[:, last:last + 1], (1, L))
            cl_last = jnp.broadcast_to(cl[:, last:last + 1], (1, L))
            w_src = jnp.exp(cl_last - cl + g8s[j][d][h:h + 1] - a_last)
            r_s[d, h * M_AUG:(h + 1) * M_AUG, pl.ds(r0s[j], L)] = intra
            for kind, row in enumerate((cl, a, cl_last, a_last, w_src)):
                row_s[d, kind, h:h + 1, pl.ds(r0s[j], L)] = row

    def local_loop(first, count, n_group):
        assert count % n_group == 0

        def body(i, carry):
            local(first + i * n_group, n_group)
            return carry

        lax.fori_loop(0, count // n_group, body, 0)

    local_loop(0, n_ctx_chunks, M_LOCAL_GROUP)
    local_loop(n_ctx_chunks, n_chunks - n_ctx_chunks, M_LATENT_GROUP)

    def step(c, d, c_olds, m_prevs):
        r0 = pl.multiple_of(c * L, L)
        kc = k_s[pl.ds(r0, L), :]
        qtc = qt_s[:, pl.ds(r0, L)]
        h_ts, c_news, m_news = [], [], []
        for h in range(M_HEADS):
            cl, a, cl_last, a_last, w_src = [row_s[d, kind, h:h + 1, pl.ds(r0, L)] for kind in range(5)]
            m_prev = m_prevs[h]
            inter = cl + m_prev
            m_t = jnp.maximum(inter, a)
            r = (jnp.exp(inter - m_t) * _dot(c_olds[h].astype(BF16), qt_head(qtc, h))
                 + jnp.exp(a - m_t) * r_s[d, h * M_AUG:(h + 1) * M_AUG, pl.ds(r0, L)])
            denom = jnp.maximum(jnp.abs(r[D:D + 1]), jnp.exp(-m_t))
            h_ts.append(r[0:D] * (1.0 / denom))
            m_new = jnp.maximum(cl_last + m_prev, a_last)
            w_old = jnp.exp(cl_last + m_prev - m_new)
            w_upd = jnp.exp(a_last - m_new)
            vw = (vt_s[h, :, pl.ds(r0, L)].astype(F32) * w_src).astype(BF16)
            scale = lambda w: jnp.concatenate([w] * (W // L), axis=1)
            c_news.append(scale(w_old) * c_olds[h] + scale(w_upd) * _dot(vw, kc))
            m_news.append(m_new)
        return r0, h_ts, c_news, m_news

    def body(i, m_all):
        chunks = [i, jnp.where(i < n_ctx_chunks, n_ctx_chunks - 1 - i, n_chunks - 1 + n_ctx_chunks - i)]
        c_olds = [[c_s[d, h] for h in range(M_HEADS)] for d in range(2)]
        outs = [step(chunks[d], d, c_olds[d], m_all[d]) for d in range(2)]
        for d in range(2):
            r0, h_ts, c_news, _ = outs[d]
            for h in range(M_HEADS):
                r_s[d, h * M_AUG:h * M_AUG + D, pl.ds(r0, L)] = h_ts[h]
                c_s[d, h] = c_news[h]
        return tuple(tuple(outs[d][3]) for d in range(2))

    m0 = tuple(tuple(jnp.zeros((1, L), F32) for _ in range(M_HEADS)) for _ in range(2))
    lax.fori_loop(0, n_chunks, body, m0)

    ng = jnp.broadcast_to(ng_ref[...], (W, L))

    def post(j, carry):
        r0s = [pl.multiple_of((j * M_LOCAL_GROUP + i) * L, L) for i in range(M_LOCAL_GROUP)]
        hts = [[r_s[0, h * M_AUG:h * M_AUG + D, pl.ds(r0, L)] + r_s[1, h * M_AUG:h * M_AUG + D, pl.ds(r0, L)]
                for h in range(M_HEADS)] for r0 in r0s]
        scales = [[lax.rsqrt(jnp.sum(ht * ht, axis=0, keepdims=True) * (1.0 / D) + EPS) for ht in hs] for hs in hts]
        hnts = [(jnp.concatenate([ht * sc for ht, sc in zip(hs, scs)], axis=0) * ng).T for hs, scs in zip(hts, scales)]
        for r0, hnt in zip(r0s, hnts):
            o_ref[0, pl.ds(r0, L), :] = (_sigmoid(og_ref[0, pl.ds(r0, L), :]) * hnt).astype(o_ref.dtype)
        return carry

    lax.fori_loop(0, n_chunks // M_LOCAL_GROUP, post, 0)


def _rope_tables(n_rows):
    nf = HEAD_DIM // 4
    inv = ROPE_BASE ** (-jnp.arange(nf, dtype=F32) / nf)
    d = np.arange(M_WIDTH) % HEAD_DIM
    f_idx = d % nf
    is_row = d < HEAD_DIM // 2
    first_quarter = (d % (HEAD_DIM // 2)) < nf

    def tables(pos, sel):
        ang = pos[:, None] * inv[None, :]
        ang = ang[:, f_idx]
        cos = jnp.where(sel, jnp.cos(ang), 0.0)
        sin = jnp.where(sel, jnp.sin(ang), 0.0)
        sin_a = jnp.where(first_quarter, -sin, 0.0)
        sin_b = jnp.where(first_quarter, 0.0, sin)
        return jnp.stack([cos, sin_a, sin_b]).astype(F32)

    row_t = tables(jnp.arange(n_rows, dtype=F32), is_row)
    col_t = tables(jnp.arange(GRID_W, dtype=F32), ~is_row)
    return row_t, col_t


def _mlstm(u, g_rows, gate_b, norm_g, rope_row, rope_col, n_ctx):
    b, t, _ = u.shape
    assert t % M_CHUNK == 0 and n_ctx % M_CHUNK == 0 and M_CHUNK % GRID_W == 0
    n_chunks = t // M_CHUNK
    kern = functools.partial(_mlstm_kernel, n_ctx_chunks=n_ctx // M_CHUNK, n_chunks=n_chunks)
    n_rows = rope_row.shape[1]
    return pl.pallas_call(
        kern,
        grid=(b,),
        in_specs=[
            pl.BlockSpec((1, t, U_MO), lambda bi: (bi, 0, 0), pipeline_mode=pl.Buffered(1)),
            pl.BlockSpec((1, t, M_WIDTH), lambda bi: (bi, 0, U_MO // M_WIDTH)),
            pl.BlockSpec((1, N_GATES, t), lambda bi: (bi, 0, 0)),
            pl.BlockSpec((N_GATES, 1), lambda bi: (0, 0)),
            pl.BlockSpec((M_WIDTH, 1), lambda bi: (0, 0)),
            pl.BlockSpec((3, n_rows, M_WIDTH), lambda bi: (0, 0, 0)),
            pl.BlockSpec((3, GRID_W, M_WIDTH), lambda bi: (0, 0, 0)),
        ],
        out_specs=pl.BlockSpec((1, t, M_WIDTH), lambda bi: (bi, 0, 0)),
        out_shape=jax.ShapeDtypeStruct((b, t, M_WIDTH), BF16),
        scratch_shapes=[
            pltpu.VMEM((M_WIDTH, t), BF16),
            pltpu.VMEM((t, M_WIDTH), BF16),
            pltpu.VMEM((M_HEADS, M_AUG, t), BF16),
            pltpu.VMEM((2, M_HEADS * M_AUG, t), F32),
            pltpu.VMEM((2, 5, 8, t), F32),
            pltpu.VMEM((2, M_HEADS, M_AUG, M_WIDTH), F32),
        ],
        compiler_params=_params(("arbitrary",)),
    )(u, u, g_rows, gate_b.reshape(N_GATES, 1), norm_g.reshape(M_WIDTH, 1), rope_row, rope_col)


def _na_kernel(q_ref, k_ref, v_ref, qg_ref, kg_ref, bias_ref, o_ref, qt_s, kz_s, vt_s, s_s, *, n_ctx, n_rows):
    t = q_ref.shape[1]
    lane = lax.broadcasted_iota(jnp.int32, (1, LANES), 1)
    half = [lane < HEAD_DIM, lane >= HEAD_DIM]
    hi = lax.broadcasted_iota(jnp.int32, (LANES, LANES), 0) // HEAD_DIM
    hj = lax.broadcasted_iota(jnp.int32, (LANES, LANES), 1) // HEAD_DIM
    same_head = (hi == hj).astype(BF16)
    qg = qg_ref[...] * (HEAD_DIM ** -0.5 * LOG2E)
    kg = kg_ref[...]
    qblk = NA_QROWS * GRID_W
    kc = NA_KEY_CHUNK
    sub = 8

    for hh in range(2):
        vt_s[hh, HEAD_DIM:, :] = jnp.ones((vt_s.shape[1] - HEAD_DIM, t), BF16)

    def prep(r0s):
        qs = [q_ref[0, pl.ds(r0, ROW_TILE), :] for r0 in r0s]
        ks = [k_ref[0, pl.ds(r0, ROW_TILE), :] for r0 in r0s]
        vts = [v_ref[0, pl.ds(r0, ROW_TILE), :].T for r0 in r0s]
        qms = [_dot_f32_rhs01(q * q, same_head) * (1.0 / HEAD_DIM) for q in qs]
        kms = [_dot_f32_rhs01(k * k, same_head) * (1.0 / HEAD_DIM) for k in ks]
        qnts = [(q * lax.rsqrt(m + EPS) * qg).T for q, m in zip(qs, qms)]
        kns = [k * lax.rsqrt(m + EPS) * kg for k, m in zip(ks, kms)]
        for r0, qnt, kn, vt in zip(r0s, qnts, kns, vts):
            qt_s[:, pl.ds(r0, ROW_TILE)] = qnt.astype(BF16)
            for hh in range(2):
                kz_s[hh, pl.ds(r0, ROW_TILE), :] = jnp.where(half[hh], kn, 0.0).astype(BF16)
                vt_s[hh, 0:HEAD_DIM, pl.ds(r0, ROW_TILE)] = vt[hh * HEAD_DIM:(hh + 1) * HEAD_DIM].astype(BF16)

    n_tiles = t // ROW_TILE

    def prep_pair(j, carry):
        prep([pl.multiple_of((2 * j + i) * ROW_TILE, ROW_TILE) for i in range(2)])
        return carry

    lax.fori_loop(0, n_tiles // 2, prep_pair, 0)
    if n_tiles % 2:
        prep([(n_tiles - 1) * ROW_TILE])

    def attend(blocks):
        jobs = [(b, hh) for b in range(len(blocks)) for hh in range(2)]
        qts = [qt_s[:, pl.ds(q0, qblk)] for q0, _ in blocks]
        n_chunks = len(blocks[0][1])
        mx = [None] * len(jobs)
        for ci in range(n_chunks):
            for ji, (b, hh) in enumerate(jobs):
                k0, tab = blocks[b][1][ci]
                s = _dot(kz_s[hh, pl.ds(k0, kc), :], qts[b])
                if tab is not None:
                    s = s + bias_ref[0, hh, tab[0], tab[1]:tab[1] + kc, :]
                s_s[ji, ci * kc:(ci + 1) * kc, :] = s
                cm = functools.reduce(jnp.maximum, [s[i * sub:(i + 1) * sub] for i in range(kc // sub)])
                mx[ji] = cm if mx[ji] is None else jnp.maximum(mx[ji], cm)
        ms = [m.max(axis=0, keepdims=True) for m in mx]
        accs = [None] * len(jobs)
        for ci in range(n_chunks):
            for ji, (b, hh) in enumerate(jobs):
                k0, _ = blocks[b][1][ci]
                p = jnp.exp2(s_s[ji, ci * kc:(ci + 1) * kc, :] - ms[ji]).astype(BF16)
                pv = _dot(vt_s[hh, :, pl.ds(k0, kc)], p)
                accs[ji] = pv if accs[ji] is None else accs[ji] + pv
        outs = []
        for b in range(len(blocks)):
            heads = [accs[2 * b + hh][0:HEAD_DIM] / accs[2 * b + hh][HEAD_DIM:HEAD_DIM + 1] for hh in range(2)]
            outs.append(jnp.concatenate(heads, axis=0).T)
        return outs

    ctx_chunks = [(ci * kc, None) for ci in range(n_ctx // kc)]
    o_ref[0, 0:n_ctx, :] = attend([(0, ctx_chunks)])[0].astype(o_ref.dtype)

    n_blocks = n_rows // NA_QROWS

    def group(gi, carry):
        blocks = []
        for j in range(NA_GROUP):
            bi = gi * NA_GROUP + j
            r0 = bi * NA_QROWS
            sb = jnp.clip(r0 - WIN_R // 2, 0, n_rows - NA_BAND)
            case = jnp.where(bi == 0, 0, jnp.where(bi == n_blocks - 1, 2, 1))
            q0 = pl.multiple_of(n_ctx + r0 * GRID_W, qblk)
            k0 = pl.multiple_of(n_ctx + sb * GRID_W, qblk)
            local = [(pl.multiple_of(k0 + ci * kc, kc), (case, ci * kc)) for ci in range(NA_BAND * GRID_W // kc)]
            blocks.append((q0, local + ctx_chunks))
        for (q0, _), out in zip(blocks, attend(blocks)):
            o_ref[0, pl.ds(q0, qblk), :] = out.astype(o_ref.dtype)
        return carry

    lax.fori_loop(0, n_blocks // NA_GROUP, group, 0)


def _na_bias_tables(rpb):
    n_layers, n_heads = rpb.shape[:2]
    cols = np.arange(GRID_W)
    col0 = np.clip(cols - WIN_C // 2, 0, GRID_W - WIN_C)
    valid = (cols[:, None] >= col0[None, :]) & (cols[:, None] < col0[None, :] + WIN_C)
    dc = cols[:, None] - cols[None, :] + WIN_C - 1
    col_sel = ((dc[..., None] == np.arange(RPB_C)) & valid[..., None]).astype(np.float32)
    toep = jnp.einsum('lhrd,xcd->lhrxc', rpb, col_sel, precision=lax.Precision.HIGHEST)
    toep = jnp.where(valid, toep * LOG2E, NEG * LOG2E).astype(F32)
    nk, nq = NA_BAND * GRID_W, NA_QROWS * GRID_W
    return pl.pallas_call(
        _na_bias_kernel,
        grid=(n_layers * n_heads,),
        in_specs=[pl.BlockSpec((1, RPB_R, GRID_W, GRID_W), lambda i: (i, 0, 0, 0))],
        out_specs=pl.BlockSpec((1, 3, nk, nq), lambda i: (i, 0, 0, 0)),
        out_shape=jax.ShapeDtypeStruct((n_layers * n_heads, 3, nk, nq), F32),
        compiler_params=_params(("arbitrary",)),
    )(toep.reshape(n_layers * n_heads, RPB_R, GRID_W, GRID_W)).reshape(n_layers, n_heads, 3, nk, nq)


def _na_bias_kernel(t_ref, o_ref):
    blank = jnp.full((GRID_W, GRID_W), NEG * LOG2E, F32)
    for case in range(3):
        for jj in range(NA_BAND):
            tiles = []
            for i in range(NA_QROWS):
                off = (0, i, NA_BAND - WIN_R)[case]
                rel = (jj - i, jj - i - WIN_R // 2, jj - i - NA_BAND + NA_QROWS)[case]
                tiles.append(t_ref[0, rel + WIN_R - 1] if off <= jj < off + WIN_R else blank)
            o_ref[0, case, jj * GRID_W:(jj + 1) * GRID_W, :] = jnp.concatenate(tiles, axis=1)


def _na(u, qn_g, kn_g, bias, layer, n_ctx):
    b, _, t, _ = u.shape
    n_rows = (t - n_ctx) // GRID_W
    assert n_rows % (NA_QROWS * NA_GROUP) == 0 and n_rows >= NA_BAND and n_ctx == NA_QROWS * GRID_W
    pairs = NA_HEADS // 2
    nk, nq = NA_BAND * GRID_W, NA_QROWS * GRID_W
    bias = bias.reshape(bias.shape[0], pairs, 2, 3, nk, nq)
    qg = jnp.tile(qn_g, 2).reshape(1, LANES)
    kg = jnp.tile(kn_g, 2).reshape(1, LANES)
    kern = functools.partial(_na_kernel, n_ctx=n_ctx, n_rows=n_rows)
    col = lambda which: pl.BlockSpec((1, None, t, LANES), lambda hp, bi: (bi, which * pairs + hp, 0, 0))
    return pl.pallas_call(
        kern,
        grid=(pairs, b),
        in_specs=[
            col(0), col(1), col(2),
            pl.BlockSpec((1, LANES), lambda hp, bi: (0, 0)),
            pl.BlockSpec((1, LANES), lambda hp, bi: (0, 0)),
            pl.BlockSpec((None, 1, 2, 3, nk, nq), lambda hp, bi: (layer, hp, 0, 0, 0, 0)),
        ],
        out_specs=col(0),
        out_shape=jax.ShapeDtypeStruct((b, pairs, t, LANES), BF16),
        scratch_shapes=[
            pltpu.VMEM((LANES, t), BF16),
            pltpu.VMEM((2, t, LANES), BF16),
            pltpu.VMEM((2, HEAD_DIM + NA_ONES_ROWS, t), BF16),
            pltpu.VMEM((2 * NA_GROUP, nk + n_ctx, nq), F32),
        ],
        compiler_params=_params(("arbitrary", "arbitrary")),
    )(u, u, u, qg, kg, bias)


def _lru_kernel(x_ref, g_ref, cw_ref, cb_ref, wa_ref, ba_ref, wx_ref, bx_ref, lam_ref, o_ref,
                xp_s, xc_s, hf_s, a_s, b_s, h_s, *, n_tiles):
    tl = LRU_TILE
    halo = LRU_HALO
    t = x_ref.shape[1]
    width = x_ref.shape[2]
    group = 8

    xp_s[0:halo, :] = jnp.zeros((halo, width), F32)
    xp_s[halo + t:halo + t + halo, :] = jnp.zeros((halo, width), F32)

    def fill(i, carry):
        r0 = pl.multiple_of(i * tl, tl)
        xp_s[pl.ds(pl.multiple_of(r0 + halo, halo), tl), :] = x_ref[0, pl.ds(r0, tl), :]
        return carry

    lax.fori_loop(0, n_tiles, fill, 0)

    row = lax.broadcasted_iota(jnp.int32, (tl, 1), 0)
    rowg = row % group
    cw = cw_ref[...]
    cb = cb_ref[...]

    def conv_tile(i):
        r0 = pl.multiple_of(i * tl, tl)
        win = xp_s[pl.ds(r0, tl + 2 * halo), :]
        n_win = tl + 2 * halo
        xm2 = pltpu.roll(win, 2, 0)[halo:halo + tl]
        xm1 = pltpu.roll(win, 1, 0)[halo:halo + tl]
        x0 = win[halo:halo + tl]
        xp1 = pltpu.roll(win, n_win - 1, 0)[halo:halo + tl]
        lo = i <= 1
        hi = i == 0
        xm2 = jnp.where(lo & (row < 2), 0.0, xm2)
        xm1 = jnp.where(lo & (row < 1), 0.0, xm1)
        xp1 = jnp.where(hi & (row == tl - 1), 0.0, xp1)
        return xm2 * cw[0:1] + xm1 * cw[1:2] + x0 * cw[2:3] + xp1 * cw[3:4] + cb

    def gates(i, d):
        r0 = pl.multiple_of(i * tl, tl)
        if d == 0:
            xc = conv_tile(i)
            xc_s[pl.ds(r0, tl), :] = xc
        else:
            xc = xc_s[pl.ds(r0, tl), :]
        xb = xc.astype(BF16)
        r = _sigmoid(_dot(xb, wa_ref[d]) + ba_ref[d:d + 1, :])
        ig = _sigmoid(_dot(xb, wx_ref[d]) + bx_ref[d:d + 1, :])
        log_a = -LRU_C * r * _softplus(-lam_ref[d:d + 1, :])
        a = jnp.exp(log_a)
        bt = jnp.sqrt(-jnp.tanh(log_a) * (a * a + 1.0)) * (ig * xc)
        return a, bt

    def scan_tile(i, d, carry):
        a, bt = gates(i, d)

        def shift(x, kk):
            x3 = x.reshape(tl // group, group, width)
            return pltpu.roll(x3, kk if d == 0 else group - kk, 1).reshape(tl, width)

        for kk in (1, 2, 4):
            mk = rowg >= kk if d == 0 else rowg < group - kk
            bt = jnp.where(mk, a * shift(bt, kk) + bt, bt)
            a = jnp.where(mk, a * shift(a, kk), a)
        a_s[...] = a
        b_s[...] = bt
        n_groups = tl // group

        def grp(gidx, c):
            gg = gidx if d == 0 else n_groups - 1 - gidx
            g0 = pl.multiple_of(gg * group, group)
            h = b_s[pl.ds(g0, group), :] + a_s[pl.ds(g0, group), :] * c
            h_s[pl.ds(g0, group), :] = h
            return h[group - 1:group, :] if d == 0 else h[0:1, :]

        return lax.fori_loop(0, n_groups, grp, carry, unroll=8)

    def fwd(i, carry):
        carry = scan_tile(i, 0, carry)
        r0 = pl.multiple_of(i * tl, tl)
        hf_s[pl.ds(r0, tl), :] = h_s[...]
        return carry

    def bwd(i, carry):
        carry = scan_tile(i, 1, carry)
        r0 = pl.multiple_of(i * tl, tl)
        y = (hf_s[pl.ds(r0, tl), :] + h_s[...]) * jax.nn.gelu(g_ref[0, pl.ds(r0, tl), :])
        o_ref[0, pl.ds(r0, tl), :] = y.astype(o_ref.dtype)
        return carry

    h0 = jnp.zeros((1, width), F32)
    lax.fori_loop(0, n_tiles, fwd, h0)
    carry = bwd(jnp.int32(0), h0)
    lax.fori_loop(1, n_tiles, lambda j, c: bwd(n_tiles - j, c), carry)


def _block_diag_dense(w):
    nd, k, c, _ = w.shape
    eye = jnp.eye(k, dtype=w.dtype)
    return jnp.einsum('dkce,kj->dkcje', w, eye).reshape(nd, k * c, k * c)


def _lru(u, conv_w, conv_b, w_a, b_a, w_x, b_x, lam, n_ctx):
    b, _, t, _ = u.shape
    width = conv_w.shape[1]
    assert n_ctx == LRU_TILE and t % LRU_TILE == 0
    n_tiles = t // LRU_TILE
    wa = _block_diag_dense(w_a).astype(BF16)
    wx = _block_diag_dense(w_x).astype(BF16)
    kern = functools.partial(_lru_kernel, n_tiles=n_tiles)
    full = lambda shape: pl.BlockSpec(shape, lambda bi: (0,) * len(shape))
    return pl.pallas_call(
        kern,
        grid=(b,),
        in_specs=[
            pl.BlockSpec((1, None, t, width), lambda bi: (bi, 0, 0, 0)),
            pl.BlockSpec((1, None, t, width), lambda bi: (bi, 1, 0, 0)),
            full((CONV_W, width)),
            full((1, width)),
            full((2, width, width)),
            full((2, width)),
            full((2, width, width)),
            full((2, width)),
            full((2, width)),
        ],
        out_specs=pl.BlockSpec((1, t, width), lambda bi: (bi, 0, 0)),
        out_shape=jax.ShapeDtypeStruct((b, t, width), BF16),
        scratch_shapes=[
            pltpu.VMEM((t + 2 * LRU_HALO, width), F32),
            pltpu.VMEM((t, width), F32),
            pltpu.VMEM((t, width), F32),
            pltpu.VMEM((LRU_TILE, width), F32),
            pltpu.VMEM((LRU_TILE, width), F32),
            pltpu.VMEM((LRU_TILE, width), F32),
        ],
        compiler_params=_params(("arbitrary",)),
    )(u, u, conv_w, conv_b.reshape(1, width), wa, b_a, wx, b_x, lam)


def _out_mlp_kernel(*refs, hid_tile):
    xc_ref, xl_ref = refs[0:2]
    ym_refs, yn_refs, yl_refs = (refs[2 + k * MLP_SUB:2 + (k + 1) * MLP_SUB] for k in range(3))
    mod_ref, g_ref, wo_ref, w1_ref, w2_ref, oc_ref, ol_ref = refs[2 + 3 * MLP_SUB:]

    def body(x, ym, yn, yl, o_ref):
        mod = mod_ref[0, 0]
        att = _dot(ym.astype(BF16), wo_ref[0:M_WIDTH, :])
        att = att + _dot(yn.astype(BF16), wo_ref[M_WIDTH:M_WIDTH + NA_WIDTH, :])
        att = att + _dot(yl.astype(BF16), wo_ref[M_WIDTH + NA_WIDTH:, :])
        x1 = x + mod[2:3, :] * att
        ms = jnp.mean(x1 * x1, axis=-1, keepdims=True)
        y = x1 * lax.rsqrt(ms + EPS) * g_ref[...]
        h = (y * (1.0 + mod[4:5, :]) + mod[3:4, :]).astype(BF16)
        hidden = w1_ref.shape[1]
        acc = None
        for j in range(hidden // hid_tile):
            a = jnp.maximum(_dot(h, w1_ref[:, j * hid_tile:(j + 1) * hid_tile]), 0.0)
            part = _dot((a * a).astype(BF16), w2_ref[j * hid_tile:(j + 1) * hid_tile, :])
            acc = part if acc is None else acc + part
        o_ref[0] = x1 + mod[5:6, :] * acc

    def heads(ref):
        return jnp.concatenate([ref[0, j] for j in range(ref.shape[1])], axis=1)

    def rows(tiles):
        return jnp.concatenate(tiles, axis=0)

    @pl.when(pl.program_id(1) == 0)
    def _():
        body(xc_ref[0], ym_refs[0][0], heads(yn_refs[0]), yl_refs[0][0], oc_ref)

    @pl.when(pl.program_id(1) > 0)
    def _():
        body(xl_ref[0], rows([r[0] for r in ym_refs]), rows([heads(r) for r in yn_refs]),
             rows([r[0] for r in yl_refs]), ol_ref)


def _out_mlp(xc, xl, ym, yn, yl, modv, g2, w_out, w1, w2, layer):
    b, n_ctx, d = xc.shape
    n_lat = xl.shape[1]
    lat_tile = MLP_SUB * ROW_TILE
    assert n_ctx == ROW_TILE and n_lat % lat_tile == 0
    hidden = w1.shape[2]
    blk = lambda j: (lambda s: jnp.maximum(MLP_SUB * (s - 1) + 1 + j, 0))
    tok = lambda w: [pl.BlockSpec((1, ROW_TILE, w), lambda bi, s, f=blk(j): (bi, f(s), 0)) for j in range(MLP_SUB)]
    pairs = [pl.BlockSpec((1, NA_HEADS // 2, ROW_TILE, LANES), lambda bi, s, f=blk(j): (bi, 0, f(s), 0))
             for j in range(MLP_SUB)]
    stream = [
        pl.BlockSpec((1, ROW_TILE, d), lambda bi, s: (bi, 0, 0)),
        pl.BlockSpec((1, lat_tile, d), lambda bi, s: (bi, jnp.maximum(s - 1, 0), 0)),
    ]
    kern = functools.partial(_out_mlp_kernel, hid_tile=1024)
    return pl.pallas_call(
        kern,
        grid=(b, 1 + n_lat // lat_tile),
        in_specs=stream + tok(M_WIDTH) + pairs + tok(d - M_WIDTH - NA_WIDTH) + [
            pl.BlockSpec((1, 1, 6, d), lambda bi, s: (bi, jnp.minimum(s, 1), 0, 0)),
            pl.BlockSpec((1, d), lambda bi, s: (0, 0)),
            _resident((None, d, d), lambda bi, s: (layer, 0, 0)),
            _resident((None, d, hidden), lambda bi, s: (layer, 0, 0)),
            _resident((None, hidden, d), lambda bi, s: (layer, 0, 0)),
        ],
        out_specs=stream,
        out_shape=[jax.ShapeDtypeStruct(xc.shape, F32), jax.ShapeDtypeStruct(xl.shape, F32)],
        compiler_params=_params(("arbitrary", "arbitrary")),
    )(xc, xl, *([ym] * MLP_SUB + [yn] * MLP_SUB + [yl] * MLP_SUB), modv, g2.reshape(1, d), w_out, w1, w2)


def _split_w_in(w_in):
    m_end = 4 * M_WIDTH
    na_end = m_end + N_GATES + 3 * NA_WIDTH
    main = jnp.concatenate([w_in[..., :m_end], w_in[..., na_end:], w_in[..., m_end + N_GATES:na_end]], axis=-1)
    gates = jnp.swapaxes(w_in[..., m_end:m_end + N_GATES], -1, -2)
    return main.astype(BF16), gates.astype(BF16)


def kernel(x, c, ctx, c_ctx, w_mod, b_mod, norm1_g, norm2_g, w_in, mlstm_gate_b, mlstm_norm_g, na_q_norm_g, na_k_norm_g, na_rpb, lru_conv_w, lru_conv_b, lru_w_a, lru_b_a, lru_w_x, lru_b_x, lru_lambda, w_out, w_mlp1, w_mlp2):
    b, n, d = x.shape
    n_ctx = ctx.shape[1]
    depth = w_in.shape[0]
    assert n_ctx == ROW_TILE and n % ROW_TILE == 0 and n % GRID_W == 0

    cond_rows = 8 * ((b + 1 + 7) // 8)
    cond = jnp.zeros((cond_rows, d), F32).at[:b].set(c).at[b].set(c_ctx)
    mod_all = _modulation(cond, w_mod, b_mod)
    rope_row, rope_col = _rope_tables(n // GRID_W)
    w_in_p, w_gate_t = _split_w_in(w_in)
    na_bias = _na_bias_tables(na_rpb)
    w_out_b, w1_b, w2_b = w_out.astype(BF16), w_mlp1.astype(BF16), w_mlp2.astype(BF16)

    xc, xl = ctx, x
    for l in range(depth):
        lat = mod_all[l, :b].reshape(b, 6, d)
        cx = jnp.broadcast_to(mod_all[l, b].reshape(1, 6, d), (b, 6, d))
        modv = jnp.stack([cx, lat], axis=1)
        um, ul, un, g_rows = _inproj(xc, xl, modv, norm1_g[l], w_in_p, w_gate_t, l)
        ym = _mlstm(um, g_rows, mlstm_gate_b[l], mlstm_norm_g[l], rope_row, rope_col, n_ctx)
        yn = _na(un, na_q_norm_g[l], na_k_norm_g[l], na_bias, l, n_ctx)
        yl = _lru(ul, lru_conv_w[l], lru_conv_b[l], lru_w_a[l], lru_b_a[l], lru_w_x[l], lru_b_x[l], lru_lambda[l], n_ctx)
        xc, xl = _out_mlp(xc, xl, ym, yn, yl, modv, norm2_g[l], w_out_b, w1_b, w2_b, l)
    return xl
```

```python
import functools
import math

import numpy as np
import jax
import jax.numpy as jnp
from jax import lax
from jax.experimental import pallas as pl
from jax.experimental.pallas import tpu as pltpu

F32 = jnp.float32
BF16 = jnp.bfloat16

HEAD_DIM = 64
M_HEADS = 4
M_WIDTH = M_HEADS * HEAD_DIM
NA_HEADS = 8
NA_WIDTH = NA_HEADS * HEAD_DIM
N_GATES = 4 * M_HEADS
GRID_W = 64
M_CHUNK = 128
M_AUG = HEAD_DIM + 16
M_LOCAL_GROUP = 2
M_LATENT_GROUP = 4
WIN_R = 8
WIN_C = 16
RPB_R = 2 * WIN_R - 1
RPB_C = 2 * WIN_C - 1
NA_QROWS = 4
NA_BAND = NA_QROWS + WIN_R
NA_KEY_CHUNK = 128
NA_ONES_ROWS = 16
NA_GROUP = 8
LOG2E = 1.4426950408889634
CONV_W = 4
LRU_C = 8.0
ROPE_BASE = 10000.0
EPS = 1e-6
NEG = -1e30

LANES = 128
ROW_TILE = 256
MLP_SUB = 4
LRU_TILE = 256
LRU_HALO = 8
VMEM_LIMIT = 56 * 1024 * 1024

U_MQ, U_MK, U_MV, U_MO = 0, 256, 512, 768
U_XR, U_GT = 1024, 1280
U_NQ, U_NK, U_NV = 1536, 2048, 2560
U_COLS = 3072
LRU_WIDTH = U_NQ - U_GT
NA_COL_BLOCKS = (U_COLS - U_NQ) // LANES


def _dot(a, b):
    return jnp.dot(a, b, preferred_element_type=F32)


def _dot_f32_rhs01(x, m01):
    hi = x.astype(BF16)
    lo = (x - hi.astype(F32)).astype(BF16)
    return _dot(hi, m01) + _dot(lo, m01)


def _sigmoid(x):
    return 0.5 * jnp.tanh(0.5 * x) + 0.5


def _log_sigmoid(x):
    return jnp.minimum(x, 0.0) - jnp.log1p(jnp.exp(-jnp.abs(x)))


def _softplus(x):
    return jnp.maximum(x, 0.0) + jnp.log1p(jnp.exp(-jnp.abs(x)))


def _params(sem):
    return pltpu.CompilerParams(dimension_semantics=sem, vmem_limit_bytes=VMEM_LIMIT)


def _resident(shape, index_map):
    return pl.BlockSpec(shape, index_map, pipeline_mode=pl.Buffered(1))


def _mod_kernel(a_ref, w_ref, b_ref, o_ref):
    a = a_ref[...]
    act = (a * jax.nn.sigmoid(a)).astype(BF16)
    o_ref[0] = _dot(act, w_ref[0].astype(BF16)) + b_ref[0]


def _modulation(cond, w_mod, b_mod):
    depth, d, n6 = w_mod.shape
    rows = cond.shape[0]
    tn = 1536
    return pl.pallas_call(
        _mod_kernel,
        grid=(depth, n6 // tn),
        in_specs=[
            pl.BlockSpec((rows, d), lambda l, j: (0, 0)),
            pl.BlockSpec((1, d, tn), lambda l, j: (l, 0, j)),
            pl.BlockSpec((1, 1, tn), lambda l, j: (l, 0, j)),
        ],
        out_specs=pl.BlockSpec((1, rows, tn), lambda l, j: (l, 0, j)),
        out_shape=jax.ShapeDtypeStruct((depth, rows, n6), F32),
        compiler_params=_params(("arbitrary", "arbitrary")),
    )(cond, w_mod, b_mod.reshape(depth, 1, n6))


def _stream_specs(n_ctx, d):
    assert n_ctx == ROW_TILE
    return [
        pl.BlockSpec((1, ROW_TILE, d), lambda bi, i: (bi, 0, 0)),
        pl.BlockSpec((1, ROW_TILE, d), lambda bi, i: (bi, jnp.maximum(i - 1, 0), 0)),
    ]


def _on_stream_tile(body, xc_ref, xl_ref, *outs):
    @pl.when(pl.program_id(1) == 0)
    def _():
        body(xc_ref, *[o[0] for o in outs])

    @pl.when(pl.program_id(1) > 0)
    def _():
        body(xl_ref, *[o[1] for o in outs])


def _inproj_kernel(xc_ref, xl_ref, mod_ref, g_ref, w_ref, wg_ref, um_ref, ul_ref, un_ref, gt_ref):
    def body(x_ref):
        x = x_ref[0]
        ms = jnp.mean(x * x, axis=-1, keepdims=True)
        y = x * lax.rsqrt(ms + EPS) * g_ref[...]
        mod = mod_ref[0, 0]
        h = (y * (1.0 + mod[1:2, :]) + mod[0:1, :]).astype(BF16)
        u = _dot(h, w_ref[...])
        um_ref[0] = u[:, 0:U_XR]
        for j in range(ul_ref.shape[1]):
            ul_ref[0, j] = u[:, U_XR + j * LRU_WIDTH:U_XR + (j + 1) * LRU_WIDTH]
        for j in range(un_ref.shape[1]):
            un_ref[0, j] = u[:, U_NQ + j * LANES:U_NQ + (j + 1) * LANES]
        gt_ref[0] = lax.dot_general(wg_ref[...], h, (((1,), (1,)), ((), ())), preferred_element_type=F32)

    _on_stream_tile(body, xc_ref, xl_ref)


def _inproj(xc, xl, modv, g1, w_in_p, w_gate_t, layer):
    b, n_ctx, d = xc.shape
    t = n_ctx + xl.shape[1]
    return pl.pallas_call(
        _inproj_kernel,
        grid=(b, t // ROW_TILE),
        in_specs=_stream_specs(n_ctx, d) + [
            pl.BlockSpec((1, 1, 6, d), lambda bi, i: (bi, jnp.minimum(i, 1), 0, 0)),
            pl.BlockSpec((1, d), lambda bi, i: (0, 0)),
            _resident((None, d, U_COLS), lambda bi, i: (layer, 0, 0)),
            _resident((None, N_GATES, d), lambda bi, i: (layer, 0, 0)),
        ],
        out_specs=[
            pl.BlockSpec((1, ROW_TILE, U_XR), lambda bi, i: (bi, i, 0)),
            pl.BlockSpec((1, 2, ROW_TILE, LRU_WIDTH), lambda bi, i: (bi, 0, i, 0)),
            pl.BlockSpec((1, NA_COL_BLOCKS, ROW_TILE, LANES), lambda bi, i: (bi, 0, i, 0)),
            pl.BlockSpec((1, N_GATES, ROW_TILE), lambda bi, i: (bi, 0, i)),
        ],
        out_shape=[
            jax.ShapeDtypeStruct((b, t, U_XR), F32),
            jax.ShapeDtypeStruct((b, 2, t, LRU_WIDTH), F32),
            jax.ShapeDtypeStruct((b, NA_COL_BLOCKS, t, LANES), F32),
            jax.ShapeDtypeStruct((b, N_GATES, t), F32),
        ],
        compiler_params=_params(("arbitrary", "arbitrary")),
    )(xc, xl, modv, g1.reshape(1, d), w_in_p, w_gate_t)


def _mlstm_kernel(u_ref, og_ref, gr_ref, gb_ref, ng_ref, rrow_ref, rcol_ref, o_ref,
                  qt_s, k_s, vt_s, r_s, row_s, c_s, *, n_ctx_chunks, n_chunks):
    L = M_CHUNK
    W = M_WIDTH
    D = HEAD_DIM
    groups = L // GRID_W

    for h in range(M_HEADS):
        vt_s[h, D:, :] = jnp.ones((M_AUG - D, vt_s.shape[2]), BF16)
    c_s[...] = jnp.zeros(c_s.shape, F32)

    def prep(cs, latent):
        r0s = [pl.multiple_of(c * L, L) for c in cs]
        qs = [u_ref[0, pl.ds(r0, L), U_MQ:U_MQ + W] for r0 in r0s]
        ks = [u_ref[0, pl.ds(r0, L), U_MK:U_MK + W] * D ** -0.5 for r0 in r0s]
        vts = [u_ref[0, pl.ds(r0, L), U_MV:U_MV + W].T for r0 in r0s]
        if latent:
            def table(c, i):
                first = (c - n_ctx_chunks) * groups
                return jnp.concatenate([rrow_ref[i, pl.ds(first + g, 1), :] + rcol_ref[i] for g in range(groups)], axis=0)

            def rope(x, c):
                return (x * table(c, 0) + pltpu.roll(x, W - D // 4, 1) * table(c, 1)
                        + pltpu.roll(x, D // 4, 1) * table(c, 2))

            qs = [rope(q, c) for q, c in zip(qs, cs)]
            ks = [rope(k, c) for k, c in zip(ks, cs)]
        qts = [q.T for q in qs]
        for r0, qt, k, vt in zip(r0s, qts, ks, vts):
            qt_s[:, pl.ds(r0, L)] = qt.astype(BF16)
            k_s[pl.ds(r0, L), :] = k.astype(BF16)
            for h in range(M_HEADS):
                vt_s[h, 0:D, pl.ds(r0, L)] = vt[h * D:(h + 1) * D].astype(BF16)

    assert n_ctx_chunks % M_LOCAL_GROUP == 0 and n_chunks % M_LOCAL_GROUP == 0

    def prep_group(latent):
        def body(j, carry):
            first = j * M_LOCAL_GROUP + (n_ctx_chunks if latent else 0)
            prep([first + i for i in range(M_LOCAL_GROUP)], latent)
            return carry
        return body

    lax.fori_loop(0, n_ctx_chunks // M_LOCAL_GROUP, prep_group(False), 0)
    lax.fori_loop(0, (n_chunks - n_ctx_chunks) // M_LOCAL_GROUP, prep_group(True), 0)

    si = lax.broadcasted_iota(jnp.int32, (L, L), 0)
    li = lax.broadcasted_iota(jnp.int32, (L, L), 1)
    keeps = [si <= li, si >= li]
    gbias = [jnp.broadcast_to(gb_ref[8 * d:8 * d + 8, :], (8, L)) for d in range(2)]
    zrows = jnp.zeros((D, L), BF16)

    def qt_head(qtc, h):
        parts = [zrows] * M_HEADS
        parts[h] = qtc[h * D:(h + 1) * D]
        return jnp.concatenate(parts, axis=0)

    def local(first, n_group):
        r0s = [pl.multiple_of((first + j) * L, L) for j in range(n_group)]
        scores = [[_dot(k_s[pl.ds(r0, L), :], qt_head(qt_s[:, pl.ds(r0, L)], h)) for h in range(M_HEADS)]
                  for r0 in r0s]
        vtas = [[vt_s[h, :, pl.ds(r0, L)] for h in range(M_HEADS)] for r0 in r0s]
        g8s = [[gr_ref[0, 8 * d:8 * d + 8, pl.ds(r0, L)] + gbias[d] for d in range(2)] for r0 in r0s]
        cum4s = [[pltpu.roll(_dot_f32_rhs01(_log_sigmoid(g8[d]), keeps[d].astype(BF16)), M_HEADS, 0)
                  for d in range(2)] for g8 in g8s]
        src_ts = [[(g8[d] - cum4[d]).T for d in range(2)] for g8, cum4 in zip(g8s, cum4s)]
        jdh = [(j, d, h) for j in range(n_group) for d in range(2) for h in range(M_HEADS)]
        logws = [jnp.where(keeps[d], cum4s[j][d][h:h + 1] + jnp.broadcast_to(src_ts[j][d][:, h:h + 1], (L, L)), NEG)
                 for j, d, h in jdh]
        amaxs = [lw.max(axis=0, keepdims=True) for lw in logws]
        ps = [(scores[j][h] * jnp.exp(logw - a)).astype(BF16) for (j, d, h), logw, a in zip(jdh, logws, amaxs)]
        intras = [_dot(vtas[j][h], p) for (j, d, h), p in zip(jdh, ps)]
        for (j, d, h), a, intra in zip(jdh, amaxs, intras):
            last = L - 1 if d == 0 else 0
            cl = cum4s[j][d][h:h + 1]
            a_last = jnp.broadcast_to(a[:, last:last + 1], (1, L))
            cl_last = jnp.broadcast_to(cl[:, last:last + 1], (1, L))
            w_src = jnp.exp(cl_last - cl + g8s[j][d][h:h + 1] - a_last)
            r_s[d, h * M_AUG:(h + 1) * M_AUG, pl.ds(r0s[j], L)] = intra
            for kind, row in enumerate((cl, a, cl_last, a_last, w_src)):
                row_s[d, kind, h:h + 1, pl.ds(r0s[j], L)] = row

    def local_loop(first, count, n_group):
        assert count % n_group == 0

        def body(i, carry):
            local(first + i * n_group, n_group)
            return carry

        lax.fori_loop(0, count // n_group, body, 0)

    local_loop(0, n_ctx_chunks, M_LOCAL_GROUP)
    local_loop(n_ctx_chunks, n_chunks - n_ctx_chunks, M_LATENT_GROUP)

    def step(c, d, c_olds, m_prevs):
        r0 = pl.multiple_of(c * L, L)
        kc = k_s[pl.ds(r0, L), :]
        qtc = qt_s[:, pl.ds(r0, L)]
        h_ts, c_news, m_news = [], [], []
        rows = [[row_s[d, kind, h:h + 1, pl.ds(r0, L)] for kind in range(5)] for h in range(M_HEADS)]
        vws = [(vt_s[h, :, pl.ds(r0, L)].astype(F32) * rows[h][4]).astype(BF16) for h in range(M_HEADS)]
        upd = _dot(jnp.concatenate(vws, axis=0), kc)
        inters = [_dot(c_olds[h].astype(BF16), qt_head(qtc, h)) for h in range(M_HEADS)]
        for h in range(M_HEADS):
            cl, a, cl_last, a_last, _ = rows[h]
            m_prev = m_prevs[h]
            inter = cl + m_prev
            m_t = jnp.maximum(inter, a)
            r = (jnp.exp(inter - m_t) * inters[h]
                 + jnp.exp(a - m_t) * r_s[d, h * M_AUG:(h + 1) * M_AUG, pl.ds(r0, L)])
            denom = jnp.maximum(jnp.abs(r[D:D + 1]), jnp.exp(-m_t))
            h_ts.append(r[0:D] * (1.0 / denom))
            m_new = jnp.maximum(cl_last + m_prev, a_last)
            w_old = jnp.exp(cl_last + m_prev - m_new)
            w_upd = jnp.exp(a_last - m_new)
            scale = lambda w: jnp.concatenate([w] * (W // L), axis=1)
            c_news.append(scale(w_old) * c_olds[h] + scale(w_upd) * upd[h * M_AUG:(h + 1) * M_AUG])
            m_news.append(m_new)
        return r0, h_ts, c_news, m_news

    def body(i, m_all):
        chunks = [i, jnp.where(i < n_ctx_chunks, n_ctx_chunks - 1 - i, n_chunks - 1 + n_ctx_chunks - i)]
        c_olds = [[c_s[d, h] for h in range(M_HEADS)] for d in range(2)]
        outs = [step(chunks[d], d, c_olds[d], m_all[d]) for d in range(2)]
        for d in range(2):
            r0, h_ts, c_news, _ = outs[d]
            for h in range(M_HEADS):
                r_s[d, h * M_AUG:h * M_AUG + D, pl.ds(r0, L)] = h_ts[h]
                c_s[d, h] = c_news[h]
        return tuple(tuple(outs[d][3]) for d in range(2))

    m0 = tuple(tuple(jnp.zeros((1, L), F32) for _ in range(M_HEADS)) for _ in range(2))
    lax.fori_loop(0, n_chunks, body, m0)

    ng = jnp.broadcast_to(ng_ref[...], (W, L))

    def post(j, carry):
        r0s = [pl.multiple_of((j * M_LOCAL_GROUP + i) * L, L) for i in range(M_LOCAL_GROUP)]
        hts = [[r_s[0, h * M_AUG:h * M_AUG + D, pl.ds(r0, L)] + r_s[1, h * M_AUG:h * M_AUG + D, pl.ds(r0, L)]
                for h in range(M_HEADS)] for r0 in r0s]
        scales = [[lax.rsqrt(jnp.sum(ht * ht, axis=0, keepdims=True) * (1.0 / D) + EPS) for ht in hs] for hs in hts]
        hnts = [(jnp.concatenate([ht * sc for ht, sc in zip(hs, scs)], axis=0) * ng).T for hs, scs in zip(hts, scales)]
        for r0, hnt in zip(r0s, hnts):
            o_ref[0, pl.ds(r0, L), :] = (_sigmoid(og_ref[0, pl.ds(r0, L), :]) * hnt).astype(o_ref.dtype)
        return carry

    lax.fori_loop(0, n_chunks // M_LOCAL_GROUP, post, 0)


def _rope_tables(n_rows):
    nf = HEAD_DIM // 4
    inv = ROPE_BASE ** (-jnp.arange(nf, dtype=F32) / nf)
    d = np.arange(M_WIDTH) % HEAD_DIM
    f_idx = d % nf
    is_row = d < HEAD_DIM // 2
    first_quarter = (d % (HEAD_DIM // 2)) < nf

    def tables(pos, sel):
        ang = pos[:, None] * inv[None, :]
        ang = ang[:, f_idx]
        cos = jnp.where(sel, jnp.cos(ang), 0.0)
        sin = jnp.where(sel, jnp.sin(ang), 0.0)
        sin_a = jnp.where(first_quarter, -sin, 0.0)
        sin_b = jnp.where(first_quarter, 0.0, sin)
        return jnp.stack([cos, sin_a, sin_b]).astype(F32)

    row_t = tables(jnp.arange(n_rows, dtype=F32), is_row)
    col_t = tables(jnp.arange(GRID_W, dtype=F32), ~is_row)
    return row_t, col_t


def _mlstm(u, g_rows, gate_b, norm_g, rope_row, rope_col, n_ctx):
    b, t, _ = u.shape
    assert t % M_CHUNK == 0 and n_ctx % M_CHUNK == 0 and M_CHUNK % GRID_W == 0
    n_chunks = t // M_CHUNK
    kern = functools.partial(_mlstm_kernel, n_ctx_chunks=n_ctx // M_CHUNK, n_chunks=n_chunks)
    n_rows = rope_row.shape[1]
    return pl.pallas_call(
        kern,
        grid=(b,),
        in_specs=[
            pl.BlockSpec((1, t, U_MO), lambda bi: (bi, 0, 0), pipeline_mode=pl.Buffered(1)),
            pl.BlockSpec((1, t, M_WIDTH), lambda bi: (bi, 0, U_MO // M_WIDTH)),
            pl.BlockSpec((1, N_GATES, t), lambda bi: (bi, 0, 0)),
            pl.BlockSpec((N_GATES, 1), lambda bi: (0, 0)),
            pl.BlockSpec((M_WIDTH, 1), lambda bi: (0, 0)),
            pl.BlockSpec((3, n_rows, M_WIDTH), lambda bi: (0, 0, 0)),
            pl.BlockSpec((3, GRID_W, M_WIDTH), lambda bi: (0, 0, 0)),
        ],
        out_specs=pl.BlockSpec((1, t, M_WIDTH), lambda bi: (bi, 0, 0)),
        out_shape=jax.ShapeDtypeStruct((b, t, M_WIDTH), BF16),
        scratch_shapes=[
            pltpu.VMEM((M_WIDTH, t), BF16),
            pltpu.VMEM((t, M_WIDTH), BF16),
            pltpu.VMEM((M_HEADS, M_AUG, t), BF16),
            pltpu.VMEM((2, M_HEADS * M_AUG, t), F32),
            pltpu.VMEM((2, 5, 8, t), F32),
            pltpu.VMEM((2, M_HEADS, M_AUG, M_WIDTH), F32),
        ],
        compiler_params=_params(("arbitrary",)),
    )(u, u, g_rows, gate_b.reshape(N_GATES, 1), norm_g.reshape(M_WIDTH, 1), rope_row, rope_col)


def _na_kernel(q_ref, k_ref, v_ref, qg_ref, kg_ref, bias_ref, o_ref, qt_s, kz_s, vt_s, s_s, *, n_ctx, n_rows):
    t = q_ref.shape[1]
    lane = lax.broadcasted_iota(jnp.int32, (1, LANES), 1)
    half = [lane < HEAD_DIM, lane >= HEAD_DIM]
    hi = lax.broadcasted_iota(jnp.int32, (LANES, LANES), 0) // HEAD_DIM
    hj = lax.broadcasted_iota(jnp.int32, (LANES, LANES), 1) // HEAD_DIM
    same_head = (hi == hj).astype(BF16)
    qg = qg_ref[...] * (HEAD_DIM ** -0.5 * LOG2E)
    kg = kg_ref[...]
    qblk = NA_QROWS * GRID_W
    kc = NA_KEY_CHUNK
    sub = 8

    for hh in range(2):
        vt_s[hh, HEAD_DIM:, :] = jnp.ones((vt_s.shape[1] - HEAD_DIM, t), BF16)

    def prep(r0s):
        qs = [q_ref[0, pl.ds(r0, ROW_TILE), :] for r0 in r0s]
        ks = [k_ref[0, pl.ds(r0, ROW_TILE), :] for r0 in r0s]
        vts = [v_ref[0, pl.ds(r0, ROW_TILE), :].T for r0 in r0s]
        qms = [_dot_f32_rhs01(q * q, same_head) * (1.0 / HEAD_DIM) for q in qs]
        kms = [_dot_f32_rhs01(k * k, same_head) * (1.0 / HEAD_DIM) for k in ks]
        qnts = [(q * lax.rsqrt(m + EPS) * qg).T for q, m in zip(qs, qms)]
        kns = [k * lax.rsqrt(m + EPS) * kg for k, m in zip(ks, kms)]
        for r0, qnt, kn, vt in zip(r0s, qnts, kns, vts):
            qt_s[:, pl.ds(r0, ROW_TILE)] = qnt.astype(BF16)
            for hh in range(2):
                kz_s[hh, pl.ds(r0, ROW_TILE), :] = jnp.where(half[hh], kn, 0.0).astype(BF16)
                vt_s[hh, 0:HEAD_DIM, pl.ds(r0, ROW_TILE)] = vt[hh * HEAD_DIM:(hh + 1) * HEAD_DIM].astype(BF16)

    n_tiles = t // ROW_TILE

    def prep_pair(j, carry):
        prep([pl.multiple_of((2 * j + i) * ROW_TILE, ROW_TILE) for i in range(2)])
        return carry

    lax.fori_loop(0, n_tiles // 2, prep_pair, 0)
    if n_tiles % 2:
        prep([(n_tiles - 1) * ROW_TILE])

    def attend(blocks):
        jobs = [(b, hh) for b in range(len(blocks)) for hh in range(2)]
        qts = [qt_s[:, pl.ds(q0, qblk)] for q0, _ in blocks]
        n_chunks = len(blocks[0][1])
        mx = [None] * len(jobs)
        def continues(prev, cur):
            if prev[1] is None or cur[1] is None:
                return prev[1] is None and cur[1] is None and cur[0] == prev[0] + kc
            return cur[1][1] == prev[1][1] + kc

        runs = []
        for ci in range(n_chunks):
            if ci and continues(blocks[0][1][ci - 1], blocks[0][1][ci]):
                runs[-1].append(ci)
            else:
                runs.append([ci])
        for run in runs:
            scores = [_dot(kz_s[hh, pl.ds(blocks[b][1][run[0]][0], kc * len(run)), :], qts[b]) for b, hh in jobs]
            for ri, ci in enumerate(run):
                for ji, (b, hh) in enumerate(jobs):
                    tab = blocks[b][1][ci][1]
                    s = scores[ji][ri * kc:(ri + 1) * kc]
                    if tab is not None:
                        s = s + bias_ref[0, hh, tab[0], tab[1]:tab[1] + kc, :]
                    s_s[ji, ci * kc:(ci + 1) * kc, :] = s
                    cm = functools.reduce(jnp.maximum, [s[i * sub:(i + 1) * sub] for i in range(kc // sub)])
                    mx[ji] = cm if mx[ji] is None else jnp.maximum(mx[ji], cm)
        ms = [m.max(axis=0, keepdims=True) for m in mx]
        accs = [None] * len(jobs)
        for ci in range(n_chunks):
            for ji, (b, hh) in enumerate(jobs):
                k0, _ = blocks[b][1][ci]
                p = jnp.exp2(s_s[ji, ci * kc:(ci + 1) * kc, :] - ms[ji]).astype(BF16)
                pv = _dot(vt_s[hh, :, pl.ds(k0, kc)], p)
                accs[ji] = pv if accs[ji] is None else accs[ji] + pv
        outs = []
        for b in range(len(blocks)):
            heads = [accs[2 * b + hh][0:HEAD_DIM] / accs[2 * b + hh][HEAD_DIM:HEAD_DIM + 1] for hh in range(2)]
            outs.append(jnp.concatenate(heads, axis=0).T)
        return outs

    ctx_chunks = [(ci * kc, None) for ci in range(n_ctx // kc)]
    o_ref[0, 0:n_ctx, :] = attend([(0, ctx_chunks)])[0].astype(o_ref.dtype)

    n_blocks = n_rows // NA_QROWS

    def group(gi, carry):
        blocks = []
        for j in range(NA_GROUP):
            bi = gi * NA_GROUP + j
            r0 = bi * NA_QROWS
            sb = jnp.clip(r0 - WIN_R // 2, 0, n_rows - NA_BAND)
            case = jnp.where(bi == 0, 0, jnp.where(bi == n_blocks - 1, 2, 1))
            q0 = pl.multiple_of(n_ctx + r0 * GRID_W, qblk)
            k0 = pl.multiple_of(n_ctx + sb * GRID_W, qblk)
            local = [(pl.multiple_of(k0 + ci * kc, kc), (case, ci * kc)) for ci in range(NA_BAND * GRID_W // kc)]
            blocks.append((q0, local + ctx_chunks))
        for (q0, _), out in zip(blocks, attend(blocks)):
            o_ref[0, pl.ds(q0, qblk), :] = out.astype(o_ref.dtype)
        return carry

    lax.fori_loop(0, n_blocks // NA_GROUP, group, 0)


def _na_bias_tables(rpb):
    n_layers, n_heads = rpb.shape[:2]
    cols = np.arange(GRID_W)
    col0 = np.clip(cols - WIN_C // 2, 0, GRID_W - WIN_C)
    valid = (cols[:, None] >= col0[None, :]) & (cols[:, None] < col0[None, :] + WIN_C)
    dc = cols[:, None] - cols[None, :] + WIN_C - 1
    col_sel = ((dc[..., None] == np.arange(RPB_C)) & valid[..., None]).astype(np.float32)
    toep = jnp.einsum('lhrd,xcd->lhrxc', rpb, col_sel, precision=lax.Precision.HIGHEST)
    toep = jnp.where(valid, toep * LOG2E, NEG * LOG2E).astype(F32)
    nk, nq = NA_BAND * GRID_W, NA_QROWS * GRID_W
    return pl.pallas_call(
        _na_bias_kernel,
        grid=(n_layers * n_heads,),
        in_specs=[pl.BlockSpec((1, RPB_R, GRID_W, GRID_W), lambda i: (i, 0, 0, 0))],
        out_specs=pl.BlockSpec((1, 3, nk, nq), lambda i: (i, 0, 0, 0)),
        out_shape=jax.ShapeDtypeStruct((n_layers * n_heads, 3, nk, nq), F32),
        compiler_params=_params(("arbitrary",)),
    )(toep.reshape(n_layers * n_heads, RPB_R, GRID_W, GRID_W)).reshape(n_layers, n_heads, 3, nk, nq)


def _na_bias_kernel(t_ref, o_ref):
    blank = jnp.full((GRID_W, GRID_W), NEG * LOG2E, F32)
    for case in range(3):
        for jj in range(NA_BAND):
            tiles = []
            for i in range(NA_QROWS):
                off = (0, i, NA_BAND - WIN_R)[case]
                rel = (jj - i, jj - i - WIN_R // 2, jj - i - NA_BAND + NA_QROWS)[case]
                tiles.append(t_ref[0, rel + WIN_R - 1] if off <= jj < off + WIN_R else blank)
            o_ref[0, case, jj * GRID_W:(jj + 1) * GRID_W, :] = jnp.concatenate(tiles, axis=1)


def _na(u, qn_g, kn_g, bias, layer, n_ctx):
    b, _, t, _ = u.shape
    n_rows = (t - n_ctx) // GRID_W
    assert n_rows % (NA_QROWS * NA_GROUP) == 0 and n_rows >= NA_BAND and n_ctx == NA_QROWS * GRID_W
    pairs = NA_HEADS // 2
    nk, nq = NA_BAND * GRID_W, NA_QROWS * GRID_W
    bias = bias.reshape(bias.shape[0], pairs, 2, 3, nk, nq)
    qg = jnp.tile(qn_g, 2).reshape(1, LANES)
    kg = jnp.tile(kn_g, 2).reshape(1, LANES)
    kern = functools.partial(_na_kernel, n_ctx=n_ctx, n_rows=n_rows)
    col = lambda which: pl.BlockSpec((1, None, t, LANES), lambda hp, bi: (bi, which * pairs + hp, 0, 0))
    return pl.pallas_call(
        kern,
        grid=(pairs, b),
        in_specs=[
            col(0), col(1), col(2),
            pl.BlockSpec((1, LANES), lambda hp, bi: (0, 0)),
            pl.BlockSpec((1, LANES), lambda hp, bi: (0, 0)),
            pl.BlockSpec((None, 1, 2, 3, nk, nq), lambda hp, bi: (layer, hp, 0, 0, 0, 0)),
        ],
        out_specs=col(0),
        out_shape=jax.ShapeDtypeStruct((b, pairs, t, LANES), BF16),
        scratch_shapes=[
            pltpu.VMEM((LANES, t), BF16),
            pltpu.VMEM((2, t, LANES), BF16),
            pltpu.VMEM((2, HEAD_DIM + NA_ONES_ROWS, t), BF16),
            pltpu.VMEM((2 * NA_GROUP, nk + n_ctx, nq), F32),
        ],
        compiler_params=_params(("arbitrary", "arbitrary")),
    )(u, u, u, qg, kg, bias)


def _lru_kernel(x_ref, g_ref, cw_ref, cb_ref, wa_ref, ba_ref, wx_ref, bx_ref, lam_ref, o_ref,
                xp_s, xc_s, hf_s, a_s, b_s, h_s, *, n_tiles):
    tl = LRU_TILE
    halo = LRU_HALO
    t = x_ref.shape[1]
    width = x_ref.shape[2]
    group = 8

    xp_s[0:halo, :] = jnp.zeros((halo, width), F32)
    xp_s[halo + t:halo + t + halo, :] = jnp.zeros((halo, width), F32)

    def fill(i, carry):
        r0 = pl.multiple_of(i * tl, tl)
        xp_s[pl.ds(pl.multiple_of(r0 + halo, halo), tl), :] = x_ref[0, pl.ds(r0, tl), :]
        return carry

    lax.fori_loop(0, n_tiles, fill, 0)

    row = lax.broadcasted_iota(jnp.int32, (tl, 1), 0)
    rowg = row % group
    cw = cw_ref[...]
    cb = cb_ref[...]

    def conv_tile(i):
        r0 = pl.multiple_of(i * tl, tl)
        win = xp_s[pl.ds(r0, tl + 2 * halo), :]
        n_win = tl + 2 * halo
        xm2 = pltpu.roll(win, 2, 0)[halo:halo + tl]
        xm1 = pltpu.roll(win, 1, 0)[halo:halo + tl]
        x0 = win[halo:halo + tl]
        xp1 = pltpu.roll(win, n_win - 1, 0)[halo:halo + tl]
        lo = i <= 1
        hi = i == 0
        xm2 = jnp.where(lo & (row < 2), 0.0, xm2)
        xm1 = jnp.where(lo & (row < 1), 0.0, xm1)
        xp1 = jnp.where(hi & (row == tl - 1), 0.0, xp1)
        return xm2 * cw[0:1] + xm1 * cw[1:2] + x0 * cw[2:3] + xp1 * cw[3:4] + cb

    def gates(i, d):
        r0 = pl.multiple_of(i * tl, tl)
        if d == 0:
            xc = conv_tile(i)
            xc_s[pl.ds(r0, tl), :] = xc
        else:
            xc = xc_s[pl.ds(r0, tl), :]
        xb = xc.astype(BF16)
        r = _sigmoid(_dot(xb, wa_ref[d]) + ba_ref[d:d + 1, :])
        ig = _sigmoid(_dot(xb, wx_ref[d]) + bx_ref[d:d + 1, :])
        log_a = -LRU_C * r * _softplus(-lam_ref[d:d + 1, :])
        a = jnp.exp(log_a)
        bt = jnp.sqrt(-jnp.tanh(log_a) * (a * a + 1.0)) * (ig * xc)
        return a, bt

    def scan_tile(i, d, carry):
        a, bt = gates(i, d)

        def shift(x, kk):
            x3 = x.reshape(tl // group, group, width)
            return pltpu.roll(x3, kk if d == 0 else group - kk, 1).reshape(tl, width)

        for kk in (1, 2, 4):
            mk = rowg >= kk if d == 0 else rowg < group - kk
            bt = jnp.where(mk, a * shift(bt, kk) + bt, bt)
            a = jnp.where(mk, a * shift(a, kk), a)
        a_s[...] = a
        b_s[...] = bt
        n_groups = tl // group

        def grp(gidx, c):
            gg = gidx if d == 0 else n_groups - 1 - gidx
            g0 = pl.multiple_of(gg * group, group)
            h = b_s[pl.ds(g0, group), :] + a_s[pl.ds(g0, group), :] * c
            h_s[pl.ds(g0, group), :] = h
            return h[group - 1:group, :] if d == 0 else h[0:1, :]

        return lax.fori_loop(0, n_groups, grp, carry, unroll=8)

    def fwd(i, carry):
        carry = scan_tile(i, 0, carry)
        r0 = pl.multiple_of(i * tl, tl)
        hf_s[pl.ds(r0, tl), :] = h_s[...]
        return carry

    def bwd(i, carry):
        carry = scan_tile(i, 1, carry)
        r0 = pl.multiple_of(i * tl, tl)
        y = (hf_s[pl.ds(r0, tl), :] + h_s[...]) * jax.nn.gelu(g_ref[0, pl.ds(r0, tl), :])
        o_ref[0, pl.ds(r0, tl), :] = y.astype(o_ref.dtype)
        return carry

    h0 = jnp.zeros((1, width), F32)
    lax.fori_loop(0, n_tiles, fwd, h0)
    carry = bwd(jnp.int32(0), h0)
    lax.fori_loop(1, n_tiles, lambda j, c: bwd(n_tiles - j, c), carry)


def _block_diag_dense(w):
    nd, k, c, _ = w.shape
    eye = jnp.eye(k, dtype=w.dtype)
    return jnp.einsum('dkce,kj->dkcje', w, eye).reshape(nd, k * c, k * c)


def _lru(u, conv_w, conv_b, w_a, b_a, w_x, b_x, lam, n_ctx):
    b, _, t, _ = u.shape
    width = conv_w.shape[1]
    assert n_ctx == LRU_TILE and t % LRU_TILE == 0
    n_tiles = t // LRU_TILE
    wa = _block_diag_dense(w_a).astype(BF16)
    wx = _block_diag_dense(w_x).astype(BF16)
    kern = functools.partial(_lru_kernel, n_tiles=n_tiles)
    full = lambda shape: pl.BlockSpec(shape, lambda bi: (0,) * len(shape))
    return pl.pallas_call(
        kern,
        grid=(b,),
        in_specs=[
            pl.BlockSpec((1, None, t, width), lambda bi: (bi, 0, 0, 0)),
            pl.BlockSpec((1, None, t, width), lambda bi: (bi, 1, 0, 0)),
            full((CONV_W, width)),
            full((1, width)),
            full((2, width, width)),
            full((2, width)),
            full((2, width, width)),
            full((2, width)),
            full((2, width)),
        ],
        out_specs=pl.BlockSpec((1, t, width), lambda bi: (bi, 0, 0)),
        out_shape=jax.ShapeDtypeStruct((b, t, width), BF16),
        scratch_shapes=[
            pltpu.VMEM((t + 2 * LRU_HALO, width), F32),
            pltpu.VMEM((t, width), F32),
            pltpu.VMEM((t, width), F32),
            pltpu.VMEM((LRU_TILE, width), F32),
            pltpu.VMEM((LRU_TILE, width), F32),
            pltpu.VMEM((LRU_TILE, width), F32),
        ],
        compiler_params=_params(("arbitrary",)),
    )(u, u, conv_w, conv_b.reshape(1, width), wa, b_a, wx, b_x, lam)


def _out_mlp_kernel(*refs, hid_tile):
    xc_ref, xl_ref = refs[0:2]
    ym_refs, yn_refs, yl_refs = (refs[2 + k * MLP_SUB:2 + (k + 1) * MLP_SUB] for k in range(3))
    mod_ref, g_ref, wo_ref, w1_ref, w2_ref, oc_ref, ol_ref = refs[2 + 3 * MLP_SUB:]

    def body(x, ym, yn, yl, o_ref):
        mod = mod_ref[0, 0]
        att = _dot(ym.astype(BF16), wo_ref[0:M_WIDTH, :])
        att = att + _dot(yn.astype(BF16), wo_ref[M_WIDTH:M_WIDTH + NA_WIDTH, :])
        att = att + _dot(yl.astype(BF16), wo_ref[M_WIDTH + NA_WIDTH:, :])
        x1 = x + mod[2:3, :] * att
        ms = jnp.mean(x1 * x1, axis=-1, keepdims=True)
        y = x1 * lax.rsqrt(ms + EPS) * g_ref[...]
        h = (y * (1.0 + mod[4:5, :]) + mod[3:4, :]).astype(BF16)
        hidden = w1_ref.shape[1]
        acc = None
        for j in range(hidden // hid_tile):
            a = jnp.maximum(_dot(h, w1_ref[:, j * hid_tile:(j + 1) * hid_tile]), 0.0)
            part = _dot((a * a).astype(BF16), w2_ref[j * hid_tile:(j + 1) * hid_tile, :])
            acc = part if acc is None else acc + part
        o_ref[0] = x1 + mod[5:6, :] * acc

    def heads(ref):
        return jnp.concatenate([ref[0, j] for j in range(ref.shape[1])], axis=1)

    def rows(tiles):
        return jnp.concatenate(tiles, axis=0)

    @pl.when(pl.program_id(1) == 0)
    def _():
        body(xc_ref[0], ym_refs[0][0], heads(yn_refs[0]), yl_refs[0][0], oc_ref)

    @pl.when(pl.program_id(1) > 0)
    def _():
        body(xl_ref[0], rows([r[0] for r in ym_refs]), rows([heads(r) for r in yn_refs]),
             rows([r[0] for r in yl_refs]), ol_ref)


def _out_mlp(xc, xl, ym, yn, yl, modv, g2, w_out, w1, w2, layer):
    b, n_ctx, d = xc.shape
    n_lat = xl.shape[1]
    lat_tile = MLP_SUB * ROW_TILE
    assert n_ctx == ROW_TILE and n_lat % lat_tile == 0
    hidden = w1.shape[2]
    blk = lambda j: (lambda s: jnp.maximum(MLP_SUB * (s - 1) + 1 + j, 0))
    tok = lambda w: [pl.BlockSpec((1, ROW_TILE, w), lambda bi, s, f=blk(j): (bi, f(s), 0)) for j in range(MLP_SUB)]
    pairs = [pl.BlockSpec((1, NA_HEADS // 2, ROW_TILE, LANES), lambda bi, s, f=blk(j): (bi, 0, f(s), 0))
             for j in range(MLP_SUB)]
    stream = [
        pl.BlockSpec((1, ROW_TILE, d), lambda bi, s: (bi, 0, 0)),
        pl.BlockSpec((1, lat_tile, d), lambda bi, s: (bi, jnp.maximum(s - 1, 0), 0)),
    ]
    kern = functools.partial(_out_mlp_kernel, hid_tile=1024)
    return pl.pallas_call(
        kern,
        grid=(b, 1 + n_lat // lat_tile),
        in_specs=stream + tok(M_WIDTH) + pairs + tok(d - M_WIDTH - NA_WIDTH) + [
            pl.BlockSpec((1, 1, 6, d), lambda bi, s: (bi, jnp.minimum(s, 1), 0, 0)),
            pl.BlockSpec((1, d), lambda bi, s: (0, 0)),
            _resident((None, d, d), lambda bi, s: (layer, 0, 0)),
            _resident((None, d, hidden), lambda bi, s: (layer, 0, 0)),
            _resident((None, hidden, d), lambda bi, s: (layer, 0, 0)),
        ],
        out_specs=stream,
        out_shape=[jax.ShapeDtypeStruct(xc.shape, F32), jax.ShapeDtypeStruct(xl.shape, F32)],
        compiler_params=_params(("arbitrary", "arbitrary")),
    )(xc, xl, *([ym] * MLP_SUB + [yn] * MLP_SUB + [yl] * MLP_SUB), modv, g2.reshape(1, d), w_out, w1, w2)


def _split_w_in(w_in):
    m_end = 4 * M_WIDTH
    na_end = m_end + N_GATES + 3 * NA_WIDTH
    main = jnp.concatenate([w_in[..., :m_end], w_in[..., na_end:], w_in[..., m_end + N_GATES:na_end]], axis=-1)
    gates = jnp.swapaxes(w_in[..., m_end:m_end + N_GATES], -1, -2)
    return main.astype(BF16), gates.astype(BF16)


def kernel(x, c, ctx, c_ctx, w_mod, b_mod, norm1_g, norm2_g, w_in, mlstm_gate_b, mlstm_norm_g, na_q_norm_g, na_k_norm_g, na_rpb, lru_conv_w, lru_conv_b, lru_w_a, lru_b_a, lru_w_x, lru_b_x, lru_lambda, w_out, w_mlp1, w_mlp2):
    b, n, d = x.shape
    n_ctx = ctx.shape[1]
    depth = w_in.shape[0]
    assert n_ctx == ROW_TILE and n % ROW_TILE == 0 and n % GRID_W == 0

    cond_rows = 8 * ((b + 1 + 7) // 8)
    cond = jnp.zeros((cond_rows, d), F32).at[:b].set(c).at[b].set(c_ctx)
    mod_all = _modulation(cond, w_mod, b_mod)
    rope_row, rope_col = _rope_tables(n // GRID_W)
    w_in_p, w_gate_t = _split_w_in(w_in)
    na_bias = _na_bias_tables(na_rpb)
    w_out_b, w1_b, w2_b = w_out.astype(BF16), w_mlp1.astype(BF16), w_mlp2.astype(BF16)

    xc, xl = ctx, x
    for l in range(depth):
        lat = mod_all[l, :b].reshape(b, 6, d)
        cx = jnp.broadcast_to(mod_all[l, b].reshape(1, 6, d), (b, 6, d))
        modv = jnp.stack([cx, lat], axis=1)
        um, ul, un, g_rows = _inproj(xc, xl, modv, norm1_g[l], w_in_p, w_gate_t, l)
        ym = _mlstm(um, g_rows, mlstm_gate_b[l], mlstm_norm_g[l], rope_row, rope_col, n_ctx)
        yn = _na(un, na_q_norm_g[l], na_k_norm_g[l], na_bias, l, n_ctx)
        yl = _lru(ul, lru_conv_w[l], lru_conv_b[l], lru_w_a[l], lru_b_a[l], lru_w_x[l], lru_b_x[l], lru_lambda[l], n_ctx)
        xc, xl = _out_mlp(xc, xl, ym, yn, yl, modv, norm2_g[l], w_out_b, w1_b, w2_b, l)
    return xl
```

```python
import functools
import math

import numpy as np
import jax
import jax.numpy as jnp
from jax import lax
from jax.experimental import pallas as pl
from jax.experimental.pallas import tpu as pltpu

F32 = jnp.float32
BF16 = jnp.bfloat16

HEAD_DIM = 64
M_HEADS = 4
M_WIDTH = M_HEADS * HEAD_DIM
NA_HEADS = 8
NA_WIDTH = NA_HEADS * HEAD_DIM
N_GATES = 4 * M_HEADS
GRID_W = 64
M_CHUNK = 128
M_AUG = HEAD_DIM + 16
M_LOCAL_GROUP = 2
M_LATENT_GROUP = 4
WIN_R = 8
WIN_C = 16
RPB_R = 2 * WIN_R - 1
RPB_C = 2 * WIN_C - 1
NA_QROWS = 4
NA_BAND = NA_QROWS + WIN_R
NA_KEY_CHUNK = 128
NA_ONES_ROWS = 16
NA_GROUP = 8
LOG2E = 1.4426950408889634
CONV_W = 4
LRU_C = 8.0
ROPE_BASE = 10000.0
EPS = 1e-6
NEG = -1e30

LANES = 128
ROW_TILE = 256
MLP_SUB = 4
LRU_TILE = 256
LRU_HALO = 8
VMEM_LIMIT = 56 * 1024 * 1024

U_MQ, U_MK, U_MV, U_MO = 0, 256, 512, 768
U_XR, U_GT = 1024, 1280
U_NQ, U_NK, U_NV = 1536, 2048, 2560
U_COLS = 3072
LRU_WIDTH = U_NQ - U_GT
NA_COL_BLOCKS = (U_COLS - U_NQ) // LANES


def _dot(a, b):
    return jnp.dot(a, b, preferred_element_type=F32)


def _dot_f32_rhs01(x, m01):
    hi = x.astype(BF16)
    lo = (x - hi.astype(F32)).astype(BF16)
    return _dot(hi, m01) + _dot(lo, m01)


def _sigmoid(x):
    return 0.5 * jnp.tanh(0.5 * x) + 0.5


def _log_sigmoid(x):
    return jnp.minimum(x, 0.0) - jnp.log1p(jnp.exp(-jnp.abs(x)))


def _softplus(x):
    return jnp.maximum(x, 0.0) + jnp.log1p(jnp.exp(-jnp.abs(x)))


def _params(sem):
    return pltpu.CompilerParams(dimension_semantics=sem, vmem_limit_bytes=VMEM_LIMIT)


def _resident(shape, index_map):
    return pl.BlockSpec(shape, index_map, pipeline_mode=pl.Buffered(1))


def _mod_kernel(a_ref, w_ref, b_ref, o_ref):
    a = a_ref[...]
    act = (a * jax.nn.sigmoid(a)).astype(BF16)
    o_ref[0] = _dot(act, w_ref[0].astype(BF16)) + b_ref[0]


def _modulation(cond, w_mod, b_mod):
    depth, d, n6 = w_mod.shape
    rows = cond.shape[0]
    tn = 1536
    return pl.pallas_call(
        _mod_kernel,
        grid=(depth, n6 // tn),
        in_specs=[
            pl.BlockSpec((rows, d), lambda l, j: (0, 0)),
            pl.BlockSpec((1, d, tn), lambda l, j: (l, 0, j)),
            pl.BlockSpec((1, 1, tn), lambda l, j: (l, 0, j)),
        ],
        out_specs=pl.BlockSpec((1, rows, tn), lambda l, j: (l, 0, j)),
        out_shape=jax.ShapeDtypeStruct((depth, rows, n6), F32),
        compiler_params=_params(("arbitrary", "arbitrary")),
    )(cond, w_mod, b_mod.reshape(depth, 1, n6))


def _inproj_kernel(xc_ref, xa_ref, xb_ref, mod_ref, g_ref, w_ref, wg_ref, um_ref, ul_ref, un_ref, gt_ref):
    def half(x, mod, r0):
        ms = jnp.mean(x * x, axis=-1, keepdims=True)
        y = x * lax.rsqrt(ms + EPS) * g_ref[...]
        h = (y * (1.0 + mod[1:2, :]) + mod[0:1, :]).astype(BF16)
        u = _dot(h, w_ref[...])
        rows = slice(r0, r0 + ROW_TILE)
        um_ref[0, rows, :] = u[:, 0:U_XR]
        for j in range(ul_ref.shape[1]):
            ul_ref[0, j, rows, :] = u[:, U_XR + j * LRU_WIDTH:U_XR + (j + 1) * LRU_WIDTH]
        for j in range(un_ref.shape[1]):
            un_ref[0, j, rows, :] = u[:, U_NQ + j * LANES:U_NQ + (j + 1) * LANES]
        gt_ref[0, :, rows] = lax.dot_general(wg_ref[...], h, (((1,), (1,)), ((), ())), preferred_element_type=F32)

    mod_ctx, mod_lat = mod_ref[0, 0], mod_ref[0, 1]

    @pl.when(pl.program_id(1) == 0)
    def _():
        half(xc_ref[0], mod_ctx, 0)
        half(xb_ref[0], mod_lat, ROW_TILE)

    @pl.when(pl.program_id(1) > 0)
    def _():
        half(xa_ref[0], mod_lat, 0)
        half(xb_ref[0], mod_lat, ROW_TILE)


def _inproj(xc, xl, modv, g1, w_in_p, w_gate_t, layer):
    b, n_ctx, d = xc.shape
    assert n_ctx == ROW_TILE and xl.shape[1] % ROW_TILE == 0
    n_lat_tiles = xl.shape[1] // ROW_TILE
    t = n_ctx + xl.shape[1]
    step_rows = 2 * ROW_TILE
    return pl.pallas_call(
        _inproj_kernel,
        grid=(b, pl.cdiv(t, step_rows)),
        in_specs=[
            pl.BlockSpec((1, ROW_TILE, d), lambda bi, s: (bi, 0, 0)),
            pl.BlockSpec((1, ROW_TILE, d), lambda bi, s: (bi, jnp.maximum(2 * s - 1, 0), 0)),
            pl.BlockSpec((1, ROW_TILE, d), lambda bi, s: (bi, jnp.minimum(2 * s, n_lat_tiles - 1), 0)),
            pl.BlockSpec((1, 2, 6, d), lambda bi, s: (bi, 0, 0, 0)),
            pl.BlockSpec((1, d), lambda bi, s: (0, 0)),
            _resident((None, d, U_COLS), lambda bi, s: (layer, 0, 0)),
            _resident((None, N_GATES, d), lambda bi, s: (layer, 0, 0)),
        ],
        out_specs=[
            pl.BlockSpec((1, step_rows, U_XR), lambda bi, s: (bi, s, 0)),
            pl.BlockSpec((1, 2, step_rows, LRU_WIDTH), lambda bi, s: (bi, 0, s, 0)),
            pl.BlockSpec((1, NA_COL_BLOCKS, step_rows, LANES), lambda bi, s: (bi, 0, s, 0)),
            pl.BlockSpec((1, N_GATES, step_rows), lambda bi, s: (bi, 0, s)),
        ],
        out_shape=[
            jax.ShapeDtypeStruct((b, t, U_XR), F32),
            jax.ShapeDtypeStruct((b, 2, t, LRU_WIDTH), F32),
            jax.ShapeDtypeStruct((b, NA_COL_BLOCKS, t, LANES), F32),
            jax.ShapeDtypeStruct((b, N_GATES, t), F32),
        ],
        compiler_params=_params(("arbitrary", "arbitrary")),
    )(xc, xl, xl, modv, g1.reshape(1, d), w_in_p, w_gate_t)


def _mlstm_kernel(u_ref, og_ref, gr_ref, gb_ref, ng_ref, rrow_ref, rcol_ref, o_ref,
                  qt_s, k_s, vt_s, r_s, row_s, c_s, *, n_ctx_chunks, n_chunks):
    L = M_CHUNK
    W = M_WIDTH
    D = HEAD_DIM
    groups = L // GRID_W

    for h in range(M_HEADS):
        vt_s[h, D:, :] = jnp.ones((M_AUG - D, vt_s.shape[2]), BF16)
    c_s[...] = jnp.zeros(c_s.shape, F32)

    def prep(cs, latent):
        r0s = [pl.multiple_of(c * L, L) for c in cs]
        qs = [u_ref[0, pl.ds(r0, L), U_MQ:U_MQ + W] for r0 in r0s]
        ks = [u_ref[0, pl.ds(r0, L), U_MK:U_MK + W] * D ** -0.5 for r0 in r0s]
        vts = [u_ref[0, pl.ds(r0, L), U_MV:U_MV + W].T for r0 in r0s]
        if latent:
            def table(c, i):
                first = (c - n_ctx_chunks) * groups
                return jnp.concatenate([rrow_ref[i, pl.ds(first + g, 1), :] + rcol_ref[i] for g in range(groups)], axis=0)

            def rope(x, c):
                return (x * table(c, 0) + pltpu.roll(x, W - D // 4, 1) * table(c, 1)
                        + pltpu.roll(x, D // 4, 1) * table(c, 2))

            qs = [rope(q, c) for q, c in zip(qs, cs)]
            ks = [rope(k, c) for k, c in zip(ks, cs)]
        qts = [q.T for q in qs]
        for r0, qt, k, vt in zip(r0s, qts, ks, vts):
            qt_s[:, pl.ds(r0, L)] = qt.astype(BF16)
            k_s[pl.ds(r0, L), :] = k.astype(BF16)
            for h in range(M_HEADS):
                vt_s[h, 0:D, pl.ds(r0, L)] = vt[h * D:(h + 1) * D].astype(BF16)

    assert n_ctx_chunks % M_LOCAL_GROUP == 0 and n_chunks % M_LOCAL_GROUP == 0

    def prep_group(latent):
        def body(j, carry):
            first = j * M_LOCAL_GROUP + (n_ctx_chunks if latent else 0)
            prep([first + i for i in range(M_LOCAL_GROUP)], latent)
            return carry
        return body

    lax.fori_loop(0, n_ctx_chunks // M_LOCAL_GROUP, prep_group(False), 0)
    lax.fori_loop(0, (n_chunks - n_ctx_chunks) // M_LOCAL_GROUP, prep_group(True), 0)

    si = lax.broadcasted_iota(jnp.int32, (L, L), 0)
    li = lax.broadcasted_iota(jnp.int32, (L, L), 1)
    keeps = [si <= li, si >= li]
    gbias = [jnp.broadcast_to(gb_ref[8 * d:8 * d + 8, :], (8, L)) for d in range(2)]
    zrows = jnp.zeros((D, L), BF16)

    def qt_head(qtc, h):
        parts = [zrows] * M_HEADS
        parts[h] = qtc[h * D:(h + 1) * D]
        return jnp.concatenate(parts, axis=0)

    def local(first, n_group):
        r0s = [pl.multiple_of((first + j) * L, L) for j in range(n_group)]
        scores = [[_dot(k_s[pl.ds(r0, L), :], qt_head(qt_s[:, pl.ds(r0, L)], h)) for h in range(M_HEADS)]
                  for r0 in r0s]
        vtas = [[vt_s[h, :, pl.ds(r0, L)] for h in range(M_HEADS)] for r0 in r0s]
        g8s = [[gr_ref[0, 8 * d:8 * d + 8, pl.ds(r0, L)] + gbias[d] for d in range(2)] for r0 in r0s]
        cum4s = [[pltpu.roll(_dot_f32_rhs01(_log_sigmoid(g8[d]), keeps[d].astype(BF16)), M_HEADS, 0)
                  for d in range(2)] for g8 in g8s]
        src_ts = [[(g8[d] - cum4[d]).T for d in range(2)] for g8, cum4 in zip(g8s, cum4s)]
        jdh = [(j, d, h) for j in range(n_group) for d in range(2) for h in range(M_HEADS)]
        logws = [jnp.where(keeps[d], cum4s[j][d][h:h + 1] + jnp.broadcast_to(src_ts[j][d][:, h:h + 1], (L, L)), NEG)
                 for j, d, h in jdh]
        amaxs = [lw.max(axis=0, keepdims=True) for lw in logws]
        ps = [(scores[j][h] * jnp.exp(logw - a)).astype(BF16) for (j, d, h), logw, a in zip(jdh, logws, amaxs)]
        intras = [_dot(vtas[j][h], p) for (j, d, h), p in zip(jdh, ps)]
        for (j, d, h), a, intra in zip(jdh, amaxs, intras):
            last = L - 1 if d == 0 else 0
            cl = cum4s[j][d][h:h + 1]
            a_last = jnp.broadcast_to(a[:, last:last + 1], (1, L))
            cl_last = jnp.broadcast_to(cl[:, last:last + 1], (1, L))
            w_src = jnp.exp(cl_last - cl + g8s[j][d][h:h + 1] - a_last)
            r_s[d, h * M_AUG:(h + 1) * M_AUG, pl.ds(r0s[j], L)] = intra
            for kind, row in enumerate((cl, a, cl_last, a_last, w_src)):
                row_s[d, kind, h:h + 1, pl.ds(r0s[j], L)] = row

    def local_loop(first, count, n_group):
        assert count % n_group == 0

        def body(i, carry):
            local(first + i * n_group, n_group)
            return carry

        lax.fori_loop(0, count // n_group, body, 0)

    local_loop(0, n_ctx_chunks, M_LOCAL_GROUP)
    local_loop(n_ctx_chunks, n_chunks - n_ctx_chunks, M_LATENT_GROUP)

    def step(c, d, c_olds, m_prevs):
        r0 = pl.multiple_of(c * L, L)
        kc = k_s[pl.ds(r0, L), :]
        qtc = qt_s[:, pl.ds(r0, L)]
        h_ts, c_news, m_news = [], [], []
        rows = [[row_s[d, kind, h:h + 1, pl.ds(r0, L)] for kind in range(5)] for h in range(M_HEADS)]
        vws = [(vt_s[h, :, pl.ds(r0, L)].astype(F32) * rows[h][4]).astype(BF16) for h in range(M_HEADS)]
        upd = _dot(jnp.concatenate(vws, axis=0), kc)
        inters = [_dot(c_olds[h].astype(BF16), qt_head(qtc, h)) for h in range(M_HEADS)]
        for h in range(M_HEADS):
            cl, a, cl_last, a_last, _ = rows[h]
            m_prev = m_prevs[h]
            inter = cl + m_prev
            m_t = jnp.maximum(inter, a)
            r = (jnp.exp(inter - m_t) * inters[h]
                 + jnp.exp(a - m_t) * r_s[d, h * M_AUG:(h + 1) * M_AUG, pl.ds(r0, L)])
            denom = jnp.maximum(jnp.abs(r[D:D + 1]), jnp.exp(-m_t))
            h_ts.append(r[0:D] * (1.0 / denom))
            m_new = jnp.maximum(cl_last + m_prev, a_last)
            w_old = jnp.exp(cl_last + m_prev - m_new)
            w_upd = jnp.exp(a_last - m_new)
            scale = lambda w: jnp.concatenate([w] * (W // L), axis=1)
            c_news.append(scale(w_old) * c_olds[h] + scale(w_upd) * upd[h * M_AUG:(h + 1) * M_AUG])
            m_news.append(m_new)
        return r0, h_ts, c_news, m_news

    def body(i, m_all):
        chunks = [i, jnp.where(i < n_ctx_chunks, n_ctx_chunks - 1 - i, n_chunks - 1 + n_ctx_chunks - i)]
        c_olds = [[c_s[d, h] for h in range(M_HEADS)] for d in range(2)]
        outs = [step(chunks[d], d, c_olds[d], m_all[d]) for d in range(2)]
        for d in range(2):
            r0, h_ts, c_news, _ = outs[d]
            for h in range(M_HEADS):
                r_s[d, h * M_AUG:h * M_AUG + D, pl.ds(r0, L)] = h_ts[h]
                c_s[d, h] = c_news[h]
        return tuple(tuple(outs[d][3]) for d in range(2))

    m0 = tuple(tuple(jnp.zeros((1, L), F32) for _ in range(M_HEADS)) for _ in range(2))
    lax.fori_loop(0, n_chunks, body, m0)

    ng = jnp.broadcast_to(ng_ref[...], (W, L))

    def post(j, carry):
        r0s = [pl.multiple_of((j * M_LOCAL_GROUP + i) * L, L) for i in range(M_LOCAL_GROUP)]
        hts = [[r_s[0, h * M_AUG:h * M_AUG + D, pl.ds(r0, L)] + r_s[1, h * M_AUG:h * M_AUG + D, pl.ds(r0, L)]
                for h in range(M_HEADS)] for r0 in r0s]
        scales = [[lax.rsqrt(jnp.sum(ht * ht, axis=0, keepdims=True) * (1.0 / D) + EPS) for ht in hs] for hs in hts]
        hnts = [(jnp.concatenate([ht * sc for ht, sc in zip(hs, scs)], axis=0) * ng).T for hs, scs in zip(hts, scales)]
        for r0, hnt in zip(r0s, hnts):
            o_ref[0, pl.ds(r0, L), :] = (_sigmoid(og_ref[0, pl.ds(r0, L), :]) * hnt).astype(o_ref.dtype)
        return carry

    lax.fori_loop(0, n_chunks // M_LOCAL_GROUP, post, 0)


def _rope_tables(n_rows):
    nf = HEAD_DIM // 4
    inv = ROPE_BASE ** (-jnp.arange(nf, dtype=F32) / nf)
    d = np.arange(M_WIDTH) % HEAD_DIM
    f_idx = d % nf
    is_row = d < HEAD_DIM // 2
    first_quarter = (d % (HEAD_DIM // 2)) < nf

    def tables(pos, sel):
        ang = pos[:, None] * inv[None, :]
        ang = ang[:, f_idx]
        cos = jnp.where(sel, jnp.cos(ang), 0.0)
        sin = jnp.where(sel, jnp.sin(ang), 0.0)
        sin_a = jnp.where(first_quarter, -sin, 0.0)
        sin_b = jnp.where(first_quarter, 0.0, sin)
        return jnp.stack([cos, sin_a, sin_b]).astype(F32)

    row_t = tables(jnp.arange(n_rows, dtype=F32), is_row)
    col_t = tables(jnp.arange(GRID_W, dtype=F32), ~is_row)
    return row_t, col_t


def _mlstm(u, g_rows, gate_b, norm_g, rope_row, rope_col, n_ctx):
    b, t, _ = u.shape
    assert t % M_CHUNK == 0 and n_ctx % M_CHUNK == 0 and M_CHUNK % GRID_W == 0
    n_chunks = t // M_CHUNK
    kern = functools.partial(_mlstm_kernel, n_ctx_chunks=n_ctx // M_CHUNK, n_chunks=n_chunks)
    n_rows = rope_row.shape[1]
    return pl.pallas_call(
        kern,
        grid=(b,),
        in_specs=[
            pl.BlockSpec((1, t, U_MO), lambda bi: (bi, 0, 0), pipeline_mode=pl.Buffered(1)),
            pl.BlockSpec((1, t, M_WIDTH), lambda bi: (bi, 0, U_MO // M_WIDTH)),
            pl.BlockSpec((1, N_GATES, t), lambda bi: (bi, 0, 0)),
            pl.BlockSpec((N_GATES, 1), lambda bi: (0, 0)),
            pl.BlockSpec((M_WIDTH, 1), lambda bi: (0, 0)),
            pl.BlockSpec((3, n_rows, M_WIDTH), lambda bi: (0, 0, 0)),
            pl.BlockSpec((3, GRID_W, M_WIDTH), lambda bi: (0, 0, 0)),
        ],
        out_specs=pl.BlockSpec((1, t, M_WIDTH), lambda bi: (bi, 0, 0)),
        out_shape=jax.ShapeDtypeStruct((b, t, M_WIDTH), BF16),
        scratch_shapes=[
            pltpu.VMEM((M_WIDTH, t), BF16),
            pltpu.VMEM((t, M_WIDTH), BF16),
            pltpu.VMEM((M_HEADS, M_AUG, t), BF16),
            pltpu.VMEM((2, M_HEADS * M_AUG, t), F32),
            pltpu.VMEM((2, 5, 8, t), F32),
            pltpu.VMEM((2, M_HEADS, M_AUG, M_WIDTH), F32),
        ],
        compiler_params=_params(("arbitrary",)),
    )(u, u, g_rows, gate_b.reshape(N_GATES, 1), norm_g.reshape(M_WIDTH, 1), rope_row, rope_col)


def _na_kernel(q_ref, k_ref, v_ref, qg_ref, kg_ref, bias_ref, o_ref, qt_s, kz_s, vt_s, s_s, *, n_ctx, n_rows):
    t = q_ref.shape[1]
    lane = lax.broadcasted_iota(jnp.int32, (1, LANES), 1)
    half = [lane < HEAD_DIM, lane >= HEAD_DIM]
    hi = lax.broadcasted_iota(jnp.int32, (LANES, LANES), 0) // HEAD_DIM
    hj = lax.broadcasted_iota(jnp.int32, (LANES, LANES), 1) // HEAD_DIM
    same_head = (hi == hj).astype(BF16)
    qg = qg_ref[...] * (HEAD_DIM ** -0.5 * LOG2E)
    kg = kg_ref[...]
    qblk = NA_QROWS * GRID_W
    kc = NA_KEY_CHUNK
    sub = 8

    for hh in range(2):
        vt_s[hh, HEAD_DIM:, :] = jnp.ones((vt_s.shape[1] - HEAD_DIM, t), BF16)

    def prep(r0s):
        qs = [q_ref[0, pl.ds(r0, ROW_TILE), :] for r0 in r0s]
        ks = [k_ref[0, pl.ds(r0, ROW_TILE), :] for r0 in r0s]
        vts = [v_ref[0, pl.ds(r0, ROW_TILE), :].T for r0 in r0s]
        qms = [_dot_f32_rhs01(q * q, same_head) * (1.0 / HEAD_DIM) for q in qs]
        kms = [_dot_f32_rhs01(k * k, same_head) * (1.0 / HEAD_DIM) for k in ks]
        qnts = [(q * lax.rsqrt(m + EPS) * qg).T for q, m in zip(qs, qms)]
        kns = [k * lax.rsqrt(m + EPS) * kg for k, m in zip(ks, kms)]
        for r0, qnt, kn, vt in zip(r0s, qnts, kns, vts):
            qt_s[:, pl.ds(r0, ROW_TILE)] = qnt.astype(BF16)
            for hh in range(2):
                kz_s[hh, pl.ds(r0, ROW_TILE), :] = jnp.where(half[hh], kn, 0.0).astype(BF16)
                vt_s[hh, 0:HEAD_DIM, pl.ds(r0, ROW_TILE)] = vt[hh * HEAD_DIM:(hh + 1) * HEAD_DIM].astype(BF16)

    n_tiles = t // ROW_TILE

    def prep_pair(j, carry):
        prep([pl.multiple_of((2 * j + i) * ROW_TILE, ROW_TILE) for i in range(2)])
        return carry

    lax.fori_loop(0, n_tiles // 2, prep_pair, 0)
    if n_tiles % 2:
        prep([(n_tiles - 1) * ROW_TILE])

    def attend(blocks):
        jobs = [(b, hh) for b in range(len(blocks)) for hh in range(2)]
        qts = [qt_s[:, pl.ds(q0, qblk)] for q0, _ in blocks]
        n_chunks = len(blocks[0][1])
        mx = [None] * len(jobs)
        def continues(prev, cur):
            if prev[1] is None or cur[1] is None:
                return prev[1] is None and cur[1] is None and cur[0] == prev[0] + kc
            return cur[1][1] == prev[1][1] + kc

        runs = []
        for ci in range(n_chunks):
            if ci and continues(blocks[0][1][ci - 1], blocks[0][1][ci]):
                runs[-1].append(ci)
            else:
                runs.append([ci])
        for run in runs:
            scores = [_dot(kz_s[hh, pl.ds(blocks[b][1][run[0]][0], kc * len(run)), :], qts[b]) for b, hh in jobs]
            for ri, ci in enumerate(run):
                for ji, (b, hh) in enumerate(jobs):
                    tab = blocks[b][1][ci][1]
                    s = scores[ji][ri * kc:(ri + 1) * kc]
                    if tab is not None:
                        s = s + bias_ref[0, hh, tab[0], tab[1]:tab[1] + kc, :]
                    s_s[ji, ci * kc:(ci + 1) * kc, :] = s
                    cm = functools.reduce(jnp.maximum, [s[i * sub:(i + 1) * sub] for i in range(kc // sub)])
                    mx[ji] = cm if mx[ji] is None else jnp.maximum(mx[ji], cm)
        ms = [m.max(axis=0, keepdims=True) for m in mx]
        accs = [None] * len(jobs)
        for ci in range(n_chunks):
            for ji, (b, hh) in enumerate(jobs):
                k0, _ = blocks[b][1][ci]
                p = jnp.exp2(s_s[ji, ci * kc:(ci + 1) * kc, :] - ms[ji]).astype(BF16)
                pv = _dot(vt_s[hh, :, pl.ds(k0, kc)], p)
                accs[ji] = pv if accs[ji] is None else accs[ji] + pv
        outs = []
        for b in range(len(blocks)):
            heads = [accs[2 * b + hh][0:HEAD_DIM] / accs[2 * b + hh][HEAD_DIM:HEAD_DIM + 1] for hh in range(2)]
            outs.append(jnp.concatenate(heads, axis=0).T)
        return outs

    ctx_chunks = [(ci * kc, None) for ci in range(n_ctx // kc)]
    o_ref[0, 0:n_ctx, :] = attend([(0, ctx_chunks)])[0].astype(o_ref.dtype)

    n_blocks = n_rows // NA_QROWS

    def group(gi, carry):
        blocks = []
        for j in range(NA_GROUP):
            bi = gi * NA_GROUP + j
            r0 = bi * NA_QROWS
            sb = jnp.clip(r0 - WIN_R // 2, 0, n_rows - NA_BAND)
            case = jnp.where(bi == 0, 0, jnp.where(bi == n_blocks - 1, 2, 1))
            q0 = pl.multiple_of(n_ctx + r0 * GRID_W, qblk)
            k0 = pl.multiple_of(n_ctx + sb * GRID_W, qblk)
            local = [(pl.multiple_of(k0 + ci * kc, kc), (case, ci * kc)) for ci in range(NA_BAND * GRID_W // kc)]
            blocks.append((q0, local + ctx_chunks))
        for (q0, _), out in zip(blocks, attend(blocks)):
            o_ref[0, pl.ds(q0, qblk), :] = out.astype(o_ref.dtype)
        return carry

    lax.fori_loop(0, n_blocks // NA_GROUP, group, 0)


def _na_bias_tables(rpb):
    n_layers, n_heads = rpb.shape[:2]
    cols = np.arange(GRID_W)
    col0 = np.clip(cols - WIN_C // 2, 0, GRID_W - WIN_C)
    valid = (cols[:, None] >= col0[None, :]) & (cols[:, None] < col0[None, :] + WIN_C)
    dc = cols[:, None] - cols[None, :] + WIN_C - 1
    col_sel = ((dc[..., None] == np.arange(RPB_C)) & valid[..., None]).astype(np.float32)
    toep = jnp.einsum('lhrd,xcd->lhrxc', rpb, col_sel, precision=lax.Precision.HIGHEST)
    toep = jnp.where(valid, toep * LOG2E, NEG * LOG2E).astype(F32)
    nk, nq = NA_BAND * GRID_W, NA_QROWS * GRID_W
    return pl.pallas_call(
        _na_bias_kernel,
        grid=(n_layers * n_heads,),
        in_specs=[pl.BlockSpec((1, RPB_R, GRID_W, GRID_W), lambda i: (i, 0, 0, 0))],
        out_specs=pl.BlockSpec((1, 3, nk, nq), lambda i: (i, 0, 0, 0)),
        out_shape=jax.ShapeDtypeStruct((n_layers * n_heads, 3, nk, nq), F32),
        compiler_params=_params(("arbitrary",)),
    )(toep.reshape(n_layers * n_heads, RPB_R, GRID_W, GRID_W)).reshape(n_layers, n_heads, 3, nk, nq)


def _na_bias_kernel(t_ref, o_ref):
    blank = jnp.full((GRID_W, GRID_W), NEG * LOG2E, F32)
    for case in range(3):
        for jj in range(NA_BAND):
            tiles = []
            for i in range(NA_QROWS):
                off = (0, i, NA_BAND - WIN_R)[case]
                rel = (jj - i, jj - i - WIN_R // 2, jj - i - NA_BAND + NA_QROWS)[case]
                tiles.append(t_ref[0, rel + WIN_R - 1] if off <= jj < off + WIN_R else blank)
            o_ref[0, case, jj * GRID_W:(jj + 1) * GRID_W, :] = jnp.concatenate(tiles, axis=1)


def _na(u, qn_g, kn_g, bias, layer, n_ctx):
    b, _, t, _ = u.shape
    n_rows = (t - n_ctx) // GRID_W
    assert n_rows % (NA_QROWS * NA_GROUP) == 0 and n_rows >= NA_BAND and n_ctx == NA_QROWS * GRID_W
    pairs = NA_HEADS // 2
    nk, nq = NA_BAND * GRID_W, NA_QROWS * GRID_W
    bias = bias.reshape(bias.shape[0], pairs, 2, 3, nk, nq)
    qg = jnp.tile(qn_g, 2).reshape(1, LANES)
    kg = jnp.tile(kn_g, 2).reshape(1, LANES)
    kern = functools.partial(_na_kernel, n_ctx=n_ctx, n_rows=n_rows)
    col = lambda which: pl.BlockSpec((1, None, t, LANES), lambda hp, bi: (bi, which * pairs + hp, 0, 0))
    return pl.pallas_call(
        kern,
        grid=(pairs, b),
        in_specs=[
            col(0), col(1), col(2),
            pl.BlockSpec((1, LANES), lambda hp, bi: (0, 0)),
            pl.BlockSpec((1, LANES), lambda hp, bi: (0, 0)),
            pl.BlockSpec((None, 1, 2, 3, nk, nq), lambda hp, bi: (layer, hp, 0, 0, 0, 0)),
        ],
        out_specs=col(0),
        out_shape=jax.ShapeDtypeStruct((b, pairs, t, LANES), BF16),
        scratch_shapes=[
            pltpu.VMEM((LANES, t), BF16),
            pltpu.VMEM((2, t, LANES), BF16),
            pltpu.VMEM((2, HEAD_DIM + NA_ONES_ROWS, t), BF16),
            pltpu.VMEM((2 * NA_GROUP, nk + n_ctx, nq), F32),
        ],
        compiler_params=_params(("arbitrary", "arbitrary")),
    )(u, u, u, qg, kg, bias)


def _lru_kernel(x_ref, g_ref, cw_ref, cb_ref, wa_ref, ba_ref, wx_ref, bx_ref, lam_ref, o_ref,
                xp_s, xc_s, hf_s, a_s, b_s, h_s, *, n_tiles):
    tl = LRU_TILE
    halo = LRU_HALO
    t = x_ref.shape[1]
    width = x_ref.shape[2]
    group = 8

    xp_s[0:halo, :] = jnp.zeros((halo, width), F32)
    xp_s[halo + t:halo + t + halo, :] = jnp.zeros((halo, width), F32)

    def fill(i, carry):
        r0 = pl.multiple_of(i * tl, tl)
        xp_s[pl.ds(pl.multiple_of(r0 + halo, halo), tl), :] = x_ref[0, pl.ds(r0, tl), :]
        return carry

    lax.fori_loop(0, n_tiles, fill, 0)

    row = lax.broadcasted_iota(jnp.int32, (tl, 1), 0)
    rowg = row % group
    cw = cw_ref[...]
    cb = cb_ref[...]

    def conv_tile(i):
        r0 = pl.multiple_of(i * tl, tl)
        win = xp_s[pl.ds(r0, tl + 2 * halo), :]
        n_win = tl + 2 * halo
        xm2 = pltpu.roll(win, 2, 0)[halo:halo + tl]
        xm1 = pltpu.roll(win, 1, 0)[halo:halo + tl]
        x0 = win[halo:halo + tl]
        xp1 = pltpu.roll(win, n_win - 1, 0)[halo:halo + tl]
        lo = i <= 1
        hi = i == 0
        xm2 = jnp.where(lo & (row < 2), 0.0, xm2)
        xm1 = jnp.where(lo & (row < 1), 0.0, xm1)
        xp1 = jnp.where(hi & (row == tl - 1), 0.0, xp1)
        return xm2 * cw[0:1] + xm1 * cw[1:2] + x0 * cw[2:3] + xp1 * cw[3:4] + cb

    def gates(i, d):
        r0 = pl.multiple_of(i * tl, tl)
        if d == 0:
            xc = conv_tile(i)
            xc_s[pl.ds(r0, tl), :] = xc
        else:
            xc = xc_s[pl.ds(r0, tl), :]
        xb = xc.astype(BF16)
        r = _sigmoid(_dot(xb, wa_ref[d]) + ba_ref[d:d + 1, :])
        ig = _sigmoid(_dot(xb, wx_ref[d]) + bx_ref[d:d + 1, :])
        log_a = -LRU_C * r * _softplus(-lam_ref[d:d + 1, :])
        a = jnp.exp(log_a)
        bt = jnp.sqrt(-jnp.tanh(log_a) * (a * a + 1.0)) * (ig * xc)
        return a, bt

    def scan_tile(i, d, carry):
        a, bt = gates(i, d)

        def shift(x, kk):
            x3 = x.reshape(tl // group, group, width)
            return pltpu.roll(x3, kk if d == 0 else group - kk, 1).reshape(tl, width)

        for kk in (1, 2, 4):
            mk = rowg >= kk if d == 0 else rowg < group - kk
            bt = jnp.where(mk, a * shift(bt, kk) + bt, bt)
            a = jnp.where(mk, a * shift(a, kk), a)
        a_s[...] = a
        b_s[...] = bt
        n_groups = tl // group

        def grp(gidx, c):
            gg = gidx if d == 0 else n_groups - 1 - gidx
            g0 = pl.multiple_of(gg * group, group)
            h = b_s[pl.ds(g0, group), :] + a_s[pl.ds(g0, group), :] * c
            h_s[pl.ds(g0, group), :] = h
            return h[group - 1:group, :] if d == 0 else h[0:1, :]

        return lax.fori_loop(0, n_groups, grp, carry, unroll=8)

    def fwd(i, carry):
        carry = scan_tile(i, 0, carry)
        r0 = pl.multiple_of(i * tl, tl)
        hf_s[pl.ds(r0, tl), :] = h_s[...]
        return carry

    def bwd(i, carry):
        carry = scan_tile(i, 1, carry)
        r0 = pl.multiple_of(i * tl, tl)
        y = (hf_s[pl.ds(r0, tl), :] + h_s[...]) * jax.nn.gelu(g_ref[0, pl.ds(r0, tl), :])
        o_ref[0, pl.ds(r0, tl), :] = y.astype(o_ref.dtype)
        return carry

    h0 = jnp.zeros((1, width), F32)
    lax.fori_loop(0, n_tiles, fwd, h0)
    carry = bwd(jnp.int32(0), h0)
    lax.fori_loop(1, n_tiles, lambda j, c: bwd(n_tiles - j, c), carry)


def _block_diag_dense(w):
    nd, k, c, _ = w.shape
    eye = jnp.eye(k, dtype=w.dtype)
    return jnp.einsum('dkce,kj->dkcje', w, eye).reshape(nd, k * c, k * c)


def _lru(u, conv_w, conv_b, w_a, b_a, w_x, b_x, lam, n_ctx):
    b, _, t, _ = u.shape
    width = conv_w.shape[1]
    assert n_ctx == LRU_TILE and t % LRU_TILE == 0
    n_tiles = t // LRU_TILE
    wa = _block_diag_dense(w_a).astype(BF16)
    wx = _block_diag_dense(w_x).astype(BF16)
    kern = functools.partial(_lru_kernel, n_tiles=n_tiles)
    full = lambda shape: pl.BlockSpec(shape, lambda bi: (0,) * len(shape))
    return pl.pallas_call(
        kern,
        grid=(b,),
        in_specs=[
            pl.BlockSpec((1, None, t, width), lambda bi: (bi, 0, 0, 0)),
            pl.BlockSpec((1, None, t, width), lambda bi: (bi, 1, 0, 0)),
            full((CONV_W, width)),
            full((1, width)),
            full((2, width, width)),
            full((2, width)),
            full((2, width, width)),
            full((2, width)),
            full((2, width)),
        ],
        out_specs=pl.BlockSpec((1, t, width), lambda bi: (bi, 0, 0)),
        out_shape=jax.ShapeDtypeStruct((b, t, width), BF16),
        scratch_shapes=[
            pltpu.VMEM((t + 2 * LRU_HALO, width), F32),
            pltpu.VMEM((t, width), F32),
            pltpu.VMEM((t, width), F32),
            pltpu.VMEM((LRU_TILE, width), F32),
            pltpu.VMEM((LRU_TILE, width), F32),
            pltpu.VMEM((LRU_TILE, width), F32),
        ],
        compiler_params=_params(("arbitrary",)),
    )(u, u, conv_w, conv_b.reshape(1, width), wa, b_a, wx, b_x, lam)


def _out_mlp_kernel(*refs, hid_tile):
    xc_ref, xl_ref = refs[0:2]
    ym_refs, yn_refs, yl_refs = (refs[2 + k * MLP_SUB:2 + (k + 1) * MLP_SUB] for k in range(3))
    mod_ref, g_ref, wo_ref, w1_ref, w2_ref, oc_ref, ol_ref = refs[2 + 3 * MLP_SUB:]

    def body(x, ym, yn, yl, o_ref):
        mod = mod_ref[0, 0]
        att = _dot(ym.astype(BF16), wo_ref[0:M_WIDTH, :])
        att = att + _dot(yn.astype(BF16), wo_ref[M_WIDTH:M_WIDTH + NA_WIDTH, :])
        att = att + _dot(yl.astype(BF16), wo_ref[M_WIDTH + NA_WIDTH:, :])
        x1 = x + mod[2:3, :] * att
        ms = jnp.mean(x1 * x1, axis=-1, keepdims=True)
        y = x1 * lax.rsqrt(ms + EPS) * g_ref[...]
        h = (y * (1.0 + mod[4:5, :]) + mod[3:4, :]).astype(BF16)
        hidden = w1_ref.shape[1]
        acc = None
        for j in range(hidden // hid_tile):
            a = jnp.maximum(_dot(h, w1_ref[:, j * hid_tile:(j + 1) * hid_tile]), 0.0)
            part = _dot((a * a).astype(BF16), w2_ref[j * hid_tile:(j + 1) * hid_tile, :])
            acc = part if acc is None else acc + part
        o_ref[0] = x1 + mod[5:6, :] * acc

    def heads(ref):
        return jnp.concatenate([ref[0, j] for j in range(ref.shape[1])], axis=1)

    def rows(tiles):
        return jnp.concatenate(tiles, axis=0)

    @pl.when(pl.program_id(1) == 0)
    def _():
        body(xc_ref[0], ym_refs[0][0], heads(yn_refs[0]), yl_refs[0][0], oc_ref)

    @pl.when(pl.program_id(1) > 0)
    def _():
        body(xl_ref[0], rows([r[0] for r in ym_refs]), rows([heads(r) for r in yn_refs]),
             rows([r[0] for r in yl_refs]), ol_ref)


def _out_mlp(xc, xl, ym, yn, yl, modv, g2, w_out, w1, w2, layer):
    b, n_ctx, d = xc.shape
    n_lat = xl.shape[1]
    lat_tile = MLP_SUB * ROW_TILE
    assert n_ctx == ROW_TILE and n_lat % lat_tile == 0
    hidden = w1.shape[2]
    blk = lambda j: (lambda s: jnp.maximum(MLP_SUB * (s - 1) + 1 + j, 0))
    tok = lambda w: [pl.BlockSpec((1, ROW_TILE, w), lambda bi, s, f=blk(j): (bi, f(s), 0)) for j in range(MLP_SUB)]
    pairs = [pl.BlockSpec((1, NA_HEADS // 2, ROW_TILE, LANES), lambda bi, s, f=blk(j): (bi, 0, f(s), 0))
             for j in range(MLP_SUB)]
    stream = [
        pl.BlockSpec((1, ROW_TILE, d), lambda bi, s: (bi, 0, 0)),
        pl.BlockSpec((1, lat_tile, d), lambda bi, s: (bi, jnp.maximum(s - 1, 0), 0)),
    ]
    kern = functools.partial(_out_mlp_kernel, hid_tile=1024)
    return pl.pallas_call(
        kern,
        grid=(b, 1 + n_lat // lat_tile),
        in_specs=stream + tok(M_WIDTH) + pairs + tok(d - M_WIDTH - NA_WIDTH) + [
            pl.BlockSpec((1, 1, 6, d), lambda bi, s: (bi, jnp.minimum(s, 1), 0, 0)),
            pl.BlockSpec((1, d), lambda bi, s: (0, 0)),
            _resident((None, d, d), lambda bi, s: (layer, 0, 0)),
            _resident((None, d, hidden), lambda bi, s: (layer, 0, 0)),
            _resident((None, hidden, d), lambda bi, s: (layer, 0, 0)),
        ],
        out_specs=stream,
        out_shape=[jax.ShapeDtypeStruct(xc.shape, F32), jax.ShapeDtypeStruct(xl.shape, F32)],
        compiler_params=_params(("arbitrary", "arbitrary")),
    )(xc, xl, *([ym] * MLP_SUB + [yn] * MLP_SUB + [yl] * MLP_SUB), modv, g2.reshape(1, d), w_out, w1, w2)


def _split_w_in(w_in):
    m_end = 4 * M_WIDTH
    na_end = m_end + N_GATES + 3 * NA_WIDTH
    main = jnp.concatenate([w_in[..., :m_end], w_in[..., na_end:], w_in[..., m_end + N_GATES:na_end]], axis=-1)
    gates = jnp.swapaxes(w_in[..., m_end:m_end + N_GATES], -1, -2)
    return main.astype(BF16), gates.astype(BF16)


def kernel(x, c, ctx, c_ctx, w_mod, b_mod, norm1_g, norm2_g, w_in, mlstm_gate_b, mlstm_norm_g, na_q_norm_g, na_k_norm_g, na_rpb, lru_conv_w, lru_conv_b, lru_w_a, lru_b_a, lru_w_x, lru_b_x, lru_lambda, w_out, w_mlp1, w_mlp2):
    b, n, d = x.shape
    n_ctx = ctx.shape[1]
    depth = w_in.shape[0]
    assert n_ctx == ROW_TILE and n % ROW_TILE == 0 and n % GRID_W == 0

    cond_rows = 8 * ((b + 1 + 7) // 8)
    cond = jnp.zeros((cond_rows, d), F32).at[:b].set(c).at[b].set(c_ctx)
    mod_all = _modulation(cond, w_mod, b_mod)
    rope_row, rope_col = _rope_tables(n // GRID_W)
    w_in_p, w_gate_t = _split_w_in(w_in)
    na_bias = _na_bias_tables(na_rpb)
    w_out_b, w1_b, w2_b = w_out.astype(BF16), w_mlp1.astype(BF16), w_mlp2.astype(BF16)

    xc, xl = ctx, x
    for l in range(depth):
        lat = mod_all[l, :b].reshape(b, 6, d)
        cx = jnp.broadcast_to(mod_all[l, b].reshape(1, 6, d), (b, 6, d))
        modv = jnp.stack([cx, lat], axis=1)
        um, ul, un, g_rows = _inproj(xc, xl, modv, norm1_g[l], w_in_p, w_gate_t, l)
        ym = _mlstm(um, g_rows, mlstm_gate_b[l], mlstm_norm_g[l], rope_row, rope_col, n_ctx)
        yn = _na(un, na_q_norm_g[l], na_k_norm_g[l], na_bias, l, n_ctx)
        yl = _lru(ul, lru_conv_w[l], lru_conv_b[l], lru_w_a[l], lru_b_a[l], lru_w_x[l], lru_b_x[l], lru_lambda[l], n_ctx)
        xc, xl = _out_mlp(xc, xl, ym, yn, yl, modv, norm2_g[l], w_out_b, w1_b, w2_b, l)
    return xl
```

```python
import functools
import math

import numpy as np
import jax
import jax.numpy as jnp
from jax import lax
from jax.experimental import pallas as pl
from jax.experimental.pallas import tpu as pltpu

F32 = jnp.float32
BF16 = jnp.bfloat16

HEAD_DIM = 64
M_HEADS = 4
M_WIDTH = M_HEADS * HEAD_DIM
NA_HEADS = 8
NA_WIDTH = NA_HEADS * HEAD_DIM
N_GATES = 4 * M_HEADS
GRID_W = 64
M_CHUNK = 128
M_AUG = HEAD_DIM + 16
M_LOCAL_GROUP = 2
M_LATENT_GROUP = 4
WIN_R = 8
WIN_C = 16
RPB_R = 2 * WIN_R - 1
RPB_C = 2 * WIN_C - 1
NA_QROWS = 4
NA_BAND = NA_QROWS + WIN_R
NA_KEY_CHUNK = 128
NA_ONES_ROWS = 16
NA_GROUP = 8
LOG2E = 1.4426950408889634
CONV_W = 4
LRU_C = 8.0
ROPE_BASE = 10000.0
EPS = 1e-6
NEG = -1e30

LANES = 128
ROW_TILE = 256
MLP_SUB = 4
LRU_TILE = 256
LRU_HALO = 8
VMEM_LIMIT = 56 * 1024 * 1024

U_MQ, U_MK, U_MV, U_MO = 0, 256, 512, 768
U_XR, U_GT = 1024, 1280
U_NQ, U_NK, U_NV = 1536, 2048, 2560
U_COLS = 3072
LRU_WIDTH = U_NQ - U_GT
NA_COL_BLOCKS = (U_COLS - U_NQ) // LANES


def _dot(a, b):
    return jnp.dot(a, b, preferred_element_type=F32)


def _dot_f32_rhs01(x, m01):
    hi = x.astype(BF16)
    lo = (x - hi.astype(F32)).astype(BF16)
    return _dot(hi, m01) + _dot(lo, m01)


def _sigmoid(x):
    return 0.5 * jnp.tanh(0.5 * x) + 0.5


def _log_sigmoid(x):
    return jnp.minimum(x, 0.0) - jnp.log1p(jnp.exp(-jnp.abs(x)))


def _softplus(x):
    return jnp.maximum(x, 0.0) + jnp.log1p(jnp.exp(-jnp.abs(x)))


def _params(sem):
    return pltpu.CompilerParams(dimension_semantics=sem, vmem_limit_bytes=VMEM_LIMIT)


def _resident(shape, index_map):
    return pl.BlockSpec(shape, index_map, pipeline_mode=pl.Buffered(1))


def _mod_kernel(a_ref, w_ref, b_ref, o_ref):
    a = a_ref[...]
    act = (a * jax.nn.sigmoid(a)).astype(BF16)
    o_ref[0] = _dot(act, w_ref[0].astype(BF16)) + b_ref[0]


def _modulation(cond, w_mod, b_mod):
    depth, d, n6 = w_mod.shape
    rows = cond.shape[0]
    tn = 1536
    return pl.pallas_call(
        _mod_kernel,
        grid=(depth, n6 // tn),
        in_specs=[
            pl.BlockSpec((rows, d), lambda l, j: (0, 0)),
            pl.BlockSpec((1, d, tn), lambda l, j: (l, 0, j)),
            pl.BlockSpec((1, 1, tn), lambda l, j: (l, 0, j)),
        ],
        out_specs=pl.BlockSpec((1, rows, tn), lambda l, j: (l, 0, j)),
        out_shape=jax.ShapeDtypeStruct((depth, rows, n6), F32),
        compiler_params=_params(("arbitrary", "arbitrary")),
    )(cond, w_mod, b_mod.reshape(depth, 1, n6))


def _inproj_kernel(xc_ref, xa_ref, xb_ref, mod_ref, g_ref, w_ref, wg_ref, um_ref, ul_ref, un_ref, gt_ref,
                   *, n_steps, last_has_second):
    def half(x, mod, r0):
        ms = jnp.mean(x * x, axis=-1, keepdims=True)
        y = x * lax.rsqrt(ms + EPS) * g_ref[...]
        h = (y * (1.0 + mod[1:2, :]) + mod[0:1, :]).astype(BF16)
        u = _dot(h, w_ref[...])
        rows = slice(r0, r0 + ROW_TILE)
        um_ref[0, rows, :] = u[:, 0:U_XR]
        for j in range(ul_ref.shape[1]):
            ul_ref[0, j, rows, :] = u[:, U_XR + j * LRU_WIDTH:U_XR + (j + 1) * LRU_WIDTH]
        for j in range(un_ref.shape[1]):
            un_ref[0, j, rows, :] = u[:, U_NQ + j * LANES:U_NQ + (j + 1) * LANES]
        gt_ref[0, :, rows] = lax.dot_general(wg_ref[...], h, (((1,), (1,)), ((), ())), preferred_element_type=F32)

    mod_ctx, mod_lat = mod_ref[0, 0], mod_ref[0, 1]

    @pl.when(pl.program_id(1) == 0)
    def _():
        half(xc_ref[0], mod_ctx, 0)
        half(xb_ref[0], mod_lat, ROW_TILE)

    step = pl.program_id(1)
    both_latent = (step > 0) if last_has_second else (step > 0) & (step < n_steps - 1)

    @pl.when(both_latent)
    def _():
        half(xa_ref[0], mod_lat, 0)
        half(xb_ref[0], mod_lat, ROW_TILE)

    if not last_has_second:
        @pl.when(step == n_steps - 1)
        def _():
            half(xa_ref[0], mod_lat, 0)


def _inproj(xc, xl, modv, g1, w_in_p, w_gate_t, layer):
    b, n_ctx, d = xc.shape
    assert n_ctx == ROW_TILE and xl.shape[1] % ROW_TILE == 0
    n_lat_tiles = xl.shape[1] // ROW_TILE
    t = n_ctx + xl.shape[1]
    step_rows = 2 * ROW_TILE
    n_steps = pl.cdiv(t, step_rows)
    assert n_steps > 1
    kern = functools.partial(_inproj_kernel, n_steps=n_steps, last_has_second=(t % step_rows == 0))
    return pl.pallas_call(
        kern,
        grid=(b, n_steps),
        in_specs=[
            pl.BlockSpec((1, ROW_TILE, d), lambda bi, s: (bi, 0, 0)),
            pl.BlockSpec((1, ROW_TILE, d), lambda bi, s: (bi, jnp.maximum(2 * s - 1, 0), 0)),
            pl.BlockSpec((1, ROW_TILE, d), lambda bi, s: (bi, jnp.minimum(2 * s, n_lat_tiles - 1), 0)),
            pl.BlockSpec((1, 2, 6, d), lambda bi, s: (bi, 0, 0, 0)),
            pl.BlockSpec((1, d), lambda bi, s: (0, 0)),
            _resident((None, d, U_COLS), lambda bi, s: (layer, 0, 0)),
            _resident((None, N_GATES, d), lambda bi, s: (layer, 0, 0)),
        ],
        out_specs=[
            pl.BlockSpec((1, step_rows, U_XR), lambda bi, s: (bi, s, 0)),
            pl.BlockSpec((1, 2, step_rows, LRU_WIDTH), lambda bi, s: (bi, 0, s, 0)),
            pl.BlockSpec((1, NA_COL_BLOCKS, step_rows, LANES), lambda bi, s: (bi, 0, s, 0)),
            pl.BlockSpec((1, N_GATES, step_rows), lambda bi, s: (bi, 0, s)),
        ],
        out_shape=[
            jax.ShapeDtypeStruct((b, t, U_XR), F32),
            jax.ShapeDtypeStruct((b, 2, t, LRU_WIDTH), F32),
            jax.ShapeDtypeStruct((b, NA_COL_BLOCKS, t, LANES), F32),
            jax.ShapeDtypeStruct((b, N_GATES, t), F32),
        ],
        compiler_params=_params(("arbitrary", "arbitrary")),
    )(xc, xl, xl, modv, g1.reshape(1, d), w_in_p, w_gate_t)


def _mlstm_kernel(u_ref, og_ref, gr_ref, gb_ref, ng_ref, rrow_ref, rcol_ref, o_ref,
                  qt_s, k_s, vt_s, r_s, row_s, c_s, *, n_ctx_chunks, n_chunks):
    L = M_CHUNK
    W = M_WIDTH
    D = HEAD_DIM
    groups = L // GRID_W

    for h in range(M_HEADS):
        vt_s[h, D:, :] = jnp.ones((M_AUG - D, vt_s.shape[2]), BF16)
    c_s[...] = jnp.zeros(c_s.shape, F32)

    def prep(cs, latent):
        r0s = [pl.multiple_of(c * L, L) for c in cs]
        qs = [u_ref[0, pl.ds(r0, L), U_MQ:U_MQ + W] for r0 in r0s]
        ks = [u_ref[0, pl.ds(r0, L), U_MK:U_MK + W] * D ** -0.5 for r0 in r0s]
        vts = [u_ref[0, pl.ds(r0, L), U_MV:U_MV + W].T for r0 in r0s]
        if latent:
            def table(c, i):
                first = (c - n_ctx_chunks) * groups
                return jnp.concatenate([rrow_ref[i, pl.ds(first + g, 1), :] + rcol_ref[i] for g in range(groups)], axis=0)

            def rope(x, c):
                return (x * table(c, 0) + pltpu.roll(x, W - D // 4, 1) * table(c, 1)
                        + pltpu.roll(x, D // 4, 1) * table(c, 2))

            qs = [rope(q, c) for q, c in zip(qs, cs)]
            ks = [rope(k, c) for k, c in zip(ks, cs)]
        qts = [q.T for q in qs]
        for r0, qt, k, vt in zip(r0s, qts, ks, vts):
            qt_s[:, pl.ds(r0, L)] = qt.astype(BF16)
            k_s[pl.ds(r0, L), :] = k.astype(BF16)
            for h in range(M_HEADS):
                vt_s[h, 0:D, pl.ds(r0, L)] = vt[h * D:(h + 1) * D].astype(BF16)

    assert n_ctx_chunks % M_LOCAL_GROUP == 0 and n_chunks % M_LOCAL_GROUP == 0

    def prep_group(latent):
        def body(j, carry):
            first = j * M_LOCAL_GROUP + (n_ctx_chunks if latent else 0)
            prep([first + i for i in range(M_LOCAL_GROUP)], latent)
            return carry
        return body

    lax.fori_loop(0, n_ctx_chunks // M_LOCAL_GROUP, prep_group(False), 0)
    lax.fori_loop(0, (n_chunks - n_ctx_chunks) // M_LOCAL_GROUP, prep_group(True), 0)

    si = lax.broadcasted_iota(jnp.int32, (L, L), 0)
    li = lax.broadcasted_iota(jnp.int32, (L, L), 1)
    keeps = [si <= li, si >= li]
    gbias = [jnp.broadcast_to(gb_ref[8 * d:8 * d + 8, :], (8, L)) for d in range(2)]
    zrows = jnp.zeros((D, L), BF16)

    def qt_head(qtc, h):
        parts = [zrows] * M_HEADS
        parts[h] = qtc[h * D:(h + 1) * D]
        return jnp.concatenate(parts, axis=0)

    def local(first, n_group):
        r0s = [pl.multiple_of((first + j) * L, L) for j in range(n_group)]
        scores = [[_dot(k_s[pl.ds(r0, L), :], qt_head(qt_s[:, pl.ds(r0, L)], h)) for h in range(M_HEADS)]
                  for r0 in r0s]
        vtas = [[vt_s[h, :, pl.ds(r0, L)] for h in range(M_HEADS)] for r0 in r0s]
        g8s = [[gr_ref[0, 8 * d:8 * d + 8, pl.ds(r0, L)] + gbias[d] for d in range(2)] for r0 in r0s]
        cum4s = [[pltpu.roll(_dot_f32_rhs01(_log_sigmoid(g8[d]), keeps[d].astype(BF16)), M_HEADS, 0)
                  for d in range(2)] for g8 in g8s]
        src_ts = [[(g8[d] - cum4[d]).T for d in range(2)] for g8, cum4 in zip(g8s, cum4s)]
        jdh = [(j, d, h) for j in range(n_group) for d in range(2) for h in range(M_HEADS)]
        logws = [jnp.where(keeps[d], cum4s[j][d][h:h + 1] + jnp.broadcast_to(src_ts[j][d][:, h:h + 1], (L, L)), NEG)
                 for j, d, h in jdh]
        amaxs = [lw.max(axis=0, keepdims=True) for lw in logws]
        ps = [(scores[j][h] * jnp.exp(logw - a)).astype(BF16) for (j, d, h), logw, a in zip(jdh, logws, amaxs)]
        intras = [_dot(vtas[j][h], p) for (j, d, h), p in zip(jdh, ps)]
        for (j, d, h), a, intra in zip(jdh, amaxs, intras):
            last = L - 1 if d == 0 else 0
            cl = cum4s[j][d][h:h + 1]
            a_last = jnp.broadcast_to(a[:, last:last + 1], (1, L))
            cl_last = jnp.broadcast_to(cl[:, last:last + 1], (1, L))
            w_src = jnp.exp(cl_last - cl + g8s[j][d][h:h + 1] - a_last)
            r_s[d, h * M_AUG:(h + 1) * M_AUG, pl.ds(r0s[j], L)] = intra
            for kind, row in enumerate((cl, a, cl_last, a_last, w_src)):
                row_s[d, kind, h:h + 1, pl.ds(r0s[j], L)] = row

    def local_loop(first, count, n_group):
        assert count % n_group == 0

        def body(i, carry):
            local(first + i * n_group, n_group)
            return carry

        lax.fori_loop(0, count // n_group, body, 0)

    local_loop(0, n_ctx_chunks, M_LOCAL_GROUP)
    local_loop(n_ctx_chunks, n_chunks - n_ctx_chunks, M_LATENT_GROUP)

    def step(c, d, c_olds, m_prevs):
        r0 = pl.multiple_of(c * L, L)
        kc = k_s[pl.ds(r0, L), :]
        qtc = qt_s[:, pl.ds(r0, L)]
        h_ts, c_news, m_news = [], [], []
        rows = [[row_s[d, kind, h:h + 1, pl.ds(r0, L)] for kind in range(5)] for h in range(M_HEADS)]
        vws = [(vt_s[h, :, pl.ds(r0, L)].astype(F32) * rows[h][4]).astype(BF16) for h in range(M_HEADS)]
        upd = _dot(jnp.concatenate(vws, axis=0), kc)
        inters = [_dot(c_olds[h].astype(BF16), qt_head(qtc, h)) for h in range(M_HEADS)]
        for h in range(M_HEADS):
            cl, a, cl_last, a_last, _ = rows[h]
            m_prev = m_prevs[h]
            inter = cl + m_prev
            m_t = jnp.maximum(inter, a)
            r = (jnp.exp(inter - m_t) * inters[h]
                 + jnp.exp(a - m_t) * r_s[d, h * M_AUG:(h + 1) * M_AUG, pl.ds(r0, L)])
            denom = jnp.maximum(jnp.abs(r[D:D + 1]), jnp.exp(-m_t))
            h_ts.append(r[0:D] * (1.0 / denom))
            m_new = jnp.maximum(cl_last + m_prev, a_last)
            w_old = jnp.exp(cl_last + m_prev - m_new)
            w_upd = jnp.exp(a_last - m_new)
            scale = lambda w: jnp.concatenate([w] * (W // L), axis=1)
            c_news.append(scale(w_old) * c_olds[h] + scale(w_upd) * upd[h * M_AUG:(h + 1) * M_AUG])
            m_news.append(m_new)
        return r0, h_ts, c_news, m_news

    def body(i, m_all):
        chunks = [i, jnp.where(i < n_ctx_chunks, n_ctx_chunks - 1 - i, n_chunks - 1 + n_ctx_chunks - i)]
        c_olds = [[c_s[d, h] for h in range(M_HEADS)] for d in range(2)]
        outs = [step(chunks[d], d, c_olds[d], m_all[d]) for d in range(2)]
        for d in range(2):
            r0, h_ts, c_news, _ = outs[d]
            for h in range(M_HEADS):
                r_s[d, h * M_AUG:h * M_AUG + D, pl.ds(r0, L)] = h_ts[h]
                c_s[d, h] = c_news[h]
        return tuple(tuple(outs[d][3]) for d in range(2))

    m0 = tuple(tuple(jnp.zeros((1, L), F32) for _ in range(M_HEADS)) for _ in range(2))
    lax.fori_loop(0, n_chunks, body, m0)

    ng = jnp.broadcast_to(ng_ref[...], (W, L))

    def post(j, carry):
        r0s = [pl.multiple_of((j * M_LOCAL_GROUP + i) * L, L) for i in range(M_LOCAL_GROUP)]
        hts = [[r_s[0, h * M_AUG:h * M_AUG + D, pl.ds(r0, L)] + r_s[1, h * M_AUG:h * M_AUG + D, pl.ds(r0, L)]
                for h in range(M_HEADS)] for r0 in r0s]
        scales = [[lax.rsqrt(jnp.sum(ht * ht, axis=0, keepdims=True) * (1.0 / D) + EPS) for ht in hs] for hs in hts]
        hnts = [(jnp.concatenate([ht * sc for ht, sc in zip(hs, scs)], axis=0) * ng).T for hs, scs in zip(hts, scales)]
        for r0, hnt in zip(r0s, hnts):
            o_ref[0, pl.ds(r0, L), :] = (_sigmoid(og_ref[0, pl.ds(r0, L), :]) * hnt).astype(o_ref.dtype)
        return carry

    lax.fori_loop(0, n_chunks // M_LOCAL_GROUP, post, 0)


def _rope_tables(n_rows):
    nf = HEAD_DIM // 4
    inv = ROPE_BASE ** (-jnp.arange(nf, dtype=F32) / nf)
    d = np.arange(M_WIDTH) % HEAD_DIM
    f_idx = d % nf
    is_row = d < HEAD_DIM // 2
    first_quarter = (d % (HEAD_DIM // 2)) < nf

    def tables(pos, sel):
        ang = pos[:, None] * inv[None, :]
        ang = ang[:, f_idx]
        cos = jnp.where(sel, jnp.cos(ang), 0.0)
        sin = jnp.where(sel, jnp.sin(ang), 0.0)
        sin_a = jnp.where(first_quarter, -sin, 0.0)
        sin_b = jnp.where(first_quarter, 0.0, sin)
        return jnp.stack([cos, sin_a, sin_b]).astype(F32)

    row_t = tables(jnp.arange(n_rows, dtype=F32), is_row)
    col_t = tables(jnp.arange(GRID_W, dtype=F32), ~is_row)
    return row_t, col_t


def _mlstm(u, g_rows, gate_b, norm_g, rope_row, rope_col, n_ctx):
    b, t, _ = u.shape
    assert t % M_CHUNK == 0 and n_ctx % M_CHUNK == 0 and M_CHUNK % GRID_W == 0
    n_chunks = t // M_CHUNK
    kern = functools.partial(_mlstm_kernel, n_ctx_chunks=n_ctx // M_CHUNK, n_chunks=n_chunks)
    n_rows = rope_row.shape[1]
    return pl.pallas_call(
        kern,
        grid=(b,),
        in_specs=[
            pl.BlockSpec((1, t, U_MO), lambda bi: (bi, 0, 0), pipeline_mode=pl.Buffered(1)),
            pl.BlockSpec((1, t, M_WIDTH), lambda bi: (bi, 0, U_MO // M_WIDTH)),
            pl.BlockSpec((1, N_GATES, t), lambda bi: (bi, 0, 0)),
            pl.BlockSpec((N_GATES, 1), lambda bi: (0, 0)),
            pl.BlockSpec((M_WIDTH, 1), lambda bi: (0, 0)),
            pl.BlockSpec((3, n_rows, M_WIDTH), lambda bi: (0, 0, 0)),
            pl.BlockSpec((3, GRID_W, M_WIDTH), lambda bi: (0, 0, 0)),
        ],
        out_specs=pl.BlockSpec((1, t, M_WIDTH), lambda bi: (bi, 0, 0)),
        out_shape=jax.ShapeDtypeStruct((b, t, M_WIDTH), BF16),
        scratch_shapes=[
            pltpu.VMEM((M_WIDTH, t), BF16),
            pltpu.VMEM((t, M_WIDTH), BF16),
            pltpu.VMEM((M_HEADS, M_AUG, t), BF16),
            pltpu.VMEM((2, M_HEADS * M_AUG, t), F32),
            pltpu.VMEM((2, 5, 8, t), F32),
            pltpu.VMEM((2, M_HEADS, M_AUG, M_WIDTH), F32),
        ],
        compiler_params=_params(("arbitrary",)),
    )(u, u, g_rows, gate_b.reshape(N_GATES, 1), norm_g.reshape(M_WIDTH, 1), rope_row, rope_col)


def _na_kernel(q_ref, k_ref, v_ref, qg_ref, kg_ref, bias_ref, o_ref, qt_s, kz_s, vt_s, s_s, *, n_ctx, n_rows):
    t = q_ref.shape[1]
    lane = lax.broadcasted_iota(jnp.int32, (1, LANES), 1)
    half = [lane < HEAD_DIM, lane >= HEAD_DIM]
    hi = lax.broadcasted_iota(jnp.int32, (LANES, LANES), 0) // HEAD_DIM
    hj = lax.broadcasted_iota(jnp.int32, (LANES, LANES), 1) // HEAD_DIM
    same_head = (hi == hj).astype(BF16)
    qg = qg_ref[...] * (HEAD_DIM ** -0.5 * LOG2E)
    kg = kg_ref[...]
    qblk = NA_QROWS * GRID_W
    kc = NA_KEY_CHUNK
    sub = 8

    for hh in range(2):
        vt_s[hh, HEAD_DIM:, :] = jnp.ones((vt_s.shape[1] - HEAD_DIM, t), BF16)

    def prep(r0s):
        qs = [q_ref[0, pl.ds(r0, ROW_TILE), :] for r0 in r0s]
        ks = [k_ref[0, pl.ds(r0, ROW_TILE), :] for r0 in r0s]
        vts = [v_ref[0, pl.ds(r0, ROW_TILE), :].T for r0 in r0s]
        qms = [_dot_f32_rhs01(q * q, same_head) * (1.0 / HEAD_DIM) for q in qs]
        kms = [_dot_f32_rhs01(k * k, same_head) * (1.0 / HEAD_DIM) for k in ks]
        qnts = [(q * lax.rsqrt(m + EPS) * qg).T for q, m in zip(qs, qms)]
        kns = [k * lax.rsqrt(m + EPS) * kg for k, m in zip(ks, kms)]
        for r0, qnt, kn, vt in zip(r0s, qnts, kns, vts):
            qt_s[:, pl.ds(r0, ROW_TILE)] = qnt.astype(BF16)
            for hh in range(2):
                kz_s[hh, pl.ds(r0, ROW_TILE), :] = jnp.where(half[hh], kn, 0.0).astype(BF16)
                vt_s[hh, 0:HEAD_DIM, pl.ds(r0, ROW_TILE)] = vt[hh * HEAD_DIM:(hh + 1) * HEAD_DIM].astype(BF16)

    n_tiles = t // ROW_TILE

    def prep_pair(j, carry):
        prep([pl.multiple_of((2 * j + i) * ROW_TILE, ROW_TILE) for i in range(2)])
        return carry

    lax.fori_loop(0, n_tiles // 2, prep_pair, 0)
    if n_tiles % 2:
        prep([(n_tiles - 1) * ROW_TILE])

    def attend(blocks):
        jobs = [(b, hh) for b in range(len(blocks)) for hh in range(2)]
        qts = [qt_s[:, pl.ds(q0, qblk)] for q0, _ in blocks]
        n_chunks = len(blocks[0][1])
        mx = [None] * len(jobs)
        def continues(prev, cur):
            if prev[1] is None or cur[1] is None:
                return prev[1] is None and cur[1] is None and cur[0] == prev[0] + kc
            return cur[1][1] == prev[1][1] + kc

        runs = []
        for ci in range(n_chunks):
            if ci and continues(blocks[0][1][ci - 1], blocks[0][1][ci]):
                runs[-1].append(ci)
            else:
                runs.append([ci])
        for run in runs:
            scores = [_dot(kz_s[hh, pl.ds(blocks[b][1][run[0]][0], kc * len(run)), :], qts[b]) for b, hh in jobs]
            for ri, ci in enumerate(run):
                for ji, (b, hh) in enumerate(jobs):
                    tab = blocks[b][1][ci][1]
                    s = scores[ji][ri * kc:(ri + 1) * kc]
                    if tab is not None:
                        s = s + bias_ref[0, hh, tab[0], tab[1]:tab[1] + kc, :]
                    s_s[ji, ci * kc:(ci + 1) * kc, :] = s
                    cm = functools.reduce(jnp.maximum, [s[i * sub:(i + 1) * sub] for i in range(kc // sub)])
                    mx[ji] = cm if mx[ji] is None else jnp.maximum(mx[ji], cm)
        ms = [m.max(axis=0, keepdims=True) for m in mx]
        accs = [None] * len(jobs)
        for ci in range(n_chunks):
            for ji, (b, hh) in enumerate(jobs):
                k0, _ = blocks[b][1][ci]
                p = jnp.exp2(s_s[ji, ci * kc:(ci + 1) * kc, :] - ms[ji]).astype(BF16)
                pv = _dot(vt_s[hh, :, pl.ds(k0, kc)], p)
                accs[ji] = pv if accs[ji] is None else accs[ji] + pv
        outs = []
        for b in range(len(blocks)):
            heads = [accs[2 * b + hh][0:HEAD_DIM] / accs[2 * b + hh][HEAD_DIM:HEAD_DIM + 1] for hh in range(2)]
            outs.append(jnp.concatenate(heads, axis=0).T)
        return outs

    ctx_chunks = [(ci * kc, None) for ci in range(n_ctx // kc)]
    o_ref[0, 0:n_ctx, :] = attend([(0, ctx_chunks)])[0].astype(o_ref.dtype)

    n_blocks = n_rows // NA_QROWS

    def group(gi, carry):
        blocks = []
        for j in range(NA_GROUP):
            bi = gi * NA_GROUP + j
            r0 = bi * NA_QROWS
            sb = jnp.clip(r0 - WIN_R // 2, 0, n_rows - NA_BAND)
            case = jnp.where(bi == 0, 0, jnp.where(bi == n_blocks - 1, 2, 1))
            q0 = pl.multiple_of(n_ctx + r0 * GRID_W, qblk)
            k0 = pl.multiple_of(n_ctx + sb * GRID_W, qblk)
            local = [(pl.multiple_of(k0 + ci * kc, kc), (case, ci * kc)) for ci in range(NA_BAND * GRID_W // kc)]
            blocks.append((q0, local + ctx_chunks))
        for (q0, _), out in zip(blocks, attend(blocks)):
            o_ref[0, pl.ds(q0, qblk), :] = out.astype(o_ref.dtype)
        return carry

    lax.fori_loop(0, n_blocks // NA_GROUP, group, 0)


def _na_bias_tables(rpb):
    n_layers, n_heads = rpb.shape[:2]
    cols = np.arange(GRID_W)
    col0 = np.clip(cols - WIN_C // 2, 0, GRID_W - WIN_C)
    valid = (cols[:, None] >= col0[None, :]) & (cols[:, None] < col0[None, :] + WIN_C)
    dc = cols[:, None] - cols[None, :] + WIN_C - 1
    col_sel = ((dc[..., None] == np.arange(RPB_C)) & valid[..., None]).astype(np.float32)
    toep = jnp.einsum('lhrd,xcd->lhrxc', rpb, col_sel, precision=lax.Precision.HIGHEST)
    toep = jnp.where(valid, toep * LOG2E, NEG * LOG2E).astype(F32)
    nk, nq = NA_BAND * GRID_W, NA_QROWS * GRID_W
    return pl.pallas_call(
        _na_bias_kernel,
        grid=(n_layers * n_heads,),
        in_specs=[pl.BlockSpec((1, RPB_R, GRID_W, GRID_W), lambda i: (i, 0, 0, 0))],
        out_specs=pl.BlockSpec((1, 3, nk, nq), lambda i: (i, 0, 0, 0)),
        out_shape=jax.ShapeDtypeStruct((n_layers * n_heads, 3, nk, nq), F32),
        compiler_params=_params(("arbitrary",)),
    )(toep.reshape(n_layers * n_heads, RPB_R, GRID_W, GRID_W)).reshape(n_layers, n_heads, 3, nk, nq)


def _na_bias_kernel(t_ref, o_ref):
    blank = jnp.full((GRID_W, GRID_W), NEG * LOG2E, F32)
    for case in range(3):
        for jj in range(NA_BAND):
            tiles = []
            for i in range(NA_QROWS):
                off = (0, i, NA_BAND - WIN_R)[case]
                rel = (jj - i, jj - i - WIN_R // 2, jj - i - NA_BAND + NA_QROWS)[case]
                tiles.append(t_ref[0, rel + WIN_R - 1] if off <= jj < off + WIN_R else blank)
            o_ref[0, case, jj * GRID_W:(jj + 1) * GRID_W, :] = jnp.concatenate(tiles, axis=1)


def _na(u, qn_g, kn_g, bias, layer, n_ctx):
    b, _, t, _ = u.shape
    n_rows = (t - n_ctx) // GRID_W
    assert n_rows % (NA_QROWS * NA_GROUP) == 0 and n_rows >= NA_BAND and n_ctx == NA_QROWS * GRID_W
    pairs = NA_HEADS // 2
    nk, nq = NA_BAND * GRID_W, NA_QROWS * GRID_W
    bias = bias.reshape(bias.shape[0], pairs, 2, 3, nk, nq)
    qg = jnp.tile(qn_g, 2).reshape(1, LANES)
    kg = jnp.tile(kn_g, 2).reshape(1, LANES)
    kern = functools.partial(_na_kernel, n_ctx=n_ctx, n_rows=n_rows)
    col = lambda which: pl.BlockSpec((1, None, t, LANES), lambda hp, bi: (bi, which * pairs + hp, 0, 0))
    return pl.pallas_call(
        kern,
        grid=(pairs, b),
        in_specs=[
            col(0), col(1), col(2),
            pl.BlockSpec((1, LANES), lambda hp, bi: (0, 0)),
            pl.BlockSpec((1, LANES), lambda hp, bi: (0, 0)),
            pl.BlockSpec((None, 1, 2, 3, nk, nq), lambda hp, bi: (layer, hp, 0, 0, 0, 0)),
        ],
        out_specs=col(0),
        out_shape=jax.ShapeDtypeStruct((b, pairs, t, LANES), BF16),
        scratch_shapes=[
            pltpu.VMEM((LANES, t), BF16),
            pltpu.VMEM((2, t, LANES), BF16),
            pltpu.VMEM((2, HEAD_DIM + NA_ONES_ROWS, t), BF16),
            pltpu.VMEM((2 * NA_GROUP, nk + n_ctx, nq), F32),
        ],
        compiler_params=_params(("arbitrary", "arbitrary")),
    )(u, u, u, qg, kg, bias)


def _lru_kernel(x_ref, g_ref, cw_ref, cb_ref, wa_ref, ba_ref, wx_ref, bx_ref, lam_ref, o_ref,
                xp_s, xc_s, hf_s, a_s, b_s, h_s, *, n_tiles):
    tl = LRU_TILE
    halo = LRU_HALO
    t = x_ref.shape[1]
    width = x_ref.shape[2]
    group = 8

    xp_s[0:halo, :] = jnp.zeros((halo, width), F32)
    xp_s[halo + t:halo + t + halo, :] = jnp.zeros((halo, width), F32)

    def fill(i, carry):
        r0 = pl.multiple_of(i * tl, tl)
        xp_s[pl.ds(pl.multiple_of(r0 + halo, halo), tl), :] = x_ref[0, pl.ds(r0, tl), :]
        return carry

    lax.fori_loop(0, n_tiles, fill, 0)

    row = lax.broadcasted_iota(jnp.int32, (tl, 1), 0)
    rowg = row % group
    cw = cw_ref[...]
    cb = cb_ref[...]

    def conv_tile(i):
        r0 = pl.multiple_of(i * tl, tl)
        win = xp_s[pl.ds(r0, tl + 2 * halo), :]
        n_win = tl + 2 * halo
        xm2 = pltpu.roll(win, 2, 0)[halo:halo + tl]
        xm1 = pltpu.roll(win, 1, 0)[halo:halo + tl]
        x0 = win[halo:halo + tl]
        xp1 = pltpu.roll(win, n_win - 1, 0)[halo:halo + tl]
        lo = i <= 1
        hi = i == 0
        xm2 = jnp.where(lo & (row < 2), 0.0, xm2)
        xm1 = jnp.where(lo & (row < 1), 0.0, xm1)
        xp1 = jnp.where(hi & (row == tl - 1), 0.0, xp1)
        return xm2 * cw[0:1] + xm1 * cw[1:2] + x0 * cw[2:3] + xp1 * cw[3:4] + cb

    def gates(i, d):
        r0 = pl.multiple_of(i * tl, tl)
        if d == 0:
            xc = conv_tile(i)
            xc_s[pl.ds(r0, tl), :] = xc
        else:
            xc = xc_s[pl.ds(r0, tl), :]
        xb = xc.astype(BF16)
        r = _sigmoid(_dot(xb, wa_ref[d]) + ba_ref[d:d + 1, :])
        ig = _sigmoid(_dot(xb, wx_ref[d]) + bx_ref[d:d + 1, :])
        log_a = -LRU_C * r * _softplus(-lam_ref[d:d + 1, :])
        a = jnp.exp(log_a)
        bt = jnp.sqrt(-jnp.tanh(log_a) * (a * a + 1.0)) * (ig * xc)
        return a, bt

    def scan_tile(i, d, carry):
        a, bt = gates(i, d)

        def shift(x, kk):
            x3 = x.reshape(tl // group, group, width)
            return pltpu.roll(x3, kk if d == 0 else group - kk, 1).reshape(tl, width)

        for kk in (1, 2, 4):
            mk = rowg >= kk if d == 0 else rowg < group - kk
            bt = jnp.where(mk, a * shift(bt, kk) + bt, bt)
            a = jnp.where(mk, a * shift(a, kk), a)
        a_s[...] = a
        b_s[...] = bt
        n_groups = tl // group

        def grp(gidx, c):
            gg = gidx if d == 0 else n_groups - 1 - gidx
            g0 = pl.multiple_of(gg * group, group)
            h = b_s[pl.ds(g0, group), :] + a_s[pl.ds(g0, group), :] * c
            h_s[pl.ds(g0, group), :] = h
            return h[group - 1:group, :] if d == 0 else h[0:1, :]

        return lax.fori_loop(0, n_groups, grp, carry, unroll=8)

    def fwd(i, carry):
        carry = scan_tile(i, 0, carry)
        r0 = pl.multiple_of(i * tl, tl)
        hf_s[pl.ds(r0, tl), :] = h_s[...]
        return carry

    def bwd(i, carry):
        carry = scan_tile(i, 1, carry)
        r0 = pl.multiple_of(i * tl, tl)
        y = (hf_s[pl.ds(r0, tl), :] + h_s[...]) * jax.nn.gelu(g_ref[0, pl.ds(r0, tl), :])
        o_ref[0, pl.ds(r0, tl), :] = y.astype(o_ref.dtype)
        return carry

    h0 = jnp.zeros((1, width), F32)
    lax.fori_loop(0, n_tiles, fwd, h0)
    carry = bwd(jnp.int32(0), h0)
    lax.fori_loop(1, n_tiles, lambda j, c: bwd(n_tiles - j, c), carry)


def _block_diag_dense(w):
    nd, k, c, _ = w.shape
    eye = jnp.eye(k, dtype=w.dtype)
    return jnp.einsum('dkce,kj->dkcje', w, eye).reshape(nd, k * c, k * c)


def _lru(u, conv_w, conv_b, w_a, b_a, w_x, b_x, lam, n_ctx):
    b, _, t, _ = u.shape
    width = conv_w.shape[1]
    assert n_ctx == LRU_TILE and t % LRU_TILE == 0
    n_tiles = t // LRU_TILE
    wa = _block_diag_dense(w_a).astype(BF16)
    wx = _block_diag_dense(w_x).astype(BF16)
    kern = functools.partial(_lru_kernel, n_tiles=n_tiles)
    full = lambda shape: pl.BlockSpec(shape, lambda bi: (0,) * len(shape))
    return pl.pallas_call(
        kern,
        grid=(b,),
        in_specs=[
            pl.BlockSpec((1, None, t, width), lambda bi: (bi, 0, 0, 0)),
            pl.BlockSpec((1, None, t, width), lambda bi: (bi, 1, 0, 0)),
            full((CONV_W, width)),
            full((1, width)),
            full((2, width, width)),
            full((2, width)),
            full((2, width, width)),
            full((2, width)),
            full((2, width)),
        ],
        out_specs=pl.BlockSpec((1, t, width), lambda bi: (bi, 0, 0)),
        out_shape=jax.ShapeDtypeStruct((b, t, width), BF16),
        scratch_shapes=[
            pltpu.VMEM((t + 2 * LRU_HALO, width), F32),
            pltpu.VMEM((t, width), F32),
            pltpu.VMEM((t, width), F32),
            pltpu.VMEM((LRU_TILE, width), F32),
            pltpu.VMEM((LRU_TILE, width), F32),
            pltpu.VMEM((LRU_TILE, width), F32),
        ],
        compiler_params=_params(("arbitrary",)),
    )(u, u, conv_w, conv_b.reshape(1, width), wa, b_a, wx, b_x, lam)


def _out_mlp_kernel(*refs, hid_tile):
    xc_ref, xl_ref = refs[0:2]
    ym_refs, yn_refs, yl_refs = (refs[2 + k * MLP_SUB:2 + (k + 1) * MLP_SUB] for k in range(3))
    mod_ref, g_ref, wo_ref, w1_ref, w2_ref, oc_ref, ol_ref = refs[2 + 3 * MLP_SUB:]

    def body(x, ym, yn, yl, o_ref):
        mod = mod_ref[0, 0]
        att = _dot(ym.astype(BF16), wo_ref[0:M_WIDTH, :])
        att = att + _dot(yn.astype(BF16), wo_ref[M_WIDTH:M_WIDTH + NA_WIDTH, :])
        att = att + _dot(yl.astype(BF16), wo_ref[M_WIDTH + NA_WIDTH:, :])
        x1 = x + mod[2:3, :] * att
        ms = jnp.mean(x1 * x1, axis=-1, keepdims=True)
        y = x1 * lax.rsqrt(ms + EPS) * g_ref[...]
        h = (y * (1.0 + mod[4:5, :]) + mod[3:4, :]).astype(BF16)
        hidden = w1_ref.shape[1]
        acc = None
        for j in range(hidden // hid_tile):
            a = jnp.maximum(_dot(h, w1_ref[:, j * hid_tile:(j + 1) * hid_tile]), 0.0)
            part = _dot((a * a).astype(BF16), w2_ref[j * hid_tile:(j + 1) * hid_tile, :])
            acc = part if acc is None else acc + part
        o_ref[0] = x1 + mod[5:6, :] * acc

    def heads(ref):
        return jnp.concatenate([ref[0, j] for j in range(ref.shape[1])], axis=1)

    def rows(tiles):
        return jnp.concatenate(tiles, axis=0)

    @pl.when(pl.program_id(1) == 0)
    def _():
        body(xc_ref[0], ym_refs[0][0], heads(yn_refs[0]), yl_refs[0][0], oc_ref)

    @pl.when(pl.program_id(1) > 0)
    def _():
        body(xl_ref[0], rows([r[0] for r in ym_refs]), rows([heads(r) for r in yn_refs]),
             rows([r[0] for r in yl_refs]), ol_ref)


def _out_mlp(xc, xl, ym, yn, yl, modv, g2, w_out, w1, w2, layer):
    b, n_ctx, d = xc.shape
    n_lat = xl.shape[1]
    lat_tile = MLP_SUB * ROW_TILE
    assert n_ctx == ROW_TILE and n_lat % lat_tile == 0
    hidden = w1.shape[2]
    blk = lambda j: (lambda s: jnp.maximum(MLP_SUB * (s - 1) + 1 + j, 0))
    tok = lambda w: [pl.BlockSpec((1, ROW_TILE, w), lambda bi, s, f=blk(j): (bi, f(s), 0)) for j in range(MLP_SUB)]
    pairs = [pl.BlockSpec((1, NA_HEADS // 2, ROW_TILE, LANES), lambda bi, s, f=blk(j): (bi, 0, f(s), 0))
             for j in range(MLP_SUB)]
    stream = [
        pl.BlockSpec((1, ROW_TILE, d), lambda bi, s: (bi, 0, 0)),
        pl.BlockSpec((1, lat_tile, d), lambda bi, s: (bi, jnp.maximum(s - 1, 0), 0)),
    ]
    kern = functools.partial(_out_mlp_kernel, hid_tile=1024)
    return pl.pallas_call(
        kern,
        grid=(b, 1 + n_lat // lat_tile),
        in_specs=stream + tok(M_WIDTH) + pairs + tok(d - M_WIDTH - NA_WIDTH) + [
            pl.BlockSpec((1, 1, 6, d), lambda bi, s: (bi, jnp.minimum(s, 1), 0, 0)),
            pl.BlockSpec((1, d), lambda bi, s: (0, 0)),
            _resident((None, d, d), lambda bi, s: (layer, 0, 0)),
            _resident((None, d, hidden), lambda bi, s: (layer, 0, 0)),
            _resident((None, hidden, d), lambda bi, s: (layer, 0, 0)),
        ],
        out_specs=stream,
        out_shape=[jax.ShapeDtypeStruct(xc.shape, F32), jax.ShapeDtypeStruct(xl.shape, F32)],
        compiler_params=_params(("arbitrary", "arbitrary")),
    )(xc, xl, *([ym] * MLP_SUB + [yn] * MLP_SUB + [yl] * MLP_SUB), modv, g2.reshape(1, d), w_out, w1, w2)


def _split_w_in(w_in):
    m_end = 4 * M_WIDTH
    na_end = m_end + N_GATES + 3 * NA_WIDTH
    main = jnp.concatenate([w_in[..., :m_end], w_in[..., na_end:], w_in[..., m_end + N_GATES:na_end]], axis=-1)
    gates = jnp.swapaxes(w_in[..., m_end:m_end + N_GATES], -1, -2)
    return main.astype(BF16), gates.astype(BF16)


def kernel(x, c, ctx, c_ctx, w_mod, b_mod, norm1_g, norm2_g, w_in, mlstm_gate_b, mlstm_norm_g, na_q_norm_g, na_k_norm_g, na_rpb, lru_conv_w, lru_conv_b, lru_w_a, lru_b_a, lru_w_x, lru_b_x, lru_lambda, w_out, w_mlp1, w_mlp2):
    b, n, d = x.shape
    n_ctx = ctx.shape[1]
    depth = w_in.shape[0]
    assert n_ctx == ROW_TILE and n % ROW_TILE == 0 and n % GRID_W == 0

    cond_rows = 8 * ((b + 1 + 7) // 8)
    cond = jnp.zeros((cond_rows, d), F32).at[:b].set(c).at[b].set(c_ctx)
    mod_all = _modulation(cond, w_mod, b_mod)
    rope_row, rope_col = _rope_tables(n // GRID_W)
    w_in_p, w_gate_t = _split_w_in(w_in)
    na_bias = _na_bias_tables(na_rpb)
    w_out_b, w1_b, w2_b = w_out.astype(BF16), w_mlp1.astype(BF16), w_mlp2.astype(BF16)

    xc, xl = ctx, x
    for l in range(depth):
        lat = mod_all[l, :b].reshape(b, 6, d)
        cx = jnp.broadcast_to(mod_all[l, b].reshape(1, 6, d), (b, 6, d))
        modv = jnp.stack([cx, lat], axis=1)
        um, ul, un, g_rows = _inproj(xc, xl, modv, norm1_g[l], w_in_p, w_gate_t, l)
        ym = _mlstm(um, g_rows, mlstm_gate_b[l], mlstm_norm_g[l], rope_row, rope_col, n_ctx)
        yn = _na(un, na_q_norm_g[l], na_k_norm_g[l], na_bias, l, n_ctx)
        yl = _lru(ul, lru_conv_w[l], lru_conv_b[l], lru_w_a[l], lru_b_a[l], lru_w_x[l], lru_b_x[l], lru_lambda[l], n_ctx)
        xc, xl = _out_mlp(xc, xl, ym, yn, yl, modv, norm2_g[l], w_out_b, w1_b, w2_b, l)
    return xl
```

```python
import functools
import math

import numpy as np
import jax
import jax.numpy as jnp
from jax import lax
from jax.experimental import pallas as pl
from jax.experimental.pallas import tpu as pltpu

F32 = jnp.float32
BF16 = jnp.bfloat16

HEAD_DIM = 64
M_HEADS = 4
M_WIDTH = M_HEADS * HEAD_DIM
NA_HEADS = 8
NA_WIDTH = NA_HEADS * HEAD_DIM
N_GATES = 4 * M_HEADS
GRID_W = 64
M_CHUNK = 128
M_AUG = HEAD_DIM + 16
M_LOCAL_GROUP = 2
M_LATENT_GROUP = 4
WIN_R = 8
WIN_C = 16
RPB_R = 2 * WIN_R - 1
RPB_C = 2 * WIN_C - 1
NA_QROWS = 4
NA_BAND = NA_QROWS + WIN_R
NA_KEY_CHUNK = 128
NA_ONES_ROWS = 16
NA_GROUP = 8
LOG2E = 1.4426950408889634
CONV_W = 4
LRU_C = 8.0
ROPE_BASE = 10000.0
EPS = 1e-6
NEG = -1e30

LANES = 128
ROW_TILE = 256
MLP_SUB = 4
LRU_TILE = 256
LRU_HALO = 8
VMEM_LIMIT = 56 * 1024 * 1024

U_MQ, U_MK, U_MV, U_MO = 0, 256, 512, 768
U_XR, U_GT = 1024, 1280
U_NQ, U_NK, U_NV = 1536, 2048, 2560
U_COLS = 3072
LRU_WIDTH = U_NQ - U_GT
NA_COL_BLOCKS = (U_COLS - U_NQ) // LANES


def _dot(a, b):
    return jnp.dot(a, b, preferred_element_type=F32)


def _dot_f32_rhs01(x, m01):
    hi = x.astype(BF16)
    lo = (x - hi.astype(F32)).astype(BF16)
    return _dot(hi, m01) + _dot(lo, m01)


def _sigmoid(x):
    return 0.5 * jnp.tanh(0.5 * x) + 0.5


def _log_sigmoid(x):
    return jnp.minimum(x, 0.0) - jnp.log1p(jnp.exp(-jnp.abs(x)))


def _softplus(x):
    return jnp.maximum(x, 0.0) + jnp.log1p(jnp.exp(-jnp.abs(x)))


def _params(sem):
    return pltpu.CompilerParams(dimension_semantics=sem, vmem_limit_bytes=VMEM_LIMIT)


def _resident(shape, index_map):
    return pl.BlockSpec(shape, index_map, pipeline_mode=pl.Buffered(1))


def _mod_kernel(a_ref, w_ref, b_ref, o_ref):
    a = a_ref[...]
    act = (a * jax.nn.sigmoid(a)).astype(BF16)
    o_ref[0] = _dot(act, w_ref[0].astype(BF16)) + b_ref[0]


def _modulation(cond, w_mod, b_mod):
    depth, d, n6 = w_mod.shape
    rows = cond.shape[0]
    tn = 1536
    return pl.pallas_call(
        _mod_kernel,
        grid=(depth, n6 // tn),
        in_specs=[
            pl.BlockSpec((rows, d), lambda l, j: (0, 0)),
            pl.BlockSpec((1, d, tn), lambda l, j: (l, 0, j)),
            pl.BlockSpec((1, 1, tn), lambda l, j: (l, 0, j)),
        ],
        out_specs=pl.BlockSpec((1, rows, tn), lambda l, j: (l, 0, j)),
        out_shape=jax.ShapeDtypeStruct((depth, rows, n6), F32),
        compiler_params=_params(("arbitrary", "arbitrary")),
    )(cond, w_mod, b_mod.reshape(depth, 1, n6))


def _inproj_kernel(xc_ref, xa_ref, xb_ref, mod_ref, g_ref, w_ref, wg_ref, um_ref, ul_ref, un_ref, gt_ref,
                   *, n_steps, last_has_second):
    def half(x, mod, r0):
        ms = jnp.mean(x * x, axis=-1, keepdims=True)
        y = x * lax.rsqrt(ms + EPS) * g_ref[...]
        h = (y * (1.0 + mod[1:2, :]) + mod[0:1, :]).astype(BF16)
        u = _dot(h, w_ref[...])
        rows = slice(r0, r0 + ROW_TILE)
        um_ref[0, rows, :] = u[:, 0:U_XR]
        for j in range(ul_ref.shape[1]):
            ul_ref[0, j, rows, :] = u[:, U_XR + j * LRU_WIDTH:U_XR + (j + 1) * LRU_WIDTH]
        for j in range(un_ref.shape[1]):
            un_ref[0, j, rows, :] = u[:, U_NQ + j * LANES:U_NQ + (j + 1) * LANES]
        gt_ref[0, :, rows] = lax.dot_general(wg_ref[...], h, (((1,), (1,)), ((), ())), preferred_element_type=F32)

    mod_ctx, mod_lat = mod_ref[0, 0], mod_ref[0, 1]

    @pl.when(pl.program_id(1) == 0)
    def _():
        half(xc_ref[0], mod_ctx, 0)
        half(xb_ref[0], mod_lat, ROW_TILE)

    step = pl.program_id(1)
    both_latent = (step > 0) if last_has_second else (step > 0) & (step < n_steps - 1)

    @pl.when(both_latent)
    def _():
        half(xa_ref[0], mod_lat, 0)
        half(xb_ref[0], mod_lat, ROW_TILE)

    if not last_has_second:
        @pl.when(step == n_steps - 1)
        def _():
            half(xa_ref[0], mod_lat, 0)


def _inproj(xc, xl, modv, g1, w_in_p, w_gate_t, layer):
    b, n_ctx, d = xc.shape
    assert n_ctx == ROW_TILE and xl.shape[1] % ROW_TILE == 0
    n_lat_tiles = xl.shape[1] // ROW_TILE
    t = n_ctx + xl.shape[1]
    step_rows = 2 * ROW_TILE
    n_steps = pl.cdiv(t, step_rows)
    assert n_steps > 1
    kern = functools.partial(_inproj_kernel, n_steps=n_steps, last_has_second=(t % step_rows == 0))
    return pl.pallas_call(
        kern,
        grid=(b, n_steps),
        in_specs=[
            pl.BlockSpec((1, ROW_TILE, d), lambda bi, s: (bi, 0, 0)),
            pl.BlockSpec((1, ROW_TILE, d), lambda bi, s: (bi, jnp.maximum(2 * s - 1, 0), 0)),
            pl.BlockSpec((1, ROW_TILE, d), lambda bi, s: (bi, jnp.minimum(2 * s, n_lat_tiles - 1), 0)),
            pl.BlockSpec((1, 2, 6, d), lambda bi, s: (bi, 0, 0, 0)),
            pl.BlockSpec((1, d), lambda bi, s: (0, 0)),
            _resident((None, d, U_COLS), lambda bi, s: (layer, 0, 0)),
            _resident((None, N_GATES, d), lambda bi, s: (layer, 0, 0)),
        ],
        out_specs=[
            pl.BlockSpec((1, step_rows, U_XR), lambda bi, s: (bi, s, 0)),
            pl.BlockSpec((1, 2, step_rows, LRU_WIDTH), lambda bi, s: (bi, 0, s, 0)),
            pl.BlockSpec((1, NA_COL_BLOCKS, step_rows, LANES), lambda bi, s: (bi, 0, s, 0)),
            pl.BlockSpec((1, N_GATES, step_rows), lambda bi, s: (bi, 0, s)),
        ],
        out_shape=[
            jax.ShapeDtypeStruct((b, t, U_XR), F32),
            jax.ShapeDtypeStruct((b, 2, t, LRU_WIDTH), F32),
            jax.ShapeDtypeStruct((b, NA_COL_BLOCKS, t, LANES), F32),
            jax.ShapeDtypeStruct((b, N_GATES, t), F32),
        ],
        compiler_params=_params(("arbitrary", "arbitrary")),
    )(xc, xl, xl, modv, g1.reshape(1, d), w_in_p, w_gate_t)


def _mlstm_kernel(u_ref, og_ref, gr_ref, gb_ref, ng_ref, rrow_ref, rcol_ref, o_ref,
                  qt_s, k_s, vt_s, r_s, row_s, c_s, *, n_ctx_chunks, n_chunks):
    L = M_CHUNK
    W = M_WIDTH
    D = HEAD_DIM
    groups = L // GRID_W

    for h in range(M_HEADS):
        vt_s[h, D:, :] = jnp.ones((M_AUG - D, vt_s.shape[2]), BF16)
    c_s[...] = jnp.zeros(c_s.shape, F32)

    def prep(cs, latent):
        r0s = [pl.multiple_of(c * L, L) for c in cs]
        qs = [u_ref[0, pl.ds(r0, L), U_MQ:U_MQ + W] for r0 in r0s]
        ks = [u_ref[0, pl.ds(r0, L), U_MK:U_MK + W] * D ** -0.5 for r0 in r0s]
        vts = [u_ref[0, pl.ds(r0, L), U_MV:U_MV + W].T for r0 in r0s]
        if latent:
            def table(c, i):
                first = (c - n_ctx_chunks) * groups
                return jnp.concatenate([rrow_ref[i, pl.ds(first + g, 1), :] + rcol_ref[i] for g in range(groups)], axis=0)

            def rope(x, c):
                return (x * table(c, 0) + pltpu.roll(x, W - D // 4, 1) * table(c, 1)
                        + pltpu.roll(x, D // 4, 1) * table(c, 2))

            qs = [rope(q, c) for q, c in zip(qs, cs)]
            ks = [rope(k, c) for k, c in zip(ks, cs)]
        qts = [q.T for q in qs]
        for r0, qt, k, vt in zip(r0s, qts, ks, vts):
            qt_s[:, pl.ds(r0, L)] = qt.astype(BF16)
            k_s[pl.ds(r0, L), :] = k.astype(BF16)
            for h in range(M_HEADS):
                vt_s[h, 0:D, pl.ds(r0, L)] = vt[h * D:(h + 1) * D].astype(BF16)

    assert n_ctx_chunks % M_LOCAL_GROUP == 0 and n_chunks % M_LOCAL_GROUP == 0

    def prep_group(latent):
        def body(j, carry):
            first = j * M_LOCAL_GROUP + (n_ctx_chunks if latent else 0)
            prep([first + i for i in range(M_LOCAL_GROUP)], latent)
            return carry
        return body

    lax.fori_loop(0, n_ctx_chunks // M_LOCAL_GROUP, prep_group(False), 0)
    lax.fori_loop(0, (n_chunks - n_ctx_chunks) // M_LOCAL_GROUP, prep_group(True), 0)

    si = lax.broadcasted_iota(jnp.int32, (L, L), 0)
    li = lax.broadcasted_iota(jnp.int32, (L, L), 1)
    keeps = [si <= li, si >= li]
    gbias = [jnp.broadcast_to(gb_ref[8 * d:8 * d + 8, :], (8, L)) for d in range(2)]
    zrows = jnp.zeros((D, L), BF16)

    def qt_head(qtc, h):
        parts = [zrows] * M_HEADS
        parts[h] = qtc[h * D:(h + 1) * D]
        return jnp.concatenate(parts, axis=0)

    def local(first, n_group):
        r0s = [pl.multiple_of((first + j) * L, L) for j in range(n_group)]
        scores = [[_dot(k_s[pl.ds(r0, L), :], qt_head(qt_s[:, pl.ds(r0, L)], h)) for h in range(M_HEADS)]
                  for r0 in r0s]
        vtas = [[vt_s[h, :, pl.ds(r0, L)] for h in range(M_HEADS)] for r0 in r0s]
        g8s = [[gr_ref[0, 8 * d:8 * d + 8, pl.ds(r0, L)] + gbias[d] for d in range(2)] for r0 in r0s]
        cum4s = [[pltpu.roll(_dot_f32_rhs01(_log_sigmoid(g8[d]), keeps[d].astype(BF16)), M_HEADS, 0)
                  for d in range(2)] for g8 in g8s]
        src_ts = [[(g8[d] - cum4[d]).T for d in range(2)] for g8, cum4 in zip(g8s, cum4s)]
        jdh = [(j, d, h) for j in range(n_group) for d in range(2) for h in range(M_HEADS)]
        logws = [jnp.where(keeps[d], cum4s[j][d][h:h + 1] + jnp.broadcast_to(src_ts[j][d][:, h:h + 1], (L, L)), NEG)
                 for j, d, h in jdh]
        amaxs = [lw.max(axis=0, keepdims=True) for lw in logws]
        ps = [(scores[j][h] * jnp.exp(logw - a)).astype(BF16) for (j, d, h), logw, a in zip(jdh, logws, amaxs)]
        intras = [_dot(vtas[j][h], p) for (j, d, h), p in zip(jdh, ps)]
        for (j, d, h), a, intra in zip(jdh, amaxs, intras):
            last = L - 1 if d == 0 else 0
            cl = cum4s[j][d][h:h + 1]
            a_last = jnp.broadcast_to(a[:, last:last + 1], (1, L))
            cl_last = jnp.broadcast_to(cl[:, last:last + 1], (1, L))
            w_src = jnp.exp(cl_last - cl + g8s[j][d][h:h + 1] - a_last)
            r_s[d, h * M_AUG:(h + 1) * M_AUG, pl.ds(r0s[j], L)] = intra
            for kind, row in enumerate((cl, a, cl_last, a_last, w_src)):
                row_s[d, kind, h:h + 1, pl.ds(r0s[j], L)] = row

    def local_loop(first, count, n_group):
        assert count % n_group == 0

        def body(i, carry):
            local(first + i * n_group, n_group)
            return carry

        lax.fori_loop(0, count // n_group, body, 0)

    local_loop(0, n_ctx_chunks, M_LOCAL_GROUP)
    local_loop(n_ctx_chunks, n_chunks - n_ctx_chunks, M_LATENT_GROUP)

    def step(c, d, c_olds, m_prevs):
        r0 = pl.multiple_of(c * L, L)
        kc = k_s[pl.ds(r0, L), :]
        qtc = qt_s[:, pl.ds(r0, L)]
        h_ts, c_news, m_news = [], [], []
        rows = [[row_s[d, kind, h:h + 1, pl.ds(r0, L)] for kind in range(5)] for h in range(M_HEADS)]
        vws = [(vt_s[h, :, pl.ds(r0, L)].astype(F32) * rows[h][4]).astype(BF16) for h in range(M_HEADS)]
        upd = _dot(jnp.concatenate(vws, axis=0), kc)
        inters = [_dot(c_olds[h].astype(BF16), qt_head(qtc, h)) for h in range(M_HEADS)]
        for h in range(M_HEADS):
            cl, a, cl_last, a_last, _ = rows[h]
            m_prev = m_prevs[h]
            inter = cl + m_prev
            m_t = jnp.maximum(inter, a)
            r = (jnp.exp(inter - m_t) * inters[h]
                 + jnp.exp(a - m_t) * r_s[d, h * M_AUG:(h + 1) * M_AUG, pl.ds(r0, L)])
            denom = jnp.maximum(jnp.abs(r[D:D + 1]), jnp.exp(-m_t))
            h_ts.append(r[0:D] * (1.0 / denom))
            m_new = jnp.maximum(cl_last + m_prev, a_last)
            w_old = jnp.exp(cl_last + m_prev - m_new)
            w_upd = jnp.exp(a_last - m_new)
            scale = lambda w: jnp.concatenate([w] * (W // L), axis=1)
            c_news.append(scale(w_old) * c_olds[h] + scale(w_upd) * upd[h * M_AUG:(h + 1) * M_AUG])
            m_news.append(m_new)
        return r0, h_ts, c_news, m_news

    def body(i, m_all):
        chunks = [i, jnp.where(i < n_ctx_chunks, n_ctx_chunks - 1 - i, n_chunks - 1 + n_ctx_chunks - i)]
        c_olds = [[c_s[d, h] for h in range(M_HEADS)] for d in range(2)]
        outs = [step(chunks[d], d, c_olds[d], m_all[d]) for d in range(2)]
        for d in range(2):
            r0, h_ts, c_news, _ = outs[d]
            for h in range(M_HEADS):
                r_s[d, h * M_AUG:h * M_AUG + D, pl.ds(r0, L)] = h_ts[h]
                c_s[d, h] = c_news[h]
        return tuple(tuple(outs[d][3]) for d in range(2))

    m0 = tuple(tuple(jnp.zeros((1, L), F32) for _ in range(M_HEADS)) for _ in range(2))
    lax.fori_loop(0, n_chunks, body, m0)

    ng = jnp.broadcast_to(ng_ref[...], (W, L))

    def post(j, carry):
        r0s = [pl.multiple_of((j * M_LOCAL_GROUP + i) * L, L) for i in range(M_LOCAL_GROUP)]
        hts = [[r_s[0, h * M_AUG:h * M_AUG + D, pl.ds(r0, L)] + r_s[1, h * M_AUG:h * M_AUG + D, pl.ds(r0, L)]
                for h in range(M_HEADS)] for r0 in r0s]
        scales = [[lax.rsqrt(jnp.sum(ht * ht, axis=0, keepdims=True) * (1.0 / D) + EPS) for ht in hs] for hs in hts]
        hnts = [(jnp.concatenate([ht * sc for ht, sc in zip(hs, scs)], axis=0) * ng).T for hs, scs in zip(hts, scales)]
        for r0, hnt in zip(r0s, hnts):
            o_ref[0, pl.ds(r0, L), :] = (_sigmoid(og_ref[0, pl.ds(r0, L), :]) * hnt).astype(o_ref.dtype)
        return carry

    lax.fori_loop(0, n_chunks // M_LOCAL_GROUP, post, 0)


def _rope_tables(n_rows):
    nf = HEAD_DIM // 4
    inv = ROPE_BASE ** (-jnp.arange(nf, dtype=F32) / nf)
    d = np.arange(M_WIDTH) % HEAD_DIM
    f_idx = d % nf
    is_row = d < HEAD_DIM // 2
    first_quarter = (d % (HEAD_DIM // 2)) < nf

    def tables(pos, sel):
        ang = pos[:, None] * inv[None, :]
        ang = ang[:, f_idx]
        cos = jnp.where(sel, jnp.cos(ang), 0.0)
        sin = jnp.where(sel, jnp.sin(ang), 0.0)
        sin_a = jnp.where(first_quarter, -sin, 0.0)
        sin_b = jnp.where(first_quarter, 0.0, sin)
        return jnp.stack([cos, sin_a, sin_b]).astype(F32)

    row_t = tables(jnp.arange(n_rows, dtype=F32), is_row)
    col_t = tables(jnp.arange(GRID_W, dtype=F32), ~is_row)
    return row_t, col_t


def _mlstm(u, g_rows, gate_b, norm_g, rope_row, rope_col, n_ctx):
    b, t, _ = u.shape
    assert t % M_CHUNK == 0 and n_ctx % M_CHUNK == 0 and M_CHUNK % GRID_W == 0
    n_chunks = t // M_CHUNK
    kern = functools.partial(_mlstm_kernel, n_ctx_chunks=n_ctx // M_CHUNK, n_chunks=n_chunks)
    n_rows = rope_row.shape[1]
    return pl.pallas_call(
        kern,
        grid=(b,),
        in_specs=[
            pl.BlockSpec((1, t, U_MO), lambda bi: (bi, 0, 0), pipeline_mode=pl.Buffered(1)),
            pl.BlockSpec((1, t, M_WIDTH), lambda bi: (bi, 0, U_MO // M_WIDTH)),
            pl.BlockSpec((1, N_GATES, t), lambda bi: (bi, 0, 0)),
            pl.BlockSpec((N_GATES, 1), lambda bi: (0, 0)),
            pl.BlockSpec((M_WIDTH, 1), lambda bi: (0, 0)),
            pl.BlockSpec((3, n_rows, M_WIDTH), lambda bi: (0, 0, 0)),
            pl.BlockSpec((3, GRID_W, M_WIDTH), lambda bi: (0, 0, 0)),
        ],
        out_specs=pl.BlockSpec((1, t, M_WIDTH), lambda bi: (bi, 0, 0)),
        out_shape=jax.ShapeDtypeStruct((b, t, M_WIDTH), BF16),
        scratch_shapes=[
            pltpu.VMEM((M_WIDTH, t), BF16),
            pltpu.VMEM((t, M_WIDTH), BF16),
            pltpu.VMEM((M_HEADS, M_AUG, t), BF16),
            pltpu.VMEM((2, M_HEADS * M_AUG, t), F32),
            pltpu.VMEM((2, 5, 8, t), F32),
            pltpu.VMEM((2, M_HEADS, M_AUG, M_WIDTH), F32),
        ],
        compiler_params=_params(("arbitrary",)),
    )(u, u, g_rows, gate_b.reshape(N_GATES, 1), norm_g.reshape(M_WIDTH, 1), rope_row, rope_col)


def _na_kernel(q_ref, k_ref, v_ref, qg_ref, kg_ref, bias_ref, o_ref, qt_s, kz_s, vt_s, s_s, *, n_ctx, n_rows):
    t = q_ref.shape[1]
    lane = lax.broadcasted_iota(jnp.int32, (1, LANES), 1)
    half = [lane < HEAD_DIM, lane >= HEAD_DIM]
    hi = lax.broadcasted_iota(jnp.int32, (LANES, LANES), 0) // HEAD_DIM
    hj = lax.broadcasted_iota(jnp.int32, (LANES, LANES), 1) // HEAD_DIM
    same_head = (hi == hj).astype(BF16)
    qg = qg_ref[...] * (HEAD_DIM ** -0.5 * LOG2E)
    kg = kg_ref[...]
    qblk = NA_QROWS * GRID_W
    kc = NA_KEY_CHUNK
    sub = 8

    for hh in range(2):
        vt_s[hh, HEAD_DIM:, :] = jnp.ones((vt_s.shape[1] - HEAD_DIM, t), BF16)

    def prep(r0s):
        qs = [q_ref[0, pl.ds(r0, ROW_TILE), :] for r0 in r0s]
        ks = [k_ref[0, pl.ds(r0, ROW_TILE), :] for r0 in r0s]
        vts = [v_ref[0, pl.ds(r0, ROW_TILE), :].T for r0 in r0s]
        qms = [_dot_f32_rhs01(q * q, same_head) * (1.0 / HEAD_DIM) for q in qs]
        kms = [_dot_f32_rhs01(k * k, same_head) * (1.0 / HEAD_DIM) for k in ks]
        qnts = [(q * lax.rsqrt(m + EPS) * qg).T for q, m in zip(qs, qms)]
        kns = [k * lax.rsqrt(m + EPS) * kg for k, m in zip(ks, kms)]
        for r0, qnt, kn, vt in zip(r0s, qnts, kns, vts):
            qt_s[:, pl.ds(r0, ROW_TILE)] = qnt.astype(BF16)
            for hh in range(2):
                kz_s[hh, pl.ds(r0, ROW_TILE), :] = jnp.where(half[hh], kn, 0.0).astype(BF16)
                vt_s[hh, 0:HEAD_DIM, pl.ds(r0, ROW_TILE)] = vt[hh * HEAD_DIM:(hh + 1) * HEAD_DIM].astype(BF16)

    n_tiles = t // ROW_TILE

    def prep_pair(j, carry):
        prep([pl.multiple_of((2 * j + i) * ROW_TILE, ROW_TILE) for i in range(2)])
        return carry

    lax.fori_loop(0, n_tiles // 2, prep_pair, 0)
    if n_tiles % 2:
        prep([(n_tiles - 1) * ROW_TILE])

    def attend(blocks):
        jobs = [(b, hh) for b in range(len(blocks)) for hh in range(2)]
        qts = [qt_s[:, pl.ds(q0, qblk)] for q0, _ in blocks]
        n_chunks = len(blocks[0][1])
        mx = [None] * len(jobs)
        def continues(prev, cur):
            if prev[1] is None or cur[1] is None:
                return prev[1] is None and cur[1] is None and cur[0] == prev[0] + kc
            return cur[1][1] == prev[1][1] + kc

        runs = []
        for ci in range(n_chunks):
            if ci and continues(blocks[0][1][ci - 1], blocks[0][1][ci]):
                runs[-1].append(ci)
            else:
                runs.append([ci])
        for run in runs:
            scores = [_dot(kz_s[hh, pl.ds(blocks[b][1][run[0]][0], kc * len(run)), :], qts[b]) for b, hh in jobs]
            for ri, ci in enumerate(run):
                for ji, (b, hh) in enumerate(jobs):
                    tab = blocks[b][1][ci][1]
                    s = scores[ji][ri * kc:(ri + 1) * kc]
                    if tab is not None:
                        s = s + bias_ref[0, hh, tab[0], tab[1]:tab[1] + kc, :]
                    s_s[ji, ci * kc:(ci + 1) * kc, :] = s
                    cm = functools.reduce(jnp.maximum, [s[i * sub:(i + 1) * sub] for i in range(kc // sub)])
                    mx[ji] = cm if mx[ji] is None else jnp.maximum(mx[ji], cm)
        ms = [m.max(axis=0, keepdims=True) for m in mx]
        accs = [None] * len(jobs)
        for run in runs:
            assert len(run) % 2 == 0
            for ci in run[::2]:
                for ji, (b, hh) in enumerate(jobs):
                    k0, _ = blocks[b][1][ci]
                    p = jnp.exp2(s_s[ji, ci * kc:(ci + 2) * kc, :] - ms[ji]).astype(BF16)
                    pv = _dot(vt_s[hh, :, pl.ds(k0, 2 * kc)], p)
                    accs[ji] = pv if accs[ji] is None else accs[ji] + pv
        outs = []
        for b in range(len(blocks)):
            heads = [accs[2 * b + hh][0:HEAD_DIM] / accs[2 * b + hh][HEAD_DIM:HEAD_DIM + 1] for hh in range(2)]
            outs.append(jnp.concatenate(heads, axis=0).T)
        return outs

    ctx_chunks = [(ci * kc, None) for ci in range(n_ctx // kc)]
    o_ref[0, 0:n_ctx, :] = attend([(0, ctx_chunks)])[0].astype(o_ref.dtype)

    n_blocks = n_rows // NA_QROWS

    def group(gi, carry):
        blocks = []
        for j in range(NA_GROUP):
            bi = gi * NA_GROUP + j
            r0 = bi * NA_QROWS
            sb = jnp.clip(r0 - WIN_R // 2, 0, n_rows - NA_BAND)
            case = jnp.where(bi == 0, 0, jnp.where(bi == n_blocks - 1, 2, 1))
            q0 = pl.multiple_of(n_ctx + r0 * GRID_W, qblk)
            k0 = pl.multiple_of(n_ctx + sb * GRID_W, qblk)
            local = [(pl.multiple_of(k0 + ci * kc, kc), (case, ci * kc)) for ci in range(NA_BAND * GRID_W // kc)]
            blocks.append((q0, local + ctx_chunks))
        for (q0, _), out in zip(blocks, attend(blocks)):
            o_ref[0, pl.ds(q0, qblk), :] = out.astype(o_ref.dtype)
        return carry

    lax.fori_loop(0, n_blocks // NA_GROUP, group, 0)


def _na_bias_tables(rpb):
    n_layers, n_heads = rpb.shape[:2]
    cols = np.arange(GRID_W)
    col0 = np.clip(cols - WIN_C // 2, 0, GRID_W - WIN_C)
    valid = (cols[:, None] >= col0[None, :]) & (cols[:, None] < col0[None, :] + WIN_C)
    dc = cols[:, None] - cols[None, :] + WIN_C - 1
    col_sel = ((dc[..., None] == np.arange(RPB_C)) & valid[..., None]).astype(np.float32)
    toep = jnp.einsum('lhrd,xcd->lhrxc', rpb, col_sel, precision=lax.Precision.HIGHEST)
    toep = jnp.where(valid, toep * LOG2E, NEG * LOG2E).astype(F32)
    nk, nq = NA_BAND * GRID_W, NA_QROWS * GRID_W
    return pl.pallas_call(
        _na_bias_kernel,
        grid=(n_layers * n_heads,),
        in_specs=[pl.BlockSpec((1, RPB_R, GRID_W, GRID_W), lambda i: (i, 0, 0, 0))],
        out_specs=pl.BlockSpec((1, 3, nk, nq), lambda i: (i, 0, 0, 0)),
        out_shape=jax.ShapeDtypeStruct((n_layers * n_heads, 3, nk, nq), F32),
        compiler_params=_params(("arbitrary",)),
    )(toep.reshape(n_layers * n_heads, RPB_R, GRID_W, GRID_W)).reshape(n_layers, n_heads, 3, nk, nq)


def _na_bias_kernel(t_ref, o_ref):
    blank = jnp.full((GRID_W, GRID_W), NEG * LOG2E, F32)
    for case in range(3):
        for jj in range(NA_BAND):
            tiles = []
            for i in range(NA_QROWS):
                off = (0, i, NA_BAND - WIN_R)[case]
                rel = (jj - i, jj - i - WIN_R // 2, jj - i - NA_BAND + NA_QROWS)[case]
                tiles.append(t_ref[0, rel + WIN_R - 1] if off <= jj < off + WIN_R else blank)
            o_ref[0, case, jj * GRID_W:(jj + 1) * GRID_W, :] = jnp.concatenate(tiles, axis=1)


def _na(u, qn_g, kn_g, bias, layer, n_ctx):
    b, _, t, _ = u.shape
    n_rows = (t - n_ctx) // GRID_W
    assert n_rows % (NA_QROWS * NA_GROUP) == 0 and n_rows >= NA_BAND and n_ctx == NA_QROWS * GRID_W
    pairs = NA_HEADS // 2
    nk, nq = NA_BAND * GRID_W, NA_QROWS * GRID_W
    bias = bias.reshape(bias.shape[0], pairs, 2, 3, nk, nq)
    qg = jnp.tile(qn_g, 2).reshape(1, LANES)
    kg = jnp.tile(kn_g, 2).reshape(1, LANES)
    kern = functools.partial(_na_kernel, n_ctx=n_ctx, n_rows=n_rows)
    col = lambda which: pl.BlockSpec((1, None, t, LANES), lambda hp, bi: (bi, which * pairs + hp, 0, 0))
    return pl.pallas_call(
        kern,
        grid=(pairs, b),
        in_specs=[
            col(0), col(1), col(2),
            pl.BlockSpec((1, LANES), lambda hp, bi: (0, 0)),
            pl.BlockSpec((1, LANES), lambda hp, bi: (0, 0)),
            pl.BlockSpec((None, 1, 2, 3, nk, nq), lambda hp, bi: (layer, hp, 0, 0, 0, 0)),
        ],
        out_specs=col(0),
        out_shape=jax.ShapeDtypeStruct((b, pairs, t, LANES), BF16),
        scratch_shapes=[
            pltpu.VMEM((LANES, t), BF16),
            pltpu.VMEM((2, t, LANES), BF16),
            pltpu.VMEM((2, HEAD_DIM + NA_ONES_ROWS, t), BF16),
            pltpu.VMEM((2 * NA_GROUP, nk + n_ctx, nq), F32),
        ],
        compiler_params=_params(("arbitrary", "arbitrary")),
    )(u, u, u, qg, kg, bias)


def _lru_kernel(x_ref, g_ref, cw_ref, cb_ref, wa_ref, ba_ref, wx_ref, bx_ref, lam_ref, o_ref,
                xp_s, xc_s, hf_s, a_s, b_s, h_s, *, n_tiles):
    tl = LRU_TILE
    halo = LRU_HALO
    t = x_ref.shape[1]
    width = x_ref.shape[2]
    group = 8

    xp_s[0:halo, :] = jnp.zeros((halo, width), F32)
    xp_s[halo + t:halo + t + halo, :] = jnp.zeros((halo, width), F32)

    def fill(i, carry):
        r0 = pl.multiple_of(i * tl, tl)
        xp_s[pl.ds(pl.multiple_of(r0 + halo, halo), tl), :] = x_ref[0, pl.ds(r0, tl), :]
        return carry

    lax.fori_loop(0, n_tiles, fill, 0)

    row = lax.broadcasted_iota(jnp.int32, (tl, 1), 0)
    rowg = row % group
    cw = cw_ref[...]
    cb = cb_ref[...]

    def conv_tile(i):
        r0 = pl.multiple_of(i * tl, tl)
        win = xp_s[pl.ds(r0, tl + 2 * halo), :]
        n_win = tl + 2 * halo
        xm2 = pltpu.roll(win, 2, 0)[halo:halo + tl]
        xm1 = pltpu.roll(win, 1, 0)[halo:halo + tl]
        x0 = win[halo:halo + tl]
        xp1 = pltpu.roll(win, n_win - 1, 0)[halo:halo + tl]
        lo = i <= 1
        hi = i == 0
        xm2 = jnp.where(lo & (row < 2), 0.0, xm2)
        xm1 = jnp.where(lo & (row < 1), 0.0, xm1)
        xp1 = jnp.where(hi & (row == tl - 1), 0.0, xp1)
        return xm2 * cw[0:1] + xm1 * cw[1:2] + x0 * cw[2:3] + xp1 * cw[3:4] + cb

    def gates(i, d):
        r0 = pl.multiple_of(i * tl, tl)
        if d == 0:
            xc = conv_tile(i)
            xc_s[pl.ds(r0, tl), :] = xc
        else:
            xc = xc_s[pl.ds(r0, tl), :]
        xb = xc.astype(BF16)
        r = _sigmoid(_dot(xb, wa_ref[d]) + ba_ref[d:d + 1, :])
        ig = _sigmoid(_dot(xb, wx_ref[d]) + bx_ref[d:d + 1, :])
        log_a = -LRU_C * r * _softplus(-lam_ref[d:d + 1, :])
        a = jnp.exp(log_a)
        bt = jnp.sqrt(-jnp.tanh(log_a) * (a * a + 1.0)) * (ig * xc)
        return a, bt

    def scan_tile(i, d, carry):
        a, bt = gates(i, d)

        def shift(x, kk):
            x3 = x.reshape(tl // group, group, width)
            return pltpu.roll(x3, kk if d == 0 else group - kk, 1).reshape(tl, width)

        for kk in (1, 2, 4):
            mk = rowg >= kk if d == 0 else rowg < group - kk
            bt = jnp.where(mk, a * shift(bt, kk) + bt, bt)
            a = jnp.where(mk, a * shift(a, kk), a)
        a_s[...] = a
        b_s[...] = bt
        n_groups = tl // group

        def grp(gidx, c):
            gg = gidx if d == 0 else n_groups - 1 - gidx
            g0 = pl.multiple_of(gg * group, group)
            h = b_s[pl.ds(g0, group), :] + a_s[pl.ds(g0, group), :] * c
            h_s[pl.ds(g0, group), :] = h
            return h[group - 1:group, :] if d == 0 else h[0:1, :]

        return lax.fori_loop(0, n_groups, grp, carry, unroll=8)

    def fwd(i, carry):
        carry = scan_tile(i, 0, carry)
        r0 = pl.multiple_of(i * tl, tl)
        hf_s[pl.ds(r0, tl), :] = h_s[...]
        return carry

    def bwd(i, carry):
        carry = scan_tile(i, 1, carry)
        r0 = pl.multiple_of(i * tl, tl)
        y = (hf_s[pl.ds(r0, tl), :] + h_s[...]) * jax.nn.gelu(g_ref[0, pl.ds(r0, tl), :])
        o_ref[0, pl.ds(r0, tl), :] = y.astype(o_ref.dtype)
        return carry

    h0 = jnp.zeros((1, width), F32)
    lax.fori_loop(0, n_tiles, fwd, h0)
    carry = bwd(jnp.int32(0), h0)
    lax.fori_loop(1, n_tiles, lambda j, c: bwd(n_tiles - j, c), carry)


def _block_diag_dense(w):
    nd, k, c, _ = w.shape
    eye = jnp.eye(k, dtype=w.dtype)
    return jnp.einsum('dkce,kj->dkcje', w, eye).reshape(nd, k * c, k * c)


def _lru(u, conv_w, conv_b, w_a, b_a, w_x, b_x, lam, n_ctx):
    b, _, t, _ = u.shape
    width = conv_w.shape[1]
    assert n_ctx == LRU_TILE and t % LRU_TILE == 0
    n_tiles = t // LRU_TILE
    wa = _block_diag_dense(w_a).astype(BF16)
    wx = _block_diag_dense(w_x).astype(BF16)
    kern = functools.partial(_lru_kernel, n_tiles=n_tiles)
    full = lambda shape: pl.BlockSpec(shape, lambda bi: (0,) * len(shape))
    return pl.pallas_call(
        kern,
        grid=(b,),
        in_specs=[
            pl.BlockSpec((1, None, t, width), lambda bi: (bi, 0, 0, 0)),
            pl.BlockSpec((1, None, t, width), lambda bi: (bi, 1, 0, 0)),
            full((CONV_W, width)),
            full((1, width)),
            full((2, width, width)),
            full((2, width)),
            full((2, width, width)),
            full((2, width)),
            full((2, width)),
        ],
        out_specs=pl.BlockSpec((1, t, width), lambda bi: (bi, 0, 0)),
        out_shape=jax.ShapeDtypeStruct((b, t, width), BF16),
        scratch_shapes=[
            pltpu.VMEM((t + 2 * LRU_HALO, width), F32),
            pltpu.VMEM((t, width), F32),
            pltpu.VMEM((t, width), F32),
            pltpu.VMEM((LRU_TILE, width), F32),
            pltpu.VMEM((LRU_TILE, width), F32),
            pltpu.VMEM((LRU_TILE, width), F32),
        ],
        compiler_params=_params(("arbitrary",)),
    )(u, u, conv_w, conv_b.reshape(1, width), wa, b_a, wx, b_x, lam)


def _out_mlp_kernel(*refs, hid_tile):
    xc_ref, xl_ref = refs[0:2]
    ym_refs, yn_refs, yl_refs = (refs[2 + k * MLP_SUB:2 + (k + 1) * MLP_SUB] for k in range(3))
    mod_ref, g_ref, wo_ref, w1_ref, w2_ref, oc_ref, ol_ref = refs[2 + 3 * MLP_SUB:]

    def body(x, ym, yn, yl, o_ref):
        mod = mod_ref[0, 0]
        att = _dot(ym.astype(BF16), wo_ref[0:M_WIDTH, :])
        att = att + _dot(yn.astype(BF16), wo_ref[M_WIDTH:M_WIDTH + NA_WIDTH, :])
        att = att + _dot(yl.astype(BF16), wo_ref[M_WIDTH + NA_WIDTH:, :])
        x1 = x + mod[2:3, :] * att
        ms = jnp.mean(x1 * x1, axis=-1, keepdims=True)
        y = x1 * lax.rsqrt(ms + EPS) * g_ref[...]
        h = (y * (1.0 + mod[4:5, :]) + mod[3:4, :]).astype(BF16)
        hidden = w1_ref.shape[1]
        acc = None
        for j in range(hidden // hid_tile):
            a = jnp.maximum(_dot(h, w1_ref[:, j * hid_tile:(j + 1) * hid_tile]), 0.0)
            part = _dot((a * a).astype(BF16), w2_ref[j * hid_tile:(j + 1) * hid_tile, :])
            acc = part if acc is None else acc + part
        o_ref[0] = x1 + mod[5:6, :] * acc

    def heads(ref):
        return jnp.concatenate([ref[0, j] for j in range(ref.shape[1])], axis=1)

    def rows(tiles):
        return jnp.concatenate(tiles, axis=0)

    @pl.when(pl.program_id(1) == 0)
    def _():
        body(xc_ref[0], ym_refs[0][0], heads(yn_refs[0]), yl_refs[0][0], oc_ref)

    @pl.when(pl.program_id(1) > 0)
    def _():
        body(xl_ref[0], rows([r[0] for r in ym_refs]), rows([heads(r) for r in yn_refs]),
             rows([r[0] for r in yl_refs]), ol_ref)


def _out_mlp(xc, xl, ym, yn, yl, modv, g2, w_out, w1, w2, layer):
    b, n_ctx, d = xc.shape
    n_lat = xl.shape[1]
    lat_tile = MLP_SUB * ROW_TILE
    assert n_ctx == ROW_TILE and n_lat % lat_tile == 0
    hidden = w1.shape[2]
    blk = lambda j: (lambda s: jnp.maximum(MLP_SUB * (s - 1) + 1 + j, 0))
    tok = lambda w: [pl.BlockSpec((1, ROW_TILE, w), lambda bi, s, f=blk(j): (bi, f(s), 0)) for j in range(MLP_SUB)]
    pairs = [pl.BlockSpec((1, NA_HEADS // 2, ROW_TILE, LANES), lambda bi, s, f=blk(j): (bi, 0, f(s), 0))
             for j in range(MLP_SUB)]
    stream = [
        pl.BlockSpec((1, ROW_TILE, d), lambda bi, s: (bi, 0, 0)),
        pl.BlockSpec((1, lat_tile, d), lambda bi, s: (bi, jnp.maximum(s - 1, 0), 0)),
    ]
    kern = functools.partial(_out_mlp_kernel, hid_tile=1024)
    return pl.pallas_call(
        kern,
        grid=(b, 1 + n_lat // lat_tile),
        in_specs=stream + tok(M_WIDTH) + pairs + tok(d - M_WIDTH - NA_WIDTH) + [
            pl.BlockSpec((1, 1, 6, d), lambda bi, s: (bi, jnp.minimum(s, 1), 0, 0)),
            pl.BlockSpec((1, d), lambda bi, s: (0, 0)),
            _resident((None, d, d), lambda bi, s: (layer, 0, 0)),
            _resident((None, d, hidden), lambda bi, s: (layer, 0, 0)),
            _resident((None, hidden, d), lambda bi, s: (layer, 0, 0)),
        ],
        out_specs=stream,
        out_shape=[jax.ShapeDtypeStruct(xc.shape, F32), jax.ShapeDtypeStruct(xl.shape, F32)],
        compiler_params=_params(("arbitrary", "arbitrary")),
    )(xc, xl, *([ym] * MLP_SUB + [yn] * MLP_SUB + [yl] * MLP_SUB), modv, g2.reshape(1, d), w_out, w1, w2)


def _split_w_in(w_in):
    m_end = 4 * M_WIDTH
    na_end = m_end + N_GATES + 3 * NA_WIDTH
    main = jnp.concatenate([w_in[..., :m_end], w_in[..., na_end:], w_in[..., m_end + N_GATES:na_end]], axis=-1)
    gates = jnp.swapaxes(w_in[..., m_end:m_end + N_GATES], -1, -2)
    return main.astype(BF16), gates.astype(BF16)


def kernel(x, c, ctx, c_ctx, w_mod, b_mod, norm1_g, norm2_g, w_in, mlstm_gate_b, mlstm_norm_g, na_q_norm_g, na_k_norm_g, na_rpb, lru_conv_w, lru_conv_b, lru_w_a, lru_b_a, lru_w_x, lru_b_x, lru_lambda, w_out, w_mlp1, w_mlp2):
    b, n, d = x.shape
    n_ctx = ctx.shape[1]
    depth = w_in.shape[0]
    assert n_ctx == ROW_TILE and n % ROW_TILE == 0 and n % GRID_W == 0

    cond_rows = 8 * ((b + 1 + 7) // 8)
    cond = jnp.zeros((cond_rows, d), F32).at[:b].set(c).at[b].set(c_ctx)
    mod_all = _modulation(cond, w_mod, b_mod)
    rope_row, rope_col = _rope_tables(n // GRID_W)
    w_in_p, w_gate_t = _split_w_in(w_in)
    na_bias = _na_bias_tables(na_rpb)
    w_out_b, w1_b, w2_b = w_out.astype(BF16), w_mlp1.astype(BF16), w_mlp2.astype(BF16)

    xc, xl = ctx, x
    for l in range(depth):
        lat = mod_all[l, :b].reshape(b, 6, d)
        cx = jnp.broadcast_to(mod_all[l, b].reshape(1, 6, d), (b, 6, d))
        modv = jnp.stack([cx, lat], axis=1)
        um, ul, un, g_rows = _inproj(xc, xl, modv, norm1_g[l], w_in_p, w_gate_t, l)
        ym = _mlstm(um, g_rows, mlstm_gate_b[l], mlstm_norm_g[l], rope_row, rope_col, n_ctx)
        yn = _na(un, na_q_norm_g[l], na_k_norm_g[l], na_bias, l, n_ctx)
        yl = _lru(ul, lru_conv_w[l], lru_conv_b[l], lru_w_a[l], lru_b_a[l], lru_w_x[l], lru_b_x[l], lru_lambda[l], n_ctx)
        xc, xl = _out_mlp(xc, xl, ym, yn, yl, modv, norm2_g[l], w_out_b, w1_b, w2_b, l)
    return xl
```

```python
import functools
import math

import numpy as np
import jax
import jax.numpy as jnp
from jax import lax
from jax.experimental import pallas as pl
from jax.experimental.pallas import tpu as pltpu

F32 = jnp.float32
BF16 = jnp.bfloat16

HEAD_DIM = 64
M_HEADS = 4
M_WIDTH = M_HEADS * HEAD_DIM
NA_HEADS = 8
NA_WIDTH = NA_HEADS * HEAD_DIM
N_GATES = 4 * M_HEADS
GRID_W = 64
M_CHUNK = 128
M_AUG = HEAD_DIM + 16
M_LOCAL_GROUP = 2
M_LATENT_GROUP = 4
WIN_R = 8
WIN_C = 16
RPB_R = 2 * WIN_R - 1
RPB_C = 2 * WIN_C - 1
NA_QROWS = 4
NA_BAND = NA_QROWS + WIN_R
NA_KEY_CHUNK = 128
NA_ONES_ROWS = 16
NA_GROUP = 8
LOG2E = 1.4426950408889634
CONV_W = 4
LRU_C = 8.0
ROPE_BASE = 10000.0
EPS = 1e-6
NEG = -1e30

LANES = 128
ROW_TILE = 256
MLP_SUB = 4
LRU_TILE = 256
LRU_HALO = 8
VMEM_LIMIT = 56 * 1024 * 1024

U_MQ, U_MK, U_MV, U_MO = 0, 256, 512, 768
U_XR, U_GT = 1024, 1280
U_NQ, U_NK, U_NV = 1536, 2048, 2560
U_COLS = 3072
LRU_WIDTH = U_NQ - U_GT
NA_COL_BLOCKS = (U_COLS - U_NQ) // LANES


def _dot(a, b):
    return jnp.dot(a, b, preferred_element_type=F32)


def _dot_f32_rhs01(x, m01):
    hi = x.astype(BF16)
    lo = (x - hi.astype(F32)).astype(BF16)
    return _dot(hi, m01) + _dot(lo, m01)


def _sigmoid(x):
    return 0.5 * jnp.tanh(0.5 * x) + 0.5


def _log_sigmoid(x):
    return jnp.minimum(x, 0.0) - jnp.log1p(jnp.exp(-jnp.abs(x)))


def _softplus(x):
    return jnp.maximum(x, 0.0) + jnp.log1p(jnp.exp(-jnp.abs(x)))


def _params(sem):
    return pltpu.CompilerParams(dimension_semantics=sem, vmem_limit_bytes=VMEM_LIMIT)


def _resident(shape, index_map):
    return pl.BlockSpec(shape, index_map, pipeline_mode=pl.Buffered(1))


def _mod_kernel(a_ref, w_ref, b_ref, o_ref):
    a = a_ref[...]
    act = (a * jax.nn.sigmoid(a)).astype(BF16)
    o_ref[0] = _dot(act, w_ref[0].astype(BF16)) + b_ref[0]


def _modulation(cond, w_mod, b_mod):
    depth, d, n6 = w_mod.shape
    rows = cond.shape[0]
    tn = 1536
    return pl.pallas_call(
        _mod_kernel,
        grid=(depth, n6 // tn),
        in_specs=[
            pl.BlockSpec((rows, d), lambda l, j: (0, 0)),
            pl.BlockSpec((1, d, tn), lambda l, j: (l, 0, j)),
            pl.BlockSpec((1, 1, tn), lambda l, j: (l, 0, j)),
        ],
        out_specs=pl.BlockSpec((1, rows, tn), lambda l, j: (l, 0, j)),
        out_shape=jax.ShapeDtypeStruct((depth, rows, n6), F32),
        compiler_params=_params(("arbitrary", "arbitrary")),
    )(cond, w_mod, b_mod.reshape(depth, 1, n6))


def _inproj_kernel(xc_ref, xa_ref, xb_ref, mod_ref, g_ref, w_ref, wg_ref, um_ref, ul_ref, un_ref, gt_ref,
                   *, n_steps, last_has_second):
    def half(x, mod, r0):
        ms = jnp.mean(x * x, axis=-1, keepdims=True)
        y = x * lax.rsqrt(ms + EPS) * g_ref[...]
        h = (y * (1.0 + mod[1:2, :]) + mod[0:1, :]).astype(BF16)
        u = _dot(h, w_ref[...])
        rows = slice(r0, r0 + ROW_TILE)
        um_ref[0, rows, :] = u[:, 0:U_XR]
        for j in range(ul_ref.shape[1]):
            ul_ref[0, j, rows, :] = u[:, U_XR + j * LRU_WIDTH:U_XR + (j + 1) * LRU_WIDTH]
        for j in range(un_ref.shape[1]):
            un_ref[0, j, rows, :] = u[:, U_NQ + j * LANES:U_NQ + (j + 1) * LANES]
        gt_ref[0, :, rows] = lax.dot_general(wg_ref[...], h, (((1,), (1,)), ((), ())), preferred_element_type=F32)

    mod_ctx, mod_lat = mod_ref[0, 0], mod_ref[0, 1]

    @pl.when(pl.program_id(1) == 0)
    def _():
        half(xc_ref[0], mod_ctx, 0)
        half(xb_ref[0], mod_lat, ROW_TILE)

    step = pl.program_id(1)
    both_latent = (step > 0) if last_has_second else (step > 0) & (step < n_steps - 1)

    @pl.when(both_latent)
    def _():
        half(xa_ref[0], mod_lat, 0)
        half(xb_ref[0], mod_lat, ROW_TILE)

    if not last_has_second:
        @pl.when(step == n_steps - 1)
        def _():
            half(xa_ref[0], mod_lat, 0)


def _inproj(xc, xl, modv, g1, w_in_p, w_gate_t, layer):
    b, n_ctx, d = xc.shape
    assert n_ctx == ROW_TILE and xl.shape[1] % ROW_TILE == 0
    n_lat_tiles = xl.shape[1] // ROW_TILE
    t = n_ctx + xl.shape[1]
    step_rows = 2 * ROW_TILE
    n_steps = pl.cdiv(t, step_rows)
    assert n_steps > 1
    kern = functools.partial(_inproj_kernel, n_steps=n_steps, last_has_second=(t % step_rows == 0))
    return pl.pallas_call(
        kern,
        grid=(b, n_steps),
        in_specs=[
            pl.BlockSpec((1, ROW_TILE, d), lambda bi, s: (bi, 0, 0)),
            pl.BlockSpec((1, ROW_TILE, d), lambda bi, s: (bi, jnp.maximum(2 * s - 1, 0), 0)),
            pl.BlockSpec((1, ROW_TILE, d), lambda bi, s: (bi, jnp.minimum(2 * s, n_lat_tiles - 1), 0)),
            pl.BlockSpec((1, 2, 6, d), lambda bi, s: (bi, 0, 0, 0)),
            pl.BlockSpec((1, d), lambda bi, s: (0, 0)),
            _resident((None, d, U_COLS), lambda bi, s: (layer, 0, 0)),
            _resident((None, N_GATES, d), lambda bi, s: (layer, 0, 0)),
        ],
        out_specs=[
            pl.BlockSpec((1, step_rows, U_XR), lambda bi, s: (bi, s, 0)),
            pl.BlockSpec((1, 2, step_rows, LRU_WIDTH), lambda bi, s: (bi, 0, s, 0)),
            pl.BlockSpec((1, NA_COL_BLOCKS, step_rows, LANES), lambda bi, s: (bi, 0, s, 0)),
            pl.BlockSpec((1, N_GATES, step_rows), lambda bi, s: (bi, 0, s)),
        ],
        out_shape=[
            jax.ShapeDtypeStruct((b, t, U_XR), F32),
            jax.ShapeDtypeStruct((b, 2, t, LRU_WIDTH), F32),
            jax.ShapeDtypeStruct((b, NA_COL_BLOCKS, t, LANES), F32),
            jax.ShapeDtypeStruct((b, N_GATES, t), F32),
        ],
        compiler_params=_params(("arbitrary", "arbitrary")),
    )(xc, xl, xl, modv, g1.reshape(1, d), w_in_p, w_gate_t)


def _mlstm_kernel(u_ref, og_ref, gr_ref, gb_ref, ng_ref, rrow_ref, rcol_ref, o_ref,
                  qt_s, k_s, vt_s, r_s, row_s, c_s, *, n_ctx_chunks, n_chunks):
    L = M_CHUNK
    W = M_WIDTH
    D = HEAD_DIM
    groups = L // GRID_W

    for h in range(M_HEADS):
        vt_s[h, D:, :] = jnp.ones((M_AUG - D, vt_s.shape[2]), BF16)
    c_s[...] = jnp.zeros(c_s.shape, F32)

    def prep(cs, latent):
        r0s = [pl.multiple_of(c * L, L) for c in cs]
        qs = [u_ref[0, pl.ds(r0, L), U_MQ:U_MQ + W] for r0 in r0s]
        ks = [u_ref[0, pl.ds(r0, L), U_MK:U_MK + W] * D ** -0.5 for r0 in r0s]
        vts = [u_ref[0, pl.ds(r0, L), U_MV:U_MV + W].T for r0 in r0s]
        if latent:
            def table(c, i):
                first = (c - n_ctx_chunks) * groups
                return jnp.concatenate([rrow_ref[i, pl.ds(first + g, 1), :] + rcol_ref[i] for g in range(groups)], axis=0)

            def rope(x, c):
                return (x * table(c, 0) + pltpu.roll(x, W - D // 4, 1) * table(c, 1)
                        + pltpu.roll(x, D // 4, 1) * table(c, 2))

            qs = [rope(q, c) for q, c in zip(qs, cs)]
            ks = [rope(k, c) for k, c in zip(ks, cs)]
        qts = [q.T for q in qs]
        for r0, qt, k, vt in zip(r0s, qts, ks, vts):
            qt_s[:, pl.ds(r0, L)] = qt.astype(BF16)
            k_s[pl.ds(r0, L), :] = k.astype(BF16)
            for h in range(M_HEADS):
                vt_s[h, 0:D, pl.ds(r0, L)] = vt[h * D:(h + 1) * D].astype(BF16)

    assert n_ctx_chunks % M_LOCAL_GROUP == 0 and n_chunks % M_LOCAL_GROUP == 0

    def prep_group(latent):
        def body(j, carry):
            first = j * M_LOCAL_GROUP + (n_ctx_chunks if latent else 0)
            prep([first + i for i in range(M_LOCAL_GROUP)], latent)
            return carry
        return body

    lax.fori_loop(0, n_ctx_chunks // M_LOCAL_GROUP, prep_group(False), 0)
    lax.fori_loop(0, (n_chunks - n_ctx_chunks) // M_LOCAL_GROUP, prep_group(True), 0)

    si = lax.broadcasted_iota(jnp.int32, (L, L), 0)
    li = lax.broadcasted_iota(jnp.int32, (L, L), 1)
    keeps = [si <= li, si >= li]
    gbias = [jnp.broadcast_to(gb_ref[8 * d:8 * d + 8, :], (8, L)) for d in range(2)]
    zrows = jnp.zeros((D, L), BF16)

    def qt_head(qtc, h):
        parts = [zrows] * M_HEADS
        parts[h] = qtc[h * D:(h + 1) * D]
        return jnp.concatenate(parts, axis=0)

    def local(first, n_group):
        r0s = [pl.multiple_of((first + j) * L, L) for j in range(n_group)]
        scores = [[_dot(k_s[pl.ds(r0, L), :], qt_head(qt_s[:, pl.ds(r0, L)], h)) for h in range(M_HEADS)]
                  for r0 in r0s]
        vtas = [[vt_s[h, :, pl.ds(r0, L)] for h in range(M_HEADS)] for r0 in r0s]
        g8s = [[gr_ref[0, 8 * d:8 * d + 8, pl.ds(r0, L)] + gbias[d] for d in range(2)] for r0 in r0s]
        cum4s = [[pltpu.roll(_dot_f32_rhs01(_log_sigmoid(g8[d]), keeps[d].astype(BF16)), M_HEADS, 0)
                  for d in range(2)] for g8 in g8s]
        src_ts = [[(g8[d] - cum4[d]).T for d in range(2)] for g8, cum4 in zip(g8s, cum4s)]
        jdh = [(j, d, h) for j in range(n_group) for d in range(2) for h in range(M_HEADS)]
        logws = [jnp.where(keeps[d], cum4s[j][d][h:h + 1] + jnp.broadcast_to(src_ts[j][d][:, h:h + 1], (L, L)), NEG)
                 for j, d, h in jdh]
        amaxs = [lw.max(axis=0, keepdims=True) for lw in logws]
        ps = [(scores[j][h] * jnp.exp(logw - a)).astype(BF16) for (j, d, h), logw, a in zip(jdh, logws, amaxs)]
        p_of = dict(zip(jdh, ps))
        both = {(j, h): _dot(vtas[j][h], jnp.concatenate([p_of[j, 0, h], p_of[j, 1, h]], axis=1))
                for j in range(n_group) for h in range(M_HEADS)}
        intras = [both[j, h][:, d * L:(d + 1) * L] for j, d, h in jdh]
        for (j, d, h), a, intra in zip(jdh, amaxs, intras):
            last = L - 1 if d == 0 else 0
            cl = cum4s[j][d][h:h + 1]
            a_last = jnp.broadcast_to(a[:, last:last + 1], (1, L))
            cl_last = jnp.broadcast_to(cl[:, last:last + 1], (1, L))
            w_src = jnp.exp(cl_last - cl + g8s[j][d][h:h + 1] - a_last)
            r_s[d, h * M_AUG:(h + 1) * M_AUG, pl.ds(r0s[j], L)] = intra
            for kind, row in enumerate((cl, a, cl_last, a_last, w_src)):
                row_s[d, kind, h:h + 1, pl.ds(r0s[j], L)] = row

    def local_loop(first, count, n_group):
        assert count % n_group == 0

        def body(i, carry):
            local(first + i * n_group, n_group)
            return carry

        lax.fori_loop(0, count // n_group, body, 0)

    local_loop(0, n_ctx_chunks, M_LOCAL_GROUP)
    local_loop(n_ctx_chunks, n_chunks - n_ctx_chunks, M_LATENT_GROUP)

    def step(c, d, c_olds, m_prevs):
        r0 = pl.multiple_of(c * L, L)
        kc = k_s[pl.ds(r0, L), :]
        qtc = qt_s[:, pl.ds(r0, L)]
        h_ts, c_news, m_news = [], [], []
        rows = [[row_s[d, kind, h:h + 1, pl.ds(r0, L)] for kind in range(5)] for h in range(M_HEADS)]
        vws = [(vt_s[h, :, pl.ds(r0, L)].astype(F32) * rows[h][4]).astype(BF16) for h in range(M_HEADS)]
        upd = _dot(jnp.concatenate(vws, axis=0), kc)
        inters = [_dot(c_olds[h].astype(BF16), qt_head(qtc, h)) for h in range(M_HEADS)]
        for h in range(M_HEADS):
            cl, a, cl_last, a_last, _ = rows[h]
            m_prev = m_prevs[h]
            inter = cl + m_prev
            m_t = jnp.maximum(inter, a)
            r = (jnp.exp(inter - m_t) * inters[h]
                 + jnp.exp(a - m_t) * r_s[d, h * M_AUG:(h + 1) * M_AUG, pl.ds(r0, L)])
            denom = jnp.maximum(jnp.abs(r[D:D + 1]), jnp.exp(-m_t))
            h_ts.append(r[0:D] * (1.0 / denom))
            m_new = jnp.maximum(cl_last + m_prev, a_last)
            w_old = jnp.exp(cl_last + m_prev - m_new)
            w_upd = jnp.exp(a_last - m_new)
            scale = lambda w: jnp.concatenate([w] * (W // L), axis=1)
            c_news.append(scale(w_old) * c_olds[h] + scale(w_upd) * upd[h * M_AUG:(h + 1) * M_AUG])
            m_news.append(m_new)
        return r0, h_ts, c_news, m_news

    def body(i, m_all):
        chunks = [i, jnp.where(i < n_ctx_chunks, n_ctx_chunks - 1 - i, n_chunks - 1 + n_ctx_chunks - i)]
        c_olds = [[c_s[d, h] for h in range(M_HEADS)] for d in range(2)]
        outs = [step(chunks[d], d, c_olds[d], m_all[d]) for d in range(2)]
        for d in range(2):
            r0, h_ts, c_news, _ = outs[d]
            for h in range(M_HEADS):
                r_s[d, h * M_AUG:h * M_AUG + D, pl.ds(r0, L)] = h_ts[h]
                c_s[d, h] = c_news[h]
        return tuple(tuple(outs[d][3]) for d in range(2))

    m0 = tuple(tuple(jnp.zeros((1, L), F32) for _ in range(M_HEADS)) for _ in range(2))
    lax.fori_loop(0, n_chunks, body, m0)

    ng = jnp.broadcast_to(ng_ref[...], (W, L))

    def post(j, carry):
        r0s = [pl.multiple_of((j * M_LOCAL_GROUP + i) * L, L) for i in range(M_LOCAL_GROUP)]
        hts = [[r_s[0, h * M_AUG:h * M_AUG + D, pl.ds(r0, L)] + r_s[1, h * M_AUG:h * M_AUG + D, pl.ds(r0, L)]
                for h in range(M_HEADS)] for r0 in r0s]
        scales = [[lax.rsqrt(jnp.sum(ht * ht, axis=0, keepdims=True) * (1.0 / D) + EPS) for ht in hs] for hs in hts]
        hnts = [(jnp.concatenate([ht * sc for ht, sc in zip(hs, scs)], axis=0) * ng).T for hs, scs in zip(hts, scales)]
        for r0, hnt in zip(r0s, hnts):
            o_ref[0, pl.ds(r0, L), :] = (_sigmoid(og_ref[0, pl.ds(r0, L), :]) * hnt).astype(o_ref.dtype)
        return carry

    lax.fori_loop(0, n_chunks // M_LOCAL_GROUP, post, 0)


def _rope_tables(n_rows):
    nf = HEAD_DIM // 4
    inv = ROPE_BASE ** (-jnp.arange(nf, dtype=F32) / nf)
    d = np.arange(M_WIDTH) % HEAD_DIM
    f_idx = d % nf
    is_row = d < HEAD_DIM // 2
    first_quarter = (d % (HEAD_DIM // 2)) < nf

    def tables(pos, sel):
        ang = pos[:, None] * inv[None, :]
        ang = ang[:, f_idx]
        cos = jnp.where(sel, jnp.cos(ang), 0.0)
        sin = jnp.where(sel, jnp.sin(ang), 0.0)
        sin_a = jnp.where(first_quarter, -sin, 0.0)
        sin_b = jnp.where(first_quarter, 0.0, sin)
        return jnp.stack([cos, sin_a, sin_b]).astype(F32)

    row_t = tables(jnp.arange(n_rows, dtype=F32), is_row)
    col_t = tables(jnp.arange(GRID_W, dtype=F32), ~is_row)
    return row_t, col_t


def _mlstm(u, g_rows, gate_b, norm_g, rope_row, rope_col, n_ctx):
    b, t, _ = u.shape
    assert t % M_CHUNK == 0 and n_ctx % M_CHUNK == 0 and M_CHUNK % GRID_W == 0
    n_chunks = t // M_CHUNK
    kern = functools.partial(_mlstm_kernel, n_ctx_chunks=n_ctx // M_CHUNK, n_chunks=n_chunks)
    n_rows = rope_row.shape[1]
    return pl.pallas_call(
        kern,
        grid=(b,),
        in_specs=[
            pl.BlockSpec((1, t, U_MO), lambda bi: (bi, 0, 0), pipeline_mode=pl.Buffered(1)),
            pl.BlockSpec((1, t, M_WIDTH), lambda bi: (bi, 0, U_MO // M_WIDTH)),
            pl.BlockSpec((1, N_GATES, t), lambda bi: (bi, 0, 0)),
            pl.BlockSpec((N_GATES, 1), lambda bi: (0, 0)),
            pl.BlockSpec((M_WIDTH, 1), lambda bi: (0, 0)),
            pl.BlockSpec((3, n_rows, M_WIDTH), lambda bi: (0, 0, 0)),
            pl.BlockSpec((3, GRID_W, M_WIDTH), lambda bi: (0, 0, 0)),
        ],
        out_specs=pl.BlockSpec((1, t, M_WIDTH), lambda bi: (bi, 0, 0)),
        out_shape=jax.ShapeDtypeStruct((b, t, M_WIDTH), BF16),
        scratch_shapes=[
            pltpu.VMEM((M_WIDTH, t), BF16),
            pltpu.VMEM((t, M_WIDTH), BF16),
            pltpu.VMEM((M_HEADS, M_AUG, t), BF16),
            pltpu.VMEM((2, M_HEADS * M_AUG, t), F32),
            pltpu.VMEM((2, 5, 8, t), F32),
            pltpu.VMEM((2, M_HEADS, M_AUG, M_WIDTH), F32),
        ],
        compiler_params=_params(("arbitrary",)),
    )(u, u, g_rows, gate_b.reshape(N_GATES, 1), norm_g.reshape(M_WIDTH, 1), rope_row, rope_col)


def _na_kernel(q_ref, k_ref, v_ref, qg_ref, kg_ref, bias_ref, o_ref, qt_s, kz_s, vt_s, s_s, *, n_ctx, n_rows):
    t = q_ref.shape[1]
    lane = lax.broadcasted_iota(jnp.int32, (1, LANES), 1)
    half = [lane < HEAD_DIM, lane >= HEAD_DIM]
    hi = lax.broadcasted_iota(jnp.int32, (LANES, LANES), 0) // HEAD_DIM
    hj = lax.broadcasted_iota(jnp.int32, (LANES, LANES), 1) // HEAD_DIM
    same_head = (hi == hj).astype(BF16)
    qg = qg_ref[...] * (HEAD_DIM ** -0.5 * LOG2E)
    kg = kg_ref[...]
    qblk = NA_QROWS * GRID_W
    kc = NA_KEY_CHUNK
    sub = 8

    for hh in range(2):
        vt_s[hh, HEAD_DIM:, :] = jnp.ones((vt_s.shape[1] - HEAD_DIM, t), BF16)

    def prep(r0s):
        qs = [q_ref[0, pl.ds(r0, ROW_TILE), :] for r0 in r0s]
        ks = [k_ref[0, pl.ds(r0, ROW_TILE), :] for r0 in r0s]
        vts = [v_ref[0, pl.ds(r0, ROW_TILE), :].T for r0 in r0s]
        qms = [_dot_f32_rhs01(q * q, same_head) * (1.0 / HEAD_DIM) for q in qs]
        kms = [_dot_f32_rhs01(k * k, same_head) * (1.0 / HEAD_DIM) for k in ks]
        qnts = [(q * lax.rsqrt(m + EPS) * qg).T for q, m in zip(qs, qms)]
        kns = [k * lax.rsqrt(m + EPS) * kg for k, m in zip(ks, kms)]
        for r0, qnt, kn, vt in zip(r0s, qnts, kns, vts):
            qt_s[:, pl.ds(r0, ROW_TILE)] = qnt.astype(BF16)
            for hh in range(2):
                kz_s[hh, pl.ds(r0, ROW_TILE), :] = jnp.where(half[hh], kn, 0.0).astype(BF16)
                vt_s[hh, 0:HEAD_DIM, pl.ds(r0, ROW_TILE)] = vt[hh * HEAD_DIM:(hh + 1) * HEAD_DIM].astype(BF16)

    n_tiles = t // ROW_TILE

    def prep_pair(j, carry):
        prep([pl.multiple_of((2 * j + i) * ROW_TILE, ROW_TILE) for i in range(2)])
        return carry

    lax.fori_loop(0, n_tiles // 2, prep_pair, 0)
    if n_tiles % 2:
        prep([(n_tiles - 1) * ROW_TILE])

    def attend(blocks):
        jobs = [(b, hh) for b in range(len(blocks)) for hh in range(2)]
        qts = [qt_s[:, pl.ds(q0, qblk)] for q0, _ in blocks]
        n_chunks = len(blocks[0][1])
        mx = [None] * len(jobs)
        def continues(prev, cur):
            if prev[1] is None or cur[1] is None:
                return prev[1] is None and cur[1] is None and cur[0] == prev[0] + kc
            return cur[1][1] == prev[1][1] + kc

        runs = []
        for ci in range(n_chunks):
            if ci and continues(blocks[0][1][ci - 1], blocks[0][1][ci]):
                runs[-1].append(ci)
            else:
                runs.append([ci])
        for run in runs:
            scores = [_dot(kz_s[hh, pl.ds(blocks[b][1][run[0]][0], kc * len(run)), :], qts[b]) for b, hh in jobs]
            for ri, ci in enumerate(run):
                for ji, (b, hh) in enumerate(jobs):
                    tab = blocks[b][1][ci][1]
                    s = scores[ji][ri * kc:(ri + 1) * kc]
                    if tab is not None:
                        s = s + bias_ref[0, hh, tab[0], tab[1]:tab[1] + kc, :]
                    s_s[ji, ci * kc:(ci + 1) * kc, :] = s
                    cm = functools.reduce(jnp.maximum, [s[i * sub:(i + 1) * sub] for i in range(kc // sub)])
                    mx[ji] = cm if mx[ji] is None else jnp.maximum(mx[ji], cm)
        ms = [m.max(axis=0, keepdims=True) for m in mx]
        accs = [None] * len(jobs)
        for run in runs:
            assert len(run) % 2 == 0
            for ci in run[::2]:
                for ji, (b, hh) in enumerate(jobs):
                    k0, _ = blocks[b][1][ci]
                    p = jnp.exp2(s_s[ji, ci * kc:(ci + 2) * kc, :] - ms[ji]).astype(BF16)
                    pv = _dot(vt_s[hh, :, pl.ds(k0, 2 * kc)], p)
                    accs[ji] = pv if accs[ji] is None else accs[ji] + pv
        outs = []
        for b in range(len(blocks)):
            heads = [accs[2 * b + hh][0:HEAD_DIM] / accs[2 * b + hh][HEAD_DIM:HEAD_DIM + 1] for hh in range(2)]
            outs.append(jnp.concatenate(heads, axis=0).T)
        return outs

    ctx_chunks = [(ci * kc, None) for ci in range(n_ctx // kc)]
    o_ref[0, 0:n_ctx, :] = attend([(0, ctx_chunks)])[0].astype(o_ref.dtype)

    n_blocks = n_rows // NA_QROWS

    def group(gi, carry):
        blocks = []
        for j in range(NA_GROUP):
            bi = gi * NA_GROUP + j
            r0 = bi * NA_QROWS
            sb = jnp.clip(r0 - WIN_R // 2, 0, n_rows - NA_BAND)
            case = jnp.where(bi == 0, 0, jnp.where(bi == n_blocks - 1, 2, 1))
            q0 = pl.multiple_of(n_ctx + r0 * GRID_W, qblk)
            k0 = pl.multiple_of(n_ctx + sb * GRID_W, qblk)
            local = [(pl.multiple_of(k0 + ci * kc, kc), (case, ci * kc)) for ci in range(NA_BAND * GRID_W // kc)]
            blocks.append((q0, local + ctx_chunks))
        for (q0, _), out in zip(blocks, attend(blocks)):
            o_ref[0, pl.ds(q0, qblk), :] = out.astype(o_ref.dtype)
        return carry

    lax.fori_loop(0, n_blocks // NA_GROUP, group, 0)


def _na_bias_tables(rpb):
    n_layers, n_heads = rpb.shape[:2]
    cols = np.arange(GRID_W)
    col0 = np.clip(cols - WIN_C // 2, 0, GRID_W - WIN_C)
    valid = (cols[:, None] >= col0[None, :]) & (cols[:, None] < col0[None, :] + WIN_C)
    dc = cols[:, None] - cols[None, :] + WIN_C - 1
    col_sel = ((dc[..., None] == np.arange(RPB_C)) & valid[..., None]).astype(np.float32)
    toep = jnp.einsum('lhrd,xcd->lhrxc', rpb, col_sel, precision=lax.Precision.HIGHEST)
    toep = jnp.where(valid, toep * LOG2E, NEG * LOG2E).astype(F32)
    nk, nq = NA_BAND * GRID_W, NA_QROWS * GRID_W
    return pl.pallas_call(
        _na_bias_kernel,
        grid=(n_layers * n_heads,),
        in_specs=[pl.BlockSpec((1, RPB_R, GRID_W, GRID_W), lambda i: (i, 0, 0, 0))],
        out_specs=pl.BlockSpec((1, 3, nk, nq), lambda i: (i, 0, 0, 0)),
        out_shape=jax.ShapeDtypeStruct((n_layers * n_heads, 3, nk, nq), F32),
        compiler_params=_params(("arbitrary",)),
    )(toep.reshape(n_layers * n_heads, RPB_R, GRID_W, GRID_W)).reshape(n_layers, n_heads, 3, nk, nq)


def _na_bias_kernel(t_ref, o_ref):
    blank = jnp.full((GRID_W, GRID_W), NEG * LOG2E, F32)
    for case in range(3):
        for jj in range(NA_BAND):
            tiles = []
            for i in range(NA_QROWS):
                off = (0, i, NA_BAND - WIN_R)[case]
                rel = (jj - i, jj - i - WIN_R // 2, jj - i - NA_BAND + NA_QROWS)[case]
                tiles.append(t_ref[0, rel + WIN_R - 1] if off <= jj < off + WIN_R else blank)
            o_ref[0, case, jj * GRID_W:(jj + 1) * GRID_W, :] = jnp.concatenate(tiles, axis=1)


def _na(u, qn_g, kn_g, bias, layer, n_ctx):
    b, _, t, _ = u.shape
    n_rows = (t - n_ctx) // GRID_W
    assert n_rows % (NA_QROWS * NA_GROUP) == 0 and n_rows >= NA_BAND and n_ctx == NA_QROWS * GRID_W
    pairs = NA_HEADS // 2
    nk, nq = NA_BAND * GRID_W, NA_QROWS * GRID_W
    bias = bias.reshape(bias.shape[0], pairs, 2, 3, nk, nq)
    qg = jnp.tile(qn_g, 2).reshape(1, LANES)
    kg = jnp.tile(kn_g, 2).reshape(1, LANES)
    kern = functools.partial(_na_kernel, n_ctx=n_ctx, n_rows=n_rows)
    col = lambda which: pl.BlockSpec((1, None, t, LANES), lambda hp, bi: (bi, which * pairs + hp, 0, 0))
    return pl.pallas_call(
        kern,
        grid=(pairs, b),
        in_specs=[
            col(0), col(1), col(2),
            pl.BlockSpec((1, LANES), lambda hp, bi: (0, 0)),
            pl.BlockSpec((1, LANES), lambda hp, bi: (0, 0)),
            pl.BlockSpec((None, 1, 2, 3, nk, nq), lambda hp, bi: (layer, hp, 0, 0, 0, 0)),
        ],
        out_specs=col(0),
        out_shape=jax.ShapeDtypeStruct((b, pairs, t, LANES), BF16),
        scratch_shapes=[
            pltpu.VMEM((LANES, t), BF16),
            pltpu.VMEM((2, t, LANES), BF16),
            pltpu.VMEM((2, HEAD_DIM + NA_ONES_ROWS, t), BF16),
            pltpu.VMEM((2 * NA_GROUP, nk + n_ctx, nq), F32),
        ],
        compiler_params=_params(("arbitrary", "arbitrary")),
    )(u, u, u, qg, kg, bias)


def _lru_kernel(x_ref, g_ref, cw_ref, cb_ref, wa_ref, ba_ref, wx_ref, bx_ref, lam_ref, o_ref,
                xp_s, xc_s, hf_s, a_s, b_s, h_s, *, n_tiles):
    tl = LRU_TILE
    halo = LRU_HALO
    t = x_ref.shape[1]
    width = x_ref.shape[2]
    group = 8

    xp_s[0:halo, :] = jnp.zeros((halo, width), F32)
    xp_s[halo + t:halo + t + halo, :] = jnp.zeros((halo, width), F32)

    def fill(i, carry):
        r0 = pl.multiple_of(i * tl, tl)
        xp_s[pl.ds(pl.multiple_of(r0 + halo, halo), tl), :] = x_ref[0, pl.ds(r0, tl), :]
        return carry

    lax.fori_loop(0, n_tiles, fill, 0)

    row = lax.broadcasted_iota(jnp.int32, (tl, 1), 0)
    rowg = row % group
    cw = cw_ref[...]
    cb = cb_ref[...]

    def conv_tile(i):
        r0 = pl.multiple_of(i * tl, tl)
        win = xp_s[pl.ds(r0, tl + 2 * halo), :]
        n_win = tl + 2 * halo
        xm2 = pltpu.roll(win, 2, 0)[halo:halo + tl]
        xm1 = pltpu.roll(win, 1, 0)[halo:halo + tl]
        x0 = win[halo:halo + tl]
        xp1 = pltpu.roll(win, n_win - 1, 0)[halo:halo + tl]
        lo = i <= 1
        hi = i == 0
        xm2 = jnp.where(lo & (row < 2), 0.0, xm2)
        xm1 = jnp.where(lo & (row < 1), 0.0, xm1)
        xp1 = jnp.where(hi & (row == tl - 1), 0.0, xp1)
        return xm2 * cw[0:1] + xm1 * cw[1:2] + x0 * cw[2:3] + xp1 * cw[3:4] + cb

    def gates(i, d):
        r0 = pl.multiple_of(i * tl, tl)
        if d == 0:
            xc = conv_tile(i)
            xc_s[pl.ds(r0, tl), :] = xc
        else:
            xc = xc_s[pl.ds(r0, tl), :]
        xb = xc.astype(BF16)
        r = _sigmoid(_dot(xb, wa_ref[d]) + ba_ref[d:d + 1, :])
        ig = _sigmoid(_dot(xb, wx_ref[d]) + bx_ref[d:d + 1, :])
        log_a = -LRU_C * r * _softplus(-lam_ref[d:d + 1, :])
        a = jnp.exp(log_a)
        bt = jnp.sqrt(-jnp.tanh(log_a) * (a * a + 1.0)) * (ig * xc)
        return a, bt

    def scan_tile(i, d, carry):
        a, bt = gates(i, d)

        def shift(x, kk):
            x3 = x.reshape(tl // group, group, width)
            return pltpu.roll(x3, kk if d == 0 else group - kk, 1).reshape(tl, width)

        for kk in (1, 2, 4):
            mk = rowg >= kk if d == 0 else rowg < group - kk
            bt = jnp.where(mk, a * shift(bt, kk) + bt, bt)
            a = jnp.where(mk, a * shift(a, kk), a)
        a_s[...] = a
        b_s[...] = bt
        n_groups = tl // group

        def grp(gidx, c):
            gg = gidx if d == 0 else n_groups - 1 - gidx
            g0 = pl.multiple_of(gg * group, group)
            h = b_s[pl.ds(g0, group), :] + a_s[pl.ds(g0, group), :] * c
            h_s[pl.ds(g0, group), :] = h
            return h[group - 1:group, :] if d == 0 else h[0:1, :]

        return lax.fori_loop(0, n_groups, grp, carry, unroll=8)

    def fwd(i, carry):
        carry = scan_tile(i, 0, carry)
        r0 = pl.multiple_of(i * tl, tl)
        hf_s[pl.ds(r0, tl), :] = h_s[...]
        return carry

    def bwd(i, carry):
        carry = scan_tile(i, 1, carry)
        r0 = pl.multiple_of(i * tl, tl)
        y = (hf_s[pl.ds(r0, tl), :] + h_s[...]) * jax.nn.gelu(g_ref[0, pl.ds(r0, tl), :])
        o_ref[0, pl.ds(r0, tl), :] = y.astype(o_ref.dtype)
        return carry

    h0 = jnp.zeros((1, width), F32)
    lax.fori_loop(0, n_tiles, fwd, h0)
    carry = bwd(jnp.int32(0), h0)
    lax.fori_loop(1, n_tiles, lambda j, c: bwd(n_tiles - j, c), carry)


def _block_diag_dense(w):
    nd, k, c, _ = w.shape
    eye = jnp.eye(k, dtype=w.dtype)
    return jnp.einsum('dkce,kj->dkcje', w, eye).reshape(nd, k * c, k * c)


def _lru(u, conv_w, conv_b, w_a, b_a, w_x, b_x, lam, n_ctx):
    b, _, t, _ = u.shape
    width = conv_w.shape[1]
    assert n_ctx == LRU_TILE and t % LRU_TILE == 0
    n_tiles = t // LRU_TILE
    wa = _block_diag_dense(w_a).astype(BF16)
    wx = _block_diag_dense(w_x).astype(BF16)
    kern = functools.partial(_lru_kernel, n_tiles=n_tiles)
    full = lambda shape: pl.BlockSpec(shape, lambda bi: (0,) * len(shape))
    return pl.pallas_call(
        kern,
        grid=(b,),
        in_specs=[
            pl.BlockSpec((1, None, t, width), lambda bi: (bi, 0, 0, 0)),
            pl.BlockSpec((1, None, t, width), lambda bi: (bi, 1, 0, 0)),
            full((CONV_W, width)),
            full((1, width)),
            full((2, width, width)),
            full((2, width)),
            full((2, width, width)),
            full((2, width)),
            full((2, width)),
        ],
        out_specs=pl.BlockSpec((1, t, width), lambda bi: (bi, 0, 0)),
        out_shape=jax.ShapeDtypeStruct((b, t, width), BF16),
        scratch_shapes=[
            pltpu.VMEM((t + 2 * LRU_HALO, width), F32),
            pltpu.VMEM((t, width), F32),
            pltpu.VMEM((t, width), F32),
            pltpu.VMEM((LRU_TILE, width), F32),
            pltpu.VMEM((LRU_TILE, width), F32),
            pltpu.VMEM((LRU_TILE, width), F32),
        ],
        compiler_params=_params(("arbitrary",)),
    )(u, u, conv_w, conv_b.reshape(1, width), wa, b_a, wx, b_x, lam)


def _out_mlp_kernel(*refs, hid_tile):
    xc_ref, xl_ref = refs[0:2]
    ym_refs, yn_refs, yl_refs = (refs[2 + k * MLP_SUB:2 + (k + 1) * MLP_SUB] for k in range(3))
    mod_ref, g_ref, wo_ref, w1_ref, w2_ref, oc_ref, ol_ref = refs[2 + 3 * MLP_SUB:]

    def body(x, ym, yn, yl, o_ref):
        mod = mod_ref[0, 0]
        att = _dot(ym.astype(BF16), wo_ref[0:M_WIDTH, :])
        att = att + _dot(yn.astype(BF16), wo_ref[M_WIDTH:M_WIDTH + NA_WIDTH, :])
        att = att + _dot(yl.astype(BF16), wo_ref[M_WIDTH + NA_WIDTH:, :])
        x1 = x + mod[2:3, :] * att
        ms = jnp.mean(x1 * x1, axis=-1, keepdims=True)
        y = x1 * lax.rsqrt(ms + EPS) * g_ref[...]
        h = (y * (1.0 + mod[4:5, :]) + mod[3:4, :]).astype(BF16)
        hidden = w1_ref.shape[1]
        acc = None
        for j in range(hidden // hid_tile):
            a = jnp.maximum(_dot(h, w1_ref[:, j * hid_tile:(j + 1) * hid_tile]), 0.0)
            part = _dot((a * a).astype(BF16), w2_ref[j * hid_tile:(j + 1) * hid_tile, :])
            acc = part if acc is None else acc + part
        o_ref[0] = x1 + mod[5:6, :] * acc

    def heads(ref):
        return jnp.concatenate([ref[0, j] for j in range(ref.shape[1])], axis=1)

    def rows(tiles):
        return jnp.concatenate(tiles, axis=0)

    @pl.when(pl.program_id(1) == 0)
    def _():
        body(xc_ref[0], ym_refs[0][0], heads(yn_refs[0]), yl_refs[0][0], oc_ref)

    @pl.when(pl.program_id(1) > 0)
    def _():
        body(xl_ref[0], rows([r[0] for r in ym_refs]), rows([heads(r) for r in yn_refs]),
             rows([r[0] for r in yl_refs]), ol_ref)


def _out_mlp(xc, xl, ym, yn, yl, modv, g2, w_out, w1, w2, layer):
    b, n_ctx, d = xc.shape
    n_lat = xl.shape[1]
    lat_tile = MLP_SUB * ROW_TILE
    assert n_ctx == ROW_TILE and n_lat % lat_tile == 0
    hidden = w1.shape[2]
    blk = lambda j: (lambda s: jnp.maximum(MLP_SUB * (s - 1) + 1 + j, 0))
    tok = lambda w: [pl.BlockSpec((1, ROW_TILE, w), lambda bi, s, f=blk(j): (bi, f(s), 0)) for j in range(MLP_SUB)]
    pairs = [pl.BlockSpec((1, NA_HEADS // 2, ROW_TILE, LANES), lambda bi, s, f=blk(j): (bi, 0, f(s), 0))
             for j in range(MLP_SUB)]
    stream = [
        pl.BlockSpec((1, ROW_TILE, d), lambda bi, s: (bi, 0, 0)),
        pl.BlockSpec((1, lat_tile, d), lambda bi, s: (bi, jnp.maximum(s - 1, 0), 0)),
    ]
    kern = functools.partial(_out_mlp_kernel, hid_tile=1024)
    return pl.pallas_call(
        kern,
        grid=(b, 1 + n_lat // lat_tile),
        in_specs=stream + tok(M_WIDTH) + pairs + tok(d - M_WIDTH - NA_WIDTH) + [
            pl.BlockSpec((1, 1, 6, d), lambda bi, s: (bi, jnp.minimum(s, 1), 0, 0)),
            pl.BlockSpec((1, d), lambda bi, s: (0, 0)),
            _resident((None, d, d), lambda bi, s: (layer, 0, 0)),
            _resident((None, d, hidden), lambda bi, s: (layer, 0, 0)),
            _resident((None, hidden, d), lambda bi, s: (layer, 0, 0)),
        ],
        out_specs=stream,
        out_shape=[jax.ShapeDtypeStruct(xc.shape, F32), jax.ShapeDtypeStruct(xl.shape, F32)],
        compiler_params=_params(("arbitrary", "arbitrary")),
    )(xc, xl, *([ym] * MLP_SUB + [yn] * MLP_SUB + [yl] * MLP_SUB), modv, g2.reshape(1, d), w_out, w1, w2)


def _split_w_in(w_in):
    m_end = 4 * M_WIDTH
    na_end = m_end + N_GATES + 3 * NA_WIDTH
    main = jnp.concatenate([w_in[..., :m_end], w_in[..., na_end:], w_in[..., m_end + N_GATES:na_end]], axis=-1)
    gates = jnp.swapaxes(w_in[..., m_end:m_end + N_GATES], -1, -2)
    return main.astype(BF16), gates.astype(BF16)


def kernel(x, c, ctx, c_ctx, w_mod, b_mod, norm1_g, norm2_g, w_in, mlstm_gate_b, mlstm_norm_g, na_q_norm_g, na_k_norm_g, na_rpb, lru_conv_w, lru_conv_b, lru_w_a, lru_b_a, lru_w_x, lru_b_x, lru_lambda, w_out, w_mlp1, w_mlp2):
    b, n, d = x.shape
    n_ctx = ctx.shape[1]
    depth = w_in.shape[0]
    assert n_ctx == ROW_TILE and n % ROW_TILE == 0 and n % GRID_W == 0

    cond_rows = 8 * ((b + 1 + 7) // 8)
    cond = jnp.zeros((cond_rows, d), F32).at[:b].set(c).at[b].set(c_ctx)
    mod_all = _modulation(cond, w_mod, b_mod)
    rope_row, rope_col = _rope_tables(n // GRID_W)
    w_in_p, w_gate_t = _split_w_in(w_in)
    na_bias = _na_bias_tables(na_rpb)
    w_out_b, w1_b, w2_b = w_out.astype(BF16), w_mlp1.astype(BF16), w_mlp2.astype(BF16)

    xc, xl = ctx, x
    for l in range(depth):
        lat = mod_all[l, :b].reshape(b, 6, d)
        cx = jnp.broadcast_to(mod_all[l, b].reshape(1, 6, d), (b, 6, d))
        modv = jnp.stack([cx, lat], axis=1)
        um, ul, un, g_rows = _inproj(xc, xl, modv, norm1_g[l], w_in_p, w_gate_t, l)
        ym = _mlstm(um, g_rows, mlstm_gate_b[l], mlstm_norm_g[l], rope_row, rope_col, n_ctx)
        yn = _na(un, na_q_norm_g[l], na_k_norm_g[l], na_bias, l, n_ctx)
        yl = _lru(ul, lru_conv_w[l], lru_conv_b[l], lru_w_a[l], lru_b_a[l], lru_w_x[l], lru_b_x[l], lru_lambda[l], n_ctx)
        xc, xl = _out_mlp(xc, xl, ym, yn, yl, modv, norm2_g[l], w_out_b, w1_b, w2_b, l)
    return xl
```
